```python
import math, functools
import jax, jax.numpy as jnp
from jax import lax
import numpy as np

D_MODEL = 1024
BATCH = 2
SEQ = 8192
DEPTH = 2

GRID_W = 64
CTX_LEN = 256
N_MIXERS = 2
N_ATTN_LAYERS = (DEPTH + N_MIXERS - 1) // N_MIXERS
N_MLSTM_LAYERS = DEPTH // N_MIXERS

ATTN_HEAD_DIM = 64
ATTN_Q_HEADS = D_MODEL // ATTN_HEAD_DIM
ATTN_KV_HEADS = 4
ATTN_GROUP = ATTN_Q_HEADS // ATTN_KV_HEADS
ATTN_QD = ATTN_Q_HEADS * ATTN_HEAD_DIM
ATTN_KVD = ATTN_KV_HEADS * ATTN_HEAD_DIM
ATTN_QKV_COLS = ATTN_QD + 2 * ATTN_KVD
WINDOW = 128
ATTN_BLOCK = 128
ROPE_THETA = 10000.0

ML_HEADS = 8
ML_QK_DIM = D_MODEL // 2 // ML_HEADS
ML_V_DIM = D_MODEL // ML_HEADS
ML_QK_TOTAL = ML_HEADS * ML_QK_DIM
ML_V_TOTAL = ML_HEADS * ML_V_DIM
ML_IN_COLS = 2 * ML_QK_TOTAL + 2 * ML_V_TOTAL + 4 * ML_HEADS
ML_CHUNK = 64
ML_CONV_W = 5
GATE_CAP = 15.0

N_EXPERTS = 32
TOP_K = 4
EXPERT_FF = D_MODEL
SWIGLU_ALPHA = 1.702
SWIGLU_LIMIT = 7.0

DEEPNORM_ALPHA = (2.0 * DEPTH) ** 0.25
DEEPNORM_BETA = (8.0 * DEPTH) ** -0.25
LN_EPS = 1e-5

kernel_name = "hybrid_swa_mlstm_moe_diffusion_block"

F32 = jnp.float32


def layer_norm(h, g, b):
    hf = h.astype(F32)
    mu = jnp.mean(hf, axis=-1, keepdims=True)
    var = jnp.mean(jnp.square(hf - mu), axis=-1, keepdims=True)
    return ((hf - mu) * lax.rsqrt(var + LN_EPS) * g + b).astype(h.dtype)


def head_norm(h, g):
    hf = h.astype(F32)
    mu = jnp.mean(hf, axis=-1, keepdims=True)
    var = jnp.mean(jnp.square(hf - mu), axis=-1, keepdims=True)
    return (hf - mu) * lax.rsqrt(var + LN_EPS) * g.reshape(h.shape[-2], h.shape[-1]).astype(F32)


def adaln(cvec, w, b):
    m = jax.nn.silu(cvec) @ w + b
    return [t[:, None, :] for t in jnp.split(m, 6, axis=-1)]


def modulate(h, shift, scale):
    return h * (1 + scale) + shift


def rope_1d(h, pos):
    half = h.shape[-1] // 2
    freqs = ROPE_THETA ** (-jnp.arange(half, dtype=F32) / half)
    ang = pos.astype(F32)[:, None] * freqs[None, :]
    cos, sin = jnp.cos(ang)[:, None, :], jnp.sin(ang)[:, None, :]
    hf = h.astype(F32)
    h1, h2 = hf[..., :half], hf[..., half:]
    return jnp.concatenate([h1 * cos - h2 * sin, h2 * cos + h1 * sin], axis=-1).astype(h.dtype)


def axial_rope(h, rows, cols):
    d = h.shape[-1] // 2
    return jnp.concatenate([rope_1d(h[..., :d], rows), rope_1d(h[..., d:], cols)], axis=-1)


def attend_with_sink(scores, values, sink):
    b, kh, g, q = scores[0].shape[:4]
    sink_col = jnp.broadcast_to(sink.reshape(kh, g, 1, 1).astype(F32), (b, kh, g, q, 1))
    p = jax.nn.softmax(jnp.concatenate([*scores, sink_col], axis=-1), axis=-1)
    out = None
    off = 0
    for s, v in zip(scores, values):
        n = s.shape[-1]
        part = jnp.einsum('bkgqs,bskd->bqkgd', p[..., off:off + n].astype(v.dtype), v)
        out = part if out is None else out + part
        off += n
    return out


def attn_mixer(hx, hc, w_qkv, b_qkv, sink, w_o, b_o, rows, cols, with_ctx_out):
    B, S, _ = hx.shape
    C = hc.shape[1]

    def project(h):
        q, k, v = jnp.split(h @ w_qkv + b_qkv, [ATTN_QD, ATTN_QD + ATTN_KVD], axis=-1)
        T = h.shape[1]
        return (q.reshape(B, T, ATTN_Q_HEADS, ATTN_HEAD_DIM),
                k.reshape(B, T, ATTN_KV_HEADS, ATTN_HEAD_DIM),
                v.reshape(B, T, ATTN_KV_HEADS, ATTN_HEAD_DIM))

    scale = ATTN_HEAD_DIM ** -0.5
    qx, kx, vx = project(hx)
    qc, kc, vc = project(hc)
    qx = axial_rope(qx, rows, cols) * scale
    kx = axial_rope(kx, rows, cols)
    qc = qc * scale

    nb = S // ATTN_BLOCK
    qb = jnp.moveaxis(qx.reshape(B, nb, ATTN_BLOCK, ATTN_KV_HEADS, ATTN_GROUP, ATTN_HEAD_DIM), 1, 0)

    def band(a):
        ap = jnp.pad(a, ((0, 0), (WINDOW, WINDOW), (0, 0), (0, 0)))
        ap = ap.reshape(B, nb + 2, ATTN_BLOCK, ATTN_KV_HEADS, ATTN_HEAD_DIM)
        w = jnp.concatenate([ap[:, :-2], ap[:, 1:-1], ap[:, 2:]], axis=2)
        return jnp.moveaxis(w, 1, 0)

    kwin, vwin = band(kx), band(vx)
    blk = jnp.arange(nb)[:, None, None] * ATTN_BLOCK
    qpos = blk + jnp.arange(ATTN_BLOCK)[None, :, None]
    kpos = blk - WINDOW + jnp.arange(3 * ATTN_BLOCK)[None, None, :]
    valid = (jnp.abs(kpos - qpos) <= WINDOW) & (kpos >= 0) & (kpos < S)

    def attend_block(args):
        q_n, k_n, v_n, valid_n = args
        s_loc = jnp.einsum('bqkgd,bskd->bkgqs', q_n, k_n).astype(F32)
        s_loc = jnp.where(valid_n[None, None, None], s_loc, -jnp.inf)
        s_ctx = jnp.einsum('bqkgd,bckd->bkgqc', q_n, kc).astype(F32)
        return attend_with_sink((s_loc, s_ctx), (v_n, vc), sink)

    ox = lax.map(attend_block, (qb, kwin, vwin, valid))
    ox = jnp.moveaxis(ox, 0, 1).reshape(B, S, ATTN_QD) @ w_o + b_o

    oc = None
    if with_ctx_out:
        qcg = qc.reshape(B, C, ATTN_KV_HEADS, ATTN_GROUP, ATTN_HEAD_DIM)
        s_cc = jnp.einsum('bqkgd,bckd->bkgqc', qcg, kc).astype(F32)
        oc = attend_with_sink((s_cc,), (vc,), sink).reshape(B, C, ATTN_QD) @ w_o + b_o
    return ox, oc


def centered_conv(h, w, b):
    ch = h.shape[-1]
    y = lax.conv_general_dilated(h, w[:, None, :], window_strides=(1,),
                                 padding=[(ML_CONV_W // 2, ML_CONV_W // 2)],
                                 dimension_numbers=('NWC', 'WIO', 'NWC'),
                                 feature_group_count=ch)
    return y + b


def mlstm_scan(q, k, v, li, lf, state):
    B, H, T, _ = q.shape
    nc = T // ML_CHUNK

    def chunks(a):
        return jnp.moveaxis(a.reshape(B, H, nc, ML_CHUNK, *a.shape[3:]), 2, 0)

    tril = jnp.tril(jnp.ones((ML_CHUNK, ML_CHUNK), dtype=bool))

    def body(carry, xs):
        C, n, m = carry
        qc, kc, vc, lic, lfc = xs
        b = jnp.cumsum(lfc, axis=-1)
        d = jnp.where(tril, b[..., :, None] - b[..., None, :] + lic[..., None, :], -jnp.inf)
        inter = b + m[..., None]
        m_t = jnp.maximum(jnp.max(d, axis=-1), inter)
        p = jnp.exp(d - m_t[..., None])
        w_inter = jnp.exp(inter - m_t)
        s = jnp.einsum('bhld,bhsd->bhls', qc, kc).astype(F32) * p
        num = jnp.einsum('bhls,bhse->bhle', s, vc) + w_inter[..., None] * jnp.einsum('bhld,bhde->bhle', qc, C)
        den = jnp.sum(s, axis=-1) + w_inter * jnp.einsum('bhld,bhd->bhl', qc, n)
        h = num / jnp.maximum(jnp.abs(den), jnp.exp(-m_t))[..., None]
        b_last = b[..., -1]
        g = b_last[..., None] - b + lic
        m_new = jnp.maximum(b_last + m, jnp.max(g, axis=-1))
        decay = jnp.exp(b_last + m - m_new)
        wk = jnp.exp(g - m_new[..., None])
        C_new = decay[..., None, None] * C + jnp.einsum('bhs,bhsd,bhse->bhde', wk, kc, vc)
        n_new = decay[..., None] * n + jnp.einsum('bhs,bhsd->bhd', wk, kc)
        return (C_new, n_new, m_new), h

    state, hs = lax.scan(body, state, (chunks(q), chunks(k), chunks(v), chunks(li), chunks(lf)))
    h = jnp.moveaxis(hs, 0, 2).reshape(B, H, T, -1)
    return h, state


def soft_cap(g):
    return GATE_CAP * jnp.tanh(g / GATE_CAP)


def mlstm_mixer(hx, hc, w_in, b_in, conv_w, conv_b, norm_g, w_out, with_ctx_out):
    def project(h):
        B, T, _ = h.shape
        z = h @ w_in + b_in
        qk, v, o, gates = jnp.split(z, [2 * ML_QK_TOTAL, 2 * ML_QK_TOTAL + ML_V_TOTAL,
                                        2 * ML_QK_TOTAL + 2 * ML_V_TOTAL], axis=-1)
        qk = jax.nn.silu(centered_conv(qk, conv_w, conv_b))
        q, k = jnp.split(qk, 2, axis=-1)

        def heads(a, dim):
            return a.reshape(B, T, ML_HEADS, dim).transpose(0, 2, 1, 3)

        g = soft_cap(gates.astype(F32)).reshape(B, T, 4, ML_HEADS).transpose(2, 0, 3, 1)
        gate_f = (g[0], jax.nn.log_sigmoid(g[1]))
        gate_b = (g[2], jax.nn.log_sigmoid(g[3]))
        return heads(q, ML_QK_DIM) * ML_QK_DIM ** -0.5, heads(k, ML_QK_DIM), heads(v, ML_V_DIM), o, gate_f, gate_b

    def flip(a):
        return jnp.flip(a, axis=2)

    def bidirectional(q, k, v, gate_f, gate_b, st_f, st_b):
        h_f, st_f = mlstm_scan(q, k, v, gate_f[0], gate_f[1], st_f)
        h_b, st_b = mlstm_scan(flip(q), flip(k), flip(v), flip(gate_b[0]), flip(gate_b[1]), st_b)
        return h_f + flip(h_b), st_f, st_b

    def readout(h, o):
        B, H, T, V = h.shape
        y = head_norm(h.transpose(0, 2, 1, 3), norm_g).reshape(B, T, H * V).astype(o.dtype)
        return (jax.nn.sigmoid(o) * y) @ w_out

    qc, kc, vc, oc, gfc, gbc = project(hc)
    qx, kx, vx, ox, gfx, gbx = project(hx)
    B = hc.shape[0]
    zero = (jnp.zeros((B, ML_HEADS, ML_QK_DIM, ML_V_DIM), F32),
            jnp.zeros((B, ML_HEADS, ML_QK_DIM), F32),
            jnp.zeros((B, ML_HEADS), F32))
    h_ctx, st_f, st_b = bidirectional(qc, kc, vc, gfc, gbc, zero, zero)
    h_lat, _, _ = bidirectional(qx, kx, vx, gfx, gbx, st_f, st_b)
    y_x = readout(h_lat, ox)
    y_c = readout(h_ctx, oc) if with_ctx_out else None
    return y_x, y_c


def moe(h, router_w, router_b, w_gu, b_gu, w_down, b_down):
    logits = (h @ router_w + router_b).astype(F32)
    top_v, top_i = lax.top_k(logits, TOP_K)
    gates = jax.nn.softmax(top_v, axis=-1)
    comb = jnp.sum(jax.nn.one_hot(top_i, N_EXPERTS, dtype=F32) * gates[..., None], axis=1)
    out = jnp.zeros(h.shape, F32)
    for e in range(N_EXPERTS):
        gl, lin = jnp.split(h @ w_gu[e] + b_gu[e], 2, axis=-1)
        gl = jnp.minimum(gl, SWIGLU_LIMIT)
        lin = jnp.clip(lin, -SWIGLU_LIMIT, SWIGLU_LIMIT)
        act = gl * jax.nn.sigmoid(SWIGLU_ALPHA * gl) * (lin + 1)
        out = out + comb[:, e:e + 1] * (act @ w_down[e] + b_down[e])
    return out.astype(h.dtype)


def setup_inputs(seed: int = 0) -> dict:
    key = jax.random.key(seed)
    ks = jax.random.split(key, 25)

    def nrm(k, shape, scale):
        return scale * jax.random.normal(k, shape, F32)

    D = D_MODEL
    attn_col_scale = jnp.concatenate([jnp.ones((ATTN_QD + ATTN_KVD,), F32),
                                      jnp.full((ATTN_KVD,), DEEPNORM_BETA, F32)])
    ml_col_scale = jnp.concatenate([jnp.ones((2 * ML_QK_TOTAL,), F32),
                                    jnp.full((ML_V_TOTAL,), DEEPNORM_BETA, F32),
                                    jnp.ones((ML_V_TOTAL + 4 * ML_HEADS,), F32)])
    fb = jnp.linspace(3.0, 6.0, ML_HEADS, dtype=F32)
    zh = jnp.zeros((ML_HEADS,), F32)
    ml_bias_offset = jnp.concatenate([jnp.zeros((ML_IN_COLS - 4 * ML_HEADS,), F32), zh, fb, zh, fb])
    return {
        "x": nrm(ks[0], (BATCH, SEQ, D), 1.0),
        "c": nrm(ks[1], (BATCH, D), 1.0),
        "ctx": nrm(ks[2], (BATCH, CTX_LEN, D), 1.0),
        "c_ctx": nrm(ks[3], (D,), 1.0),
        "ada_w": nrm(ks[4], (DEPTH, D, 6 * D), 0.5 * D ** -0.5),
        "ada_b": nrm(ks[5], (DEPTH, 6 * D), 0.02),
        "ln_g": 1.0 + nrm(ks[6], (DEPTH, 2, D), 0.02),
        "ln_b": nrm(ks[7], (DEPTH, 2, D), 0.02),
        "attn_w_qkv": nrm(ks[8], (N_ATTN_LAYERS, D, ATTN_QKV_COLS), D ** -0.5) * attn_col_scale,
        "attn_b_qkv": nrm(ks[9], (N_ATTN_LAYERS, ATTN_QKV_COLS), 0.02),
        "attn_sink": nrm(ks[10], (N_ATTN_LAYERS, ATTN_Q_HEADS), 0.5),
        "attn_w_o": nrm(ks[11], (N_ATTN_LAYERS, ATTN_QD, D), ATTN_QD ** -0.5 * DEEPNORM_BETA),
        "attn_b_o": nrm(ks[12], (N_ATTN_LAYERS, D), 0.02),
        "ml_w_in": nrm(ks[13], (N_MLSTM_LAYERS, D, ML_IN_COLS), D ** -0.5) * ml_col_scale,
        "ml_b_in": nrm(ks[14], (N_MLSTM_LAYERS, ML_IN_COLS), 0.02) + ml_bias_offset,
        "ml_conv_w": nrm(ks[15], (N_MLSTM_LAYERS, ML_CONV_W, 2 * ML_QK_TOTAL), ML_CONV_W ** -0.5),
        "ml_conv_b": nrm(ks[16], (N_MLSTM_LAYERS, 2 * ML_QK_TOTAL), 0.02),
        "ml_norm_g": 1.0 + nrm(ks[17], (N_MLSTM_LAYERS, ML_V_TOTAL), 0.02),
        "ml_w_out": nrm(ks[18], (N_MLSTM_LAYERS, ML_V_TOTAL, D), ML_V_TOTAL ** -0.5 * DEEPNORM_BETA),
        "router_w": nrm(ks[19], (DEPTH, D, N_EXPERTS), D ** -0.5),
        "router_b": nrm(ks[20], (DEPTH, N_EXPERTS), 0.01),
        "exp_w_gu": nrm(ks[21], (DEPTH, N_EXPERTS, D, 2 * EXPERT_FF), D ** -0.5 * DEEPNORM_BETA),
        "exp_b_gu": nrm(ks[22], (DEPTH, N_EXPERTS, 2 * EXPERT_FF), 0.02),
        "exp_w_down": nrm(ks[23], (DEPTH, N_EXPERTS, EXPERT_FF, D), EXPERT_FF ** -0.5 * DEEPNORM_BETA),
        "exp_b_down": nrm(ks[24], (DEPTH, N_EXPERTS, D), 0.02),
    }


def reference(x, c, ctx, c_ctx, ada_w, ada_b, ln_g, ln_b,
              attn_w_qkv, attn_b_qkv, attn_sink, attn_w_o, attn_b_o,
              ml_w_in, ml_b_in, ml_conv_w, ml_conv_b, ml_norm_g, ml_w_out,
              router_w, router_b, exp_w_gu, exp_b_gu, exp_w_down, exp_b_down):
    B, S, D = x.shape
    C = ctx.shape[1]
    ROWS = S // GRID_W
    rows = jnp.repeat(jnp.arange(ROWS), GRID_W)
    cols = jnp.arange(ROWS * GRID_W) % GRID_W

    for i in range(DEPTH):
        last = i == DEPTH - 1
        j = i // N_MIXERS
        mx = adaln(c, ada_w[i], ada_b[i])
        mc = adaln(c_ctx[None, :], ada_w[i], ada_b[i])
        hx = modulate(x, mx[0], mx[1])
        hc = modulate(ctx, mc[0], mc[1])
        if i % N_MIXERS == 0:
            yx, yc = attn_mixer(hx, hc, attn_w_qkv[j], attn_b_qkv[j], attn_sink[j], attn_w_o[j], attn_b_o[j],
                                rows, cols, not last)
        else:
            yx, yc = mlstm_mixer(hx, hc, ml_w_in[j], ml_b_in[j], ml_conv_w[j], ml_conv_b[j], ml_norm_g[j],
                                 ml_w_out[j], not last)
        x = layer_norm(DEEPNORM_ALPHA * x + mx[2] * yx, ln_g[i, 0], ln_b[i, 0])
        hx = modulate(x, mx[3], mx[4])
        if last:
            y = moe(hx.reshape(-1, D), router_w[i], router_b[i], exp_w_gu[i], exp_b_gu[i],
                    exp_w_down[i], exp_b_down[i]).reshape(B, S, D)
            x = layer_norm(DEEPNORM_ALPHA * x + mx[5] * y, ln_g[i, 1], ln_b[i, 1])
        else:
            ctx = layer_norm(DEEPNORM_ALPHA * ctx + mc[2] * yc, ln_g[i, 0], ln_b[i, 0])
            hc = modulate(ctx, mc[3], mc[4])
            tokens = jnp.concatenate([hc, hx], axis=1).reshape(-1, D)
            y = moe(tokens, router_w[i], router_b[i], exp_w_gu[i], exp_b_gu[i],
                    exp_w_down[i], exp_b_down[i]).reshape(B, C + S, D)
            ctx = layer_norm(DEEPNORM_ALPHA * ctx + mc[5] * y[:, :C], ln_g[i, 1], ln_b[i, 1])
            x = layer_norm(DEEPNORM_ALPHA * x + mx[5] * y[:, C:], ln_g[i, 1], ln_b[i, 1])
    return x
```

```python
import functools

import jax
import jax.numpy as jnp
from jax import lax
from jax.experimental import pallas as pl
from jax.experimental.pallas import tpu as pltpu

F32 = jnp.float32
BF16 = jnp.bfloat16
I32 = jnp.int32
HIGHEST = lax.Precision.HIGHEST

GRID_W = 64
ATTN_HEAD_DIM = 64
ATTN_KV_HEADS = 4
WINDOW = 128
ATTN_BLOCK = 128
ROPE_THETA = 10000.0
ML_HEADS = 8
ML_CONV_W = 5
GATE_CAP = 15.0
TOP_K = 4
SWIGLU_ALPHA = 1.702
SWIGLU_LIMIT = 7.0
LN_EPS = 1e-5

LANES = 128
SUBLANES = 8
VMEM_LIMIT = 56 * 1024 * 1024
EXPERT_TILE = 256
ML_CHUNK = 128


def _params(n_axes, vmem=VMEM_LIMIT):
    return pltpu.CompilerParams(dimension_semantics=("arbitrary",) * n_axes, vmem_limit_bytes=vmem)


def _layer_norm(r, g, b):
    mu = jnp.mean(r, axis=-1, keepdims=True)
    rc = r - mu
    var = jnp.mean(rc * rc, axis=-1, keepdims=True)
    return rc * lax.rsqrt(var + LN_EPS) * g + b


def _sigmoid(x):
    return 1.0 / (1.0 + jnp.exp(-x))


def _mod_row(i, n_lat_tiles, tiles_per_batch, ctx_row):
    return jnp.where(i < n_lat_tiles, i // tiles_per_batch, ctx_row)


def _adaln_kernel(c_ref, w_ref, b_ref, o_ref):
    c = c_ref[...]
    s = c * _sigmoid(c)
    o_ref[0] = jnp.dot(s, w_ref[0], precision=HIGHEST, preferred_element_type=F32) + b_ref[0]


def _adaln(cvec, ada_w, ada_b):
    depth, d, n = ada_w.shape
    tn = 1536
    return pl.pallas_call(
        _adaln_kernel,
        grid=(depth, n // tn),
        in_specs=[pl.BlockSpec((SUBLANES, d), lambda l, j: (0, 0)),
                  pl.BlockSpec((1, d, tn), lambda l, j: (l, 0, j)),
                  pl.BlockSpec((1, 1, tn), lambda l, j: (l, 0, j))],
        out_specs=pl.BlockSpec((1, SUBLANES, tn), lambda l, j: (l, 0, j)),
        out_shape=jax.ShapeDtypeStruct((depth, SUBLANES, n), F32),
        compiler_params=_params(2),
        name="adaln",
    )(cvec, ada_w, ada_b.reshape(depth, 1, n))


def _qkv_kernel(x_ref, mod_ref, w_ref, b_ref, cos_ref, sin_ref, q_ref, k_ref, v_ref, *,
                n_lat_tiles, tiles_per_batch, ctx_row, d, qd, kvd):
    i = pl.program_id(0)
    row = _mod_row(i, n_lat_tiles, tiles_per_batch, ctx_row)
    shift = mod_ref[pl.ds(row, 1), 0:d]
    scale = mod_ref[pl.ds(row, 1), d:2 * d]
    h = x_ref[...] * (1.0 + scale) + shift
    z = jnp.dot(h.astype(BF16), w_ref[...], preferred_element_type=F32) + b_ref[...]
    nrot = qd + kvd
    qk = z[:, :nrot]
    reps = nrot // LANES
    cos = jnp.concatenate([cos_ref[...]] * reps, axis=1)
    sin = jnp.concatenate([sin_ref[...]] * reps, axis=1)
    lane = lax.broadcasted_iota(I32, qk.shape, 1)
    low_half = (lane & 16) == 0
    partner = jnp.where(low_half, pltpu.roll(qk, nrot - 16, 1), pltpu.roll(qk, 16, 1))
    qk = qk * cos + partner * sin
    q_ref[...] = (qk[:, :qd] * (ATTN_HEAD_DIM ** -0.5)).astype(BF16)
    k_ref[...] = qk[:, qd:].astype(BF16)
    v_ref[...] = z[:, nrot:].astype(BF16)


def _rope_tables(s_len, tm):
    half = ATTN_HEAD_DIM // 4
    freqs = ROPE_THETA ** (-jnp.arange(half, dtype=F32) / half)
    t = jnp.arange(s_len)
    rows = (t // GRID_W).astype(F32)[:, None] * freqs[None, :]
    cols = (t % GRID_W).astype(F32)[:, None] * freqs[None, :]
    ang = jnp.concatenate([rows, rows, cols, cols], axis=1)
    sign = jnp.tile(jnp.concatenate([-jnp.ones((half,), F32), jnp.ones((half,), F32)]), 2)
    cos = jnp.cos(ang)
    sin = jnp.sin(ang) * sign[None, :]
    reps = LANES // ATTN_HEAD_DIM
    cos = jnp.concatenate([jnp.tile(cos, (1, reps)), jnp.ones((tm, LANES), F32)], axis=0)
    sin = jnp.concatenate([jnp.tile(sin, (1, reps)), jnp.zeros((tm, LANES), F32)], axis=0)
    return cos, sin


def _attn_qkv(x_all, mod, w_qkv, b_qkv, dims):
    B, S, C, D = dims
    t_all = x_all.shape[0]
    tm = 512
    n_lat_tiles = B * S // tm
    tiles_per_batch = S // tm
    ncols = w_qkv.shape[1]
    kvd = ATTN_KV_HEADS * ATTN_HEAD_DIM
    qd = ncols - 2 * kvd
    cos, sin = _rope_tables(S, tm)

    def tab_idx(i):
        return (jnp.where(i < n_lat_tiles, i % tiles_per_batch, tiles_per_batch), 0)

    kern = functools.partial(_qkv_kernel, n_lat_tiles=n_lat_tiles, tiles_per_batch=tiles_per_batch,
                             ctx_row=B, d=D, qd=qd, kvd=kvd)
    return pl.pallas_call(
        kern,
        grid=(t_all // tm,),
        in_specs=[pl.BlockSpec((tm, D), lambda i: (i, 0)),
                  pl.BlockSpec(mod.shape, lambda i: (0, 0)),
                  pl.BlockSpec((D, ncols), lambda i: (0, 0)),
                  pl.BlockSpec((1, ncols), lambda i: (0, 0)),
                  pl.BlockSpec((tm, LANES), tab_idx),
                  pl.BlockSpec((tm, LANES), tab_idx)],
        out_specs=[pl.BlockSpec((tm, qd), lambda i: (i, 0)),
                   pl.BlockSpec((tm, kvd), lambda i: (i, 0)),
                   pl.BlockSpec((tm, kvd), lambda i: (i, 0))],
        out_shape=[jax.ShapeDtypeStruct((t_all, qd), BF16),
                   jax.ShapeDtypeStruct((t_all, kvd), BF16),
                   jax.ShapeDtypeStruct((t_all, kvd), BF16)],
        compiler_params=_params(1),
        name="attn_qkv",
    )(x_all, mod, w_qkv.astype(BF16), b_qkv.reshape(1, ncols), cos, sin)


def _attn_kernel(sink_ref, q_ref, kp_ref, ko_ref, kn_ref, kc_ref, vp_ref, vo_ref, vn_ref, vc_ref, o_ref, *,
                 n_lat_steps, nb, s_len, c_len):
    j = pl.program_id(0)
    is_lat = j < n_lat_steps
    n = j % nb
    blk = ATTN_BLOCK
    nloc = 3 * blk
    nk = nloc + c_len
    qi = lax.broadcasted_iota(I32, (blk, nk), 0)
    kj = lax.broadcasted_iota(I32, (blk, nk), 1)
    qpos = n * blk + qi
    kpos = n * blk - WINDOW + kj
    local_ok = (jnp.abs(kpos - qpos) <= WINDOW) & (kpos >= 0) & (kpos < s_len) & is_lat
    valid = local_ok | (kj >= nloc)
    hd = ATTN_HEAD_DIM
    group = q_ref.shape[1] // (ATTN_KV_HEADS * hd)
    for kh in range(ATTN_KV_HEADS):
        cs = slice(kh * hd, (kh + 1) * hd)
        kcat = jnp.concatenate([kp_ref[:, cs], ko_ref[:, cs], kn_ref[:, cs], kc_ref[:, cs]], axis=0)
        vcat = jnp.concatenate([vp_ref[:, cs], vo_ref[:, cs], vn_ref[:, cs], vc_ref[:, cs]], axis=0)
        outs = []
        for g in range(group):
            h = kh * group + g
            qh = q_ref[:, h * hd:(h + 1) * hd]
            s = lax.dot_general(qh, kcat, (((1,), (1,)), ((), ())), preferred_element_type=F32)
            s = jnp.where(valid, s, -jnp.inf)
            sk = sink_ref[h]
            m = jnp.maximum(jnp.max(s, axis=1, keepdims=True), sk)
            p = jnp.exp(s - m)
            l = jnp.sum(p, axis=1, keepdims=True) + jnp.exp(sk - m)
            o = jnp.dot(p.astype(BF16), vcat, preferred_element_type=F32) / l
            outs.append(o)
        w = group * hd
        o_ref[:, kh * w:(kh + 1) * w] = jnp.concatenate(outs, axis=1).astype(BF16)


def _attention(q_all, k_all, v_all, sink, dims):
    B, S, C, D = dims
    blk = ATTN_BLOCK
    nb = S // blk
    n_lat_steps = B * nb
    ctx_steps_per_batch = C // blk
    n_steps = n_lat_steps + B * ctx_steps_per_batch
    qd = q_all.shape[1]
    kvd = k_all.shape[1]

    def local_idx(off):
        def idx(j):
            b = j // nb
            nn = jnp.clip(j % nb + off, 0, nb - 1)
            return (jnp.where(j < n_lat_steps, b * nb + nn, j), 0)
        return idx

    def ctx_idx(j):
        b = jnp.where(j < n_lat_steps, j // nb, (j - n_lat_steps) // ctx_steps_per_batch)
        return (B * S // C + b, 0)

    loc = lambda off: pl.BlockSpec((blk, kvd), local_idx(off))
    ctxs = pl.BlockSpec((C, kvd), ctx_idx)
    kern = functools.partial(_attn_kernel, n_lat_steps=n_lat_steps, nb=nb, s_len=S, c_len=C)
    return pl.pallas_call(
        kern,
        grid=(n_steps,),
        in_specs=[pl.BlockSpec(memory_space=pltpu.SMEM),
                  pl.BlockSpec((blk, qd), lambda j: (j, 0)),
                  loc(-1), loc(0), loc(1), ctxs,
                  loc(-1), loc(0), loc(1), ctxs],
        out_specs=pl.BlockSpec((blk, qd), lambda j: (j, 0)),
        out_shape=jax.ShapeDtypeStruct((q_all.shape[0], qd), BF16),
        compiler_params=_params(1),
        name="attn_core",
    )(sink, q_all, k_all, k_all, k_all, k_all, v_all, v_all, v_all, v_all)


def _route(hx, rw_ref, rb_ref, topi_ref, gate_ref):
    logits = jnp.dot(hx, rw_ref[...], precision=HIGHEST, preferred_element_type=F32) + rb_ref[...]
    lane = lax.broadcasted_iota(I32, logits.shape, 1)
    lanef = lane.astype(F32)
    vals, idxs = [], []
    l = logits
    for _ in range(TOP_K):
        m = jnp.max(l, axis=1, keepdims=True)
        idx = jnp.min(jnp.where(l == m, lanef, float(LANES)), axis=1, keepdims=True)
        vals.append(m)
        idxs.append(idx)
        l = jnp.where(lanef == idx, -jnp.inf, l)
    es = [jnp.exp(v - vals[0]) for v in vals]
    denom = es[0]
    for e in es[1:]:
        denom = denom + e
    topi = jnp.zeros(logits.shape, F32)
    gates = jnp.zeros(logits.shape, F32)
    for k in range(TOP_K):
        topi = jnp.where(lane == k, idxs[k], topi)
        gates = jnp.where(lane == k, es[k] / denom, gates)
    topi_ref[...] = topi.astype(I32)
    gate_ref[...] = gates


def _post_common(a, x_ref, mod_ref, w_ref, b_ref, lng_ref, lnb_ref, rw_ref, rb_ref,
                 x1_ref, hx_ref, topi_ref, gate_ref, row, d, alpha):
    y = jnp.dot(a, w_ref[...], preferred_element_type=F32) + b_ref[...]
    gate_mix = mod_ref[pl.ds(row, 1), 2 * d:3 * d]
    shift = mod_ref[pl.ds(row, 1), 3 * d:4 * d]
    scale = mod_ref[pl.ds(row, 1), 4 * d:5 * d]
    x1 = _layer_norm(alpha * x_ref[...] + gate_mix * y, lng_ref[...], lnb_ref[...])
    hx = x1 * (1.0 + scale) + shift
    x1_ref[...] = x1
    hx_ref[...] = hx
    _route(hx, rw_ref, rb_ref, topi_ref, gate_ref)


def _post_attn_kernel(o_ref, x_ref, mod_ref, w_ref, b_ref, lng_ref, lnb_ref, rw_ref, rb_ref,
                      x1_ref, hx_ref, topi_ref, gate_ref, *, n_lat_tiles, tiles_per_batch, ctx_row, d, alpha):
    row = _mod_row(pl.program_id(0), n_lat_tiles, tiles_per_batch, ctx_row)
    _post_common(o_ref[...], x_ref, mod_ref, w_ref, b_ref, lng_ref, lnb_ref, rw_ref, rb_ref,
                 x1_ref, hx_ref, topi_ref, gate_ref, row, d, alpha)


def _post_mlstm_kernel(hf_ref, hb_ref, og_ref, ng_ref, x_ref, mod_ref, w_ref, b_ref, lng_ref, lnb_ref,
                       rw_ref, rb_ref, x1_ref, hx_ref, topi_ref, gate_ref, *,
                       n_lat_tiles, tiles_per_batch, ctx_row, d, alpha):
    row = _mod_row(pl.program_id(0), n_lat_tiles, tiles_per_batch, ctx_row)
    hsum = hf_ref[...] + hb_ref[...]
    vdim = hsum.shape[1] // ML_HEADS
    parts = []
    for h in range(ML_HEADS):
        seg = hsum[:, h * vdim:(h + 1) * vdim]
        mu = jnp.mean(seg, axis=1, keepdims=True)
        sc = seg - mu
        var = jnp.mean(sc * sc, axis=1, keepdims=True)
        parts.append(sc * lax.rsqrt(var + LN_EPS))
    y = jnp.concatenate(parts, axis=1) * ng_ref[...]
    a = (og_ref[...].astype(F32) * y).astype(BF16)
    _post_common(a, x_ref, mod_ref, w_ref, b_ref, lng_ref, lnb_ref, rw_ref, rb_ref,
                 x1_ref, hx_ref, topi_ref, gate_ref, row, d, alpha)


def _post_mixer(mixer_inputs, norm_g, x_all, mod, w_o, b_o, ln_g, ln_b, router_w, router_b, dims, n_rows, alpha):
    B, S, C, D = dims
    tm = 256
    n_lat_tiles = B * S // tm
    tiles_per_batch = S // tm
    n_exp = router_w.shape[1]
    rw = jnp.zeros((D, LANES), F32).at[:, :n_exp].set(router_w)
    rb = jnp.full((1, LANES), -1e30, F32).at[0, :n_exp].set(router_b)
    row_spec = lambda w: pl.BlockSpec((tm, w), lambda i: (i, 0))
    full = lambda a: pl.BlockSpec(a.shape, lambda i: (0,) * a.ndim)
    common = dict(n_lat_tiles=n_lat_tiles, tiles_per_batch=tiles_per_batch, ctx_row=B, d=D, alpha=alpha)
    w_bf = w_o.astype(BF16)
    b2 = b_o.reshape(1, D)
    tail = [x_all, mod, w_bf, b2, ln_g.reshape(1, D), ln_b.reshape(1, D), rw, rb]
    tail_specs = [row_spec(D), full(mod), full(w_bf), full(b2), pl.BlockSpec((1, D), lambda i: (0, 0)),
                  pl.BlockSpec((1, D), lambda i: (0, 0)), full(rw), full(rb)]
    if norm_g is None:
        kern = functools.partial(_post_attn_kernel, **common)
        args = list(mixer_inputs) + tail
        specs = [row_spec(mixer_inputs[0].shape[1])] + tail_specs
        name = "post_attn"
    else:
        kern = functools.partial(_post_mlstm_kernel, **common)
        ng = norm_g.reshape(1, -1)
        args = list(mixer_inputs) + [ng] + tail
        specs = [row_spec(a.shape[1]) for a in mixer_inputs] + [full(ng)] + tail_specs
        name = "post_mlstm"
    return pl.pallas_call(
        kern,
        grid=(n_rows // tm,),
        in_specs=specs,
        out_specs=[row_spec(D), row_spec(D), row_spec(LANES), row_spec(LANES)],
        out_shape=[jax.ShapeDtypeStruct((n_rows, D), F32), jax.ShapeDtypeStruct((n_rows, D), F32),
                   jax.ShapeDtypeStruct((n_rows, LANES), I32), jax.ShapeDtypeStruct((n_rows, LANES), F32)],
        compiler_params=_params(1),
        name=name,
    )(*args)


def _slots_kernel(topi_ref, pos_ref, cnt_ref, cnt_sc, base_sc, carry_sc, *, tile):
    phase = pl.program_id(0)
    i = pl.program_id(1)
    tm = topi_ref.shape[0]
    topi = topi_ref[...]
    lane = lax.broadcasted_iota(I32, topi.shape, 1)
    sel = [lane == topi[:, k:k + 1] for k in range(TOP_K)]
    mask = sel[0]
    for s in sel[1:]:
        mask = mask | s
    maskf = jnp.where(mask, 1.0, 0.0)
    colsum = jnp.sum(maskf, axis=0, keepdims=True)

    @pl.when((phase == 0) & (i == 0))
    def _():
        cnt_sc[...] = jnp.zeros_like(cnt_sc)

    @pl.when(phase == 0)
    def _():
        cnt_sc[...] = cnt_sc[...] + colsum

    @pl.when((phase == 1) & (i == 0))
    def _():
        cnt = cnt_sc[...]
        padded = jnp.ceil(cnt / tile) * tile
        r = lax.broadcasted_iota(I32, (LANES, LANES), 0)
        c = lax.broadcasted_iota(I32, (LANES, LANES), 1)
        before = jnp.where(r < c, 1.0, 0.0)
        base_sc[...] = jnp.dot(jnp.broadcast_to(padded, (SUBLANES, LANES)), before,
                               precision=HIGHEST, preferred_element_type=F32)[0:1]
        carry_sc[...] = jnp.zeros_like(carry_sc)
        cnt_ref[...] = jnp.broadcast_to(cnt, cnt_ref.shape)

    @pl.when(phase == 1)
    def _():
        r = lax.broadcasted_iota(I32, (tm, tm), 0)
        c = lax.broadcasted_iota(I32, (tm, tm), 1)
        earlier = jnp.where(c < r, 1.0, 0.0).astype(BF16)
        rank = jnp.dot(earlier, maskf.astype(BF16), preferred_element_type=F32)
        slot = rank + carry_sc[...] + base_sc[...]
        out = jnp.zeros(topi.shape, F32)
        for k in range(TOP_K):
            pk = jnp.sum(jnp.where(sel[k], slot, 0.0), axis=1, keepdims=True)
            out = jnp.where(lane == k, pk, out)
        pos_ref[...] = out.astype(I32)
        carry_sc[...] = carry_sc[...] + colsum


def _slots(topi, tile):
    t = topi.shape[0]
    tm = 256
    return pl.pallas_call(
        functools.partial(_slots_kernel, tile=float(tile)),
        grid=(2, t // tm),
        in_specs=[pl.BlockSpec((tm, LANES), lambda p, i: (i, 0))],
        out_specs=[pl.BlockSpec((tm, LANES), lambda p, i: (i * p, 0)),
                   pl.BlockSpec((SUBLANES, LANES), lambda p, i: (0, 0))],
        out_shape=[jax.ShapeDtypeStruct((t, LANES), I32), jax.ShapeDtypeStruct((SUBLANES, LANES), F32)],
        scratch_shapes=[pltpu.VMEM((1, LANES), F32), pltpu.VMEM((1, LANES), F32), pltpu.VMEM((1, LANES), F32)],
        compiler_params=_params(2),
        name="moe_slots",
    )(topi)


def _dispatch_kernel(pos_ref, hx_ref, xs_in_ref, xs_ref, sem):
    del xs_in_ref
    i = pl.program_id(0)
    tm = hx_ref.shape[0]

    def row_copy(r, slot):
        return pltpu.make_async_copy(hx_ref.at[pl.ds(r, 1)], xs_ref.at[pl.ds(slot, 1)], sem)

    def issue(r, carry):
        base = (i * tm + r) * TOP_K
        for k in range(TOP_K):
            row_copy(r, pos_ref[base + k]).start()
        return carry

    lax.fori_loop(0, tm, issue, 0)
    for k in range(TOP_K):
        pltpu.make_async_copy(hx_ref, xs_ref.at[pl.ds(0, tm)], sem).wait()


def _dispatch(pos_flat, hx, n_slots):
    t, d = hx.shape
    tm = 256
    xs0 = jnp.zeros((n_slots, d), F32)
    return pl.pallas_call(
        _dispatch_kernel,
        grid_spec=pltpu.PrefetchScalarGridSpec(
            num_scalar_prefetch=1,
            grid=(t // tm,),
            in_specs=[pl.BlockSpec((tm, d), lambda i, pos: (i, 0)),
                      pl.BlockSpec(memory_space=pl.ANY)],
            out_specs=pl.BlockSpec(memory_space=pl.ANY),
            scratch_shapes=[pltpu.SemaphoreType.DMA(())]),
        out_shape=jax.ShapeDtypeStruct((n_slots, d), F32),
        input_output_aliases={2: 0},
        compiler_params=_params(1),
        name="moe_dispatch",
    )(pos_flat, hx, xs0)


def _expert_kernel(te_ref, nu_ref, xs_ref, wgu_ref, bgu_ref, wd_ref, bd_ref, ys_ref, wgu_sc, wd_sc):
    j = pl.program_id(0)
    active = j < nu_ref[0]
    changed = (j == 0) | (te_ref[j] != te_ref[jnp.maximum(j - 1, 0)])
    ff = wd_ref.shape[1]

    @pl.when(active & changed)
    def _():
        wgu_sc[...] = wgu_ref[0].astype(BF16)
        wd_sc[...] = wd_ref[0].astype(BF16)

    @pl.when(active)
    def _():
        x = xs_ref[...].astype(BF16)
        gu = jnp.dot(x, wgu_sc[...], preferred_element_type=F32) + bgu_ref[0]
        gl = jnp.minimum(gu[:, :ff], SWIGLU_LIMIT)
        lin = jnp.clip(gu[:, ff:], -SWIGLU_LIMIT, SWIGLU_LIMIT)
        act = gl * _sigmoid(SWIGLU_ALPHA * gl) * (lin + 1.0)
        ys_ref[...] = jnp.dot(act.astype(BF16), wd_sc[...], preferred_element_type=F32) + bd_ref[0]

    @pl.when(jnp.logical_not(active))
    def _():
        ys_ref[...] = jnp.zeros_like(ys_ref)


def _experts(tile_expert, n_used, xs, w_gu, b_gu, w_down, b_down):
    n_slots, d = xs.shape
    n_exp, _, ff2 = w_gu.shape
    ff = w_down.shape[1]
    te = EXPERT_TILE
    return pl.pallas_call(
        _expert_kernel,
        grid_spec=pltpu.PrefetchScalarGridSpec(
            num_scalar_prefetch=2,
            grid=(n_slots // te,),
            in_specs=[pl.BlockSpec((te, d), lambda j, te_r, nu: (j, 0)),
                      pl.BlockSpec((1, d, ff2), lambda j, te_r, nu: (te_r[j], 0, 0)),
                      pl.BlockSpec((1, 1, ff2), lambda j, te_r, nu: (te_r[j], 0, 0)),
                      pl.BlockSpec((1, ff, d), lambda j, te_r, nu: (te_r[j], 0, 0)),
                      pl.BlockSpec((1, 1, d), lambda j, te_r, nu: (te_r[j], 0, 0))],
            out_specs=pl.BlockSpec((te, d), lambda j, te_r, nu: (j, 0)),
            scratch_shapes=[pltpu.VMEM((d, ff2), BF16), pltpu.VMEM((ff, d), BF16)]),
        out_shape=jax.ShapeDtypeStruct((n_slots, d), F32),
        compiler_params=_params(1),
        name="moe_experts",
    )(tile_expert, n_used, xs, w_gu, b_gu.reshape(n_exp, 1, ff2), w_down, b_down.reshape(n_exp, 1, d))


def _combine_kernel(pos_ref, ys_ref, gate_ref, x_ref, mod_ref, lng_ref, lnb_ref, out_ref, buf, sem, *,
                    n_lat_tiles, tiles_per_batch, ctx_row, d, alpha):
    i = pl.program_id(0)
    tm = x_ref.shape[0]

    def issue(r, carry):
        base = (i * tm + r) * TOP_K
        for k in range(TOP_K):
            pltpu.make_async_copy(ys_ref.at[pl.ds(pos_ref[base + k], 1)], buf.at[k, pl.ds(r, 1)], sem).start()
        return carry

    lax.fori_loop(0, tm, issue, 0)
    for k in range(TOP_K):
        pltpu.make_async_copy(ys_ref.at[pl.ds(0, tm)], buf.at[k], sem).wait()
    gates = gate_ref[...]
    y = gates[:, 0:1] * buf[0]
    for k in range(1, TOP_K):
        y = y + gates[:, k:k + 1] * buf[k]
    row = _mod_row(i, n_lat_tiles, tiles_per_batch, ctx_row)
    gate_mlp = mod_ref[pl.ds(row, 1), 5 * d:6 * d]
    out_ref[...] = _layer_norm(alpha * x_ref[...] + gate_mlp * y, lng_ref[...], lnb_ref[...])


def _combine(pos_flat, ys, gates, x1, mod, ln_g, ln_b, dims, alpha):
    B, S, C, D = dims
    t = x1.shape[0]
    tm = 128
    kern = functools.partial(_combine_kernel, n_lat_tiles=B * S // tm, tiles_per_batch=S // tm, ctx_row=B,
                             d=D, alpha=alpha)
    return pl.pallas_call(
        kern,
        grid_spec=pltpu.PrefetchScalarGridSpec(
            num_scalar_prefetch=1,
            grid=(t // tm,),
            in_specs=[pl.BlockSpec(memory_space=pl.ANY),
                      pl.BlockSpec((tm, LANES), lambda i, pos: (i, 0)),
                      pl.BlockSpec((tm, D), lambda i, pos: (i, 0)),
                      pl.BlockSpec(mod.shape, lambda i, pos: (0, 0)),
                      pl.BlockSpec((1, D), lambda i, pos: (0, 0)),
                      pl.BlockSpec((1, D), lambda i, pos: (0, 0))],
            out_specs=pl.BlockSpec((tm, D), lambda i, pos: (i, 0)),
            scratch_shapes=[pltpu.VMEM((TOP_K, tm, D), F32), pltpu.SemaphoreType.DMA(())]),
        out_shape=jax.ShapeDtypeStruct((t, D), F32),
        compiler_params=_params(1),
        name="moe_combine",
    )(pos_flat, ys, gates, x1, mod, ln_g.reshape(1, D), ln_b.reshape(1, D))


def _moe(hx, topi, gates, x1, mod, ln_g, ln_b, w_gu, b_gu, w_down, b_down, dims, alpha):
    t = hx.shape[0]
    n_exp = w_gu.shape[0]
    te = EXPERT_TILE
    n_tiles = t * TOP_K // te + n_exp
    pos, cnt = _slots(topi, te)
    pos_flat = pos[:, :TOP_K].reshape(-1)
    tiles_per_expert = ((cnt[0, :n_exp].astype(I32) + te - 1) // te)
    ends = jnp.cumsum(tiles_per_expert)
    n_used = ends[-1:].astype(I32)
    tile_ids = jnp.minimum(jnp.arange(n_tiles, dtype=I32), n_used[0] - 1)
    tile_expert = jnp.sum((tile_ids[:, None] >= ends[None, :]).astype(I32), axis=1)
    tile_expert = jnp.minimum(tile_expert, n_exp - 1).astype(I32)
    xs = _dispatch(pos_flat, hx, n_tiles * te)
    ys = _experts(tile_expert, n_used, xs, w_gu, b_gu, w_down, b_down)
    return _combine(pos_flat, ys, gates, x1, mod, ln_g, ln_b, dims, alpha)


def _ml_in_kernel(xp_ref, x_ref, xn_ref, mod_ref, wqk_ref, bqk_ref, wvo_ref, bvo_ref, wg_ref, bg_ref,
                  cw_ref, cb_ref, q_ref, k_ref, v_ref, og_ref, g_ref, *,
                  n_lat_tiles, tiles_per_batch, ctx_tiles_per_seq, ctx_row, d):
    i = pl.program_id(0)
    tm = x_ref.shape[0]
    halo = SUBLANES
    row = _mod_row(i, n_lat_tiles, tiles_per_batch, ctx_row)
    shift = mod_ref[pl.ds(row, 1), 0:d]
    scale = mod_ref[pl.ds(row, 1), d:2 * d]
    is_lat = i < n_lat_tiles
    seq_tile = jnp.where(is_lat, i % tiles_per_batch, (i - n_lat_tiles) % ctx_tiles_per_seq)
    seq_tiles = jnp.where(is_lat, tiles_per_batch, ctx_tiles_per_seq)
    first = seq_tile == 0
    last = seq_tile == seq_tiles - 1

    h = x_ref[...] * (1.0 + scale) + shift
    h_ext = jnp.concatenate([xp_ref[...] * (1.0 + scale) + shift, h, xn_ref[...] * (1.0 + scale) + shift], axis=0)
    z = jnp.dot(h_ext.astype(BF16), wqk_ref[...], preferred_element_type=F32) + bqk_ref[...]
    r = lax.broadcasted_iota(I32, z.shape, 0)
    outside = ((r < halo) & first) | ((r >= halo + tm) & last)
    z = jnp.where(outside, 0.0, z)
    n_ext = tm + 2 * halo
    cw = cw_ref[...]
    acc = None
    for j in range(ML_CONV_W):
        sh = (ML_CONV_W // 2 - j) % n_ext
        zj = z if sh == 0 else pltpu.roll(z, sh, 0)
        term = zj[halo:halo + tm] * cw[j:j + 1]
        acc = term if acc is None else acc + term
    qk = acc + cb_ref[...]
    qk = qk * _sigmoid(qk)
    nqk = qk.shape[1] // 2
    qk_dim = nqk // ML_HEADS
    q_ref[...] = (qk[:, :nqk] * (qk_dim ** -0.5)).astype(BF16)
    k_ref[...] = qk[:, nqk:].astype(BF16)

    vo = jnp.dot(h.astype(BF16), wvo_ref[...], preferred_element_type=F32) + bvo_ref[...]
    nv = vo.shape[1] // 2
    v_ref[...] = vo[:, :nv].astype(BF16)
    og_ref[...] = _sigmoid(vo[:, nv:]).astype(BF16)

    zg = jnp.dot(h, wg_ref[...], precision=HIGHEST, preferred_element_type=F32) + bg_ref[...]
    g = GATE_CAP * jnp.tanh(zg / GATE_CAP)
    log_sig = jnp.minimum(g, 0.0) - jnp.log(1.0 + jnp.exp(-jnp.abs(g)))
    lane = lax.broadcasted_iota(I32, g.shape, 1)
    is_forget = ((lane // ML_HEADS) % 2) == 1
    g_ref[...] = jnp.where(is_forget, log_sig, g)


def _ml_in(x_all, mod, w_in, b_in, conv_w, conv_b, dims):
    B, S, C, D = dims
    t_all = x_all.shape[0]
    tm = 256
    n_lat_tiles = B * S // tm
    nqk2 = conv_w.shape[1]
    nv = (w_in.shape[1] - nqk2 - 4 * ML_HEADS) // 2
    ng = 4 * ML_HEADS
    w_qk = w_in[:, :nqk2].astype(BF16)
    w_vo = w_in[:, nqk2:nqk2 + 2 * nv].astype(BF16)
    w_g = jnp.zeros((D, LANES), F32).at[:, :ng].set(w_in[:, nqk2 + 2 * nv:])
    b_qk = b_in[:nqk2].reshape(1, -1)
    b_vo = b_in[nqk2:nqk2 + 2 * nv].reshape(1, -1)
    b_g = jnp.zeros((1, LANES), F32).at[0, :ng].set(b_in[nqk2 + 2 * nv:])
    cw = jnp.zeros((SUBLANES, nqk2), F32).at[:ML_CONV_W].set(conv_w)
    cb = conv_b.reshape(1, -1)
    hb = tm // SUBLANES
    n_hblk = t_all // SUBLANES
    full = lambda a: pl.BlockSpec(a.shape, lambda i: (0,) * a.ndim)
    row_spec = lambda w: pl.BlockSpec((tm, w), lambda i: (i, 0))
    kern = functools.partial(_ml_in_kernel, n_lat_tiles=n_lat_tiles, tiles_per_batch=S // tm,
                             ctx_tiles_per_seq=C // tm, ctx_row=B, d=D)
    return pl.pallas_call(
        kern,
        grid=(t_all // tm,),
        in_specs=[pl.BlockSpec((SUBLANES, D), lambda i: (jnp.maximum(i * hb - 1, 0), 0)),
                  row_spec(D),
                  pl.BlockSpec((SUBLANES, D), lambda i: (jnp.minimum((i + 1) * hb, n_hblk - 1), 0)),
                  full(mod), full(w_qk), full(b_qk), full(w_vo), full(b_vo), full(w_g), full(b_g),
                  full(cw), full(cb)],
        out_specs=[row_spec(nqk2 // 2), row_spec(nqk2 // 2), row_spec(nv), row_spec(nv), row_spec(LANES)],
        out_shape=[jax.ShapeDtypeStruct((t_all, nqk2 // 2), BF16), jax.ShapeDtypeStruct((t_all, nqk2 // 2), BF16),
                   jax.ShapeDtypeStruct((t_all, nv), BF16), jax.ShapeDtypeStruct((t_all, nv), BF16),
                   jax.ShapeDtypeStruct((t_all, LANES), F32)],
        compiler_params=_params(1),
        name="mlstm_in",
    )(x_all, x_all, x_all, mod, w_qk, b_qk, w_vo, b_vo, w_g, b_g, cw, cb)


def _ml_scan_kernel(q_ref, k_ref, v_ref, g_ref, h_ref, c_sc, n_sc, m_sc, *, reverse):
    c_idx = pl.program_id(1)
    L = q_ref.shape[0]
    qk_dim = q_ref.shape[1] // ML_HEADS
    v_dim = v_ref.shape[1] // ML_HEADS

    @pl.when(c_idx == 0)
    def _():
        c_sc[...] = jnp.zeros_like(c_sc)
        n_sc[...] = jnp.zeros_like(n_sc)
        m_sc[...] = jnp.zeros_like(m_sc)

    gates = g_ref[...]
    r = lax.broadcasted_iota(I32, (L, L), 0)
    c = lax.broadcasted_iota(I32, (L, L), 1)
    tri = (c >= r) if reverse else (c <= r)
    cum = jnp.dot(jnp.where(tri, 1.0, 0.0), gates, precision=HIGHEST, preferred_element_type=F32)
    gates_t = gates.T
    cum_t = cum.T
    off = 2 * ML_HEADS if reverse else 0
    end = 0 if reverse else L - 1
    for h in range(ML_HEADS):
        li_lane = off + h
        lf_lane = off + ML_HEADS + h
        b_col = cum[:, lf_lane:lf_lane + 1]
        b_row = cum_t[lf_lane:lf_lane + 1, :]
        li_col = gates[:, li_lane:li_lane + 1]
        li_row = gates_t[li_lane:li_lane + 1, :]
        b_last = b_col[end:end + 1, :]
        m_prev = m_sc[h:h + 1, 0:1]
        dmat = jnp.where(tri, b_col - b_row + li_row, -jnp.inf)
        inter = b_col + m_prev
        m_t = jnp.maximum(jnp.max(dmat, axis=1, keepdims=True), inter)
        p = jnp.exp(dmat - m_t)
        w_inter = jnp.exp(inter - m_t)
        qh = q_ref[:, h * qk_dim:(h + 1) * qk_dim]
        kh = k_ref[:, h * qk_dim:(h + 1) * qk_dim]
        vh = v_ref[:, h * v_dim:(h + 1) * v_dim]
        s = lax.dot_general(qh, kh, (((1,), (1,)), ((), ())), preferred_element_type=F32) * p
        c_h = c_sc[h]
        n_row = n_sc[h:h + 1, 0:qk_dim]
        num = (jnp.dot(s.astype(BF16), vh, preferred_element_type=F32)
               + w_inter * jnp.dot(qh, c_h.astype(BF16), preferred_element_type=F32))
        den = (jnp.sum(s, axis=1, keepdims=True)
               + w_inter * jnp.sum(qh.astype(F32) * n_row, axis=1, keepdims=True))
        h_ref[:, h * v_dim:(h + 1) * v_dim] = num / jnp.maximum(jnp.abs(den), jnp.exp(-m_t))

        g_row = b_last - b_row + li_row
        g_col = b_last - b_col + li_col
        m_new = jnp.maximum(b_last + m_prev, jnp.max(g_row, axis=1, keepdims=True))
        decay = jnp.exp(b_last + m_prev - m_new)
        kw = kh.astype(F32) * jnp.exp(g_col - m_new)
        c_sc[h] = decay * c_h + lax.dot_general(kw.astype(BF16), vh, (((0,), (0,)), ((), ())),
                                                preferred_element_type=F32)
        n_sc[h:h + 1, 0:qk_dim] = decay * n_row + jnp.sum(kw, axis=0, keepdims=True)
        m_sc[h:h + 1, :] = jnp.broadcast_to(m_new, (1, LANES))


def _ml_scan(q, k, v, g, dims, reverse):
    B, S, C, D = dims
    L = ML_CHUNK
    nc_ctx = C // L
    nc_lat = S // L
    t_all = q.shape[0]
    qk_dim = q.shape[1] // ML_HEADS
    v_dim = v.shape[1] // ML_HEADS

    def idx(b, c):
        in_ctx = c < nc_ctx
        cl = c - nc_ctx
        if reverse:
            ctx_blk = (B * S + b * C) // L + (nc_ctx - 1 - c)
            lat_blk = (b * S) // L + (nc_lat - 1 - cl)
        else:
            ctx_blk = (B * S + b * C) // L + c
            lat_blk = (b * S) // L + cl
        return (jnp.where(in_ctx, ctx_blk, lat_blk), 0)

    spec = lambda w: pl.BlockSpec((L, w), idx)
    return pl.pallas_call(
        functools.partial(_ml_scan_kernel, reverse=reverse),
        grid=(B, nc_ctx + nc_lat),
        in_specs=[spec(q.shape[1]), spec(k.shape[1]), spec(v.shape[1]), spec(LANES)],
        out_specs=spec(v.shape[1]),
        out_shape=jax.ShapeDtypeStruct((t_all, v.shape[1]), F32),
        scratch_shapes=[pltpu.VMEM((ML_HEADS, qk_dim, v_dim), F32),
                        pltpu.VMEM((ML_HEADS, LANES), F32),
                        pltpu.VMEM((ML_HEADS, LANES), F32)],
        compiler_params=_params(2),
        name="mlstm_scan_bwd" if reverse else "mlstm_scan_fwd",
    )(q, k, v, g)


def kernel(x, c, ctx, c_ctx, ada_w, ada_b, ln_g, ln_b, attn_w_qkv, attn_b_qkv, attn_sink, attn_w_o, attn_b_o,
           ml_w_in, ml_b_in, ml_conv_w, ml_conv_b, ml_norm_g, ml_w_out, router_w, router_b,
           exp_w_gu, exp_b_gu, exp_w_down, exp_b_down):
    B, S, D = x.shape
    C = ctx.shape[1]
    depth = ada_w.shape[0]
    dims = (B, S, C, D)
    alpha = (2.0 * depth) ** 0.25
    n_lat = B * S

    cvec = jnp.zeros((SUBLANES, D), F32).at[:B].set(c).at[B].set(c_ctx)
    mods = _adaln(cvec, ada_w, ada_b)
    x_all = jnp.concatenate([x.reshape(n_lat, D), ctx.reshape(B * C, D)], axis=0)

    q, k, v = _attn_qkv(x_all, mods[0], attn_w_qkv[0], attn_b_qkv[0], dims)
    o = _attention(q, k, v, attn_sink[0], dims)
    x1, hx, topi, gates = _post_mixer((o,), None, x_all, mods[0], attn_w_o[0], attn_b_o[0], ln_g[0, 0], ln_b[0, 0],
                                      router_w[0], router_b[0], dims, x_all.shape[0], alpha)
    x_all = _moe(hx, topi, gates, x1, mods[0], ln_g[0, 1], ln_b[0, 1],
                 exp_w_gu[0], exp_b_gu[0], exp_w_down[0], exp_b_down[0], dims, alpha)

    q, k, v, og, g = _ml_in(x_all, mods[1], ml_w_in[0], ml_b_in[0], ml_conv_w[0], ml_conv_b[0], dims)
    hf = _ml_scan(q, k, v, g, dims, reverse=False)
    hb = _ml_scan(q, k, v, g, dims, reverse=True)
    zero_b = jnp.zeros((D,), F32)
    x1, hx, topi, gates = _post_mixer((hf, hb, og), ml_norm_g[0], x_all, mods[1], ml_w_out[0], zero_b,
                                      ln_g[1, 0], ln_b[1, 0], router_w[1], router_b[1], dims, n_lat, alpha)
    out = _moe(hx, topi, gates, x1, mods[1], ln_g[1, 1], ln_b[1, 1],
               exp_w_gu[1], exp_b_gu[1], exp_w_down[1], exp_b_down[1], dims, alpha)
    return out.reshape(B, S, D)
```

```python
import functools

import jax
import jax.numpy as jnp
from jax import lax
from jax.experimental import pallas as pl
from jax.experimental.pallas import tpu as pltpu

F32 = jnp.float32
BF16 = jnp.bfloat16
I32 = jnp.int32
HIGHEST = lax.Precision.HIGHEST

GRID_W = 64
ATTN_HEAD_DIM = 64
ATTN_KV_HEADS = 4
WINDOW = 128
ATTN_BLOCK = 128
ROPE_THETA = 10000.0
ML_HEADS = 8
ML_CONV_W = 5
GATE_CAP = 15.0
TOP_K = 4
SWIGLU_ALPHA = 1.702
SWIGLU_LIMIT = 7.0
LN_EPS = 1e-5

LANES = 128
SUBLANES = 8
VMEM_LIMIT = 56 * 1024 * 1024
EXPERT_TILE = 512
ROUTE_TILE = 256
ML_CHUNK = 128
STAGE_ROWS = ROUTE_TILE * TOP_K + 2 * LANES
TABLE_W = 2 * LANES
ZERO_ROWS = 64


def _params(n_axes, vmem=VMEM_LIMIT):
    return pltpu.CompilerParams(dimension_semantics=("arbitrary",) * n_axes, vmem_limit_bytes=vmem)


def _layer_norm(r, g, b):
    mu = jnp.mean(r, axis=-1, keepdims=True)
    rc = r - mu
    var = jnp.mean(rc * rc, axis=-1, keepdims=True)
    return rc * lax.rsqrt(var + LN_EPS) * g + b


def _sigmoid(x):
    return 1.0 / (1.0 + jnp.exp(-x))


def _mod_row(i, n_lat_tiles, tiles_per_batch, ctx_row):
    return jnp.where(i < n_lat_tiles, i // tiles_per_batch, ctx_row)


def _adaln_kernel(c_ref, w_ref, b_ref, o_ref):
    c = c_ref[...]
    s = c * _sigmoid(c)
    o_ref[0] = jnp.dot(s, w_ref[0], precision=HIGHEST, preferred_element_type=F32) + b_ref[0]


def _adaln(cvec, ada_w, ada_b):
    depth, d, n = ada_w.shape
    tn = 1536
    return pl.pallas_call(
        _adaln_kernel,
        grid=(depth, n // tn),
        in_specs=[pl.BlockSpec((SUBLANES, d), lambda l, j: (0, 0)),
                  pl.BlockSpec((1, d, tn), lambda l, j: (l, 0, j)),
                  pl.BlockSpec((1, 1, tn), lambda l, j: (l, 0, j))],
        out_specs=pl.BlockSpec((1, SUBLANES, tn), lambda l, j: (l, 0, j)),
        out_shape=jax.ShapeDtypeStruct((depth, SUBLANES, n), F32),
        compiler_params=_params(2),
        name="adaln",
    )(cvec, ada_w, ada_b.reshape(depth, 1, n))


def _qkv_kernel(x_ref, mod_ref, w_ref, b_ref, cos_ref, sin_ref, q_ref, k_ref, v_ref, *,
                n_lat_tiles, tiles_per_batch, ctx_row, d, qd, kvd):
    i = pl.program_id(0)
    row = _mod_row(i, n_lat_tiles, tiles_per_batch, ctx_row)
    shift = mod_ref[pl.ds(row, 1), 0:d]
    scale = mod_ref[pl.ds(row, 1), d:2 * d]
    h = x_ref[...] * (1.0 + scale) + shift
    z = jnp.dot(h.astype(BF16), w_ref[...], preferred_element_type=F32) + b_ref[...]
    nrot = qd + kvd
    qk = z[:, :nrot]
    reps = nrot // LANES
    cos = jnp.concatenate([cos_ref[...]] * reps, axis=1)
    sin = jnp.concatenate([sin_ref[...]] * reps, axis=1)
    lane = lax.broadcasted_iota(I32, qk.shape, 1)
    low_half = (lane & 16) == 0
    partner = jnp.where(low_half, pltpu.roll(qk, nrot - 16, 1), pltpu.roll(qk, 16, 1))
    qk = qk * cos + partner * sin
    q_ref[...] = (qk[:, :qd] * (ATTN_HEAD_DIM ** -0.5)).astype(BF16)
    k_ref[...] = qk[:, qd:].astype(BF16)
    v_ref[...] = z[:, nrot:].astype(BF16)


def _rope_tables(s_len, tm):
    half = ATTN_HEAD_DIM // 4
    freqs = ROPE_THETA ** (-jnp.arange(half, dtype=F32) / half)
    t = jnp.arange(s_len)
    rows = (t // GRID_W).astype(F32)[:, None] * freqs[None, :]
    cols = (t % GRID_W).astype(F32)[:, None] * freqs[None, :]
    ang = jnp.concatenate([rows, rows, cols, cols], axis=1)
    sign = jnp.tile(jnp.concatenate([-jnp.ones((half,), F32), jnp.ones((half,), F32)]), 2)
    cos = jnp.cos(ang)
    sin = jnp.sin(ang) * sign[None, :]
    reps = LANES // ATTN_HEAD_DIM
    cos = jnp.concatenate([jnp.tile(cos, (1, reps)), jnp.ones((tm, LANES), F32)], axis=0)
    sin = jnp.concatenate([jnp.tile(sin, (1, reps)), jnp.zeros((tm, LANES), F32)], axis=0)
    return cos, sin


def _attn_qkv(x_all, mod, w_qkv, b_qkv, dims):
    B, S, C, D = dims
    t_all = x_all.shape[0]
    tm = 512
    n_lat_tiles = B * S // tm
    tiles_per_batch = S // tm
    ncols = w_qkv.shape[1]
    kvd = ATTN_KV_HEADS * ATTN_HEAD_DIM
    qd = ncols - 2 * kvd
    cos, sin = _rope_tables(S, tm)

    def tab_idx(i):
        return (jnp.where(i < n_lat_tiles, i % tiles_per_batch, tiles_per_batch), 0)

    kern = functools.partial(_qkv_kernel, n_lat_tiles=n_lat_tiles, tiles_per_batch=tiles_per_batch,
                             ctx_row=B, d=D, qd=qd, kvd=kvd)
    return pl.pallas_call(
        kern,
        grid=(t_all // tm,),
        in_specs=[pl.BlockSpec((tm, D), lambda i: (i, 0)),
                  pl.BlockSpec(mod.shape, lambda i: (0, 0)),
                  pl.BlockSpec((D, ncols), lambda i: (0, 0)),
                  pl.BlockSpec((1, ncols), lambda i: (0, 0)),
                  pl.BlockSpec((tm, LANES), tab_idx),
                  pl.BlockSpec((tm, LANES), tab_idx)],
        out_specs=[pl.BlockSpec((tm, qd), lambda i: (i, 0)),
                   pl.BlockSpec((tm, kvd), lambda i: (i, 0)),
                   pl.BlockSpec((tm, kvd), lambda i: (i, 0))],
        out_shape=[jax.ShapeDtypeStruct((t_all, qd), BF16),
                   jax.ShapeDtypeStruct((t_all, kvd), BF16),
                   jax.ShapeDtypeStruct((t_all, kvd), BF16)],
        compiler_params=_params(1),
        name="attn_qkv",
    )(x_all, mod, w_qkv.astype(BF16), b_qkv.reshape(1, ncols), cos, sin)


def _attn_kernel(sink_ref, q_ref, kp_ref, ko_ref, kn_ref, kc_ref, vp_ref, vo_ref, vn_ref, vc_ref, o_ref, *,
                 n_lat_steps, nb, s_len, c_len):
    j = pl.program_id(0)
    is_lat = j < n_lat_steps
    n = j % nb
    blk = ATTN_BLOCK
    nloc = 3 * blk
    nk = nloc + c_len
    qi = lax.broadcasted_iota(I32, (blk, nk), 0)
    kj = lax.broadcasted_iota(I32, (blk, nk), 1)
    qpos = n * blk + qi
    kpos = n * blk - WINDOW + kj
    local_ok = (jnp.abs(kpos - qpos) <= WINDOW) & (kpos >= 0) & (kpos < s_len) & is_lat
    valid = local_ok | (kj >= nloc)
    hd = ATTN_HEAD_DIM
    group = q_ref.shape[1] // (ATTN_KV_HEADS * hd)
    for kh in range(ATTN_KV_HEADS):
        cs = slice(kh * hd, (kh + 1) * hd)
        kcat = jnp.concatenate([kp_ref[:, cs], ko_ref[:, cs], kn_ref[:, cs], kc_ref[:, cs]], axis=0)
        vcat = jnp.concatenate([vp_ref[:, cs], vo_ref[:, cs], vn_ref[:, cs], vc_ref[:, cs]], axis=0)
        outs = []
        for g in range(group):
            h = kh * group + g
            qh = q_ref[:, h * hd:(h + 1) * hd]
            s = lax.dot_general(qh, kcat, (((1,), (1,)), ((), ())), preferred_element_type=F32)
            s = jnp.where(valid, s, -jnp.inf)
            sk = sink_ref[h]
            m = jnp.maximum(jnp.max(s, axis=1, keepdims=True), sk)
            p = jnp.exp(s - m)
            l = jnp.sum(p, axis=1, keepdims=True) + jnp.exp(sk - m)
            o = jnp.dot(p.astype(BF16), vcat, preferred_element_type=F32) / l
            outs.append(o)
        w = group * hd
        o_ref[:, kh * w:(kh + 1) * w] = jnp.concatenate(outs, axis=1).astype(BF16)


def _attention(q_all, k_all, v_all, sink, dims):
    B, S, C, D = dims
    blk = ATTN_BLOCK
    nb = S // blk
    n_lat_steps = B * nb
    ctx_steps_per_batch = C // blk
    n_steps = n_lat_steps + B * ctx_steps_per_batch
    qd = q_all.shape[1]
    kvd = k_all.shape[1]

    def local_idx(off):
        def idx(j):
            b = j // nb
            nn = jnp.clip(j % nb + off, 0, nb - 1)
            return (jnp.where(j < n_lat_steps, b * nb + nn, j), 0)
        return idx

    def ctx_idx(j):
        b = jnp.where(j < n_lat_steps, j // nb, (j - n_lat_steps) // ctx_steps_per_batch)
        return (B * S // C + b, 0)

    loc = lambda off: pl.BlockSpec((blk, kvd), local_idx(off))
    ctxs = pl.BlockSpec((C, kvd), ctx_idx)
    kern = functools.partial(_attn_kernel, n_lat_steps=n_lat_steps, nb=nb, s_len=S, c_len=C)
    return pl.pallas_call(
        kern,
        grid=(n_steps,),
        in_specs=[pl.BlockSpec(memory_space=pltpu.SMEM),
                  pl.BlockSpec((blk, qd), lambda j: (j, 0)),
                  loc(-1), loc(0), loc(1), ctxs,
                  loc(-1), loc(0), loc(1), ctxs],
        out_specs=pl.BlockSpec((blk, qd), lambda j: (j, 0)),
        out_shape=jax.ShapeDtypeStruct((q_all.shape[0], qd), BF16),
        compiler_params=_params(1),
        name="attn_core",
    )(sink, q_all, k_all, k_all, k_all, k_all, v_all, v_all, v_all, v_all)


def _route(hx, rw_ref, rb_ref, topi_ref, gate_ref):
    logits = jnp.dot(hx, rw_ref[...], precision=HIGHEST, preferred_element_type=F32) + rb_ref[...]
    lane = lax.broadcasted_iota(I32, logits.shape, 1)
    lanef = lane.astype(F32)
    vals, idxs = [], []
    l = logits
    for _ in range(TOP_K):
        m = jnp.max(l, axis=1, keepdims=True)
        idx = jnp.min(jnp.where(l == m, lanef, float(LANES)), axis=1, keepdims=True)
        vals.append(m)
        idxs.append(idx)
        l = jnp.where(lanef == idx, -jnp.inf, l)
    es = [jnp.exp(v - vals[0]) for v in vals]
    denom = es[0]
    for e in es[1:]:
        denom = denom + e
    topi = jnp.zeros(logits.shape, F32)
    gates = jnp.zeros(logits.shape, F32)
    for k in range(TOP_K):
        topi = jnp.where(lane == k, idxs[k], topi)
        gates = jnp.where(lane == k, es[k] / denom, gates)
    topi_ref[...] = topi.astype(I32)
    gate_ref[...] = gates


def _post_common(a, x_ref, mod_ref, w_ref, b_ref, lng_ref, lnb_ref, rw_ref, rb_ref,
                 x1_ref, hx_ref, topi_ref, gate_ref, row, d, alpha):
    y = jnp.dot(a, w_ref[...], preferred_element_type=F32) + b_ref[...]
    gate_mix = mod_ref[pl.ds(row, 1), 2 * d:3 * d]
    shift = mod_ref[pl.ds(row, 1), 3 * d:4 * d]
    scale = mod_ref[pl.ds(row, 1), 4 * d:5 * d]
    x1 = _layer_norm(alpha * x_ref[...] + gate_mix * y, lng_ref[...], lnb_ref[...])
    hx = x1 * (1.0 + scale) + shift
    x1_ref[...] = x1
    hx_ref[...] = hx.astype(BF16)
    _route(hx, rw_ref, rb_ref, topi_ref, gate_ref)


def _post_attn_kernel(o_ref, x_ref, mod_ref, w_ref, b_ref, lng_ref, lnb_ref, rw_ref, rb_ref,
                      x1_ref, hx_ref, topi_ref, gate_ref, *, n_lat_tiles, tiles_per_batch, ctx_row, d, alpha):
    row = _mod_row(pl.program_id(0), n_lat_tiles, tiles_per_batch, ctx_row)
    _post_common(o_ref[...], x_ref, mod_ref, w_ref, b_ref, lng_ref, lnb_ref, rw_ref, rb_ref,
                 x1_ref, hx_ref, topi_ref, gate_ref, row, d, alpha)


def _post_mlstm_kernel(hf_ref, hb_ref, og_ref, ng_ref, x_ref, mod_ref, w_ref, b_ref, lng_ref, lnb_ref,
                       rw_ref, rb_ref, x1_ref, hx_ref, topi_ref, gate_ref, *,
                       n_lat_tiles, tiles_per_batch, ctx_row, d, alpha):
    row = _mod_row(pl.program_id(0), n_lat_tiles, tiles_per_batch, ctx_row)
    hsum = hf_ref[...] + hb_ref[...]
    vdim = hsum.shape[1] // ML_HEADS
    parts = []
    for h in range(ML_HEADS):
        seg = hsum[:, h * vdim:(h + 1) * vdim]
        mu = jnp.mean(seg, axis=1, keepdims=True)
        sc = seg - mu
        var = jnp.mean(sc * sc, axis=1, keepdims=True)
        parts.append(sc * lax.rsqrt(var + LN_EPS))
    y = jnp.concatenate(parts, axis=1) * ng_ref[...]
    a = (og_ref[...].astype(F32) * y).astype(BF16)
    _post_common(a, x_ref, mod_ref, w_ref, b_ref, lng_ref, lnb_ref, rw_ref, rb_ref,
                 x1_ref, hx_ref, topi_ref, gate_ref, row, d, alpha)


def _post_mixer(mixer_inputs, norm_g, x_all, mod, w_o, b_o, ln_g, ln_b, router_w, router_b, dims, n_rows, alpha):
    B, S, C, D = dims
    tm = 256
    n_lat_tiles = B * S // tm
    tiles_per_batch = S // tm
    n_exp = router_w.shape[1]
    rw = jnp.zeros((D, LANES), F32).at[:, :n_exp].set(router_w)
    rb = jnp.full((1, LANES), -1e30, F32).at[0, :n_exp].set(router_b)
    row_spec = lambda w: pl.BlockSpec((tm, w), lambda i: (i, 0))
    full = lambda a: pl.BlockSpec(a.shape, lambda i: (0,) * a.ndim)
    common = dict(n_lat_tiles=n_lat_tiles, tiles_per_batch=tiles_per_batch, ctx_row=B, d=D, alpha=alpha)
    w_bf = w_o.astype(BF16)
    b2 = b_o.reshape(1, D)
    tail = [x_all, mod, w_bf, b2, ln_g.reshape(1, D), ln_b.reshape(1, D), rw, rb]
    tail_specs = [row_spec(D), full(mod), full(w_bf), full(b2), pl.BlockSpec((1, D), lambda i: (0, 0)),
                  pl.BlockSpec((1, D), lambda i: (0, 0)), full(rw), full(rb)]
    if norm_g is None:
        kern = functools.partial(_post_attn_kernel, **common)
        args = list(mixer_inputs) + tail
        specs = [row_spec(mixer_inputs[0].shape[1])] + tail_specs
        name = "post_attn"
    else:
        kern = functools.partial(_post_mlstm_kernel, **common)
        ng = norm_g.reshape(1, -1)
        args = list(mixer_inputs) + [ng] + tail
        specs = [row_spec(a.shape[1]) for a in mixer_inputs] + [full(ng)] + tail_specs
        name = "post_mlstm"
    return pl.pallas_call(
        kern,
        grid=(n_rows // tm,),
        in_specs=specs,
        out_specs=[row_spec(D), row_spec(D), row_spec(LANES), row_spec(LANES)],
        out_shape=[jax.ShapeDtypeStruct((n_rows, D), F32), jax.ShapeDtypeStruct((n_rows, D), BF16),
                   jax.ShapeDtypeStruct((n_rows, LANES), I32), jax.ShapeDtypeStruct((n_rows, LANES), F32)],
        compiler_params=_params(1),
        name=name,
    )(*args)


def _exclusive_lane_cumsum(row):
    r = lax.broadcasted_iota(I32, (LANES, LANES), 0)
    c = lax.broadcasted_iota(I32, (LANES, LANES), 1)
    before = jnp.where(r < c, 1.0, 0.0)
    return jnp.dot(jnp.broadcast_to(row, (SUBLANES, LANES)), before,
                   precision=HIGHEST, preferred_element_type=F32)[0:1]


def _lane_row_to_column(row):
    r = lax.broadcasted_iota(I32, (LANES, LANES), 0)
    c = lax.broadcasted_iota(I32, (LANES, LANES), 1)
    return jnp.sum(jnp.where(r == c, jnp.broadcast_to(row, (LANES, LANES)), 0.0), axis=1, keepdims=True)


def _slots_kernel(topi_ref, col_ref, colt_ref, tab_ref, meta_ref, tot_sc, base_sc, carry_sc, *, tile):
    phase = pl.program_id(0)
    i = pl.program_id(1)
    tm = topi_ref.shape[0]
    topi = topi_ref[...]
    lane = lax.broadcasted_iota(I32, topi.shape, 1)
    sel = [lane == topi[:, k:k + 1] for k in range(TOP_K)]
    maskf = jnp.where(sel[0], 1.0, 0.0)
    for s in sel[1:]:
        maskf = maskf + jnp.where(s, 1.0, 0.0)
    n8 = jnp.ceil(jnp.sum(maskf, axis=0, keepdims=True) / SUBLANES) * SUBLANES

    @pl.when((phase == 0) & (i == 0))
    def _():
        tot_sc[...] = jnp.zeros_like(tot_sc)

    @pl.when(phase == 0)
    def _():
        tot_sc[...] = tot_sc[...] + n8

    @pl.when((phase == 1) & (i == 0))
    def _():
        tot = tot_sc[...]
        padded = jnp.ceil(tot / tile) * tile
        base = _exclusive_lane_cumsum(padded)
        base_sc[...] = base
        carry_sc[...] = jnp.zeros_like(carry_sc)
        rowi = lax.broadcasted_iota(I32, meta_ref.shape, 0)
        meta_ref[...] = jnp.where(rowi == 0, tot, jnp.where(rowi == 1, base, padded)).astype(I32)

    @pl.when(phase == 1)
    def _():
        start = base_sc[...] + carry_sc[...]
        off = _exclusive_lane_cumsum(n8)
        r = lax.broadcasted_iota(I32, (tm, tm), 0)
        c = lax.broadcasted_iota(I32, (tm, tm), 1)
        earlier = jnp.where(c < r, 1.0, 0.0).astype(BF16)
        rank = jnp.dot(earlier, maskf.astype(BF16), preferred_element_type=F32)
        stage_row = rank + off
        out = jnp.zeros(topi.shape, F32)
        for k in range(TOP_K):
            pk = jnp.sum(jnp.where(sel[k], stage_row, 0.0), axis=1, keepdims=True)
            out = jnp.where(lane == k, pk, out)
        col_ref[...] = out.astype(I32)
        colt_ref[0] = out.T[0:SUBLANES].astype(I32)

        off_c = _lane_row_to_column(off)
        n8_c = _lane_row_to_column(n8)
        start_c = _lane_row_to_column(start)
        u8 = (lax.broadcasted_iota(I32, (LANES, TABLE_W), 1) * SUBLANES).astype(F32)
        inside = (u8 >= off_c) & (u8 < off_c + n8_c)
        src = jnp.sum(jnp.where(inside, start_c + u8 - off_c, 0.0), axis=0, keepdims=True)
        n_units = jnp.sum(n8, axis=1, keepdims=True) / SUBLANES
        lane_t = lax.broadcasted_iota(I32, (1, TABLE_W), 1)
        tab_ref[0] = jnp.where(lane_t == TABLE_W - 1, n_units, src).astype(I32)
        carry_sc[...] = carry_sc[...] + n8


def _slots(topi, tile):
    t = topi.shape[0]
    tm = ROUTE_TILE
    nt = t // tm
    return pl.pallas_call(
        functools.partial(_slots_kernel, tile=float(tile)),
        grid=(2, nt),
        in_specs=[pl.BlockSpec((tm, LANES), lambda p, i: (i, 0))],
        out_specs=[pl.BlockSpec((tm, LANES), lambda p, i: (i * p, 0)),
                   pl.BlockSpec((1, SUBLANES, tm), lambda p, i: (i * p, 0, 0)),
                   pl.BlockSpec((1, 1, TABLE_W), lambda p, i: (i * p, 0, 0)),
                   pl.BlockSpec((SUBLANES, LANES), lambda p, i: (0, 0))],
        out_shape=[jax.ShapeDtypeStruct((t, LANES), I32),
                   jax.ShapeDtypeStruct((nt, SUBLANES, tm), I32),
                   jax.ShapeDtypeStruct((nt, 1, TABLE_W), I32),
                   jax.ShapeDtypeStruct((SUBLANES, LANES), I32)],
        scratch_shapes=[pltpu.VMEM((1, LANES), F32), pltpu.VMEM((1, LANES), F32), pltpu.VMEM((1, LANES), F32)],
        compiler_params=_params(2),
        name="moe_slots",
    )(topi)


def _dispatch_kernel(tab_ref, pad_start_ref, pad_units_ref, tail_ref, colt_ref, hx_ref, xs_ref, stage, zeros, sem, zsem):
    i = pl.program_id(0)
    nt = pl.num_programs(0)
    slot = i % 2
    kb = stage.shape[1]
    tm = hx_ref.shape[0]

    def unit_copy(tile, u, buf_slot):
        dst = pl.multiple_of(tab_ref[tile * TABLE_W + u], SUBLANES)
        return pltpu.make_async_copy(stage.at[buf_slot, pl.ds(pl.multiple_of(u * SUBLANES, SUBLANES), SUBLANES)],
                                     xs_ref.at[pl.ds(dst, SUBLANES)], sem.at[buf_slot])

    def drain(tile, buf_slot):
        def body(u, carry):
            unit_copy(tile, u, buf_slot).wait()
            return carry
        lax.fori_loop(0, tab_ref[tile * TABLE_W + TABLE_W - 1], body, 0)

    @pl.when(i >= 2)
    def _():
        drain(i - 2, slot)

    colt = colt_ref[0]
    c = lax.broadcasted_iota(I32, (kb, tm), 0)
    onehot = jnp.zeros((kb, tm), F32)
    for k in range(TOP_K):
        onehot = jnp.where(c == colt[k:k + 1, :], 1.0, onehot)
    stage[slot] = jnp.dot(onehot.astype(BF16), hx_ref[...], preferred_element_type=F32)

    def issue(u, carry):
        unit_copy(i, u, slot).start()
        return carry
    lax.fori_loop(0, tab_ref[i * TABLE_W + TABLE_W - 1], issue, 0)

    @pl.when(i == nt - 1)
    def _():
        zeros[...] = jnp.zeros_like(zeros)
        n_exp = pad_start_ref.shape[0]

        def zero_copy(e, u):
            dst = pl.multiple_of(pad_start_ref[e] + u * SUBLANES, SUBLANES)
            return pltpu.make_async_copy(zeros.at[pl.ds(0, SUBLANES)], xs_ref.at[pl.ds(dst, SUBLANES)], zsem)

        def per_expert(fn):
            def outer(e, carry):
                def inner(u, carry2):
                    fn(e, u)
                    return carry2
                lax.fori_loop(0, pad_units_ref[e], inner, 0)
                return carry
            lax.fori_loop(0, n_exp, outer, 0)

        per_expert(lambda e, u: zero_copy(e, u).start())
        per_expert(lambda e, u: zero_copy(e, u).wait())

        zrows = zeros.shape[0]

        def tail_copy(u):
            dst = pl.multiple_of(tail_ref[0] + u * zrows, zrows)
            return pltpu.make_async_copy(zeros, xs_ref.at[pl.ds(dst, zrows)], zsem)

        def tail_loop(fn):
            def body(u, carry):
                fn(u)
                return carry
            lax.fori_loop(0, tail_ref[1], body, 0)

        tail_loop(lambda u: tail_copy(u).start())
        tail_loop(lambda u: tail_copy(u).wait())

        @pl.when(nt >= 2)
        def _():
            drain(i - 1, 1 - slot)
        drain(i, slot)


def _dispatch(tab, pad_start, pad_units, tail, colt, hx, n_slots):
    t, d = hx.shape
    tm = ROUTE_TILE
    return pl.pallas_call(
        _dispatch_kernel,
        grid_spec=pltpu.PrefetchScalarGridSpec(
            num_scalar_prefetch=4,
            grid=(t // tm,),
            in_specs=[pl.BlockSpec((1, SUBLANES, tm), lambda i, *_: (i, 0, 0)),
                      pl.BlockSpec((tm, d), lambda i, *_: (i, 0))],
            out_specs=pl.BlockSpec(memory_space=pl.ANY),
            scratch_shapes=[pltpu.VMEM((2, STAGE_ROWS, d), F32), pltpu.VMEM((ZERO_ROWS, d), F32),
                            pltpu.SemaphoreType.DMA((2,)), pltpu.SemaphoreType.DMA(())]),
        out_shape=jax.ShapeDtypeStruct((n_slots, d), F32),
        compiler_params=_params(1),
        name="moe_dispatch",
    )(tab, pad_start, pad_units, tail, colt, hx)


def _expert_kernel(te_ref, nu_ref, xs_ref, wgu_ref, bgu_ref, wd_ref, bd_ref, ys_ref, wgu_sc, wd_sc):
    j = pl.program_id(0)
    active = j < nu_ref[0]
    changed = (j == 0) | (te_ref[j] != te_ref[jnp.maximum(j - 1, 0)])
    ff = wd_ref.shape[0]

    @pl.when(active & changed)
    def _():
        wgu_sc[...] = wgu_ref[...].astype(BF16)
        wd_sc[...] = wd_ref[...].astype(BF16)

    @pl.when(active)
    def _():
        x = xs_ref[...].astype(BF16)
        gu = jnp.dot(x, wgu_sc[...], preferred_element_type=F32) + bgu_ref[...]
        gl = jnp.minimum(gu[:, :ff], SWIGLU_LIMIT)
        lin = jnp.clip(gu[:, ff:], -SWIGLU_LIMIT, SWIGLU_LIMIT)
        act = gl * _sigmoid(SWIGLU_ALPHA * gl) * (lin + 1.0)
        ys_ref[...] = jnp.dot(act.astype(BF16), wd_sc[...], preferred_element_type=F32) + bd_ref[...]

    @pl.when(jnp.logical_not(active))
    def _():
        ys_ref[...] = jnp.zeros_like(ys_ref)


def _experts(tile_expert, n_used, xs, layer, w_gu, b_gu, w_down, b_down):
    n_slots, d = xs.shape
    depth, n_exp, _, ff2 = w_gu.shape
    ff = w_down.shape[2]
    te = EXPERT_TILE
    row_idx = lambda j, te_r, nu: (jnp.minimum(j, nu[0] - 1), 0)
    w_idx = lambda j, te_r, nu: (layer, te_r[j], 0, 0)
    return pl.pallas_call(
        _expert_kernel,
        grid_spec=pltpu.PrefetchScalarGridSpec(
            num_scalar_prefetch=2,
            grid=(n_slots // te,),
            in_specs=[pl.BlockSpec((te, d), row_idx),
                      pl.BlockSpec((None, None, d, ff2), w_idx),
                      pl.BlockSpec((None, None, 1, ff2), w_idx),
                      pl.BlockSpec((None, None, ff, d), w_idx),
                      pl.BlockSpec((None, None, 1, d), w_idx)],
            out_specs=pl.BlockSpec((te, d), lambda j, te_r, nu: (j, 0)),
            scratch_shapes=[pltpu.VMEM((d, ff2), BF16), pltpu.VMEM((ff, d), BF16)]),
        out_shape=jax.ShapeDtypeStruct((n_slots, d), F32),
        compiler_params=_params(1),
        name="moe_experts",
    )(tile_expert, n_used, xs, w_gu, b_gu.reshape(depth, n_exp, 1, ff2), w_down, b_down.reshape(depth, n_exp, 1, d))


def _combine_kernel(tab_ref, col_ref, gate_ref, ys_ref, x_ref, mod_ref, lng_ref, lnb_ref, out_ref, stage, sem, *,
                    n_lat_tiles, tiles_per_batch, ctx_row, d, alpha):
    i = pl.program_id(0)
    nt = pl.num_programs(0)
    slot = i % 2
    kb = stage.shape[1]
    tm = x_ref.shape[0]

    def unit_copy(tile, u, buf_slot):
        src = pl.multiple_of(tab_ref[tile * TABLE_W + u], SUBLANES)
        return pltpu.make_async_copy(ys_ref.at[pl.ds(src, SUBLANES)],
                                     stage.at[buf_slot, pl.ds(pl.multiple_of(u * SUBLANES, SUBLANES), SUBLANES)],
                                     sem.at[buf_slot])

    def for_units(tile, fn):
        def body(u, carry):
            fn(u)
            return carry
        lax.fori_loop(0, tab_ref[tile * TABLE_W + TABLE_W - 1], body, 0)

    @pl.when(i == 0)
    def _():
        stage[...] = jnp.zeros_like(stage)
        for_units(0, lambda u: unit_copy(0, u, 0).start())

    @pl.when(i + 1 < nt)
    def _():
        for_units(i + 1, lambda u: unit_copy(i + 1, u, 1 - slot).start())

    for_units(i, lambda u: unit_copy(i, u, slot).wait())

    col = col_ref[...]
    gates = gate_ref[...]
    c = lax.broadcasted_iota(I32, (tm, kb), 1)
    weights = jnp.zeros((tm, kb), F32)
    for k in range(TOP_K):
        weights = jnp.where(c == col[:, k:k + 1], gates[:, k:k + 1], weights)
    y = jnp.dot(weights.astype(BF16), stage[slot].astype(BF16), preferred_element_type=F32)
    row = _mod_row(i, n_lat_tiles, tiles_per_batch, ctx_row)
    gate_mlp = mod_ref[pl.ds(row, 1), 5 * d:6 * d]
    out_ref[...] = _layer_norm(alpha * x_ref[...] + gate_mlp * y, lng_ref[...], lnb_ref[...])


def _combine(tab, col, ys, gates, x1, mod, ln_g, ln_b, dims, alpha):
    B, S, C, D = dims
    t = x1.shape[0]
    tm = ROUTE_TILE
    kern = functools.partial(_combine_kernel, n_lat_tiles=B * S // tm, tiles_per_batch=S // tm, ctx_row=B,
                             d=D, alpha=alpha)
    return pl.pallas_call(
        kern,
        grid_spec=pltpu.PrefetchScalarGridSpec(
            num_scalar_prefetch=1,
            grid=(t // tm,),
            in_specs=[pl.BlockSpec((tm, LANES), lambda i, tab_r: (i, 0)),
                      pl.BlockSpec((tm, LANES), lambda i, tab_r: (i, 0)),
                      pl.BlockSpec(memory_space=pl.ANY),
                      pl.BlockSpec((tm, D), lambda i, tab_r: (i, 0)),
                      pl.BlockSpec(mod.shape, lambda i, tab_r: (0, 0)),
                      pl.BlockSpec((1, D), lambda i, tab_r: (0, 0)),
                      pl.BlockSpec((1, D), lambda i, tab_r: (0, 0))],
            out_specs=pl.BlockSpec((tm, D), lambda i, tab_r: (i, 0)),
            scratch_shapes=[pltpu.VMEM((2, STAGE_ROWS, D), F32), pltpu.SemaphoreType.DMA((2,))]),
        out_shape=jax.ShapeDtypeStruct((t, D), F32),
        compiler_params=_params(1),
        name="moe_combine",
    )(tab, col, gates, ys, x1, mod, ln_g.reshape(1, D), ln_b.reshape(1, D))


def _moe(hx, topi, gates, x1, mod, ln_g, ln_b, layer, w_gu, b_gu, w_down, b_down, dims, alpha):
    t = hx.shape[0]
    n_exp = w_gu.shape[1]
    te = EXPERT_TILE
    n_route_tiles = t // ROUTE_TILE
    max_rows = t * TOP_K + (SUBLANES - 1) * n_exp * n_route_tiles
    n_tiles = -(-max_rows // te) + n_exp
    col, colt, tab, meta = _slots(topi, te)
    tab = tab.reshape(-1)
    tot, base, padded = meta[0, :n_exp], meta[1, :n_exp], meta[2, :n_exp]
    ends = jnp.cumsum(padded // te)
    n_used = ends[-1:].astype(I32)
    tile_ids = jnp.minimum(jnp.arange(n_tiles, dtype=I32), n_used[0] - 1)
    tile_expert = jnp.sum((tile_ids[:, None] >= ends[None, :]).astype(I32), axis=1)
    tile_expert = jnp.minimum(tile_expert, n_exp - 1).astype(I32)
    pad_start = (base + tot).astype(I32)
    pad_units = ((padded - tot) // SUBLANES).astype(I32)
    used_rows = n_used[0] * te
    tail = jnp.stack([used_rows, (n_tiles * te - used_rows) // ZERO_ROWS]).astype(I32)
    xs = _dispatch(tab, pad_start, pad_units, tail, colt, hx, n_tiles * te)
    ys = _experts(tile_expert, n_used, xs, layer, w_gu, b_gu, w_down, b_down)
    return _combine(tab, col, ys, gates, x1, mod, ln_g, ln_b, dims, alpha)


def _ml_in_kernel(xp_ref, x_ref, xn_ref, mod_ref, wqk_ref, bqk_ref, wvo_ref, bvo_ref, wg_ref, bg_ref,
                  cw_ref, cb_ref, q_ref, k_ref, v_ref, og_ref, g_ref, *,
                  n_lat_tiles, tiles_per_batch, ctx_tiles_per_seq, ctx_row, d):
    i = pl.program_id(0)
    tm = x_ref.shape[0]
    halo = SUBLANES
    row = _mod_row(i, n_lat_tiles, tiles_per_batch, ctx_row)
    shift = mod_ref[pl.ds(row, 1), 0:d]
    scale = mod_ref[pl.ds(row, 1), d:2 * d]
    is_lat = i < n_lat_tiles
    seq_tile = jnp.where(is_lat, i % tiles_per_batch, (i - n_lat_tiles) % ctx_tiles_per_seq)
    seq_tiles = jnp.where(is_lat, tiles_per_batch, ctx_tiles_per_seq)
    first = seq_tile == 0
    last = seq_tile == seq_tiles - 1

    h = x_ref[...] * (1.0 + scale) + shift
    h_ext = jnp.concatenate([xp_ref[...] * (1.0 + scale) + shift, h, xn_ref[...] * (1.0 + scale) + shift], axis=0)
    z = jnp.dot(h_ext.astype(BF16), wqk_ref[...], preferred_element_type=F32) + bqk_ref[...]
    r = lax.broadcasted_iota(I32, z.shape, 0)
    outside = ((r < halo) & first) | ((r >= halo + tm) & last)
    z = jnp.where(outside, 0.0, z)
    n_ext = tm + 2 * halo
    cw = cw_ref[...]
    acc = None
    for j in range(ML_CONV_W):
        sh = (ML_CONV_W // 2 - j) % n_ext
        zj = z if sh == 0 else pltpu.roll(z, sh, 0)
        term = zj[halo:halo + tm] * cw[j:j + 1]
        acc = term if acc is None else acc + term
    qk = acc + cb_ref[...]
    qk = qk * _sigmoid(qk)
    nqk = qk.shape[1] // 2
    qk_dim = nqk // ML_HEADS
    q_ref[...] = (qk[:, :nqk] * (qk_dim ** -0.5)).astype(BF16)
    k_ref[...] = qk[:, nqk:].astype(BF16)

    vo = jnp.dot(h.astype(BF16), wvo_ref[...], preferred_element_type=F32) + bvo_ref[...]
    nv = vo.shape[1] // 2
    v_ref[...] = vo[:, :nv].astype(BF16)
    og_ref[...] = _sigmoid(vo[:, nv:]).astype(BF16)

    zg = jnp.dot(h, wg_ref[...], precision=HIGHEST, preferred_element_type=F32) + bg_ref[...]
    g = GATE_CAP * jnp.tanh(zg / GATE_CAP)
    log_sig = jnp.minimum(g, 0.0) - jnp.log(1.0 + jnp.exp(-jnp.abs(g)))
    lane = lax.broadcasted_iota(I32, g.shape, 1)
    is_forget = ((lane // ML_HEADS) % 2) == 1
    g_ref[...] = jnp.where(is_forget, log_sig, g)


def _ml_in(x_all, mod, w_in, b_in, conv_w, conv_b, dims):
    B, S, C, D = dims
    t_all = x_all.shape[0]
    tm = 256
    n_lat_tiles = B * S // tm
    nqk2 = conv_w.shape[1]
    nv = (w_in.shape[1] - nqk2 - 4 * ML_HEADS) // 2
    ng = 4 * ML_HEADS
    w_qk = w_in[:, :nqk2].astype(BF16)
    w_vo = w_in[:, nqk2:nqk2 + 2 * nv].astype(BF16)
    w_g = jnp.zeros((D, LANES), F32).at[:, :ng].set(w_in[:, nqk2 + 2 * nv:])
    b_qk = b_in[:nqk2].reshape(1, -1)
    b_vo = b_in[nqk2:nqk2 + 2 * nv].reshape(1, -1)
    b_g = jnp.zeros((1, LANES), F32).at[0, :ng].set(b_in[nqk2 + 2 * nv:])
    cw = jnp.zeros((SUBLANES, nqk2), F32).at[:ML_CONV_W].set(conv_w)
    cb = conv_b.reshape(1, -1)
    hb = tm // SUBLANES
    n_hblk = t_all // SUBLANES
    full = lambda a: pl.BlockSpec(a.shape, lambda i: (0,) * a.ndim)
    row_spec = lambda w: pl.BlockSpec((tm, w), lambda i: (i, 0))
    kern = functools.partial(_ml_in_kernel, n_lat_tiles=n_lat_tiles, tiles_per_batch=S // tm,
                             ctx_tiles_per_seq=C // tm, ctx_row=B, d=D)
    return pl.pallas_call(
        kern,
        grid=(t_all // tm,),
        in_specs=[pl.BlockSpec((SUBLANES, D), lambda i: (jnp.maximum(i * hb - 1, 0), 0)),
                  row_spec(D),
                  pl.BlockSpec((SUBLANES, D), lambda i: (jnp.minimum((i + 1) * hb, n_hblk - 1), 0)),
                  full(mod), full(w_qk), full(b_qk), full(w_vo), full(b_vo), full(w_g), full(b_g),
                  full(cw), full(cb)],
        out_specs=[row_spec(nqk2 // 2), row_spec(nqk2 // 2), row_spec(nv), row_spec(nv), row_spec(LANES)],
        out_shape=[jax.ShapeDtypeStruct((t_all, nqk2 // 2), BF16), jax.ShapeDtypeStruct((t_all, nqk2 // 2), BF16),
                   jax.ShapeDtypeStruct((t_all, nv), BF16), jax.ShapeDtypeStruct((t_all, nv), BF16),
                   jax.ShapeDtypeStruct((t_all, LANES), F32)],
        compiler_params=_params(1),
        name="mlstm_in",
    )(x_all, x_all, x_all, mod, w_qk, b_qk, w_vo, b_vo, w_g, b_g, cw, cb)


def _ml_scan_kernel(q_ref, k_ref, v_ref, g_ref, h_ref, c_sc, n_sc, m_sc, *, reverse):
    c_idx = pl.program_id(1)
    L = q_ref.shape[0]
    qk_dim = q_ref.shape[1] // ML_HEADS
    v_dim = v_ref.shape[1] // ML_HEADS

    @pl.when(c_idx == 0)
    def _():
        c_sc[...] = jnp.zeros_like(c_sc)
        n_sc[...] = jnp.zeros_like(n_sc)
        m_sc[...] = jnp.zeros_like(m_sc)

    gates = g_ref[...]
    r = lax.broadcasted_iota(I32, (L, L), 0)
    c = lax.broadcasted_iota(I32, (L, L), 1)
    tri = (c >= r) if reverse else (c <= r)
    cum = jnp.dot(jnp.where(tri, 1.0, 0.0), gates, precision=HIGHEST, preferred_element_type=F32)
    gates_t = gates.T
    cum_t = cum.T
    off = 2 * ML_HEADS if reverse else 0
    end = 0 if reverse else L - 1
    for h in range(ML_HEADS):
        li_lane = off + h
        lf_lane = off + ML_HEADS + h
        b_col = cum[:, lf_lane:lf_lane + 1]
        b_row = cum_t[lf_lane:lf_lane + 1, :]
        li_col = gates[:, li_lane:li_lane + 1]
        li_row = gates_t[li_lane:li_lane + 1, :]
        b_last = b_col[end:end + 1, :]
        m_prev = m_sc[h:h + 1, 0:1]
        dmat = jnp.where(tri, b_col - b_row + li_row, -jnp.inf)
        inter = b_col + m_prev
        m_t = jnp.maximum(jnp.max(dmat, axis=1, keepdims=True), inter)
        p = jnp.exp(dmat - m_t)
        w_inter = jnp.exp(inter - m_t)
        qh = q_ref[:, h * qk_dim:(h + 1) * qk_dim]
        kh = k_ref[:, h * qk_dim:(h + 1) * qk_dim]
        vh = v_ref[:, h * v_dim:(h + 1) * v_dim]
        s = lax.dot_general(qh, kh, (((1,), (1,)), ((), ())), preferred_element_type=F32) * p
        c_h = c_sc[h]
        n_row = n_sc[h:h + 1, 0:qk_dim]
        num = (jnp.dot(s.astype(BF16), vh, preferred_element_type=F32)
               + w_inter * jnp.dot(qh, c_h.astype(BF16), preferred_element_type=F32))
        den = (jnp.sum(s, axis=1, keepdims=True)
               + w_inter * jnp.sum(qh.astype(F32) * n_row, axis=1, keepdims=True))
        h_ref[:, h * v_dim:(h + 1) * v_dim] = num / jnp.maximum(jnp.abs(den), jnp.exp(-m_t))

        g_row = b_last - b_row + li_row
        g_col = b_last - b_col + li_col
        m_new = jnp.maximum(b_last + m_prev, jnp.max(g_row, axis=1, keepdims=True))
        decay = jnp.exp(b_last + m_prev - m_new)
        kw = kh.astype(F32) * jnp.exp(g_col - m_new)
        c_sc[h] = decay * c_h + lax.dot_general(kw.astype(BF16), vh, (((0,), (0,)), ((), ())),
                                                preferred_element_type=F32)
        n_sc[h:h + 1, 0:qk_dim] = decay * n_row + jnp.sum(kw, axis=0, keepdims=True)
        m_sc[h:h + 1, :] = jnp.broadcast_to(m_new, (1, LANES))


def _ml_scan(q, k, v, g, dims, reverse):
    B, S, C, D = dims
    L = ML_CHUNK
    nc_ctx = C // L
    nc_lat = S // L
    t_all = q.shape[0]
    qk_dim = q.shape[1] // ML_HEADS
    v_dim = v.shape[1] // ML_HEADS

    def idx(b, c):
        in_ctx = c < nc_ctx
        cl = c - nc_ctx
        if reverse:
            ctx_blk = (B * S + b * C) // L + (nc_ctx - 1 - c)
            lat_blk = (b * S) // L + (nc_lat - 1 - cl)
        else:
            ctx_blk = (B * S + b * C) // L + c
            lat_blk = (b * S) // L + cl
        return (jnp.where(in_ctx, ctx_blk, lat_blk), 0)

    spec = lambda w: pl.BlockSpec((L, w), idx)
    return pl.pallas_call(
        functools.partial(_ml_scan_kernel, reverse=reverse),
        grid=(B, nc_ctx + nc_lat),
        in_specs=[spec(q.shape[1]), spec(k.shape[1]), spec(v.shape[1]), spec(LANES)],
        out_specs=spec(v.shape[1]),
        out_shape=jax.ShapeDtypeStruct((t_all, v.shape[1]), F32),
        scratch_shapes=[pltpu.VMEM((ML_HEADS, qk_dim, v_dim), F32),
                        pltpu.VMEM((ML_HEADS, LANES), F32),
                        pltpu.VMEM((ML_HEADS, LANES), F32)],
        compiler_params=_params(2),
        name="mlstm_scan_bwd" if reverse else "mlstm_scan_fwd",
    )(q, k, v, g)


def kernel(x, c, ctx, c_ctx, ada_w, ada_b, ln_g, ln_b, attn_w_qkv, attn_b_qkv, attn_sink, attn_w_o, attn_b_o,
           ml_w_in, ml_b_in, ml_conv_w, ml_conv_b, ml_norm_g, ml_w_out, router_w, router_b,
           exp_w_gu, exp_b_gu, exp_w_down, exp_b_down):
    B, S, D = x.shape
    C = ctx.shape[1]
    depth = ada_w.shape[0]
    dims = (B, S, C, D)
    alpha = (2.0 * depth) ** 0.25
    n_lat = B * S

    cvec = jnp.zeros((SUBLANES, D), F32).at[:B].set(c).at[B].set(c_ctx)
    mods = _adaln(cvec, ada_w, ada_b)
    x_all = jnp.concatenate([x.reshape(n_lat, D), ctx.reshape(B * C, D)], axis=0)

    q, k, v = _attn_qkv(x_all, mods[0], attn_w_qkv[0], attn_b_qkv[0], dims)
    o = _attention(q, k, v, attn_sink[0], dims)
    x1, hx, topi, gates = _post_mixer((o,), None, x_all, mods[0], attn_w_o[0], attn_b_o[0], ln_g[0, 0], ln_b[0, 0],
                                      router_w[0], router_b[0], dims, x_all.shape[0], alpha)
    x_all = _moe(hx, topi, gates, x1, mods[0], ln_g[0, 1], ln_b[0, 1],
                 0, exp_w_gu, exp_b_gu, exp_w_down, exp_b_down, dims, alpha)

    q, k, v, og, g = _ml_in(x_all, mods[1], ml_w_in[0], ml_b_in[0], ml_conv_w[0], ml_conv_b[0], dims)
    hf = _ml_scan(q, k, v, g, dims, reverse=False)
    hb = _ml_scan(q, k, v, g, dims, reverse=True)
    zero_b = jnp.zeros((D,), F32)
    x1, hx, topi, gates = _post_mixer((hf, hb, og), ml_norm_g[0], x_all, mods[1], ml_w_out[0], zero_b,
                                      ln_g[1, 0], ln_b[1, 0], router_w[1], router_b[1], dims, n_lat, alpha)
    out = _moe(hx, topi, gates, x1, mods[1], ln_g[1, 1], ln_b[1, 1],
               1, exp_w_gu, exp_b_gu, exp_w_down, exp_b_down, dims, alpha)
    return out.reshape(B, S, D)
```

```python
import functools

import jax
import jax.numpy as jnp
from jax import lax
from jax.experimental import pallas as pl
from jax.experimental.pallas import tpu as pltpu

F32 = jnp.float32
BF16 = jnp.bfloat16
I32 = jnp.int32
HIGHEST = lax.Precision.HIGHEST

GRID_W = 64
ATTN_HEAD_DIM = 64
ATTN_KV_HEADS = 4
WINDOW = 128
ATTN_BLOCK = 128
ROPE_THETA = 10000.0
ML_HEADS = 8
ML_CONV_W = 5
GATE_CAP = 15.0
TOP_K = 4
SWIGLU_ALPHA = 1.702
SWIGLU_LIMIT = 7.0
LN_EPS = 1e-5

LANES = 128
SUBLANES = 8
VMEM_LIMIT = 56 * 1024 * 1024
EXPERT_TILE = 512
ROUTE_TILE = 256
ML_CHUNK = 128
STAGE_ROWS = ROUTE_TILE * TOP_K + 2 * LANES
TABLE_W = 2 * LANES
ZERO_ROWS = 64


def _params(n_axes, vmem=VMEM_LIMIT):
    return pltpu.CompilerParams(dimension_semantics=("arbitrary",) * n_axes, vmem_limit_bytes=vmem)


def _layer_norm(r, g, b):
    mu = jnp.mean(r, axis=-1, keepdims=True)
    rc = r - mu
    var = jnp.mean(rc * rc, axis=-1, keepdims=True)
    return rc * lax.rsqrt(var + LN_EPS) * g + b


def _sigmoid(x):
    return 1.0 / (1.0 + jnp.exp(-x))


def _mod_row(i, n_lat_tiles, tiles_per_batch, ctx_row):
    return jnp.where(i < n_lat_tiles, i // tiles_per_batch, ctx_row)


def _adaln_kernel(c_ref, w_ref, b_ref, o_ref):
    c = c_ref[...]
    s = c * _sigmoid(c)
    o_ref[0] = jnp.dot(s, w_ref[0], precision=HIGHEST, preferred_element_type=F32) + b_ref[0]


def _adaln(cvec, ada_w, ada_b):
    depth, d, n = ada_w.shape
    tn = 1536
    return pl.pallas_call(
        _adaln_kernel,
        grid=(depth, n // tn),
        in_specs=[pl.BlockSpec((SUBLANES, d), lambda l, j: (0, 0)),
                  pl.BlockSpec((1, d, tn), lambda l, j: (l, 0, j)),
                  pl.BlockSpec((1, 1, tn), lambda l, j: (l, 0, j))],
        out_specs=pl.BlockSpec((1, SUBLANES, tn), lambda l, j: (l, 0, j)),
        out_shape=jax.ShapeDtypeStruct((depth, SUBLANES, n), F32),
        compiler_params=_params(2),
        name="adaln",
    )(cvec, ada_w, ada_b.reshape(depth, 1, n))


def _qkv_kernel(x_ref, mod_ref, w_ref, b_ref, cos_ref, sin_ref, q_ref, k_ref, vt_ref, *,
                n_lat_tiles, tiles_per_batch, ctx_row, d, qd, kvd):
    i = pl.program_id(0)
    row = _mod_row(i, n_lat_tiles, tiles_per_batch, ctx_row)
    shift = mod_ref[pl.ds(row, 1), 0:d]
    scale = mod_ref[pl.ds(row, 1), d:2 * d]
    h = x_ref[...] * (1.0 + scale) + shift
    z = jnp.dot(h.astype(BF16), w_ref[...], preferred_element_type=F32) + b_ref[...]
    nrot = qd + kvd
    qk = z[:, :nrot]
    reps = nrot // LANES
    cos = jnp.concatenate([cos_ref[...]] * reps, axis=1)
    sin = jnp.concatenate([sin_ref[...]] * reps, axis=1)
    lane = lax.broadcasted_iota(I32, qk.shape, 1)
    low_half = (lane & 16) == 0
    partner = jnp.where(low_half, pltpu.roll(qk, nrot - 16, 1), pltpu.roll(qk, 16, 1))
    qk = qk * cos + partner * sin
    q_ref[...] = (qk[:, :qd] * (ATTN_HEAD_DIM ** -0.5)).astype(BF16)
    k_ref[...] = qk[:, qd:].astype(BF16)
    vt_ref[...] = z[:, nrot:].T.astype(BF16)


def _rope_tables(s_len, tm):
    half = ATTN_HEAD_DIM // 4
    freqs = ROPE_THETA ** (-jnp.arange(half, dtype=F32) / half)
    t = jnp.arange(s_len)
    rows = (t // GRID_W).astype(F32)[:, None] * freqs[None, :]
    cols = (t % GRID_W).astype(F32)[:, None] * freqs[None, :]
    ang = jnp.concatenate([rows, rows, cols, cols], axis=1)
    sign = jnp.tile(jnp.concatenate([-jnp.ones((half,), F32), jnp.ones((half,), F32)]), 2)
    cos = jnp.cos(ang)
    sin = jnp.sin(ang) * sign[None, :]
    reps = LANES // ATTN_HEAD_DIM
    cos = jnp.concatenate([jnp.tile(cos, (1, reps)), jnp.ones((tm, LANES), F32)], axis=0)
    sin = jnp.concatenate([jnp.tile(sin, (1, reps)), jnp.zeros((tm, LANES), F32)], axis=0)
    return cos, sin


def _attn_qkv(x_all, mod, w_qkv, b_qkv, dims):
    B, S, C, D = dims
    t_all = x_all.shape[0]
    tm = 512
    n_lat_tiles = B * S // tm
    tiles_per_batch = S // tm
    ncols = w_qkv.shape[1]
    kvd = ATTN_KV_HEADS * ATTN_HEAD_DIM
    qd = ncols - 2 * kvd
    cos, sin = _rope_tables(S, tm)

    def tab_idx(i):
        return (jnp.where(i < n_lat_tiles, i % tiles_per_batch, tiles_per_batch), 0)

    kern = functools.partial(_qkv_kernel, n_lat_tiles=n_lat_tiles, tiles_per_batch=tiles_per_batch,
                             ctx_row=B, d=D, qd=qd, kvd=kvd)
    return pl.pallas_call(
        kern,
        grid=(t_all // tm,),
        in_specs=[pl.BlockSpec((tm, D), lambda i: (i, 0)),
                  pl.BlockSpec(mod.shape, lambda i: (0, 0)),
                  pl.BlockSpec((D, ncols), lambda i: (0, 0)),
                  pl.BlockSpec((1, ncols), lambda i: (0, 0)),
                  pl.BlockSpec((tm, LANES), tab_idx),
                  pl.BlockSpec((tm, LANES), tab_idx)],
        out_specs=[pl.BlockSpec((tm, qd), lambda i: (i, 0)),
                   pl.BlockSpec((tm, kvd), lambda i: (i, 0)),
                   pl.BlockSpec((kvd, tm), lambda i: (0, i))],
        out_shape=[jax.ShapeDtypeStruct((t_all, qd), BF16),
                   jax.ShapeDtypeStruct((t_all, kvd), BF16),
                   jax.ShapeDtypeStruct((kvd, t_all), BF16)],
        compiler_params=_params(1),
        name="attn_qkv",
    )(x_all, mod, w_qkv.astype(BF16), b_qkv.reshape(1, ncols), cos, sin)


def _attn_kernel(sink_ref, q_ref, kp_ref, ko_ref, kn_ref, kc_ref, vp_ref, vo_ref, vn_ref, vc_ref, o_ref,
                 bias_sc, s_sc, p_sc, ot_sc, *,
                 n_lat_steps, nb, s_len, c_len):
    j = pl.program_id(0)
    is_lat = j < n_lat_steps
    n = j % nb
    blk = ATTN_BLOCK
    nloc = 3 * blk
    nk = nloc + c_len
    ki = lax.broadcasted_iota(I32, (nloc, blk), 0)
    qj = lax.broadcasted_iota(I32, (nloc, blk), 1)
    kpos = n * blk - WINDOW + ki
    qpos = n * blk + qj
    local_ok = (jnp.abs(kpos - qpos) <= WINDOW) & (kpos >= 0) & (kpos < s_len) & is_lat
    bias_sc[...] = jnp.where(local_ok, 0.0, -jnp.inf)
    hd = ATTN_HEAD_DIM
    group = q_ref.shape[1] // (ATTN_KV_HEADS * hd)
    kcat = jnp.concatenate([kp_ref[...], ko_ref[...], kn_ref[...], kc_ref[...]], axis=0)
    vcat_t = jnp.concatenate([vp_ref[...], vo_ref[...], vn_ref[...], vc_ref[...]], axis=1)
    for kh in range(ATTN_KV_HEADS):
        q_grp = jnp.concatenate([q_ref[:, (kh * group + g) * hd:(kh * group + g + 1) * hd]
                                 for g in range(group)], axis=0)
        s_sc[...] = lax.dot_general(kcat[:, kh * hd:(kh + 1) * hd], q_grp, (((1,), (1,)), ((), ())),
                                    preferred_element_type=F32)
        inv_l = []
        for g in range(group):
            cols = slice(g * blk, (g + 1) * blk)
            s_loc = s_sc[0:nloc, cols] + bias_sc[...]
            s_ctx = s_sc[nloc:nk, cols]
            sk = sink_ref[kh * group + g]
            m = jnp.maximum(jnp.maximum(jnp.max(s_loc, axis=0, keepdims=True),
                                        jnp.max(s_ctx, axis=0, keepdims=True)), sk)
            p_loc = jnp.exp(s_loc - m)
            p_ctx = jnp.exp(s_ctx - m)
            l = (jnp.sum(p_loc, axis=0, keepdims=True) + jnp.sum(p_ctx, axis=0, keepdims=True)
                 + jnp.exp(sk - m))
            p_sc[0:nloc, cols] = p_loc.astype(BF16)
            p_sc[nloc:nk, cols] = p_ctx.astype(BF16)
            inv_l.append(1.0 / l)
        o_t = jnp.dot(vcat_t[kh * hd:(kh + 1) * hd, :], p_sc[...], preferred_element_type=F32)
        o_t = o_t * jnp.concatenate(inv_l, axis=1)
        for g in range(group):
            h = kh * group + g
            ot_sc[h * hd:(h + 1) * hd, :] = o_t[:, g * blk:(g + 1) * blk]
    o_ref[...] = ot_sc[...].T.astype(BF16)


def _attention(q_all, k_all, vt_all, sink, dims):
    B, S, C, D = dims
    blk = ATTN_BLOCK
    nb = S // blk
    n_lat_steps = B * nb
    ctx_steps_per_batch = C // blk
    n_steps = n_lat_steps + B * ctx_steps_per_batch
    qd = q_all.shape[1]
    kvd = k_all.shape[1]

    def local_idx(off):
        def idx(j):
            b = j // nb
            nn = jnp.clip(j % nb + off, 0, nb - 1)
            return (jnp.where(j < n_lat_steps, b * nb + nn, j), 0)
        return idx

    def ctx_idx(j):
        b = jnp.where(j < n_lat_steps, j // nb, (j - n_lat_steps) // ctx_steps_per_batch)
        return (B * S // C + b, 0)

    swap = lambda f: (lambda j: f(j)[::-1])
    loc = lambda off: pl.BlockSpec((blk, kvd), local_idx(off))
    ctxs = pl.BlockSpec((C, kvd), ctx_idx)
    loc_t = lambda off: pl.BlockSpec((kvd, blk), swap(local_idx(off)))
    ctxs_t = pl.BlockSpec((kvd, C), swap(ctx_idx))
    group = qd // kvd
    nk = 3 * blk + C
    kern = functools.partial(_attn_kernel, n_lat_steps=n_lat_steps, nb=nb, s_len=S, c_len=C)
    return pl.pallas_call(
        kern,
        grid=(n_steps,),
        in_specs=[pl.BlockSpec(memory_space=pltpu.SMEM),
                  pl.BlockSpec((blk, qd), lambda j: (j, 0)),
                  loc(-1), loc(0), loc(1), ctxs,
                  loc_t(-1), loc_t(0), loc_t(1), ctxs_t],
        out_specs=pl.BlockSpec((blk, qd), lambda j: (j, 0)),
        out_shape=jax.ShapeDtypeStruct((q_all.shape[0], qd), BF16),
        scratch_shapes=[pltpu.VMEM((3 * blk, blk), F32),
                        pltpu.VMEM((nk, group * blk), F32),
                        pltpu.VMEM((nk, group * blk), BF16),
                        pltpu.VMEM((qd, blk), F32)],
        compiler_params=_params(1),
        name="attn_core",
    )(sink, q_all, k_all, k_all, k_all, k_all, vt_all, vt_all, vt_all, vt_all)


def _split_bf16(a):
    hi = a.astype(BF16)
    return hi, (a - hi.astype(F32)).astype(BF16)


def _dot_3pass(a, w_hi_ref, w_lo_ref):
    a_hi, a_lo = _split_bf16(a)
    w_hi = w_hi_ref[...]
    return (jnp.dot(a_hi, w_hi, preferred_element_type=F32)
            + jnp.dot(a_lo, w_hi, preferred_element_type=F32)
            + jnp.dot(a_hi, w_lo_ref[...], preferred_element_type=F32))


def _route(hx, rw_hi_ref, rw_lo_ref, rb_ref, topi_ref, gate_ref):
    logits = _dot_3pass(hx, rw_hi_ref, rw_lo_ref) + rb_ref[...]
    lane = lax.broadcasted_iota(I32, logits.shape, 1)
    lanef = lane.astype(F32)
    vals, idxs = [], []
    l = logits
    for _ in range(TOP_K):
        m = jnp.max(l, axis=1, keepdims=True)
        idx = jnp.min(jnp.where(l == m, lanef, float(LANES)), axis=1, keepdims=True)
        vals.append(m)
        idxs.append(idx)
        l = jnp.where(lanef == idx, -jnp.inf, l)
    es = [jnp.exp(v - vals[0]) for v in vals]
    denom = es[0]
    for e in es[1:]:
        denom = denom + e
    topi = jnp.zeros(logits.shape, F32)
    gates = jnp.zeros(logits.shape, F32)
    for k in range(TOP_K):
        topi = jnp.where(lane == k, idxs[k], topi)
        gates = jnp.where(lane == k, es[k] / denom, gates)
    topi_ref[...] = topi.astype(I32)
    gate_ref[...] = gates


def _post_common(a, x_ref, mod_ref, w_ref, b_ref, lng_ref, lnb_ref, rw_ref, rwl_ref, rb_ref,
                 x1_ref, hx_ref, topi_ref, gate_ref, row, d, alpha):
    y = jnp.dot(a, w_ref[...], preferred_element_type=F32) + b_ref[...]
    gate_mix = mod_ref[pl.ds(row, 1), 2 * d:3 * d]
    shift = mod_ref[pl.ds(row, 1), 3 * d:4 * d]
    scale = mod_ref[pl.ds(row, 1), 4 * d:5 * d]
    x1 = _layer_norm(alpha * x_ref[...] + gate_mix * y, lng_ref[...], lnb_ref[...])
    hx = x1 * (1.0 + scale) + shift
    x1_ref[...] = x1
    hx_ref[...] = hx.astype(BF16)
    _route(hx, rw_ref, rwl_ref, rb_ref, topi_ref, gate_ref)


def _post_attn_kernel(o_ref, x_ref, mod_ref, w_ref, b_ref, lng_ref, lnb_ref, rw_ref, rwl_ref, rb_ref,
                      x1_ref, hx_ref, topi_ref, gate_ref, *, n_lat_tiles, tiles_per_batch, ctx_row, d, alpha):
    row = _mod_row(pl.program_id(0), n_lat_tiles, tiles_per_batch, ctx_row)
    _post_common(o_ref[...], x_ref, mod_ref, w_ref, b_ref, lng_ref, lnb_ref, rw_ref, rwl_ref, rb_ref,
                 x1_ref, hx_ref, topi_ref, gate_ref, row, d, alpha)


def _post_mlstm_kernel(hf_ref, hb_ref, og_ref, ng_ref, x_ref, mod_ref, w_ref, b_ref, lng_ref, lnb_ref,
                       rw_ref, rwl_ref, rb_ref, x1_ref, hx_ref, topi_ref, gate_ref, *,
                       n_lat_tiles, tiles_per_batch, ctx_row, d, alpha):
    row = _mod_row(pl.program_id(0), n_lat_tiles, tiles_per_batch, ctx_row)
    hsum = hf_ref[...] + hb_ref[...]
    vdim = hsum.shape[1] // ML_HEADS
    parts = []
    for h in range(ML_HEADS):
        seg = hsum[:, h * vdim:(h + 1) * vdim]
        mu = jnp.mean(seg, axis=1, keepdims=True)
        sc = seg - mu
        var = jnp.mean(sc * sc, axis=1, keepdims=True)
        parts.append(sc * lax.rsqrt(var + LN_EPS))
    y = jnp.concatenate(parts, axis=1) * ng_ref[...]
    a = (og_ref[...].astype(F32) * y).astype(BF16)
    _post_common(a, x_ref, mod_ref, w_ref, b_ref, lng_ref, lnb_ref, rw_ref, rwl_ref, rb_ref,
                 x1_ref, hx_ref, topi_ref, gate_ref, row, d, alpha)


def _post_mixer(mixer_inputs, norm_g, x_all, mod, w_o, b_o, ln_g, ln_b, router_w, router_b, dims, n_rows, alpha):
    B, S, C, D = dims
    tm = 512
    n_lat_tiles = B * S // tm
    tiles_per_batch = S // tm
    n_exp = router_w.shape[1]
    rw_f32 = jnp.zeros((D, LANES), F32).at[:, :n_exp].set(router_w)
    rw = rw_f32.astype(BF16)
    rw_lo = (rw_f32 - rw.astype(F32)).astype(BF16)
    rb = jnp.full((1, LANES), -1e30, F32).at[0, :n_exp].set(router_b)
    row_spec = lambda w: pl.BlockSpec((tm, w), lambda i: (i, 0))
    full = lambda a: pl.BlockSpec(a.shape, lambda i: (0,) * a.ndim)
    common = dict(n_lat_tiles=n_lat_tiles, tiles_per_batch=tiles_per_batch, ctx_row=B, d=D, alpha=alpha)
    w_bf = w_o.astype(BF16)
    b2 = b_o.reshape(1, D)
    tail = [x_all, mod, w_bf, b2, ln_g.reshape(1, D), ln_b.reshape(1, D), rw, rw_lo, rb]
    tail_specs = [row_spec(D), full(mod), full(w_bf), full(b2), pl.BlockSpec((1, D), lambda i: (0, 0)),
                  pl.BlockSpec((1, D), lambda i: (0, 0)), full(rw), full(rw_lo), full(rb)]
    if norm_g is None:
        kern = functools.partial(_post_attn_kernel, **common)
        args = list(mixer_inputs) + tail
        specs = [row_spec(mixer_inputs[0].shape[1])] + tail_specs
        name = "post_attn"
    else:
        kern = functools.partial(_post_mlstm_kernel, **common)
        ng = norm_g.reshape(1, -1)
        args = list(mixer_inputs) + [ng] + tail
        specs = [row_spec(a.shape[1]) for a in mixer_inputs] + [full(ng)] + tail_specs
        name = "post_mlstm"
    return pl.pallas_call(
        kern,
        grid=(n_rows // tm,),
        in_specs=specs,
        out_specs=[row_spec(D), row_spec(D), row_spec(LANES), row_spec(LANES)],
        out_shape=[jax.ShapeDtypeStruct((n_rows, D), F32), jax.ShapeDtypeStruct((n_rows, D), BF16),
                   jax.ShapeDtypeStruct((n_rows, LANES), I32), jax.ShapeDtypeStruct((n_rows, LANES), F32)],
        compiler_params=_params(1),
        name=name,
    )(*args)


def _exclusive_lane_cumsum(row):
    r = lax.broadcasted_iota(I32, (LANES, LANES), 0)
    c = lax.broadcasted_iota(I32, (LANES, LANES), 1)
    before = jnp.where(r < c, 1.0, 0.0)
    return jnp.dot(jnp.broadcast_to(row, (SUBLANES, LANES)), before,
                   precision=HIGHEST, preferred_element_type=F32)[0:1]


def _lane_row_to_column(row):
    r = lax.broadcasted_iota(I32, (LANES, LANES), 0)
    c = lax.broadcasted_iota(I32, (LANES, LANES), 1)
    return jnp.sum(jnp.where(r == c, jnp.broadcast_to(row, (LANES, LANES)), 0.0), axis=1, keepdims=True)


def _slots_kernel(topi_ref, col_ref, colt_ref, tab_ref, meta_ref, tot_sc, base_sc, carry_sc, *, tile):
    phase = pl.program_id(0)
    i = pl.program_id(1)
    tm = topi_ref.shape[0]
    topi = topi_ref[...]
    lane = lax.broadcasted_iota(I32, topi.shape, 1)
    sel = [lane == topi[:, k:k + 1] for k in range(TOP_K)]
    maskf = jnp.where(sel[0], 1.0, 0.0)
    for s in sel[1:]:
        maskf = maskf + jnp.where(s, 1.0, 0.0)
    n8 = jnp.ceil(jnp.sum(maskf, axis=0, keepdims=True) / SUBLANES) * SUBLANES

    @pl.when((phase == 0) & (i == 0))
    def _():
        tot_sc[...] = jnp.zeros_like(tot_sc)

    @pl.when(phase == 0)
    def _():
        tot_sc[...] = tot_sc[...] + n8

    @pl.when((phase == 1) & (i == 0))
    def _():
        tot = tot_sc[...]
        padded = jnp.ceil(tot / tile) * tile
        base = _exclusive_lane_cumsum(padded)
        base_sc[...] = base
        carry_sc[...] = jnp.zeros_like(carry_sc)
        rowi = lax.broadcasted_iota(I32, meta_ref.shape, 0)
        meta_ref[...] = jnp.where(rowi == 0, tot, jnp.where(rowi == 1, base, padded)).astype(I32)

    @pl.when(phase == 1)
    def _():
        start = base_sc[...] + carry_sc[...]
        off = _exclusive_lane_cumsum(n8)
        r = lax.broadcasted_iota(I32, (tm, tm), 0)
        c = lax.broadcasted_iota(I32, (tm, tm), 1)
        earlier = jnp.where(c < r, 1.0, 0.0).astype(BF16)
        rank = jnp.dot(earlier, maskf.astype(BF16), preferred_element_type=F32)
        stage_row = rank + off
        out = jnp.zeros(topi.shape, F32)
        for k in range(TOP_K):
            pk = jnp.sum(jnp.where(sel[k], stage_row, 0.0), axis=1, keepdims=True)
            out = jnp.where(lane == k, pk, out)
        col_ref[...] = out.astype(I32)
        colt_ref[0] = out.T[0:SUBLANES].astype(I32)

        off_c = _lane_row_to_column(off)
        n8_c = _lane_row_to_column(n8)
        start_c = _lane_row_to_column(start)
        u8 = (lax.broadcasted_iota(I32, (LANES, TABLE_W), 1) * SUBLANES).astype(F32)
        inside = (u8 >= off_c) & (u8 < off_c + n8_c)
        src = jnp.sum(jnp.where(inside, start_c + u8 - off_c, 0.0), axis=0, keepdims=True)
        n_units = jnp.sum(n8, axis=1, keepdims=True) / SUBLANES
        lane_t = lax.broadcasted_iota(I32, (1, TABLE_W), 1)
        tab_ref[0] = jnp.where(lane_t == TABLE_W - 1, n_units, src).astype(I32)
        carry_sc[...] = carry_sc[...] + n8


def _slots(topi, tile):
    t = topi.shape[0]
    tm = ROUTE_TILE
    nt = t // tm
    return pl.pallas_call(
        functools.partial(_slots_kernel, tile=float(tile)),
        grid=(2, nt),
        in_specs=[pl.BlockSpec((tm, LANES), lambda p, i: (i, 0))],
        out_specs=[pl.BlockSpec((tm, LANES), lambda p, i: (i * p, 0)),
                   pl.BlockSpec((1, SUBLANES, tm), lambda p, i: (i * p, 0, 0)),
                   pl.BlockSpec((1, 1, TABLE_W), lambda p, i: (i * p, 0, 0)),
                   pl.BlockSpec((SUBLANES, LANES), lambda p, i: (0, 0))],
        out_shape=[jax.ShapeDtypeStruct((t, LANES), I32),
                   jax.ShapeDtypeStruct((nt, SUBLANES, tm), I32),
                   jax.ShapeDtypeStruct((nt, 1, TABLE_W), I32),
                   jax.ShapeDtypeStruct((SUBLANES, LANES), I32)],
        scratch_shapes=[pltpu.VMEM((1, LANES), F32), pltpu.VMEM((1, LANES), F32), pltpu.VMEM((1, LANES), F32)],
        compiler_params=_params(2),
        name="moe_slots",
    )(topi)


def _dispatch_kernel(tab_ref, pad_start_ref, pad_units_ref, tail_ref, colt_ref, hx_ref, xs_ref, stage, zeros, sem, zsem):
    i = pl.program_id(0)
    nt = pl.num_programs(0)
    slot = i % 2
    kb = stage.shape[1]
    tm = hx_ref.shape[0]

    def unit_copy(tile, u, buf_slot):
        dst = pl.multiple_of(tab_ref[tile * TABLE_W + u], SUBLANES)
        return pltpu.make_async_copy(stage.at[buf_slot, pl.ds(pl.multiple_of(u * SUBLANES, SUBLANES), SUBLANES)],
                                     xs_ref.at[pl.ds(dst, SUBLANES)], sem.at[buf_slot])

    def drain(tile, buf_slot):
        def body(u, carry):
            unit_copy(tile, u, buf_slot).wait()
            return carry
        lax.fori_loop(0, tab_ref[tile * TABLE_W + TABLE_W - 1], body, 0)

    @pl.when(i >= 2)
    def _():
        drain(i - 2, slot)

    colt = colt_ref[0]
    c = lax.broadcasted_iota(I32, (kb, tm), 0)
    onehot = jnp.zeros((kb, tm), F32)
    for k in range(TOP_K):
        onehot = jnp.where(c == colt[k:k + 1, :], 1.0, onehot)
    stage[slot] = jnp.dot(onehot.astype(BF16), hx_ref[...], preferred_element_type=F32)

    def issue(u, carry):
        unit_copy(i, u, slot).start()
        return carry
    lax.fori_loop(0, tab_ref[i * TABLE_W + TABLE_W - 1], issue, 0)

    @pl.when(i == nt - 1)
    def _():
        zeros[...] = jnp.zeros_like(zeros)
        n_exp = pad_start_ref.shape[0]

        def zero_copy(e, u):
            dst = pl.multiple_of(pad_start_ref[e] + u * SUBLANES, SUBLANES)
            return pltpu.make_async_copy(zeros.at[pl.ds(0, SUBLANES)], xs_ref.at[pl.ds(dst, SUBLANES)], zsem)

        def per_expert(fn):
            def outer(e, carry):
                def inner(u, carry2):
                    fn(e, u)
                    return carry2
                lax.fori_loop(0, pad_units_ref[e], inner, 0)
                return carry
            lax.fori_loop(0, n_exp, outer, 0)

        per_expert(lambda e, u: zero_copy(e, u).start())
        per_expert(lambda e, u: zero_copy(e, u).wait())

        zrows = zeros.shape[0]

        def tail_copy(u):
            dst = pl.multiple_of(tail_ref[0] + u * zrows, zrows)
            return pltpu.make_async_copy(zeros, xs_ref.at[pl.ds(dst, zrows)], zsem)

        def tail_loop(fn):
            def body(u, carry):
                fn(u)
                return carry
            lax.fori_loop(0, tail_ref[1], body, 0)

        tail_loop(lambda u: tail_copy(u).start())
        tail_loop(lambda u: tail_copy(u).wait())

        @pl.when(nt >= 2)
        def _():
            drain(i - 1, 1 - slot)
        drain(i, slot)


def _dispatch(tab, pad_start, pad_units, tail, colt, hx, n_slots):
    t, d = hx.shape
    tm = ROUTE_TILE
    return pl.pallas_call(
        _dispatch_kernel,
        grid_spec=pltpu.PrefetchScalarGridSpec(
            num_scalar_prefetch=4,
            grid=(t // tm,),
            in_specs=[pl.BlockSpec((1, SUBLANES, tm), lambda i, *_: (i, 0, 0)),
                      pl.BlockSpec((tm, d), lambda i, *_: (i, 0))],
            out_specs=pl.BlockSpec(memory_space=pl.ANY),
            scratch_shapes=[pltpu.VMEM((2, STAGE_ROWS, d), F32), pltpu.VMEM((ZERO_ROWS, d), F32),
                            pltpu.SemaphoreType.DMA((2,)), pltpu.SemaphoreType.DMA(())]),
        out_shape=jax.ShapeDtypeStruct((n_slots, d), F32),
        compiler_params=_params(1),
        name="moe_dispatch",
    )(tab, pad_start, pad_units, tail, colt, hx)


def _expert_kernel(te_ref, nu_ref, xs_ref, wgu_ref, bgu_ref, wd_ref, bd_ref, ys_ref, wgu_sc, wd_sc):
    j = pl.program_id(0)
    active = j < nu_ref[0]
    changed = (j == 0) | (te_ref[j] != te_ref[jnp.maximum(j - 1, 0)])
    ff = wd_ref.shape[0]

    @pl.when(active & changed)
    def _():
        wgu_sc[...] = wgu_ref[...].astype(BF16)
        wd_sc[...] = wd_ref[...].astype(BF16)

    @pl.when(active)
    def _():
        x = xs_ref[...].astype(BF16)
        gu = jnp.dot(x, wgu_sc[...], preferred_element_type=F32) + bgu_ref[...]
        gl = jnp.minimum(gu[:, :ff], SWIGLU_LIMIT)
        lin = jnp.clip(gu[:, ff:], -SWIGLU_LIMIT, SWIGLU_LIMIT)
        act = gl * _sigmoid(SWIGLU_ALPHA * gl) * (lin + 1.0)
        ys_ref[...] = jnp.dot(act.astype(BF16), wd_sc[...], preferred_element_type=F32) + bd_ref[...]

    @pl.when(jnp.logical_not(active))
    def _():
        ys_ref[...] = jnp.zeros_like(ys_ref)


def _experts(tile_expert, n_used, xs, layer, w_gu, b_gu, w_down, b_down):
    n_slots, d = xs.shape
    depth, n_exp, _, ff2 = w_gu.shape
    ff = w_down.shape[2]
    te = EXPERT_TILE
    row_idx = lambda j, te_r, nu: (jnp.maximum(jnp.minimum(j, nu[0] - 1), 0), 0)
    w_idx = lambda j, te_r, nu: (layer, te_r[j], 0, 0)
    return pl.pallas_call(
        _expert_kernel,
        grid_spec=pltpu.PrefetchScalarGridSpec(
            num_scalar_prefetch=2,
            grid=(n_slots // te,),
            in_specs=[pl.BlockSpec((te, d), row_idx),
                      pl.BlockSpec((None, None, d, ff2), w_idx),
                      pl.BlockSpec((None, None, 1, ff2), w_idx),
                      pl.BlockSpec((None, None, ff, d), w_idx),
                      pl.BlockSpec((None, None, 1, d), w_idx)],
            out_specs=pl.BlockSpec((te, d), lambda j, te_r, nu: (j, 0)),
            scratch_shapes=[pltpu.VMEM((d, ff2), BF16), pltpu.VMEM((ff, d), BF16)]),
        out_shape=jax.ShapeDtypeStruct((n_slots, d), F32),
        compiler_params=_params(1),
        name="moe_experts",
    )(tile_expert, n_used, xs, w_gu, b_gu.reshape(depth, n_exp, 1, ff2), w_down, b_down.reshape(depth, n_exp, 1, d))


def _combine_kernel(tab_ref, col_ref, gate_ref, ys_ref, x_ref, mod_ref, lng_ref, lnb_ref, out_ref, stage, sem, *,
                    n_lat_tiles, tiles_per_batch, ctx_row, d, alpha):
    i = pl.program_id(0)
    nt = pl.num_programs(0)
    slot = i % 2
    kb = stage.shape[1]
    tm = x_ref.shape[0]

    def unit_copy(tile, u, buf_slot):
        src = pl.multiple_of(tab_ref[tile * TABLE_W + u], SUBLANES)
        return pltpu.make_async_copy(ys_ref.at[pl.ds(src, SUBLANES)],
                                     stage.at[buf_slot, pl.ds(pl.multiple_of(u * SUBLANES, SUBLANES), SUBLANES)],
                                     sem.at[buf_slot])

    def for_units(tile, fn):
        def body(u, carry):
            fn(u)
            return carry
        lax.fori_loop(0, tab_ref[tile * TABLE_W + TABLE_W - 1], body, 0)

    @pl.when(i == 0)
    def _():
        stage[...] = jnp.zeros_like(stage)
        for_units(0, lambda u: unit_copy(0, u, 0).start())

    @pl.when(i + 1 < nt)
    def _():
        for_units(i + 1, lambda u: unit_copy(i + 1, u, 1 - slot).start())

    for_units(i, lambda u: unit_copy(i, u, slot).wait())

    col = col_ref[...]
    gates = gate_ref[...]
    c = lax.broadcasted_iota(I32, (tm, kb), 1)
    weights = jnp.zeros((tm, kb), F32)
    for k in range(TOP_K):
        weights = jnp.where(c == col[:, k:k + 1], gates[:, k:k + 1], weights)
    y = jnp.dot(weights.astype(BF16), stage[slot].astype(BF16), preferred_element_type=F32)
    row = _mod_row(i, n_lat_tiles, tiles_per_batch, ctx_row)
    gate_mlp = mod_ref[pl.ds(row, 1), 5 * d:6 * d]
    out_ref[...] = _layer_norm(alpha * x_ref[...] + gate_mlp * y, lng_ref[...], lnb_ref[...])


def _combine(tab, col, ys, gates, x1, mod, ln_g, ln_b, dims, alpha):
    B, S, C, D = dims
    t = x1.shape[0]
    tm = ROUTE_TILE
    kern = functools.partial(_combine_kernel, n_lat_tiles=B * S // tm, tiles_per_batch=S // tm, ctx_row=B,
                             d=D, alpha=alpha)
    return pl.pallas_call(
        kern,
        grid_spec=pltpu.PrefetchScalarGridSpec(
            num_scalar_prefetch=1,
            grid=(t // tm,),
            in_specs=[pl.BlockSpec((tm, LANES), lambda i, tab_r: (i, 0)),
                      pl.BlockSpec((tm, LANES), lambda i, tab_r: (i, 0)),
                      pl.BlockSpec(memory_space=pl.ANY),
                      pl.BlockSpec((tm, D), lambda i, tab_r: (i, 0)),
                      pl.BlockSpec(mod.shape, lambda i, tab_r: (0, 0)),
                      pl.BlockSpec((1, D), lambda i, tab_r: (0, 0)),
                      pl.BlockSpec((1, D), lambda i, tab_r: (0, 0))],
            out_specs=pl.BlockSpec((tm, D), lambda i, tab_r: (i, 0)),
            scratch_shapes=[pltpu.VMEM((2, STAGE_ROWS, D), F32), pltpu.SemaphoreType.DMA((2,))]),
        out_shape=jax.ShapeDtypeStruct((t, D), F32),
        compiler_params=_params(1),
        name="moe_combine",
    )(tab, col, gates, ys, x1, mod, ln_g.reshape(1, D), ln_b.reshape(1, D))


def _moe(hx, topi, gates, x1, mod, ln_g, ln_b, layer, w_gu, b_gu, w_down, b_down, dims, alpha):
    t = hx.shape[0]
    n_exp = w_gu.shape[1]
    te = EXPERT_TILE
    n_route_tiles = t // ROUTE_TILE
    max_rows = t * TOP_K + (SUBLANES - 1) * n_exp * n_route_tiles
    n_tiles = -(-max_rows // te) + n_exp
    col, colt, tab, meta = _slots(topi, te)
    tab = tab.reshape(-1)
    tot, base, padded = meta[0, :n_exp], meta[1, :n_exp], meta[2, :n_exp]
    ends = jnp.cumsum(padded // te)
    n_used = ends[-1:].astype(I32)
    tile_ids = jnp.minimum(jnp.arange(n_tiles, dtype=I32), n_used[0] - 1)
    tile_expert = jnp.sum((tile_ids[:, None] >= ends[None, :]).astype(I32), axis=1)
    tile_expert = jnp.minimum(tile_expert, n_exp - 1).astype(I32)
    pad_start = (base + tot).astype(I32)
    pad_units = ((padded - tot) // SUBLANES).astype(I32)
    used_rows = n_used[0] * te
    tail = jnp.stack([used_rows, (n_tiles * te - used_rows) // ZERO_ROWS]).astype(I32)
    xs = _dispatch(tab, pad_start, pad_units, tail, colt, hx, n_tiles * te)
    ys = _experts(tile_expert, n_used, xs, layer, w_gu, b_gu, w_down, b_down)
    return _combine(tab, col, ys, gates, x1, mod, ln_g, ln_b, dims, alpha)


def _ml_in_kernel(xp_ref, x_ref, xn_ref, mod_ref, wqk_ref, bqk_ref, wvo_ref, bvo_ref, wg_ref, wgl_ref, bg_ref,
                  cw_ref, cb_ref, q_ref, k_ref, v_ref, og_ref, g_ref, *,
                  n_lat_tiles, tiles_per_batch, ctx_tiles_per_seq, ctx_row, d):
    i = pl.program_id(0)
    tm = x_ref.shape[0]
    halo = SUBLANES
    row = _mod_row(i, n_lat_tiles, tiles_per_batch, ctx_row)
    shift = mod_ref[pl.ds(row, 1), 0:d]
    scale = mod_ref[pl.ds(row, 1), d:2 * d]
    is_lat = i < n_lat_tiles
    seq_tile = jnp.where(is_lat, i % tiles_per_batch, (i - n_lat_tiles) % ctx_tiles_per_seq)
    seq_tiles = jnp.where(is_lat, tiles_per_batch, ctx_tiles_per_seq)
    first = seq_tile == 0
    last = seq_tile == seq_tiles - 1

    h = x_ref[...] * (1.0 + scale) + shift
    h_ext = jnp.concatenate([xp_ref[...] * (1.0 + scale) + shift, h, xn_ref[...] * (1.0 + scale) + shift], axis=0)
    z = jnp.dot(h_ext.astype(BF16), wqk_ref[...], preferred_element_type=F32) + bqk_ref[...]
    r = lax.broadcasted_iota(I32, z.shape, 0)
    outside = ((r < halo) & first) | ((r >= halo + tm) & last)
    z = jnp.where(outside, 0.0, z)
    n_ext = tm + 2 * halo
    cw = cw_ref[...]
    acc = None
    for j in range(ML_CONV_W):
        sh = (ML_CONV_W // 2 - j) % n_ext
        zj = z if sh == 0 else pltpu.roll(z, sh, 0)
        term = zj[halo:halo + tm] * cw[j:j + 1]
        acc = term if acc is None else acc + term
    qk = acc + cb_ref[...]
    qk = qk * _sigmoid(qk)
    nqk = qk.shape[1] // 2
    qk_dim = nqk // ML_HEADS
    q_ref[...] = (qk[:, :nqk] * (qk_dim ** -0.5)).astype(BF16)
    k_ref[...] = qk[:, nqk:].astype(BF16)

    vo = jnp.dot(h.astype(BF16), wvo_ref[...], preferred_element_type=F32) + bvo_ref[...]
    nv = vo.shape[1] // 2
    v_ref[...] = vo[:, :nv].astype(BF16)
    og_ref[...] = _sigmoid(vo[:, nv:]).astype(BF16)

    zg = _dot_3pass(h, wg_ref, wgl_ref) + bg_ref[...]
    g = GATE_CAP * jnp.tanh(zg / GATE_CAP)
    log_sig = jnp.minimum(g, 0.0) - jnp.log(1.0 + jnp.exp(-jnp.abs(g)))
    lane = lax.broadcasted_iota(I32, g.shape, 1)
    is_forget = ((lane // ML_HEADS) % 2) == 1
    g_ref[...] = jnp.where(is_forget, log_sig, g)


def _ml_in(x_all, mod, w_in, b_in, conv_w, conv_b, dims):
    B, S, C, D = dims
    t_all = x_all.shape[0]
    tm = 256
    n_lat_tiles = B * S // tm
    nqk2 = conv_w.shape[1]
    nv = (w_in.shape[1] - nqk2 - 4 * ML_HEADS) // 2
    ng = 4 * ML_HEADS
    w_qk = w_in[:, :nqk2].astype(BF16)
    w_vo = w_in[:, nqk2:nqk2 + 2 * nv].astype(BF16)
    w_g_f32 = jnp.zeros((D, LANES), F32).at[:, :ng].set(w_in[:, nqk2 + 2 * nv:])
    w_g = w_g_f32.astype(BF16)
    w_g_lo = (w_g_f32 - w_g.astype(F32)).astype(BF16)
    b_qk = b_in[:nqk2].reshape(1, -1)
    b_vo = b_in[nqk2:nqk2 + 2 * nv].reshape(1, -1)
    b_g = jnp.zeros((1, LANES), F32).at[0, :ng].set(b_in[nqk2 + 2 * nv:])
    cw = jnp.zeros((SUBLANES, nqk2), F32).at[:ML_CONV_W].set(conv_w)
    cb = conv_b.reshape(1, -1)
    hb = tm // SUBLANES
    n_hblk = t_all // SUBLANES
    full = lambda a: pl.BlockSpec(a.shape, lambda i: (0,) * a.ndim)
    row_spec = lambda w: pl.BlockSpec((tm, w), lambda i: (i, 0))
    kern = functools.partial(_ml_in_kernel, n_lat_tiles=n_lat_tiles, tiles_per_batch=S // tm,
                             ctx_tiles_per_seq=C // tm, ctx_row=B, d=D)
    return pl.pallas_call(
        kern,
        grid=(t_all // tm,),
        in_specs=[pl.BlockSpec((SUBLANES, D), lambda i: (jnp.maximum(i * hb - 1, 0), 0)),
                  row_spec(D),
                  pl.BlockSpec((SUBLANES, D), lambda i: (jnp.minimum((i + 1) * hb, n_hblk - 1), 0)),
                  full(mod), full(w_qk), full(b_qk), full(w_vo), full(b_vo), full(w_g), full(w_g_lo), full(b_g),
                  full(cw), full(cb)],
        out_specs=[row_spec(nqk2 // 2), row_spec(nqk2 // 2), row_spec(nv), row_spec(nv), row_spec(LANES)],
        out_shape=[jax.ShapeDtypeStruct((t_all, nqk2 // 2), BF16), jax.ShapeDtypeStruct((t_all, nqk2 // 2), BF16),
                   jax.ShapeDtypeStruct((t_all, nv), BF16), jax.ShapeDtypeStruct((t_all, nv), BF16),
                   jax.ShapeDtypeStruct((t_all, LANES), F32)],
        compiler_params=_params(1),
        name="mlstm_in",
    )(x_all, x_all, x_all, mod, w_qk, b_qk, w_vo, b_vo, w_g, w_g_lo, b_g, cw, cb)


def _ml_chunk(q_ref, k_ref, v_ref, g_ref, h_ref, s_sc, m_sc, reverse):
    L = q_ref.shape[0]
    qk_dim = q_ref.shape[1] // ML_HEADS
    v_dim = v_ref.shape[1] // ML_HEADS
    nh = ML_HEADS
    gates_t = g_ref[...].T
    sr = lax.broadcasted_iota(I32, (L, L), 0)
    lc = lax.broadcasted_iota(I32, (L, L), 1)
    upto = (sr >= lc) if reverse else (sr <= lc)
    cum_t = jnp.dot(gates_t, jnp.where(upto, 1.0, 0.0), precision=HIGHEST, preferred_element_type=F32)
    off = 2 * nh if reverse else 0
    li = gates_t[off:off + nh]
    b = cum_t[off + nh:off + 2 * nh]
    c = li - b
    lane = lax.broadcasted_iota(I32, c.shape, 1)
    mu = c
    d = 1
    while d < L:
        if reverse:
            shifted = jnp.where(lane < L - d, pltpu.roll(mu, L - d, 1), -jnp.inf)
        else:
            shifted = jnp.where(lane >= d, pltpu.roll(mu, d, 1), -jnp.inf)
        mu = jnp.maximum(mu, shifted)
        d *= 2
    m_prev = m_sc[...]
    mu = jnp.maximum(mu, m_prev)
    m_t = b + mu
    end = 0 if reverse else L - 1
    mu_end = mu[:, end:end + 1]
    decay = jnp.exp(m_prev[:, 0:1] - mu_end)
    wk = jnp.exp(c - mu_end)
    m_sc[...] = jnp.broadcast_to(b[:, end:end + 1] + mu_end, m_prev.shape)
    cols = jnp.concatenate([mu, m_t, jnp.zeros((L - 2 * nh, L), F32)], axis=0).T
    k_t = k_ref[...].astype(F32).T
    rl = lax.broadcasted_iota(I32, (L, L), 0)
    cs = lax.broadcasted_iota(I32, (L, L), 1)
    allowed = (cs >= rl) if reverse else (cs <= rl)
    ones = jnp.ones((L, v_dim), BF16)
    for h in range(nh):
        mu_col = jnp.broadcast_to(cols[:, h:h + 1], (L, L))
        mt_col = jnp.broadcast_to(cols[:, nh + h:nh + h + 1], (L, v_dim))
        p = jnp.exp(jnp.where(allowed, c[h:h + 1, :] - mu_col, -jnp.inf))
        w_inter = jnp.exp(m_prev[h:h + 1, :] - mu_col)
        qh = q_ref[:, h * qk_dim:(h + 1) * qk_dim]
        kt_h = k_t[h * qk_dim:(h + 1) * qk_dim, :]
        v_aug = jnp.concatenate([v_ref[:, h * v_dim:(h + 1) * v_dim], ones], axis=1)
        s = jnp.dot(qh, kt_h.astype(BF16), preferred_element_type=F32) * p
        state = s_sc[h]
        nd = (jnp.dot(s.astype(BF16), v_aug, preferred_element_type=F32)
              + jnp.concatenate([w_inter, w_inter], axis=1)
              * jnp.dot(qh, state.astype(BF16), preferred_element_type=F32))
        num = nd[:, :v_dim]
        den = nd[:, v_dim:]
        h_ref[:, h * v_dim:(h + 1) * v_dim] = num / jnp.maximum(jnp.abs(den), jnp.exp(-mt_col))
        kw_t = (kt_h * wk[h:h + 1, :]).astype(BF16)
        s_sc[h] = decay[h:h + 1, :] * state + jnp.dot(kw_t, v_aug, preferred_element_type=F32)


def _ml_scan_kernel(qf_ref, kf_ref, vf_ref, gf_ref, qb_ref, kb_ref, vb_ref, gb_ref, hf_ref, hb_ref, s_sc, m_sc):
    @pl.when(pl.program_id(1) == 0)
    def _():
        s_sc[...] = jnp.zeros_like(s_sc)
        m_sc[...] = jnp.zeros_like(m_sc)

    _ml_chunk(qf_ref, kf_ref, vf_ref, gf_ref, hf_ref, s_sc.at[0], m_sc.at[0], False)
    _ml_chunk(qb_ref, kb_ref, vb_ref, gb_ref, hb_ref, s_sc.at[1], m_sc.at[1], True)


def _ml_scan(q, k, v, g, dims):
    B, S, C, D = dims
    L = ML_CHUNK
    assert L == LANES
    nc_ctx = C // L
    nc_lat = S // L
    t_all = q.shape[0]
    qk_dim = q.shape[1] // ML_HEADS
    v_dim = v.shape[1] // ML_HEADS

    def index(reverse):
        def idx(b, c):
            in_ctx = c < nc_ctx
            cl = c - nc_ctx
            if reverse:
                ctx_blk = (B * S + b * C) // L + (nc_ctx - 1 - c)
                lat_blk = (b * S) // L + (nc_lat - 1 - cl)
            else:
                ctx_blk = (B * S + b * C) // L + c
                lat_blk = (b * S) // L + cl
            return (jnp.where(in_ctx, ctx_blk, lat_blk), 0)
        return idx

    specs = lambda rev: [pl.BlockSpec((L, w), index(rev)) for w in (q.shape[1], k.shape[1], v.shape[1], LANES)]
    out_spec = lambda rev: pl.BlockSpec((L, v.shape[1]), index(rev))
    out = jax.ShapeDtypeStruct((t_all, v.shape[1]), F32)
    return pl.pallas_call(
        _ml_scan_kernel,
        grid=(B, nc_ctx + nc_lat),
        in_specs=specs(False) + specs(True),
        out_specs=[out_spec(False), out_spec(True)],
        out_shape=[out, out],
        scratch_shapes=[pltpu.VMEM((2, ML_HEADS, qk_dim, 2 * v_dim), F32),
                        pltpu.VMEM((2, ML_HEADS, LANES), F32)],
        compiler_params=_params(2),
        name="mlstm_scan",
    )(q, k, v, g, q, k, v, g)


def kernel(x, c, ctx, c_ctx, ada_w, ada_b, ln_g, ln_b, attn_w_qkv, attn_b_qkv, attn_sink, attn_w_o, attn_b_o,
           ml_w_in, ml_b_in, ml_conv_w, ml_conv_b, ml_norm_g, ml_w_out, router_w, router_b,
           exp_w_gu, exp_b_gu, exp_w_down, exp_b_down):
    B, S, D = x.shape
    C = ctx.shape[1]
    depth = ada_w.shape[0]
    dims = (B, S, C, D)
    alpha = (2.0 * depth) ** 0.25
    n_lat = B * S

    cvec = jnp.zeros((SUBLANES, D), F32).at[:B].set(c).at[B].set(c_ctx)
    mods = _adaln(cvec, ada_w, ada_b)
    x_all = jnp.concatenate([x.reshape(n_lat, D), ctx.reshape(B * C, D)], axis=0)

    q, k, v = _attn_qkv(x_all, mods[0], attn_w_qkv[0], attn_b_qkv[0], dims)
    o = _attention(q, k, v, attn_sink[0], dims)
    x1, hx, topi, gates = _post_mixer((o,), None, x_all, mods[0], attn_w_o[0], attn_b_o[0], ln_g[0, 0], ln_b[0, 0],
                                      router_w[0], router_b[0], dims, x_all.shape[0], alpha)
    x_all = _moe(hx, topi, gates, x1, mods[0], ln_g[0, 1], ln_b[0, 1],
                 0, exp_w_gu, exp_b_gu, exp_w_down, exp_b_down, dims, alpha)

    q, k, v, og, g = _ml_in(x_all, mods[1], ml_w_in[0], ml_b_in[0], ml_conv_w[0], ml_conv_b[0], dims)
    hf, hb = _ml_scan(q, k, v, g, dims)
    zero_b = jnp.zeros((D,), F32)
    x1, hx, topi, gates = _post_mixer((hf, hb, og), ml_norm_g[0], x_all, mods[1], ml_w_out[0], zero_b,
                                      ln_g[1, 0], ln_b[1, 0], router_w[1], router_b[1], dims, n_lat, alpha)
    out = _moe(hx, topi, gates, x1, mods[1], ln_g[1, 1], ln_b[1, 1],
               1, exp_w_gu, exp_b_gu, exp_w_down, exp_b_down, dims, alpha)
    return out.reshape(B, S, D)
```

```python
import functools

import jax
import jax.numpy as jnp
from jax import lax
from jax.experimental import pallas as pl
from jax.experimental.pallas import tpu as pltpu

F32 = jnp.float32
BF16 = jnp.bfloat16
I32 = jnp.int32
HIGHEST = lax.Precision.HIGHEST

GRID_W = 64
ATTN_HEAD_DIM = 64
ATTN_KV_HEADS = 4
WINDOW = 128
ATTN_BLOCK = 128
ROPE_THETA = 10000.0
ML_HEADS = 8
ML_CONV_W = 5
GATE_CAP = 15.0
TOP_K = 4
SWIGLU_ALPHA = 1.702
SWIGLU_LIMIT = 7.0
LN_EPS = 1e-5

LANES = 128
SUBLANES = 8
VMEM_LIMIT = 56 * 1024 * 1024
EXPERT_TILE = 512
ROUTE_TILE = 256
ML_CHUNK = 128
STAGE_ROWS = ROUTE_TILE * TOP_K + 2 * LANES
BIG_UNIT = 4 * SUBLANES
TABLE_W = 4 * LANES
ZERO_ROWS = 64


def _params(n_axes, vmem=VMEM_LIMIT):
    return pltpu.CompilerParams(dimension_semantics=("arbitrary",) * n_axes, vmem_limit_bytes=vmem)


def _layer_norm(r, g, b):
    mu = jnp.mean(r, axis=-1, keepdims=True)
    rc = r - mu
    var = jnp.mean(rc * rc, axis=-1, keepdims=True)
    return rc * lax.rsqrt(var + LN_EPS) * g + b


def _sigmoid(x):
    return 1.0 / (1.0 + jnp.exp(-x))


def _mod_row(i, n_lat_tiles, tiles_per_batch, ctx_row):
    return jnp.where(i < n_lat_tiles, i // tiles_per_batch, ctx_row)


def _adaln_kernel(c_ref, w_ref, b_ref, o_ref):
    c = c_ref[...]
    s = c * _sigmoid(c)
    o_ref[0] = jnp.dot(s, w_ref[0], precision=HIGHEST, preferred_element_type=F32) + b_ref[0]


def _adaln(cvec, ada_w, ada_b):
    depth, d, n = ada_w.shape
    tn = 1536
    return pl.pallas_call(
        _adaln_kernel,
        grid=(depth, n // tn),
        in_specs=[pl.BlockSpec((SUBLANES, d), lambda l, j: (0, 0)),
                  pl.BlockSpec((1, d, tn), lambda l, j: (l, 0, j)),
                  pl.BlockSpec((1, 1, tn), lambda l, j: (l, 0, j))],
        out_specs=pl.BlockSpec((1, SUBLANES, tn), lambda l, j: (l, 0, j)),
        out_shape=jax.ShapeDtypeStruct((depth, SUBLANES, n), F32),
        compiler_params=_params(2),
        name="adaln",
    )(cvec, ada_w, ada_b.reshape(depth, 1, n))


def _token_specs(tm, d, n_lat_tiles):
    return [pl.BlockSpec((tm, d), lambda i, *_: (jnp.minimum(i, n_lat_tiles - 1), 0)),
            pl.BlockSpec((tm, d), lambda i, *_: (jnp.maximum(i - n_lat_tiles, 0), 0))]


def _qkv_kernel(x_ref, c_ref, mod_ref, w_ref, b_ref, cos_ref, sin_ref, q_ref, k_ref, vt_ref, *,
                n_lat_tiles, tiles_per_batch, ctx_row, d, qd, kvd):
    i = pl.program_id(0)
    row = _mod_row(i, n_lat_tiles, tiles_per_batch, ctx_row)
    shift = mod_ref[pl.ds(row, 1), 0:d]
    scale = mod_ref[pl.ds(row, 1), d:2 * d]
    x = jnp.where(i < n_lat_tiles, x_ref[...], c_ref[...])
    h = x * (1.0 + scale) + shift
    z = jnp.dot(h.astype(BF16), w_ref[...], preferred_element_type=F32) + b_ref[...]
    nrot = qd + kvd
    qk = z[:, :nrot]
    reps = nrot // LANES
    cos = jnp.concatenate([cos_ref[...]] * reps, axis=1)
    sin = jnp.concatenate([sin_ref[...]] * reps, axis=1)
    lane = lax.broadcasted_iota(I32, qk.shape, 1)
    low_half = (lane & 16) == 0
    partner = jnp.where(low_half, pltpu.roll(qk, nrot - 16, 1), pltpu.roll(qk, 16, 1))
    qk = qk * cos + partner * sin
    q_ref[...] = (qk[:, :qd] * (ATTN_HEAD_DIM ** -0.5)).astype(BF16)
    k_ref[...] = qk[:, qd:].astype(BF16)
    vt_ref[...] = z[:, nrot:].T.astype(BF16)


def _rope_tables(s_len, tm):
    half = ATTN_HEAD_DIM // 4
    freqs = ROPE_THETA ** (-jnp.arange(half, dtype=F32) / half)
    t = jnp.arange(s_len)
    rows = (t // GRID_W).astype(F32)[:, None] * freqs[None, :]
    cols = (t % GRID_W).astype(F32)[:, None] * freqs[None, :]
    ang = jnp.concatenate([rows, rows, cols, cols], axis=1)
    sign = jnp.tile(jnp.concatenate([-jnp.ones((half,), F32), jnp.ones((half,), F32)]), 2)
    cos = jnp.cos(ang)
    sin = jnp.sin(ang) * sign[None, :]
    reps = LANES // ATTN_HEAD_DIM
    cos = jnp.concatenate([jnp.tile(cos, (1, reps)), jnp.ones((tm, LANES), F32)], axis=0)
    sin = jnp.concatenate([jnp.tile(sin, (1, reps)), jnp.zeros((tm, LANES), F32)], axis=0)
    return cos, sin


def _attn_qkv(x_lat, x_ctx, mod, w_qkv, b_qkv, dims):
    B, S, C, D = dims
    t_all = x_lat.shape[0] + x_ctx.shape[0]
    tm = 512
    n_lat_tiles = B * S // tm
    tiles_per_batch = S // tm
    ncols = w_qkv.shape[1]
    kvd = ATTN_KV_HEADS * ATTN_HEAD_DIM
    qd = ncols - 2 * kvd
    cos, sin = _rope_tables(S, tm)

    def tab_idx(i):
        return (jnp.where(i < n_lat_tiles, i % tiles_per_batch, tiles_per_batch), 0)

    kern = functools.partial(_qkv_kernel, n_lat_tiles=n_lat_tiles, tiles_per_batch=tiles_per_batch,
                             ctx_row=B, d=D, qd=qd, kvd=kvd)
    return pl.pallas_call(
        kern,
        grid=(t_all // tm,),
        in_specs=_token_specs(tm, D, n_lat_tiles) + [
                  pl.BlockSpec(mod.shape, lambda i: (0, 0)),
                  pl.BlockSpec((D, ncols), lambda i: (0, 0)),
                  pl.BlockSpec((1, ncols), lambda i: (0, 0)),
                  pl.BlockSpec((tm, LANES), tab_idx),
                  pl.BlockSpec((tm, LANES), tab_idx)],
        out_specs=[pl.BlockSpec((tm, qd), lambda i: (i, 0)),
                   pl.BlockSpec((tm, kvd), lambda i: (i, 0)),
                   pl.BlockSpec((kvd, tm), lambda i: (0, i))],
        out_shape=[jax.ShapeDtypeStruct((t_all, qd), BF16),
                   jax.ShapeDtypeStruct((t_all, kvd), BF16),
                   jax.ShapeDtypeStruct((kvd, t_all), BF16)],
        compiler_params=_params(1),
        name="attn_qkv",
    )(x_lat, x_ctx, mod, w_qkv.astype(BF16), b_qkv.reshape(1, ncols), cos, sin)


def _attn_kernel(sink_ref, q_ref, kp_ref, ko_ref, kn_ref, kc_ref, vp_ref, vo_ref, vn_ref, vc_ref, o_ref,
                 bias_sc, s_sc, p_sc, ot_sc, *,
                 n_lat_steps, nb, s_len, c_len):
    j = pl.program_id(0)
    is_lat = j < n_lat_steps
    n = j % nb
    blk = ATTN_BLOCK
    nloc = 3 * blk
    nk = nloc + c_len
    ki = lax.broadcasted_iota(I32, (nloc, blk), 0)
    qj = lax.broadcasted_iota(I32, (nloc, blk), 1)
    kpos = n * blk - WINDOW + ki
    qpos = n * blk + qj
    local_ok = (jnp.abs(kpos - qpos) <= WINDOW) & (kpos >= 0) & (kpos < s_len) & is_lat
    bias_sc[...] = jnp.where(local_ok, 0.0, -jnp.inf)
    hd = ATTN_HEAD_DIM
    group = q_ref.shape[1] // (ATTN_KV_HEADS * hd)
    kcat = jnp.concatenate([kp_ref[...], ko_ref[...], kn_ref[...], kc_ref[...]], axis=0)
    vcat_t = jnp.concatenate([vp_ref[...], vo_ref[...], vn_ref[...], vc_ref[...]], axis=1)
    for kh in range(ATTN_KV_HEADS):
        q_grp = jnp.concatenate([q_ref[:, (kh * group + g) * hd:(kh * group + g + 1) * hd]
                                 for g in range(group)], axis=0)
        s_sc[...] = lax.dot_general(kcat[:, kh * hd:(kh + 1) * hd], q_grp, (((1,), (1,)), ((), ())),
                                    preferred_element_type=F32)
        inv_l = []
        for g in range(group):
            cols = slice(g * blk, (g + 1) * blk)
            s_loc = s_sc[0:nloc, cols] + bias_sc[...]
            s_ctx = s_sc[nloc:nk, cols]
            sk = sink_ref[kh * group + g]
            m = jnp.maximum(jnp.maximum(jnp.max(s_loc, axis=0, keepdims=True),
                                        jnp.max(s_ctx, axis=0, keepdims=True)), sk)
            p_loc = jnp.exp(s_loc - m)
            p_ctx = jnp.exp(s_ctx - m)
            l = (jnp.sum(p_loc, axis=0, keepdims=True) + jnp.sum(p_ctx, axis=0, keepdims=True)
                 + jnp.exp(sk - m))
            p_sc[0:nloc, cols] = p_loc.astype(BF16)
            p_sc[nloc:nk, cols] = p_ctx.astype(BF16)
            inv_l.append(1.0 / l)
        o_t = jnp.dot(vcat_t[kh * hd:(kh + 1) * hd, :], p_sc[...], preferred_element_type=F32)
        o_t = o_t * jnp.concatenate(inv_l, axis=1)
        for g in range(group):
            h = kh * group + g
            ot_sc[h * hd:(h + 1) * hd, :] = o_t[:, g * blk:(g + 1) * blk]
    o_ref[...] = ot_sc[...].T.astype(BF16)


def _attention(q_all, k_all, vt_all, sink, dims):
    B, S, C, D = dims
    blk = ATTN_BLOCK
    nb = S // blk
    n_lat_steps = B * nb
    ctx_steps_per_batch = C // blk
    n_steps = n_lat_steps + B * ctx_steps_per_batch
    qd = q_all.shape[1]
    kvd = k_all.shape[1]

    def local_idx(off):
        def idx(j):
            b = j // nb
            nn = jnp.clip(j % nb + off, 0, nb - 1)
            return (jnp.where(j < n_lat_steps, b * nb + nn, j), 0)
        return idx

    def ctx_idx(j):
        b = jnp.where(j < n_lat_steps, j // nb, (j - n_lat_steps) // ctx_steps_per_batch)
        return (B * S // C + b, 0)

    swap = lambda f: (lambda j: f(j)[::-1])
    loc = lambda off: pl.BlockSpec((blk, kvd), local_idx(off))
    ctxs = pl.BlockSpec((C, kvd), ctx_idx)
    loc_t = lambda off: pl.BlockSpec((kvd, blk), swap(local_idx(off)))
    ctxs_t = pl.BlockSpec((kvd, C), swap(ctx_idx))
    group = qd // kvd
    nk = 3 * blk + C
    kern = functools.partial(_attn_kernel, n_lat_steps=n_lat_steps, nb=nb, s_len=S, c_len=C)
    return pl.pallas_call(
        kern,
        grid=(n_steps,),
        in_specs=[pl.BlockSpec(memory_space=pltpu.SMEM),
                  pl.BlockSpec((blk, qd), lambda j: (j, 0)),
                  loc(-1), loc(0), loc(1), ctxs,
                  loc_t(-1), loc_t(0), loc_t(1), ctxs_t],
        out_specs=pl.BlockSpec((blk, qd), lambda j: (j, 0)),
        out_shape=jax.ShapeDtypeStruct((q_all.shape[0], qd), BF16),
        scratch_shapes=[pltpu.VMEM((3 * blk, blk), F32),
                        pltpu.VMEM((nk, group * blk), F32),
                        pltpu.VMEM((nk, group * blk), BF16),
                        pltpu.VMEM((qd, blk), F32)],
        compiler_params=_params(1),
        name="attn_core",
    )(sink, q_all, k_all, k_all, k_all, k_all, vt_all, vt_all, vt_all, vt_all)


def _split_bf16(a):
    hi = a.astype(BF16)
    return hi, (a - hi.astype(F32)).astype(BF16)


def _dot_3pass(a, w_hi_ref, w_lo_ref):
    a_hi, a_lo = _split_bf16(a)
    w_hi = w_hi_ref[...]
    return (jnp.dot(a_hi, w_hi, preferred_element_type=F32)
            + jnp.dot(a_lo, w_hi, preferred_element_type=F32)
            + jnp.dot(a_hi, w_lo_ref[...], preferred_element_type=F32))


def _route(hx, rw_hi_ref, rw_lo_ref, rb_ref, topi_ref, gate_ref):
    logits = _dot_3pass(hx, rw_hi_ref, rw_lo_ref) + rb_ref[...]
    lane = lax.broadcasted_iota(I32, logits.shape, 1)
    lanef = lane.astype(F32)
    vals, idxs = [], []
    l = logits
    for _ in range(TOP_K):
        m = jnp.max(l, axis=1, keepdims=True)
        idx = jnp.min(jnp.where(l == m, lanef, float(LANES)), axis=1, keepdims=True)
        vals.append(m)
        idxs.append(idx)
        l = jnp.where(lanef == idx, -jnp.inf, l)
    es = [jnp.exp(v - vals[0]) for v in vals]
    denom = es[0]
    for e in es[1:]:
        denom = denom + e
    topi = jnp.zeros(logits.shape, F32)
    gates = jnp.zeros(logits.shape, F32)
    for k in range(TOP_K):
        topi = jnp.where(lane == k, idxs[k], topi)
        gates = jnp.where(lane == k, es[k] / denom, gates)
    topi_ref[...] = topi.astype(I32)
    gate_ref[...] = gates


def _post_common(a, x, mod_ref, w_ref, b_ref, lng_ref, lnb_ref, rw_ref, rwl_ref, rb_ref,
                 x1_ref, hx_ref, topi_ref, gate_ref, row, d, alpha):
    y = jnp.dot(a, w_ref[...], preferred_element_type=F32) + b_ref[...]
    gate_mix = mod_ref[pl.ds(row, 1), 2 * d:3 * d]
    shift = mod_ref[pl.ds(row, 1), 3 * d:4 * d]
    scale = mod_ref[pl.ds(row, 1), 4 * d:5 * d]
    x1 = _layer_norm(alpha * x + gate_mix * y, lng_ref[...], lnb_ref[...])
    hx = x1 * (1.0 + scale) + shift
    x1_ref[...] = x1
    hx_ref[...] = hx.astype(BF16)
    _route(hx, rw_ref, rwl_ref, rb_ref, topi_ref, gate_ref)


def _post_attn_kernel(o_ref, x_ref, c_ref, mod_ref, w_ref, b_ref, lng_ref, lnb_ref, rw_ref, rwl_ref, rb_ref,
                      x1_ref, hx_ref, topi_ref, gate_ref, *, n_lat_tiles, tiles_per_batch, ctx_row, d, alpha):
    i = pl.program_id(0)
    row = _mod_row(i, n_lat_tiles, tiles_per_batch, ctx_row)
    x = jnp.where(i < n_lat_tiles, x_ref[...], c_ref[...])
    _post_common(o_ref[...], x, mod_ref, w_ref, b_ref, lng_ref, lnb_ref, rw_ref, rwl_ref, rb_ref,
                 x1_ref, hx_ref, topi_ref, gate_ref, row, d, alpha)


def _post_mlstm_kernel(hf_ref, hb_ref, og_ref, ng_ref, x_ref, mod_ref, w_ref, b_ref, lng_ref, lnb_ref,
                       rw_ref, rwl_ref, rb_ref, x1_ref, hx_ref, topi_ref, gate_ref, *,
                       n_lat_tiles, tiles_per_batch, ctx_row, d, alpha):
    row = _mod_row(pl.program_id(0), n_lat_tiles, tiles_per_batch, ctx_row)
    hsum = hf_ref[...] + hb_ref[...]
    vdim = hsum.shape[1] // ML_HEADS
    parts = []
    for h in range(ML_HEADS):
        seg = hsum[:, h * vdim:(h + 1) * vdim]
        mu = jnp.mean(seg, axis=1, keepdims=True)
        sc = seg - mu
        var = jnp.mean(sc * sc, axis=1, keepdims=True)
        parts.append(sc * lax.rsqrt(var + LN_EPS))
    y = jnp.concatenate(parts, axis=1) * ng_ref[...]
    a = (og_ref[...].astype(F32) * y).astype(BF16)
    _post_common(a, x_ref[...], mod_ref, w_ref, b_ref, lng_ref, lnb_ref, rw_ref, rwl_ref, rb_ref,
                 x1_ref, hx_ref, topi_ref, gate_ref, row, d, alpha)


def _post_mixer(mixer_inputs, norm_g, x_all, mod, w_o, b_o, ln_g, ln_b, router_w, router_b, dims, n_rows, alpha):
    B, S, C, D = dims
    tm = 512
    n_lat_tiles = B * S // tm
    tiles_per_batch = S // tm
    n_exp = router_w.shape[1]
    rw_f32 = jnp.zeros((D, LANES), F32).at[:, :n_exp].set(router_w)
    rw = rw_f32.astype(BF16)
    rw_lo = (rw_f32 - rw.astype(F32)).astype(BF16)
    rb = jnp.full((1, LANES), -1e30, F32).at[0, :n_exp].set(router_b)
    row_spec = lambda w: pl.BlockSpec((tm, w), lambda i: (i, 0))
    full = lambda a: pl.BlockSpec(a.shape, lambda i: (0,) * a.ndim)
    common = dict(n_lat_tiles=n_lat_tiles, tiles_per_batch=tiles_per_batch, ctx_row=B, d=D, alpha=alpha)
    w_bf = w_o.astype(BF16)
    b2 = b_o.reshape(1, D)
    if isinstance(x_all, tuple):
        x_args, x_specs = list(x_all), _token_specs(tm, D, n_lat_tiles)
    else:
        x_args, x_specs = [x_all], [row_spec(D)]
    tail = x_args + [mod, w_bf, b2, ln_g.reshape(1, D), ln_b.reshape(1, D), rw, rw_lo, rb]
    tail_specs = x_specs + [full(mod), full(w_bf), full(b2), pl.BlockSpec((1, D), lambda i: (0, 0)),
                            pl.BlockSpec((1, D), lambda i: (0, 0)), full(rw), full(rw_lo), full(rb)]
    if norm_g is None:
        kern = functools.partial(_post_attn_kernel, **common)
        args = list(mixer_inputs) + tail
        specs = [row_spec(mixer_inputs[0].shape[1])] + tail_specs
        name = "post_attn"
    else:
        kern = functools.partial(_post_mlstm_kernel, **common)
        ng = norm_g.reshape(1, -1)
        args = list(mixer_inputs) + [ng] + tail
        specs = [row_spec(a.shape[1]) for a in mixer_inputs] + [full(ng)] + tail_specs
        name = "post_mlstm"
    return pl.pallas_call(
        kern,
        grid=(n_rows // tm,),
        in_specs=specs,
        out_specs=[row_spec(D), row_spec(D), row_spec(LANES), row_spec(LANES)],
        out_shape=[jax.ShapeDtypeStruct((n_rows, D), F32), jax.ShapeDtypeStruct((n_rows, D), BF16),
                   jax.ShapeDtypeStruct((n_rows, LANES), I32), jax.ShapeDtypeStruct((n_rows, LANES), F32)],
        compiler_params=_params(1),
        name=name,
    )(*args)


def _exclusive_lane_cumsum(row):
    r = lax.broadcasted_iota(I32, (LANES, LANES), 0)
    c = lax.broadcasted_iota(I32, (LANES, LANES), 1)
    before = jnp.where(r < c, 1.0, 0.0)
    return jnp.dot(jnp.broadcast_to(row, (SUBLANES, LANES)), before,
                   precision=HIGHEST, preferred_element_type=F32)[0:1]


def _for_each_unit(tab_ref, tile, fn):
    base = tile * TABLE_W
    for blk, rows in ((0, BIG_UNIT), (2, SUBLANES)):
        def body(j, carry, blk=blk, rows=rows):
            slot_row = pl.multiple_of(tab_ref[base + blk * LANES + j], SUBLANES)
            stage_row = pl.multiple_of(tab_ref[base + (blk + 1) * LANES + j], SUBLANES)
            fn(slot_row, stage_row, rows)
            return carry
        lax.fori_loop(0, tab_ref[base + blk * LANES + LANES - 1], body, 0)


def _slots_kernel(topi_ref, col_ref, colt_ref, tab_ref, meta_ref, tot_sc, base_sc, carry_sc, *, tile):
    phase = pl.program_id(0)
    i = pl.program_id(1)
    tm = topi_ref.shape[0]
    topi = topi_ref[...]
    lane = lax.broadcasted_iota(I32, topi.shape, 1)
    sel = [lane == topi[:, k:k + 1] for k in range(TOP_K)]
    maskf = jnp.where(sel[0], 1.0, 0.0)
    for s in sel[1:]:
        maskf = maskf + jnp.where(s, 1.0, 0.0)
    n8 = jnp.ceil(jnp.sum(maskf, axis=0, keepdims=True) / SUBLANES) * SUBLANES

    @pl.when((phase == 0) & (i == 0))
    def _():
        tot_sc[...] = jnp.zeros_like(tot_sc)

    @pl.when(phase == 0)
    def _():
        tot_sc[...] = tot_sc[...] + n8

    @pl.when((phase == 1) & (i == 0))
    def _():
        tot = tot_sc[...]
        padded = jnp.ceil(tot / tile) * tile
        base = _exclusive_lane_cumsum(padded)
        base_sc[...] = base
        carry_sc[...] = jnp.zeros_like(carry_sc)
        rowi = lax.broadcasted_iota(I32, meta_ref.shape, 0)
        meta_ref[...] = jnp.where(rowi == 0, tot, jnp.where(rowi == 1, base, padded)).astype(I32)

    @pl.when(phase == 1)
    def _():
        start = base_sc[...] + carry_sc[...]
        off = _exclusive_lane_cumsum(n8)
        r = lax.broadcasted_iota(I32, (tm, tm), 0)
        c = lax.broadcasted_iota(I32, (tm, tm), 1)
        earlier = jnp.where(c < r, 1.0, 0.0).astype(BF16)
        rank = jnp.dot(earlier, maskf.astype(BF16), preferred_element_type=F32)
        stage_row = rank + off
        out = jnp.zeros(topi.shape, F32)
        for k in range(TOP_K):
            pk = jnp.sum(jnp.where(sel[k], stage_row, 0.0), axis=1, keepdims=True)
            out = jnp.where(lane == k, pk, out)
        col_ref[...] = out.astype(I32)
        colt_ref[0] = out.T[0:SUBLANES].astype(I32)

        n_big = jnp.floor(n8 / BIG_UNIT)
        n_small = (n8 - BIG_UNIT * n_big) / SUBLANES
        first_big = _exclusive_lane_cumsum(n_big)
        first_small = _exclusive_lane_cumsum(n_small)
        rows = jnp.concatenate([off, start, first_big, n_big, first_small, n_small,
                                jnp.zeros((LANES - 6, LANES), F32)], axis=0)
        per_expert = rows.T
        off_c, start_c, fb_c, nb_c, fs_c, ns_c = (per_expert[:, k:k + 1] for k in range(6))
        u = lax.broadcasted_iota(I32, (LANES, LANES), 1).astype(F32)
        lane_t = lax.broadcasted_iota(I32, (1, LANES), 1)

        def unit_list(first_c, count_c, rel, count_row):
            inside = (u >= first_c) & (u < first_c + count_c)
            src = jnp.sum(jnp.where(inside, start_c + rel, 0.0), axis=0, keepdims=True)
            dst = jnp.sum(jnp.where(inside, off_c + rel, 0.0), axis=0, keepdims=True)
            total = jnp.sum(count_row, axis=1, keepdims=True)
            return [jnp.where(lane_t == LANES - 1, total, src), dst]

        lists = (unit_list(fb_c, nb_c, BIG_UNIT * (u - fb_c), n_big)
                 + unit_list(fs_c, ns_c, BIG_UNIT * nb_c + SUBLANES * (u - fs_c), n_small))
        tab_ref[0] = jnp.concatenate(lists, axis=1).astype(I32)
        carry_sc[...] = carry_sc[...] + n8


def _slots(topi, tile):
    t = topi.shape[0]
    tm = ROUTE_TILE
    nt = t // tm
    return pl.pallas_call(
        functools.partial(_slots_kernel, tile=float(tile)),
        grid=(2, nt),
        in_specs=[pl.BlockSpec((tm, LANES), lambda p, i: (i, 0))],
        out_specs=[pl.BlockSpec((tm, LANES), lambda p, i: (i * p, 0)),
                   pl.BlockSpec((1, SUBLANES, tm), lambda p, i: (i * p, 0, 0)),
                   pl.BlockSpec((1, 1, TABLE_W), lambda p, i: (i * p, 0, 0)),
                   pl.BlockSpec((SUBLANES, LANES), lambda p, i: (0, 0))],
        out_shape=[jax.ShapeDtypeStruct((t, LANES), I32),
                   jax.ShapeDtypeStruct((nt, SUBLANES, tm), I32),
                   jax.ShapeDtypeStruct((nt, 1, TABLE_W), I32),
                   jax.ShapeDtypeStruct((SUBLANES, LANES), I32)],
        scratch_shapes=[pltpu.VMEM((1, LANES), F32), pltpu.VMEM((1, LANES), F32), pltpu.VMEM((1, LANES), F32)],
        compiler_params=_params(2),
        name="moe_slots",
    )(topi)


def _dispatch_kernel(tab_ref, pad_start_ref, pad_units_ref, tail_ref, colt_ref, hx_ref, xs_ref, stage, zeros, sem, zsem):
    i = pl.program_id(0)
    nt = pl.num_programs(0)
    slot = i % 2
    kb = stage.shape[1]
    tm = hx_ref.shape[0]

    def scatter(buf_slot):
        def copy(slot_row, stage_row, rows):
            return pltpu.make_async_copy(stage.at[buf_slot, pl.ds(stage_row, rows)],
                                         xs_ref.at[pl.ds(slot_row, rows)], sem.at[buf_slot])
        return copy

    def drain(tile, buf_slot):
        _for_each_unit(tab_ref, tile, lambda *u: scatter(buf_slot)(*u).wait())

    @pl.when(i >= 2)
    def _():
        drain(i - 2, slot)

    colt = colt_ref[0]
    c = lax.broadcasted_iota(I32, (kb, tm), 0)
    onehot = jnp.zeros((kb, tm), F32)
    for k in range(TOP_K):
        onehot = jnp.where(c == colt[k:k + 1, :], 1.0, onehot)
    stage[slot] = jnp.dot(onehot.astype(BF16), hx_ref[...], preferred_element_type=F32)

    _for_each_unit(tab_ref, i, lambda *u: scatter(slot)(*u).start())

    @pl.when(i == nt - 1)
    def _():
        zeros[...] = jnp.zeros_like(zeros)
        n_exp = pad_start_ref.shape[0]

        def zero_copy(e, u):
            dst = pl.multiple_of(pad_start_ref[e] + u * SUBLANES, SUBLANES)
            return pltpu.make_async_copy(zeros.at[pl.ds(0, SUBLANES)], xs_ref.at[pl.ds(dst, SUBLANES)], zsem)

        def per_expert(fn):
            def outer(e, carry):
                def inner(u, carry2):
                    fn(e, u)
                    return carry2
                lax.fori_loop(0, pad_units_ref[e], inner, 0)
                return carry
            lax.fori_loop(0, n_exp, outer, 0)

        per_expert(lambda e, u: zero_copy(e, u).start())
        per_expert(lambda e, u: zero_copy(e, u).wait())

        zrows = zeros.shape[0]

        def tail_copy(u):
            dst = pl.multiple_of(tail_ref[0] + u * zrows, zrows)
            return pltpu.make_async_copy(zeros, xs_ref.at[pl.ds(dst, zrows)], zsem)

        def tail_loop(fn):
            def body(u, carry):
                fn(u)
                return carry
            lax.fori_loop(0, tail_ref[1], body, 0)

        tail_loop(lambda u: tail_copy(u).start())
        tail_loop(lambda u: tail_copy(u).wait())

        @pl.when(nt >= 2)
        def _():
            drain(i - 1, 1 - slot)
        drain(i, slot)


def _dispatch(tab, pad_start, pad_units, tail, colt, hx, n_slots):
    t, d = hx.shape
    tm = ROUTE_TILE
    return pl.pallas_call(
        _dispatch_kernel,
        grid_spec=pltpu.PrefetchScalarGridSpec(
            num_scalar_prefetch=4,
            grid=(t // tm,),
            in_specs=[pl.BlockSpec((1, SUBLANES, tm), lambda i, *_: (i, 0, 0)),
                      pl.BlockSpec((tm, d), lambda i, *_: (i, 0))],
            out_specs=pl.BlockSpec(memory_space=pl.ANY),
            scratch_shapes=[pltpu.VMEM((2, STAGE_ROWS, d), F32), pltpu.VMEM((ZERO_ROWS, d), F32),
                            pltpu.SemaphoreType.DMA((2,)), pltpu.SemaphoreType.DMA(())]),
        out_shape=jax.ShapeDtypeStruct((n_slots, d), F32),
        compiler_params=_params(1),
        name="moe_dispatch",
    )(tab, pad_start, pad_units, tail, colt, hx)


def _expert_kernel(te_ref, rows_ref, nu_ref, xs_ref, wgu_ref, bgu_ref, wd_ref, bd_ref, ys_ref, wgu_sc, wd_sc):
    j = pl.program_id(0)
    active = j < nu_ref[0]
    changed = (j == 0) | (te_ref[j] != te_ref[jnp.maximum(j - 1, 0)])
    ff = wd_ref.shape[0]
    te = xs_ref.shape[0]
    half = te // 2

    @pl.when(active & changed)
    def _():
        wgu_sc[...] = wgu_ref[...].astype(BF16)
        wd_sc[...] = wd_ref[...].astype(BF16)

    def run(rows):
        x = xs_ref[0:rows, :].astype(BF16)
        gu = jnp.dot(x, wgu_sc[...], preferred_element_type=F32) + bgu_ref[...]
        gl = jnp.minimum(gu[:, :ff], SWIGLU_LIMIT)
        lin = jnp.clip(gu[:, ff:], -SWIGLU_LIMIT, SWIGLU_LIMIT)
        act = gl * _sigmoid(SWIGLU_ALPHA * gl) * (lin + 1.0)
        ys_ref[0:rows, :] = jnp.dot(act.astype(BF16), wd_sc[...], preferred_element_type=F32) + bd_ref[...]
        if rows < te:
            ys_ref[rows:te, :] = jnp.zeros((te - rows, ys_ref.shape[1]), F32)

    valid_rows = rows_ref[j]

    @pl.when(active & (valid_rows > half))
    def _():
        run(te)

    @pl.when(active & (valid_rows <= half))
    def _():
        run(half)

    @pl.when(jnp.logical_not(active))
    def _():
        ys_ref[...] = jnp.zeros_like(ys_ref)


def _experts(tile_expert, tile_rows, n_used, xs, layer, w_gu, b_gu, w_down, b_down):
    n_slots, d = xs.shape
    depth, n_exp, _, ff2 = w_gu.shape
    ff = w_down.shape[2]
    te = EXPERT_TILE
    row_idx = lambda j, te_r, rows_r, nu: (jnp.maximum(jnp.minimum(j, nu[0] - 1), 0), 0)
    w_idx = lambda j, te_r, rows_r, nu: (layer, te_r[j], 0, 0)
    return pl.pallas_call(
        _expert_kernel,
        grid_spec=pltpu.PrefetchScalarGridSpec(
            num_scalar_prefetch=3,
            grid=(n_slots // te,),
            in_specs=[pl.BlockSpec((te, d), row_idx),
                      pl.BlockSpec((None, None, d, ff2), w_idx),
                      pl.BlockSpec((None, None, 1, ff2), w_idx),
                      pl.BlockSpec((None, None, ff, d), w_idx),
                      pl.BlockSpec((None, None, 1, d), w_idx)],
            out_specs=pl.BlockSpec((te, d), lambda j, te_r, rows_r, nu: (j, 0)),
            scratch_shapes=[pltpu.VMEM((d, ff2), BF16), pltpu.VMEM((ff, d), BF16)]),
        out_shape=jax.ShapeDtypeStruct((n_slots, d), F32),
        compiler_params=_params(1),
        name="moe_experts",
    )(tile_expert, tile_rows, n_used, xs, w_gu, b_gu.reshape(depth, n_exp, 1, ff2), w_down, b_down.reshape(depth, n_exp, 1, d))


def _combine_kernel(tab_ref, col_ref, gate_ref, ys_ref, x_ref, mod_ref, lng_ref, lnb_ref, out_ref, stage, sem, *,
                    n_lat_tiles, tiles_per_batch, ctx_row, d, alpha):
    i = pl.program_id(0)
    nt = pl.num_programs(0)
    slot = i % 2
    kb = stage.shape[1]
    tm = x_ref.shape[0]

    def gather(buf_slot):
        def copy(slot_row, stage_row, rows):
            return pltpu.make_async_copy(ys_ref.at[pl.ds(slot_row, rows)],
                                         stage.at[buf_slot, pl.ds(stage_row, rows)], sem.at[buf_slot])
        return copy

    @pl.when(i == 0)
    def _():
        stage[...] = jnp.zeros_like(stage)
        _for_each_unit(tab_ref, 0, lambda *u: gather(0)(*u).start())

    @pl.when(i + 1 < nt)
    def _():
        _for_each_unit(tab_ref, i + 1, lambda *u: gather(1 - slot)(*u).start())

    _for_each_unit(tab_ref, i, lambda *u: gather(slot)(*u).wait())

    col = col_ref[...]
    gates = gate_ref[...]
    c = lax.broadcasted_iota(I32, (tm, kb), 1)
    weights = jnp.zeros((tm, kb), F32)
    for k in range(TOP_K):
        weights = jnp.where(c == col[:, k:k + 1], gates[:, k:k + 1], weights)
    y = jnp.dot(weights.astype(BF16), stage[slot].astype(BF16), preferred_element_type=F32)
    row = _mod_row(i, n_lat_tiles, tiles_per_batch, ctx_row)
    gate_mlp = mod_ref[pl.ds(row, 1), 5 * d:6 * d]
    out_ref[...] = _layer_norm(alpha * x_ref[...] + gate_mlp * y, lng_ref[...], lnb_ref[...])


def _combine(tab, col, ys, gates, x1, mod, ln_g, ln_b, dims, alpha):
    B, S, C, D = dims
    t = x1.shape[0]
    tm = ROUTE_TILE
    kern = functools.partial(_combine_kernel, n_lat_tiles=B * S // tm, tiles_per_batch=S // tm, ctx_row=B,
                             d=D, alpha=alpha)
    return pl.pallas_call(
        kern,
        grid_spec=pltpu.PrefetchScalarGridSpec(
            num_scalar_prefetch=1,
            grid=(t // tm,),
            in_specs=[pl.BlockSpec((tm, LANES), lambda i, tab_r: (i, 0)),
                      pl.BlockSpec((tm, LANES), lambda i, tab_r: (i, 0)),
                      pl.BlockSpec(memory_space=pl.ANY),
                      pl.BlockSpec((tm, D), lambda i, tab_r: (i, 0)),
                      pl.BlockSpec(mod.shape, lambda i, tab_r: (0, 0)),
                      pl.BlockSpec((1, D), lambda i, tab_r: (0, 0)),
                      pl.BlockSpec((1, D), lambda i, tab_r: (0, 0))],
            out_specs=pl.BlockSpec((tm, D), lambda i, tab_r: (i, 0)),
            scratch_shapes=[pltpu.VMEM((2, STAGE_ROWS, D), F32), pltpu.SemaphoreType.DMA((2,))]),
        out_shape=jax.ShapeDtypeStruct((t, D), F32),
        compiler_params=_params(1),
        name="moe_combine",
    )(tab, col, gates, ys, x1, mod, ln_g.reshape(1, D), ln_b.reshape(1, D))


def _moe(hx, topi, gates, x1, mod, ln_g, ln_b, layer, w_gu, b_gu, w_down, b_down, dims, alpha):
    t = hx.shape[0]
    n_exp = w_gu.shape[1]
    te = EXPERT_TILE
    n_route_tiles = t // ROUTE_TILE
    max_rows = t * TOP_K + (SUBLANES - 1) * n_exp * n_route_tiles
    n_tiles = -(-max_rows // te) + n_exp
    col, colt, tab, meta = _slots(topi, te)
    tab = tab.reshape(-1)
    tot, base, padded = meta[0, :n_exp], meta[1, :n_exp], meta[2, :n_exp]
    ends = jnp.cumsum(padded // te)
    n_used = ends[-1:].astype(I32)
    tile_ids = jnp.minimum(jnp.arange(n_tiles, dtype=I32), n_used[0] - 1)
    tile_expert = jnp.sum((tile_ids[:, None] >= ends[None, :]).astype(I32), axis=1)
    tile_expert = jnp.minimum(tile_expert, n_exp - 1).astype(I32)
    pad_start = (base + tot).astype(I32)
    pad_units = ((padded - tot) // SUBLANES).astype(I32)
    used_rows = n_used[0] * te
    tail = jnp.stack([used_rows, (n_tiles * te - used_rows) // ZERO_ROWS]).astype(I32)
    xs = _dispatch(tab, pad_start, pad_units, tail, colt, hx, n_tiles * te)
    tile_rows = jnp.clip((base + tot)[tile_expert] - tile_ids * te, 0, te).astype(I32)
    ys = _experts(tile_expert, tile_rows, n_used, xs, layer, w_gu, b_gu, w_down, b_down)
    return _combine(tab, col, ys, gates, x1, mod, ln_g, ln_b, dims, alpha)


def _ml_in_kernel(xp_ref, x_ref, xn_ref, mod_ref, wqk_ref, bqk_ref, wvo_ref, bvo_ref, wg_ref, wgl_ref, bg_ref,
                  cw_ref, cb_ref, q_ref, k_ref, v_ref, og_ref, g_ref, *,
                  n_lat_tiles, tiles_per_batch, ctx_tiles_per_seq, ctx_row, d):
    i = pl.program_id(0)
    tm = x_ref.shape[0]
    halo = SUBLANES
    row = _mod_row(i, n_lat_tiles, tiles_per_batch, ctx_row)
    shift = mod_ref[pl.ds(row, 1), 0:d]
    scale = mod_ref[pl.ds(row, 1), d:2 * d]
    is_lat = i < n_lat_tiles
    seq_tile = jnp.where(is_lat, i % tiles_per_batch, (i - n_lat_tiles) % ctx_tiles_per_seq)
    seq_tiles = jnp.where(is_lat, tiles_per_batch, ctx_tiles_per_seq)
    first = seq_tile == 0
    last = seq_tile == seq_tiles - 1

    h = x_ref[...] * (1.0 + scale) + shift
    h_ext = jnp.concatenate([xp_ref[...] * (1.0 + scale) + shift, h, xn_ref[...] * (1.0 + scale) + shift], axis=0)
    z = jnp.dot(h_ext.astype(BF16), wqk_ref[...], preferred_element_type=F32) + bqk_ref[...]
    r = lax.broadcasted_iota(I32, z.shape, 0)
    outside = ((r < halo) & first) | ((r >= halo + tm) & last)
    z = jnp.where(outside, 0.0, z)
    n_ext = tm + 2 * halo
    cw = cw_ref[...]
    acc = None
    for j in range(ML_CONV_W):
        sh = (ML_CONV_W // 2 - j) % n_ext
        zj = z if sh == 0 else pltpu.roll(z, sh, 0)
        term = zj[halo:halo + tm] * cw[j:j + 1]
        acc = term if acc is None else acc + term
    qk = acc + cb_ref[...]
    qk = qk * _sigmoid(qk)
    nqk = qk.shape[1] // 2
    qk_dim = nqk // ML_HEADS
    q_ref[...] = (qk[:, :nqk] * (qk_dim ** -0.5)).astype(BF16)
    k_ref[...] = qk[:, nqk:].astype(BF16)

    vo = jnp.dot(h.astype(BF16), wvo_ref[...], preferred_element_type=F32) + bvo_ref[...]
    nv = vo.shape[1] // 2
    v_ref[...] = vo[:, :nv].astype(BF16)
    og_ref[...] = _sigmoid(vo[:, nv:]).astype(BF16)

    zg = _dot_3pass(h, wg_ref, wgl_ref) + bg_ref[...]
    g = GATE_CAP * jnp.tanh(zg / GATE_CAP)
    log_sig = jnp.minimum(g, 0.0) - jnp.log(1.0 + jnp.exp(-jnp.abs(g)))
    lane = lax.broadcasted_iota(I32, g.shape, 1)
    is_forget = ((lane // ML_HEADS) % 2) == 1
    g_ref[...] = jnp.where(is_forget, log_sig, g)


def _ml_in(x_all, mod, w_in, b_in, conv_w, conv_b, dims):
    B, S, C, D = dims
    t_all = x_all.shape[0]
    tm = 256
    n_lat_tiles = B * S // tm
    nqk2 = conv_w.shape[1]
    nv = (w_in.shape[1] - nqk2 - 4 * ML_HEADS) // 2
    ng = 4 * ML_HEADS
    w_qk = w_in[:, :nqk2].astype(BF16)
    w_vo = w_in[:, nqk2:nqk2 + 2 * nv].astype(BF16)
    w_g_f32 = jnp.zeros((D, LANES), F32).at[:, :ng].set(w_in[:, nqk2 + 2 * nv:])
    w_g = w_g_f32.astype(BF16)
    w_g_lo = (w_g_f32 - w_g.astype(F32)).astype(BF16)
    b_qk = b_in[:nqk2].reshape(1, -1)
    b_vo = b_in[nqk2:nqk2 + 2 * nv].reshape(1, -1)
    b_g = jnp.zeros((1, LANES), F32).at[0, :ng].set(b_in[nqk2 + 2 * nv:])
    cw = jnp.zeros((SUBLANES, nqk2), F32).at[:ML_CONV_W].set(conv_w)
    cb = conv_b.reshape(1, -1)
    hb = tm // SUBLANES
    n_hblk = t_all // SUBLANES
    full = lambda a: pl.BlockSpec(a.shape, lambda i: (0,) * a.ndim)
    row_spec = lambda w: pl.BlockSpec((tm, w), lambda i: (i, 0))
    kern = functools.partial(_ml_in_kernel, n_lat_tiles=n_lat_tiles, tiles_per_batch=S // tm,
                             ctx_tiles_per_seq=C // tm, ctx_row=B, d=D)
    return pl.pallas_call(
        kern,
        grid=(t_all // tm,),
        in_specs=[pl.BlockSpec((SUBLANES, D), lambda i: (jnp.maximum(i * hb - 1, 0), 0)),
                  row_spec(D),
                  pl.BlockSpec((SUBLANES, D), lambda i: (jnp.minimum((i + 1) * hb, n_hblk - 1), 0)),
                  full(mod), full(w_qk), full(b_qk), full(w_vo), full(b_vo), full(w_g), full(w_g_lo), full(b_g),
                  full(cw), full(cb)],
        out_specs=[row_spec(nqk2 // 2), row_spec(nqk2 // 2), row_spec(nv), row_spec(nv), row_spec(LANES)],
        out_shape=[jax.ShapeDtypeStruct((t_all, nqk2 // 2), BF16), jax.ShapeDtypeStruct((t_all, nqk2 // 2), BF16),
                   jax.ShapeDtypeStruct((t_all, nv), BF16), jax.ShapeDtypeStruct((t_all, nv), BF16),
                   jax.ShapeDtypeStruct((t_all, LANES), F32)],
        compiler_params=_params(1),
        name="mlstm_in",
    )(x_all, x_all, x_all, mod, w_qk, b_qk, w_vo, b_vo, w_g, w_g_lo, b_g, cw, cb)


def _ml_chunk(q_ref, k_ref, v_ref, g_ref, h_ref, s_sc, m_sc, reverse):
    L = q_ref.shape[0]
    qk_dim = q_ref.shape[1] // ML_HEADS
    v_dim = v_ref.shape[1] // ML_HEADS
    nh = ML_HEADS
    gates_t = g_ref[...].T
    sr = lax.broadcasted_iota(I32, (L, L), 0)
    lc = lax.broadcasted_iota(I32, (L, L), 1)
    upto = (sr >= lc) if reverse else (sr <= lc)
    cum_t = jnp.dot(gates_t, jnp.where(upto, 1.0, 0.0), precision=HIGHEST, preferred_element_type=F32)
    off = 2 * nh if reverse else 0
    li = gates_t[off:off + nh]
    b = cum_t[off + nh:off + 2 * nh]
    c = li - b
    lane = lax.broadcasted_iota(I32, c.shape, 1)
    mu = c
    d = 1
    while d < L:
        if reverse:
            shifted = jnp.where(lane < L - d, pltpu.roll(mu, L - d, 1), -jnp.inf)
        else:
            shifted = jnp.where(lane >= d, pltpu.roll(mu, d, 1), -jnp.inf)
        mu = jnp.maximum(mu, shifted)
        d *= 2
    m_prev = m_sc[...]
    mu = jnp.maximum(mu, m_prev)
    m_t = b + mu
    end = 0 if reverse else L - 1
    mu_end = mu[:, end:end + 1]
    decay = jnp.exp(m_prev[:, 0:1] - mu_end)
    wk = jnp.exp(c - mu_end)
    m_sc[...] = jnp.broadcast_to(b[:, end:end + 1] + mu_end, m_prev.shape)
    cols = jnp.concatenate([mu, m_t, jnp.zeros((L - 2 * nh, L), F32)], axis=0).T
    k_t = k_ref[...].astype(F32).T
    rl = lax.broadcasted_iota(I32, (L, L), 0)
    cs = lax.broadcasted_iota(I32, (L, L), 1)
    allowed = (cs >= rl) if reverse else (cs <= rl)
    ones = jnp.ones((L, v_dim), BF16)

    def head(h):
        mu_col = jnp.broadcast_to(cols[:, h:h + 1], (L, L))
        mt_col = jnp.broadcast_to(cols[:, nh + h:nh + h + 1], (L, v_dim))
        p = jnp.exp(jnp.where(allowed, c[h:h + 1, :] - mu_col, -jnp.inf))
        w_inter = jnp.exp(m_prev[h:h + 1, :] - mu_col)
        qh = q_ref[:, h * qk_dim:(h + 1) * qk_dim]
        kt_h = k_t[h * qk_dim:(h + 1) * qk_dim, :]
        v_aug = jnp.concatenate([v_ref[:, h * v_dim:(h + 1) * v_dim], ones], axis=1)
        s = jnp.dot(qh, kt_h.astype(BF16), preferred_element_type=F32) * p
        state = s_sc[h]
        nd = (jnp.dot(s.astype(BF16), v_aug, preferred_element_type=F32)
              + jnp.concatenate([w_inter, w_inter], axis=1)
              * jnp.dot(qh, state.astype(BF16), preferred_element_type=F32))
        num = nd[:, :v_dim]
        den = nd[:, v_dim:]
        h_ref[:, h * v_dim:(h + 1) * v_dim] = num / jnp.maximum(jnp.abs(den), jnp.exp(-mt_col))
        kw_t = (kt_h * wk[h:h + 1, :]).astype(BF16)
        s_sc[h] = decay[h:h + 1, :] * state + jnp.dot(kw_t, v_aug, preferred_element_type=F32)

    return head


def _ml_scan_kernel(*refs, n_batch):
    n_in = 8 * n_batch
    hf_ref, hb_ref, s_sc, m_sc = refs[n_in:]

    @pl.when(pl.program_id(0) == 0)
    def _():
        s_sc[...] = jnp.zeros_like(s_sc)
        m_sc[...] = jnp.zeros_like(m_sc)

    heads = []
    for b in range(n_batch):
        qf, kf, vf, gf, qb, kb, vb, gb = refs[8 * b:8 * b + 8]
        heads.append(_ml_chunk(qf, kf, vf, gf, hf_ref.at[b], s_sc.at[b, 0], m_sc.at[b, 0], False))
        heads.append(_ml_chunk(qb, kb, vb, gb, hb_ref.at[b], s_sc.at[b, 1], m_sc.at[b, 1], True))
    for h in range(ML_HEADS):
        for head in heads:
            head(h)


def _ml_scan(q, k, v, g, dims):
    B, S, C, D = dims
    L = ML_CHUNK
    assert L == LANES
    nc_ctx = C // L
    nc_lat = S // L
    qk_dim = q.shape[1] // ML_HEADS
    v_dim = v.shape[1] // ML_HEADS

    def in_index(b, reverse):
        def idx(c):
            in_ctx = c < nc_ctx
            cl = c - nc_ctx
            if reverse:
                ctx_blk = (B * S + b * C) // L + (nc_ctx - 1 - c)
                lat_blk = (b * S) // L + (nc_lat - 1 - cl)
            else:
                ctx_blk = (B * S + b * C) // L + c
                lat_blk = (b * S) // L + cl
            return (jnp.where(in_ctx, ctx_blk, lat_blk), 0)
        return idx

    def out_index(reverse):
        def idx(c):
            cl = jnp.maximum(c - nc_ctx, 0)
            return (0, nc_lat - 1 - cl if reverse else cl, 0)
        return idx

    widths = (q.shape[1], k.shape[1], v.shape[1], LANES)
    in_specs, args = [], []
    for b in range(B):
        for rev in (False, True):
            in_specs += [pl.BlockSpec((L, w), in_index(b, rev)) for w in widths]
            args += [q, k, v, g]
    out = jax.ShapeDtypeStruct((B, S, v.shape[1]), F32)
    return pl.pallas_call(
        functools.partial(_ml_scan_kernel, n_batch=B),
        grid=(nc_ctx + nc_lat,),
        in_specs=in_specs,
        out_specs=[pl.BlockSpec((B, L, v.shape[1]), out_index(False)),
                   pl.BlockSpec((B, L, v.shape[1]), out_index(True))],
        out_shape=[out, out],
        scratch_shapes=[pltpu.VMEM((B, 2, ML_HEADS, qk_dim, 2 * v_dim), F32),
                        pltpu.VMEM((B, 2, ML_HEADS, LANES), F32)],
        compiler_params=_params(1),
        name="mlstm_scan",
    )(*args)


def kernel(x, c, ctx, c_ctx, ada_w, ada_b, ln_g, ln_b, attn_w_qkv, attn_b_qkv, attn_sink, attn_w_o, attn_b_o,
           ml_w_in, ml_b_in, ml_conv_w, ml_conv_b, ml_norm_g, ml_w_out, router_w, router_b,
           exp_w_gu, exp_b_gu, exp_w_down, exp_b_down):
    B, S, D = x.shape
    C = ctx.shape[1]
    depth = ada_w.shape[0]
    dims = (B, S, C, D)
    alpha = (2.0 * depth) ** 0.25
    n_lat = B * S

    cvec = jnp.zeros((SUBLANES, D), F32).at[:B].set(c).at[B].set(c_ctx)
    mods = _adaln(cvec, ada_w, ada_b)
    x_lat, x_ctx = x.reshape(n_lat, D), ctx.reshape(B * C, D)

    q, k, v = _attn_qkv(x_lat, x_ctx, mods[0], attn_w_qkv[0], attn_b_qkv[0], dims)
    o = _attention(q, k, v, attn_sink[0], dims)
    x1, hx, topi, gates = _post_mixer((o,), None, (x_lat, x_ctx), mods[0], attn_w_o[0], attn_b_o[0],
                                      ln_g[0, 0], ln_b[0, 0], router_w[0], router_b[0], dims, n_lat + B * C, alpha)
    x_all = _moe(hx, topi, gates, x1, mods[0], ln_g[0, 1], ln_b[0, 1],
                 0, exp_w_gu, exp_b_gu, exp_w_down, exp_b_down, dims, alpha)

    q, k, v, og, g = _ml_in(x_all, mods[1], ml_w_in[0], ml_b_in[0], ml_conv_w[0], ml_conv_b[0], dims)
    hf, hb = (h.reshape(n_lat, -1) for h in _ml_scan(q, k, v, g, dims))
    zero_b = jnp.zeros((D,), F32)
    x1, hx, topi, gates = _post_mixer((hf, hb, og), ml_norm_g[0], x_all, mods[1], ml_w_out[0], zero_b,
                                      ln_g[1, 0], ln_b[1, 0], router_w[1], router_b[1], dims, n_lat, alpha)
    out = _moe(hx, topi, gates, x1, mods[1], ln_g[1, 1], ln_b[1, 1],
               1, exp_w_gu, exp_b_gu, exp_w_down, exp_b_down, dims, alpha)
    return out.reshape(B, S, D)
```

```python
import functools

import jax
import jax.numpy as jnp
from jax import lax
from jax.experimental import pallas as pl
from jax.experimental.pallas import tpu as pltpu

F32 = jnp.float32
BF16 = jnp.bfloat16
I32 = jnp.int32
HIGHEST = lax.Precision.HIGHEST

GRID_W = 64
ATTN_HEAD_DIM = 64
ATTN_KV_HEADS = 4
WINDOW = 128
ATTN_BLOCK = 128
ROPE_THETA = 10000.0
ML_HEADS = 8
ML_CONV_W = 5
GATE_CAP = 15.0
TOP_K = 4
SWIGLU_ALPHA = 1.702
SWIGLU_LIMIT = 7.0
LN_EPS = 1e-5

LANES = 128
SUBLANES = 8
VMEM_LIMIT = 56 * 1024 * 1024
EXPERT_TILE = 512
ROUTE_TILE = 256
ML_CHUNK = 128
STAGE_ROWS = ROUTE_TILE * TOP_K + 2 * LANES
BIG_UNIT = 4 * SUBLANES
TABLE_W = 4 * LANES
ZERO_ROWS = 64


def _params(n_axes, vmem=VMEM_LIMIT):
    return pltpu.CompilerParams(dimension_semantics=("arbitrary",) * n_axes, vmem_limit_bytes=vmem)


def _layer_norm(r, g, b):
    mu = jnp.mean(r, axis=-1, keepdims=True)
    rc = r - mu
    var = jnp.mean(rc * rc, axis=-1, keepdims=True)
    return rc * lax.rsqrt(var + LN_EPS) * g + b


def _sigmoid(x):
    return 1.0 / (1.0 + jnp.exp(-x))


def _mod_row(i, n_lat_tiles, tiles_per_batch, ctx_row):
    return jnp.where(i < n_lat_tiles, i // tiles_per_batch, ctx_row)


def _adaln_kernel(c_ref, w_ref, b_ref, o_ref):
    c = c_ref[...]
    s = c * _sigmoid(c)
    o_ref[0] = jnp.dot(s, w_ref[0], precision=HIGHEST, preferred_element_type=F32) + b_ref[0]


def _adaln(cvec, ada_w, ada_b):
    depth, d, n = ada_w.shape
    tn = 1536
    return pl.pallas_call(
        _adaln_kernel,
        grid=(depth, n // tn),
        in_specs=[pl.BlockSpec((SUBLANES, d), lambda l, j: (0, 0)),
                  pl.BlockSpec((1, d, tn), lambda l, j: (l, 0, j)),
                  pl.BlockSpec((1, 1, tn), lambda l, j: (l, 0, j))],
        out_specs=pl.BlockSpec((1, SUBLANES, tn), lambda l, j: (l, 0, j)),
        out_shape=jax.ShapeDtypeStruct((depth, SUBLANES, n), F32),
        compiler_params=_params(2),
        name="adaln",
    )(cvec, ada_w, ada_b.reshape(depth, 1, n))


def _token_specs(tm, d, n_lat_tiles):
    return [pl.BlockSpec((tm, d), lambda i, *_: (jnp.minimum(i, n_lat_tiles - 1), 0)),
            pl.BlockSpec((tm, d), lambda i, *_: (jnp.maximum(i - n_lat_tiles, 0), 0))]


def _qkv_kernel(x_ref, c_ref, mod_ref, w_ref, b_ref, cos_ref, sin_ref, q_ref, k_ref, vt_ref, *,
                n_lat_tiles, tiles_per_batch, ctx_row, d, qd, kvd):
    i = pl.program_id(0)
    row = _mod_row(i, n_lat_tiles, tiles_per_batch, ctx_row)
    shift = mod_ref[pl.ds(row, 1), 0:d]
    scale = mod_ref[pl.ds(row, 1), d:2 * d]
    x = jnp.where(i < n_lat_tiles, x_ref[...], c_ref[...])
    h = x * (1.0 + scale) + shift
    z = jnp.dot(h.astype(BF16), w_ref[...], preferred_element_type=F32) + b_ref[...]
    nrot = qd + kvd
    qk = z[:, :nrot]
    reps = nrot // LANES
    cos = jnp.concatenate([cos_ref[...]] * reps, axis=1)
    sin = jnp.concatenate([sin_ref[...]] * reps, axis=1)
    lane = lax.broadcasted_iota(I32, qk.shape, 1)
    low_half = (lane & 16) == 0
    partner = jnp.where(low_half, pltpu.roll(qk, nrot - 16, 1), pltpu.roll(qk, 16, 1))
    qk = qk * cos + partner * sin
    q_ref[...] = (qk[:, :qd] * (ATTN_HEAD_DIM ** -0.5)).astype(BF16)
    k_ref[...] = qk[:, qd:].astype(BF16)
    vt_ref[...] = z[:, nrot:].T.astype(BF16)


def _rope_tables(s_len, tm):
    half = ATTN_HEAD_DIM // 4
    freqs = ROPE_THETA ** (-jnp.arange(half, dtype=F32) / half)
    t = jnp.arange(s_len)
    rows = (t // GRID_W).astype(F32)[:, None] * freqs[None, :]
    cols = (t % GRID_W).astype(F32)[:, None] * freqs[None, :]
    ang = jnp.concatenate([rows, rows, cols, cols], axis=1)
    sign = jnp.tile(jnp.concatenate([-jnp.ones((half,), F32), jnp.ones((half,), F32)]), 2)
    cos = jnp.cos(ang)
    sin = jnp.sin(ang) * sign[None, :]
    reps = LANES // ATTN_HEAD_DIM
    cos = jnp.concatenate([jnp.tile(cos, (1, reps)), jnp.ones((tm, LANES), F32)], axis=0)
    sin = jnp.concatenate([jnp.tile(sin, (1, reps)), jnp.zeros((tm, LANES), F32)], axis=0)
    return cos, sin


def _attn_qkv(x_lat, x_ctx, mod, w_qkv, b_qkv, dims):
    B, S, C, D = dims
    t_all = x_lat.shape[0] + x_ctx.shape[0]
    tm = 512
    n_lat_tiles = B * S // tm
    tiles_per_batch = S // tm
    ncols = w_qkv.shape[1]
    kvd = ATTN_KV_HEADS * ATTN_HEAD_DIM
    qd = ncols - 2 * kvd
    cos, sin = _rope_tables(S, tm)

    def tab_idx(i):
        return (jnp.where(i < n_lat_tiles, i % tiles_per_batch, tiles_per_batch), 0)

    kern = functools.partial(_qkv_kernel, n_lat_tiles=n_lat_tiles, tiles_per_batch=tiles_per_batch,
                             ctx_row=B, d=D, qd=qd, kvd=kvd)
    return pl.pallas_call(
        kern,
        grid=(t_all // tm,),
        in_specs=_token_specs(tm, D, n_lat_tiles) + [
                  pl.BlockSpec(mod.shape, lambda i: (0, 0)),
                  pl.BlockSpec((D, ncols), lambda i: (0, 0)),
                  pl.BlockSpec((1, ncols), lambda i: (0, 0)),
                  pl.BlockSpec((tm, LANES), tab_idx),
                  pl.BlockSpec((tm, LANES), tab_idx)],
        out_specs=[pl.BlockSpec((tm, qd), lambda i: (i, 0)),
                   pl.BlockSpec((tm, kvd), lambda i: (i, 0)),
                   pl.BlockSpec((kvd, tm), lambda i: (0, i))],
        out_shape=[jax.ShapeDtypeStruct((t_all, qd), BF16),
                   jax.ShapeDtypeStruct((t_all, kvd), BF16),
                   jax.ShapeDtypeStruct((kvd, t_all), BF16)],
        compiler_params=_params(1),
        name="attn_qkv",
    )(x_lat, x_ctx, mod, w_qkv.astype(BF16), b_qkv.reshape(1, ncols), cos, sin)


def _attn_kernel(sink_ref, q_ref, kp_ref, ko_ref, kn_ref, kc_ref, vp_ref, vo_ref, vn_ref, vc_ref, o_ref,
                 bias_sc, s_sc, p_sc, ot_sc, *,
                 n_lat_steps, nb, s_len, c_len):
    j = pl.program_id(0)
    is_lat = j < n_lat_steps
    n = j % nb
    blk = ATTN_BLOCK
    nloc = 3 * blk
    nk = nloc + c_len
    ki = lax.broadcasted_iota(I32, (nloc, blk), 0)
    qj = lax.broadcasted_iota(I32, (nloc, blk), 1)
    kpos = n * blk - WINDOW + ki
    qpos = n * blk + qj
    local_ok = (jnp.abs(kpos - qpos) <= WINDOW) & (kpos >= 0) & (kpos < s_len) & is_lat
    bias_sc[...] = jnp.where(local_ok, 0.0, -jnp.inf)
    hd = ATTN_HEAD_DIM
    group = q_ref.shape[1] // (ATTN_KV_HEADS * hd)
    kcat = jnp.concatenate([kp_ref[...], ko_ref[...], kn_ref[...], kc_ref[...]], axis=0)
    vcat_t = jnp.concatenate([vp_ref[...], vo_ref[...], vn_ref[...], vc_ref[...]], axis=1)
    for kh in range(ATTN_KV_HEADS):
        q_grp = jnp.concatenate([q_ref[:, (kh * group + g) * hd:(kh * group + g + 1) * hd]
                                 for g in range(group)], axis=0)
        s_sc[...] = lax.dot_general(kcat[:, kh * hd:(kh + 1) * hd], q_grp, (((1,), (1,)), ((), ())),
                                    preferred_element_type=F32)
        inv_l = []
        for g in range(group):
            cols = slice(g * blk, (g + 1) * blk)
            s_loc = s_sc[0:nloc, cols] + bias_sc[...]
            s_ctx = s_sc[nloc:nk, cols]
            sk = sink_ref[kh * group + g]
            m = jnp.maximum(jnp.maximum(jnp.max(s_loc, axis=0, keepdims=True),
                                        jnp.max(s_ctx, axis=0, keepdims=True)), sk)
            p_loc = jnp.exp(s_loc - m)
            p_ctx = jnp.exp(s_ctx - m)
            l = (jnp.sum(p_loc, axis=0, keepdims=True) + jnp.sum(p_ctx, axis=0, keepdims=True)
                 + jnp.exp(sk - m))
            p_sc[0:nloc, cols] = p_loc.astype(BF16)
            p_sc[nloc:nk, cols] = p_ctx.astype(BF16)
            inv_l.append(1.0 / l)
        o_t = jnp.dot(vcat_t[kh * hd:(kh + 1) * hd, :], p_sc[...], preferred_element_type=F32)
        o_t = o_t * jnp.concatenate(inv_l, axis=1)
        for g in range(group):
            h = kh * group + g
            ot_sc[h * hd:(h + 1) * hd, :] = o_t[:, g * blk:(g + 1) * blk]
    o_ref[...] = ot_sc[...].T.astype(BF16)


def _attention(q_all, k_all, vt_all, sink, dims):
    B, S, C, D = dims
    blk = ATTN_BLOCK
    nb = S // blk
    n_lat_steps = B * nb
    ctx_steps_per_batch = C // blk
    n_steps = n_lat_steps + B * ctx_steps_per_batch
    qd = q_all.shape[1]
    kvd = k_all.shape[1]

    def local_idx(off):
        def idx(j):
            b = j // nb
            nn = jnp.clip(j % nb + off, 0, nb - 1)
            return (jnp.where(j < n_lat_steps, b * nb + nn, j), 0)
        return idx

    def ctx_idx(j):
        b = jnp.where(j < n_lat_steps, j // nb, (j - n_lat_steps) // ctx_steps_per_batch)
        return (B * S // C + b, 0)

    swap = lambda f: (lambda j: f(j)[::-1])
    loc = lambda off: pl.BlockSpec((blk, kvd), local_idx(off))
    ctxs = pl.BlockSpec((C, kvd), ctx_idx)
    loc_t = lambda off: pl.BlockSpec((kvd, blk), swap(local_idx(off)))
    ctxs_t = pl.BlockSpec((kvd, C), swap(ctx_idx))
    group = qd // kvd
    nk = 3 * blk + C
    kern = functools.partial(_attn_kernel, n_lat_steps=n_lat_steps, nb=nb, s_len=S, c_len=C)
    return pl.pallas_call(
        kern,
        grid=(n_steps,),
        in_specs=[pl.BlockSpec(memory_space=pltpu.SMEM),
                  pl.BlockSpec((blk, qd), lambda j: (j, 0)),
                  loc(-1), loc(0), loc(1), ctxs,
                  loc_t(-1), loc_t(0), loc_t(1), ctxs_t],
        out_specs=pl.BlockSpec((blk, qd), lambda j: (j, 0)),
        out_shape=jax.ShapeDtypeStruct((q_all.shape[0], qd), BF16),
        scratch_shapes=[pltpu.VMEM((3 * blk, blk), F32),
                        pltpu.VMEM((nk, group * blk), F32),
                        pltpu.VMEM((nk, group * blk), BF16),
                        pltpu.VMEM((qd, blk), F32)],
        compiler_params=_params(1),
        name="attn_core",
    )(sink, q_all, k_all, k_all, k_all, k_all, vt_all, vt_all, vt_all, vt_all)


def _split_bf16(a):
    hi = a.astype(BF16)
    return hi, (a - hi.astype(F32)).astype(BF16)


def _split_weight(w_f32):
    hi = w_f32.astype(BF16)
    return jnp.concatenate([hi, (w_f32 - hi.astype(F32)).astype(BF16)], axis=1)


def _dot_split(a, w_ref):
    a_hi, a_lo = _split_bf16(a)
    n = w_ref.shape[1] // 2
    both = jnp.dot(a_hi, w_ref[...], preferred_element_type=F32)
    return both[:, :n] + both[:, n:] + jnp.dot(a_lo, w_ref[:, :n], preferred_element_type=F32)


def _route(hx, rw_ref, rb_ref, topi_ref, gate_ref):
    logits = _dot_split(hx, rw_ref) + rb_ref[...]
    lane = lax.broadcasted_iota(I32, logits.shape, 1)
    lanef = lane.astype(F32)
    vals, idxs = [], []
    l = logits
    for _ in range(TOP_K):
        m = jnp.max(l, axis=1, keepdims=True)
        idx = jnp.min(jnp.where(l == m, lanef, float(LANES)), axis=1, keepdims=True)
        vals.append(m)
        idxs.append(idx)
        l = jnp.where(lanef == idx, -jnp.inf, l)
    es = [jnp.exp(v - vals[0]) for v in vals]
    denom = es[0]
    for e in es[1:]:
        denom = denom + e
    topi = jnp.zeros(logits.shape, F32)
    gates = jnp.zeros(logits.shape, F32)
    for k in range(TOP_K):
        topi = jnp.where(lane == k, idxs[k], topi)
        gates = jnp.where(lane == k, es[k] / denom, gates)
    topi_ref[...] = topi.astype(I32)
    gate_ref[...] = gates


def _post_common(a, x, mod_ref, w_ref, b_ref, lng_ref, lnb_ref, rw_ref, rb_ref,
                 x1_ref, hx_ref, topi_ref, gate_ref, row, d, alpha):
    y = jnp.dot(a, w_ref[...], preferred_element_type=F32) + b_ref[...]
    gate_mix = mod_ref[pl.ds(row, 1), 2 * d:3 * d]
    shift = mod_ref[pl.ds(row, 1), 3 * d:4 * d]
    scale = mod_ref[pl.ds(row, 1), 4 * d:5 * d]
    x1 = _layer_norm(alpha * x + gate_mix * y, lng_ref[...], lnb_ref[...])
    hx = x1 * (1.0 + scale) + shift
    x1_ref[...] = x1
    hx_ref[...] = hx.astype(BF16)
    _route(hx, rw_ref, rb_ref, topi_ref, gate_ref)


def _post_attn_kernel(o_ref, x_ref, c_ref, mod_ref, w_ref, b_ref, lng_ref, lnb_ref, rw_ref, rb_ref,
                      x1_ref, hx_ref, topi_ref, gate_ref, *, n_lat_tiles, tiles_per_batch, ctx_row, d, alpha):
    i = pl.program_id(0)
    row = _mod_row(i, n_lat_tiles, tiles_per_batch, ctx_row)
    x = jnp.where(i < n_lat_tiles, x_ref[...], c_ref[...])
    _post_common(o_ref[...], x, mod_ref, w_ref, b_ref, lng_ref, lnb_ref, rw_ref, rb_ref,
                 x1_ref, hx_ref, topi_ref, gate_ref, row, d, alpha)


def _post_mlstm_kernel(hf_ref, hb_ref, og_ref, ng_ref, x_ref, mod_ref, w_ref, b_ref, lng_ref, lnb_ref,
                       rw_ref, rb_ref, x1_ref, hx_ref, topi_ref, gate_ref, *,
                       n_lat_tiles, tiles_per_batch, ctx_row, d, alpha):
    row = _mod_row(pl.program_id(0), n_lat_tiles, tiles_per_batch, ctx_row)
    hsum = hf_ref[...] + hb_ref[...]
    vdim = hsum.shape[1] // ML_HEADS
    parts = []
    for h in range(ML_HEADS):
        seg = hsum[:, h * vdim:(h + 1) * vdim]
        mu = jnp.mean(seg, axis=1, keepdims=True)
        sc = seg - mu
        var = jnp.mean(sc * sc, axis=1, keepdims=True)
        parts.append(sc * lax.rsqrt(var + LN_EPS))
    y = jnp.concatenate(parts, axis=1) * ng_ref[...]
    a = (og_ref[...].astype(F32) * y).astype(BF16)
    _post_common(a, x_ref[...], mod_ref, w_ref, b_ref, lng_ref, lnb_ref, rw_ref, rb_ref,
                 x1_ref, hx_ref, topi_ref, gate_ref, row, d, alpha)


def _post_mixer(mixer_inputs, norm_g, x_all, mod, w_o, b_o, ln_g, ln_b, router_w, router_b, dims, n_rows, alpha):
    B, S, C, D = dims
    tm = 512
    n_lat_tiles = B * S // tm
    tiles_per_batch = S // tm
    n_exp = router_w.shape[1]
    rw = _split_weight(jnp.zeros((D, LANES), F32).at[:, :n_exp].set(router_w))
    rb =jnp.full((1, LANES), -1e30, F32).at[0, :n_exp].set(router_b)
    row_spec = lambda w: pl.BlockSpec((tm, w), lambda i: (i, 0))
    full = lambda a: pl.BlockSpec(a.shape, lambda i: (0,) * a.ndim)
    common = dict(n_lat_tiles=n_lat_tiles, tiles_per_batch=tiles_per_batch, ctx_row=B, d=D, alpha=alpha)
    w_bf = w_o.astype(BF16)
    b2 = b_o.reshape(1, D)
    if isinstance(x_all, tuple):
        x_args, x_specs = list(x_all), _token_specs(tm, D, n_lat_tiles)
    else:
        x_args, x_specs = [x_all], [row_spec(D)]
    tail = x_args + [mod, w_bf, b2, ln_g.reshape(1, D), ln_b.reshape(1, D), rw, rb]
    tail_specs = x_specs + [full(mod), full(w_bf), full(b2), pl.BlockSpec((1, D), lambda i: (0, 0)),
                            pl.BlockSpec((1, D), lambda i: (0, 0)), full(rw), full(rb)]
    if norm_g is None:
        kern = functools.partial(_post_attn_kernel, **common)
        args = list(mixer_inputs) + tail
        specs = [row_spec(mixer_inputs[0].shape[1])] + tail_specs
        name = "post_attn"
    else:
        kern = functools.partial(_post_mlstm_kernel, **common)
        ng = norm_g.reshape(1, -1)
        args = list(mixer_inputs) + [ng] + tail
        specs = [row_spec(a.shape[1]) for a in mixer_inputs] + [full(ng)] + tail_specs
        name = "post_mlstm"
    return pl.pallas_call(
        kern,
        grid=(n_rows // tm,),
        in_specs=specs,
        out_specs=[row_spec(D), row_spec(D), row_spec(LANES), row_spec(LANES)],
        out_shape=[jax.ShapeDtypeStruct((n_rows, D), F32), jax.ShapeDtypeStruct((n_rows, D), BF16),
                   jax.ShapeDtypeStruct((n_rows, LANES), I32), jax.ShapeDtypeStruct((n_rows, LANES), F32)],
        compiler_params=_params(1),
        name=name,
    )(*args)


def _exclusive_lane_cumsum(row):
    r = lax.broadcasted_iota(I32, (LANES, LANES), 0)
    c = lax.broadcasted_iota(I32, (LANES, LANES), 1)
    before = jnp.where(r < c, 1.0, 0.0)
    return jnp.dot(jnp.broadcast_to(row, (SUBLANES, LANES)), before,
                   precision=HIGHEST, preferred_element_type=F32)[0:1]


def _for_each_unit(tab_ref, tile, fn):
    base = tile * TABLE_W
    for blk, rows in ((0, BIG_UNIT), (2, SUBLANES)):
        def body(j, carry, blk=blk, rows=rows):
            slot_row = pl.multiple_of(tab_ref[base + blk * LANES + j], SUBLANES)
            stage_row = pl.multiple_of(tab_ref[base + (blk + 1) * LANES + j], SUBLANES)
            fn(slot_row, stage_row, rows)
            return carry
        lax.fori_loop(0, tab_ref[base + blk * LANES + LANES - 1], body, 0)


def _slots_kernel(topi_ref, col_ref, colt_ref, tab_ref, meta_ref, tot_sc, base_sc, carry_sc, *, tile):
    phase = pl.program_id(0)
    i = pl.program_id(1)
    tm = topi_ref.shape[0]
    topi = topi_ref[...]
    lane = lax.broadcasted_iota(I32, topi.shape, 1)
    sel = [lane == topi[:, k:k + 1] for k in range(TOP_K)]
    maskf = jnp.where(sel[0], 1.0, 0.0)
    for s in sel[1:]:
        maskf = maskf + jnp.where(s, 1.0, 0.0)
    n8 = jnp.ceil(jnp.sum(maskf, axis=0, keepdims=True) / SUBLANES) * SUBLANES

    @pl.when((phase == 0) & (i == 0))
    def _():
        tot_sc[...] = jnp.zeros_like(tot_sc)

    @pl.when(phase == 0)
    def _():
        tot_sc[...] = tot_sc[...] + n8

    @pl.when((phase == 1) & (i == 0))
    def _():
        tot = tot_sc[...]
        padded = jnp.ceil(tot / tile) * tile
        base = _exclusive_lane_cumsum(padded)
        base_sc[...] = base
        carry_sc[...] = jnp.zeros_like(carry_sc)
        rowi = lax.broadcasted_iota(I32, meta_ref.shape, 0)
        meta_ref[...] = jnp.where(rowi == 0, tot, jnp.where(rowi == 1, base, padded)).astype(I32)

    @pl.when(phase == 1)
    def _():
        start = base_sc[...] + carry_sc[...]
        off = _exclusive_lane_cumsum(n8)
        r = lax.broadcasted_iota(I32, (tm, tm), 0)
        c = lax.broadcasted_iota(I32, (tm, tm), 1)
        earlier = jnp.where(c < r, 1.0, 0.0).astype(BF16)
        rank = jnp.dot(earlier, maskf.astype(BF16), preferred_element_type=F32)
        stage_row = rank + off
        out = jnp.zeros(topi.shape, F32)
        for k in range(TOP_K):
            pk = jnp.sum(jnp.where(sel[k], stage_row, 0.0), axis=1, keepdims=True)
            out = jnp.where(lane == k, pk, out)
        col_ref[...] = out.astype(I32)
        colt_ref[0] = out.T[0:SUBLANES].astype(I32)

        n_big = jnp.floor(n8 / BIG_UNIT)
        n_small = (n8 - BIG_UNIT * n_big) / SUBLANES
        first_big = _exclusive_lane_cumsum(n_big)
        first_small = _exclusive_lane_cumsum(n_small)
        rows = jnp.concatenate([off, start, first_big, n_big, first_small, n_small,
                                jnp.zeros((LANES - 6, LANES), F32)], axis=0)
        per_expert = rows.T
        off_c, start_c, fb_c, nb_c, fs_c, ns_c = (per_expert[:, k:k + 1] for k in range(6))
        u = lax.broadcasted_iota(I32, (LANES, LANES), 1).astype(F32)
        lane_t = lax.broadcasted_iota(I32, (1, LANES), 1)

        def unit_list(first_c, count_c, rel, count_row):
            inside = (u >= first_c) & (u < first_c + count_c)
            src = jnp.sum(jnp.where(inside, start_c + rel, 0.0), axis=0, keepdims=True)
            dst = jnp.sum(jnp.where(inside, off_c + rel, 0.0), axis=0, keepdims=True)
            total = jnp.sum(count_row, axis=1, keepdims=True)
            return [jnp.where(lane_t == LANES - 1, total, src), dst]

        lists = (unit_list(fb_c, nb_c, BIG_UNIT * (u - fb_c), n_big)
                 + unit_list(fs_c, ns_c, BIG_UNIT * nb_c + SUBLANES * (u - fs_c), n_small))
        tab_ref[0] = jnp.concatenate(lists, axis=1).astype(I32)
        carry_sc[...] = carry_sc[...] + n8


def _slots(topi, tile):
    t = topi.shape[0]
    tm = ROUTE_TILE
    nt = t // tm
    return pl.pallas_call(
        functools.partial(_slots_kernel, tile=float(tile)),
        grid=(2, nt),
        in_specs=[pl.BlockSpec((tm, LANES), lambda p, i: (i, 0))],
        out_specs=[pl.BlockSpec((tm, LANES), lambda p, i: (i * p, 0)),
                   pl.BlockSpec((1, SUBLANES, tm), lambda p, i: (i * p, 0, 0)),
                   pl.BlockSpec((1, 1, TABLE_W), lambda p, i: (i * p, 0, 0)),
                   pl.BlockSpec((SUBLANES, LANES), lambda p, i: (0, 0))],
        out_shape=[jax.ShapeDtypeStruct((t, LANES), I32),
                   jax.ShapeDtypeStruct((nt, SUBLANES, tm), I32),
                   jax.ShapeDtypeStruct((nt, 1, TABLE_W), I32),
                   jax.ShapeDtypeStruct((SUBLANES, LANES), I32)],
        scratch_shapes=[pltpu.VMEM((1, LANES), F32), pltpu.VMEM((1, LANES), F32), pltpu.VMEM((1, LANES), F32)],
        compiler_params=_params(2),
        name="moe_slots",
    )(topi)


def _dispatch_kernel(tab_ref, pad_start_ref, pad_units_ref, tail_ref, colt_ref, hx_ref, xs_ref, stage, zeros, sem, zsem):
    i = pl.program_id(0)
    nt = pl.num_programs(0)
    slot = i % 2
    kb = stage.shape[1]
    tm = hx_ref.shape[0]

    def scatter(buf_slot):
        def copy(slot_row, stage_row, rows):
            return pltpu.make_async_copy(stage.at[buf_slot, pl.ds(stage_row, rows)],
                                         xs_ref.at[pl.ds(slot_row, rows)], sem.at[buf_slot])
        return copy

    def drain(tile, buf_slot):
        _for_each_unit(tab_ref, tile, lambda *u: scatter(buf_slot)(*u).wait())

    @pl.when(i >= 2)
    def _():
        drain(i - 2, slot)

    colt = colt_ref[0]
    c = lax.broadcasted_iota(I32, (kb, tm), 0)
    onehot = jnp.zeros((kb, tm), F32)
    for k in range(TOP_K):
        onehot = jnp.where(c == colt[k:k + 1, :], 1.0, onehot)
    stage[slot] = jnp.dot(onehot.astype(BF16), hx_ref[...], preferred_element_type=F32)

    _for_each_unit(tab_ref, i, lambda *u: scatter(slot)(*u).start())

    @pl.when(i == nt - 1)
    def _():
        zeros[...] = jnp.zeros_like(zeros)
        n_exp = pad_start_ref.shape[0]

        def zero_copy(e, u):
            dst = pl.multiple_of(pad_start_ref[e] + u * SUBLANES, SUBLANES)
            return pltpu.make_async_copy(zeros.at[pl.ds(0, SUBLANES)], xs_ref.at[pl.ds(dst, SUBLANES)], zsem)

        def per_expert(fn):
            def outer(e, carry):
                def inner(u, carry2):
                    fn(e, u)
                    return carry2
                lax.fori_loop(0, pad_units_ref[e], inner, 0)
                return carry
            lax.fori_loop(0, n_exp, outer, 0)

        per_expert(lambda e, u: zero_copy(e, u).start())
        per_expert(lambda e, u: zero_copy(e, u).wait())

        zrows = zeros.shape[0]

        def tail_copy(u):
            dst = pl.multiple_of(tail_ref[0] + u * zrows, zrows)
            return pltpu.make_async_copy(zeros, xs_ref.at[pl.ds(dst, zrows)], zsem)

        def tail_loop(fn):
            def body(u, carry):
                fn(u)
                return carry
            lax.fori_loop(0, tail_ref[1], body, 0)

        tail_loop(lambda u: tail_copy(u).start())
        tail_loop(lambda u: tail_copy(u).wait())

        @pl.when(nt >= 2)
        def _():
            drain(i - 1, 1 - slot)
        drain(i, slot)


def _dispatch(tab, pad_start, pad_units, tail, colt, hx, n_slots):
    t, d = hx.shape
    tm = ROUTE_TILE
    return pl.pallas_call(
        _dispatch_kernel,
        grid_spec=pltpu.PrefetchScalarGridSpec(
            num_scalar_prefetch=4,
            grid=(t // tm,),
            in_specs=[pl.BlockSpec((1, SUBLANES, tm), lambda i, *_: (i, 0, 0)),
                      pl.BlockSpec((tm, d), lambda i, *_: (i, 0))],
            out_specs=pl.BlockSpec(memory_space=pl.ANY),
            scratch_shapes=[pltpu.VMEM((2, STAGE_ROWS, d), F32), pltpu.VMEM((ZERO_ROWS, d), F32),
                            pltpu.SemaphoreType.DMA((2,)), pltpu.SemaphoreType.DMA(())]),
        out_shape=jax.ShapeDtypeStruct((n_slots, d), F32),
        compiler_params=_params(1),
        name="moe_dispatch",
    )(tab, pad_start, pad_units, tail, colt, hx)


def _expert_kernel(te_ref, rows_ref, next_ref, nu_ref, xs_ref, wgu_hbm, bgu_ref, wd_hbm, bd_ref, ys_ref,
                   wgu_f32, wd_f32, wgu_sc, wd_sc, wsem, *, layer):
    j = pl.program_id(0)
    active = j < nu_ref[0]
    changed = (j == 0) | (te_ref[j] != te_ref[jnp.maximum(j - 1, 0)])
    ff = wd_sc.shape[0]
    te = xs_ref.shape[0]
    half = te // 2

    def weight_copies(e):
        return (pltpu.make_async_copy(wgu_hbm.at[layer, e], wgu_f32, wsem.at[0]),
                pltpu.make_async_copy(wd_hbm.at[layer, e], wd_f32, wsem.at[1]))

    @pl.when(active & changed)
    def _():
        @pl.when(j == 0)
        def _():
            for cp in weight_copies(te_ref[0]):
                cp.start()

        for cp in weight_copies(te_ref[j]):
            cp.wait()
        wgu_sc[...] = wgu_f32[...].astype(BF16)
        wd_sc[...] = wd_f32[...].astype(BF16)

        @pl.when(next_ref[j] >= 0)
        def _():
            for cp in weight_copies(next_ref[j]):
                cp.start()

    def run(rows):
        x = xs_ref[0:rows, :].astype(BF16)
        gu = jnp.dot(x, wgu_sc[...], preferred_element_type=F32) + bgu_ref[...]
        gl = jnp.minimum(gu[:, :ff], SWIGLU_LIMIT)
        lin = jnp.clip(gu[:, ff:], -SWIGLU_LIMIT, SWIGLU_LIMIT)
        act = gl * _sigmoid(SWIGLU_ALPHA * gl) * (lin + 1.0)
        ys_ref[0:rows, :] = jnp.dot(act.astype(BF16), wd_sc[...], preferred_element_type=F32) + bd_ref[...]
        if rows < te:
            ys_ref[rows:te, :] = jnp.zeros((te - rows, ys_ref.shape[1]), F32)

    valid_rows = rows_ref[j]

    @pl.when(active & (valid_rows > half))
    def _():
        run(te)

    @pl.when(active & (valid_rows <= half))
    def _():
        run(half)

    @pl.when(jnp.logical_not(active))
    def _():
        ys_ref[...] = jnp.zeros_like(ys_ref)


def _experts(tile_expert, tile_rows, next_expert, n_used, xs, layer, w_gu, b_gu, w_down, b_down):
    n_slots, d = xs.shape
    depth, n_exp, _, ff2 = w_gu.shape
    ff = w_down.shape[2]
    te = EXPERT_TILE
    row_idx = lambda j, te_r, rows_r, nx_r, nu: (jnp.maximum(jnp.minimum(j, nu[0] - 1), 0), 0)
    b_idx = lambda j, te_r, rows_r, nx_r, nu: (layer, te_r[j], 0, 0)
    return pl.pallas_call(
        functools.partial(_expert_kernel, layer=layer),
        grid_spec=pltpu.PrefetchScalarGridSpec(
            num_scalar_prefetch=4,
            grid=(n_slots // te,),
            in_specs=[pl.BlockSpec((te, d), row_idx),
                      pl.BlockSpec(memory_space=pl.ANY),
                      pl.BlockSpec((None, None, 1, ff2), b_idx),
                      pl.BlockSpec(memory_space=pl.ANY),
                      pl.BlockSpec((None, None, 1, d), b_idx)],
            out_specs=pl.BlockSpec((te, d), lambda j, te_r, rows_r, nx_r, nu: (j, 0)),
            scratch_shapes=[pltpu.VMEM((d, ff2), F32), pltpu.VMEM((ff, d), F32),
                            pltpu.VMEM((d, ff2), BF16), pltpu.VMEM((ff, d), BF16),
                            pltpu.SemaphoreType.DMA((2,))]),
        out_shape=jax.ShapeDtypeStruct((n_slots, d), F32),
        compiler_params=_params(1),
        name="moe_experts",
    )(tile_expert, tile_rows, next_expert, n_used, xs, w_gu, b_gu.reshape(depth, n_exp, 1, ff2),
      w_down, b_down.reshape(depth, n_exp, 1, d))


def _combine_kernel(tab_ref, col_ref, gate_ref, ys_ref, x_ref, mod_ref, lng_ref, lnb_ref, out_ref, stage, sem, *,
                    n_lat_tiles, tiles_per_batch, ctx_row, d, alpha):
    i = pl.program_id(0)
    nt = pl.num_programs(0)
    slot = i % 2
    kb = stage.shape[1]
    tm = x_ref.shape[0]

    def gather(buf_slot):
        def copy(slot_row, stage_row, rows):
            return pltpu.make_async_copy(ys_ref.at[pl.ds(slot_row, rows)],
                                         stage.at[buf_slot, pl.ds(stage_row, rows)], sem.at[buf_slot])
        return copy

    @pl.when(i == 0)
    def _():
        stage[...] = jnp.zeros_like(stage)
        _for_each_unit(tab_ref, 0, lambda *u: gather(0)(*u).start())

    @pl.when(i + 1 < nt)
    def _():
        _for_each_unit(tab_ref, i + 1, lambda *u: gather(1 - slot)(*u).start())

    _for_each_unit(tab_ref, i, lambda *u: gather(slot)(*u).wait())

    col = col_ref[...]
    gates = gate_ref[...]
    c = lax.broadcasted_iota(I32, (tm, kb), 1)
    weights = jnp.zeros((tm, kb), F32)
    for k in range(TOP_K):
        weights = jnp.where(c == col[:, k:k + 1], gates[:, k:k + 1], weights)
    y = jnp.dot(weights.astype(BF16), stage[slot].astype(BF16), preferred_element_type=F32)
    row = _mod_row(i, n_lat_tiles, tiles_per_batch, ctx_row)
    gate_mlp = mod_ref[pl.ds(row, 1), 5 * d:6 * d]
    out_ref[...] = _layer_norm(alpha * x_ref[...] + gate_mlp * y, lng_ref[...], lnb_ref[...])


def _combine(tab, col, ys, gates, x1, mod, ln_g, ln_b, dims, alpha):
    B, S, C, D = dims
    t = x1.shape[0]
    tm = ROUTE_TILE
    kern = functools.partial(_combine_kernel, n_lat_tiles=B * S // tm, tiles_per_batch=S // tm, ctx_row=B,
                             d=D, alpha=alpha)
    return pl.pallas_call(
        kern,
        grid_spec=pltpu.PrefetchScalarGridSpec(
            num_scalar_prefetch=1,
            grid=(t // tm,),
            in_specs=[pl.BlockSpec((tm, LANES), lambda i, tab_r: (i, 0)),
                      pl.BlockSpec((tm, LANES), lambda i, tab_r: (i, 0)),
                      pl.BlockSpec(memory_space=pl.ANY),
                      pl.BlockSpec((tm, D), lambda i, tab_r: (i, 0)),
                      pl.BlockSpec(mod.shape, lambda i, tab_r: (0, 0)),
                      pl.BlockSpec((1, D), lambda i, tab_r: (0, 0)),
                      pl.BlockSpec((1, D), lambda i, tab_r: (0, 0))],
            out_specs=pl.BlockSpec((tm, D), lambda i, tab_r: (i, 0)),
            scratch_shapes=[pltpu.VMEM((2, STAGE_ROWS, D), F32), pltpu.SemaphoreType.DMA((2,))]),
        out_shape=jax.ShapeDtypeStruct((t, D), F32),
        compiler_params=_params(1),
        name="moe_combine",
    )(tab, col, gates, ys, x1, mod, ln_g.reshape(1, D), ln_b.reshape(1, D))


def _moe(hx, topi, gates, x1, mod, ln_g, ln_b, layer, w_gu, b_gu, w_down, b_down, dims, alpha):
    t = hx.shape[0]
    n_exp = w_gu.shape[1]
    te = EXPERT_TILE
    n_route_tiles = t // ROUTE_TILE
    max_rows = t * TOP_K + (SUBLANES - 1) * n_exp * n_route_tiles
    n_tiles = -(-max_rows // te) + n_exp
    col, colt, tab, meta = _slots(topi, te)
    tab = tab.reshape(-1)
    tot, base, padded = meta[0, :n_exp], meta[1, :n_exp], meta[2, :n_exp]
    ends = jnp.cumsum(padded // te)
    n_used = ends[-1:].astype(I32)
    tile_ids = jnp.minimum(jnp.arange(n_tiles, dtype=I32), n_used[0] - 1)
    tile_expert = jnp.sum((tile_ids[:, None] >= ends[None, :]).astype(I32), axis=1)
    tile_expert = jnp.minimum(tile_expert, n_exp - 1).astype(I32)
    pad_start = (base + tot).astype(I32)
    pad_units = ((padded - tot) // SUBLANES).astype(I32)
    used_rows = n_used[0] * te
    tail = jnp.stack([used_rows, (n_tiles * te - used_rows) // ZERO_ROWS]).astype(I32)
    xs = _dispatch(tab, pad_start, pad_units, tail, colt, hx, n_tiles * te)
    experts = jnp.arange(n_exp, dtype=I32)
    of_tile = tile_expert[:, None] == experts[None, :]
    pick = lambda per_expert: jnp.sum(jnp.where(of_tile, per_expert[None, :], 0), axis=1).astype(I32)
    tile_rows = jnp.clip(pick(base + tot) - tile_ids * te, 0, te).astype(I32)
    later = (padded[None, :] > 0) & (experts[None, :] > experts[:, None])
    next_of = jnp.min(jnp.where(later, experts[None, :], n_exp), axis=1)
    next_expert = pick(jnp.where(next_of == n_exp, -1, next_of))
    ys = _experts(tile_expert, tile_rows, next_expert, n_used, xs, layer, w_gu, b_gu, w_down, b_down)
    return _combine(tab, col, ys, gates, x1, mod, ln_g, ln_b, dims, alpha)


def _ml_in_kernel(xp_ref, x_ref, xn_ref, mod_ref, wqk_ref, bqk_ref, wvo_ref, bvo_ref, wg_ref, bg_ref,
                  cw_ref, cb_ref, q_ref, k_ref, v_ref, og_ref, g_ref, *,
                  n_lat_tiles, tiles_per_batch, ctx_tiles_per_seq, ctx_row, d):
    i = pl.program_id(0)
    tm = x_ref.shape[0]
    halo = SUBLANES
    row = _mod_row(i, n_lat_tiles, tiles_per_batch, ctx_row)
    shift = mod_ref[pl.ds(row, 1), 0:d]
    scale = mod_ref[pl.ds(row, 1), d:2 * d]
    is_lat = i < n_lat_tiles
    seq_tile = jnp.where(is_lat, i % tiles_per_batch, (i - n_lat_tiles) % ctx_tiles_per_seq)
    seq_tiles = jnp.where(is_lat, tiles_per_batch, ctx_tiles_per_seq)
    first = seq_tile == 0
    last = seq_tile == seq_tiles - 1

    h = x_ref[...] * (1.0 + scale) + shift
    h_ext = jnp.concatenate([xp_ref[...] * (1.0 + scale) + shift, h, xn_ref[...] * (1.0 + scale) + shift], axis=0)
    z = jnp.dot(h_ext.astype(BF16), wqk_ref[...], preferred_element_type=F32) + bqk_ref[...]
    n_ext = tm + 2 * halo
    cw = cw_ref[...]
    r = lax.broadcasted_iota(I32, z.shape, 0)
    z = jnp.where(((r < halo) & first) | ((r >= halo + tm) & last), 0.0, z)
    acc = None
    for j in range(ML_CONV_W):
        sh = (ML_CONV_W // 2 - j) % n_ext
        zj = z if sh == 0 else pltpu.roll(z, sh, 0)
        term = zj[halo:halo + tm] * cw[j:j + 1]
        acc = term if acc is None else acc + term
    qk = acc + cb_ref[...]
    qk = qk * _sigmoid(qk)
    nqk = qk.shape[1] // 2
    qk_dim = nqk // ML_HEADS
    q_ref[...] = (qk[:, :nqk] * (qk_dim ** -0.5)).astype(BF16)
    k_ref[...] = qk[:, nqk:].astype(BF16)

    vo = jnp.dot(h.astype(BF16), wvo_ref[...], preferred_element_type=F32) + bvo_ref[...]
    nv = vo.shape[1] // 2
    v_ref[...] = vo[:, :nv].astype(BF16)
    og_ref[...] = _sigmoid(vo[:, nv:]).astype(BF16)

    zg = _dot_split(h, wg_ref) + bg_ref[...]
    g = GATE_CAP * jnp.tanh(zg / GATE_CAP)
    log_sig = jnp.minimum(g, 0.0) - jnp.log(1.0 + jnp.exp(-jnp.abs(g)))
    lane = lax.broadcasted_iota(I32, g.shape, 1)
    is_forget = ((lane // ML_HEADS) % 2) == 1
    g_ref[...] = jnp.where(is_forget, log_sig, g)


def _ml_in(x_all, mod, w_in, b_in, conv_w, conv_b, dims):
    B, S, C, D = dims
    t_all = x_all.shape[0]
    tm = 256
    assert C % tm == 0 and S % tm == 0
    n_lat_tiles = B * S // tm
    nqk2 = conv_w.shape[1]
    nv = (w_in.shape[1] - nqk2 - 4 * ML_HEADS) // 2
    ng = 4 * ML_HEADS
    w_qk = w_in[:, :nqk2].astype(BF16)
    w_vo = w_in[:, nqk2:nqk2 + 2 * nv].astype(BF16)
    w_g = _split_weight(jnp.zeros((D, LANES), F32).at[:, :ng].set(w_in[:, nqk2 + 2 * nv:]))
    b_qk = b_in[:nqk2].reshape(1, -1)
    b_vo = b_in[nqk2:nqk2 + 2 * nv].reshape(1, -1)
    b_g = jnp.zeros((1, LANES), F32).at[0, :ng].set(b_in[nqk2 + 2 * nv:])
    cw = jnp.zeros((SUBLANES, nqk2), F32).at[:ML_CONV_W].set(conv_w)
    cb = conv_b.reshape(1, -1)
    hb = tm // SUBLANES
    n_hblk = t_all // SUBLANES
    full = lambda a: pl.BlockSpec(a.shape, lambda i: (0,) * a.ndim)
    row_spec = lambda w: pl.BlockSpec((tm, w), lambda i: (i, 0))
    kern = functools.partial(_ml_in_kernel, n_lat_tiles=n_lat_tiles, tiles_per_batch=S // tm,
                             ctx_tiles_per_seq=C // tm, ctx_row=B, d=D)
    return pl.pallas_call(
        kern,
        grid=(t_all // tm,),
        in_specs=[pl.BlockSpec((SUBLANES, D), lambda i: (jnp.maximum(i * hb - 1, 0), 0)),
                  row_spec(D),
                  pl.BlockSpec((SUBLANES, D), lambda i: (jnp.minimum((i + 1) * hb, n_hblk - 1), 0)),
                  full(mod), full(w_qk), full(b_qk), full(w_vo), full(b_vo), full(w_g), full(b_g),
                  full(cw), full(cb)],
        out_specs=[row_spec(nqk2 // 2), row_spec(nqk2 // 2), row_spec(nv), row_spec(nv), row_spec(LANES)],
        out_shape=[jax.ShapeDtypeStruct((t_all, nqk2 // 2), BF16), jax.ShapeDtypeStruct((t_all, nqk2 // 2), BF16),
                   jax.ShapeDtypeStruct((t_all, nv), BF16), jax.ShapeDtypeStruct((t_all, nv), BF16),
                   jax.ShapeDtypeStruct((t_all, LANES), F32)],
        compiler_params=_params(1),
        name="mlstm_in",
    )(x_all, x_all, x_all, mod, w_qk, b_qk, w_vo, b_vo, w_g, b_g, cw, cb)


def _ml_chunk(q_ref, k_ref, v_ref, g_ref, h_ref, s_sc, m_sc, reverse):
    L = q_ref.shape[0]
    qk_dim = q_ref.shape[1] // ML_HEADS
    v_dim = v_ref.shape[1] // ML_HEADS
    nh = ML_HEADS
    gates_t = g_ref[...].T
    sr = lax.broadcasted_iota(I32, (L, L), 0)
    lc = lax.broadcasted_iota(I32, (L, L), 1)
    upto = (sr >= lc) if reverse else (sr <= lc)
    cum_t = jnp.dot(gates_t, jnp.where(upto, 1.0, 0.0), precision=HIGHEST, preferred_element_type=F32)
    off = 2 * nh if reverse else 0
    li = gates_t[off:off + nh]
    b = cum_t[off + nh:off + 2 * nh]
    c = li - b
    lane = lax.broadcasted_iota(I32, c.shape, 1)
    mu = c
    d = 1
    while d < L:
        if reverse:
            shifted = jnp.where(lane < L - d, pltpu.roll(mu, L - d, 1), -jnp.inf)
        else:
            shifted = jnp.where(lane >= d, pltpu.roll(mu, d, 1), -jnp.inf)
        mu = jnp.maximum(mu, shifted)
        d *= 2
    m_prev = m_sc[...]
    mu = jnp.maximum(mu, m_prev)
    m_t = b + mu
    end = 0 if reverse else L - 1
    mu_end = mu[:, end:end + 1]
    decay = jnp.exp(m_prev[:, 0:1] - mu_end)
    wk = jnp.exp(c - mu_end)
    m_sc[...] = jnp.broadcast_to(b[:, end:end + 1] + mu_end, m_prev.shape)
    cols = jnp.concatenate([mu, m_t, jnp.zeros((L - 2 * nh, L), F32)], axis=0).T
    k_t = k_ref[...].astype(F32).T
    rl = lax.broadcasted_iota(I32, (L, L), 0)
    cs = lax.broadcasted_iota(I32, (L, L), 1)
    allowed = (cs >= rl) if reverse else (cs <= rl)
    ones = jnp.ones((L, v_dim), BF16)

    def head(h):
        mu_col = jnp.broadcast_to(cols[:, h:h + 1], (L, L))
        mt_col = jnp.broadcast_to(cols[:, nh + h:nh + h + 1], (L, v_dim))
        p = jnp.exp(jnp.where(allowed, c[h:h + 1, :] - mu_col, -jnp.inf))
        w_inter = jnp.exp(m_prev[h:h + 1, :] - mu_col)
        qh = q_ref[:, h * qk_dim:(h + 1) * qk_dim]
        kt_h = k_t[h * qk_dim:(h + 1) * qk_dim, :]
        v_aug = jnp.concatenate([v_ref[:, h * v_dim:(h + 1) * v_dim], ones], axis=1)
        s = jnp.dot(qh, kt_h.astype(BF16), preferred_element_type=F32) * p
        state = s_sc[h]
        nd = (jnp.dot(s.astype(BF16), v_aug, preferred_element_type=F32)
              + jnp.concatenate([w_inter, w_inter], axis=1)
              * jnp.dot(qh, state.astype(BF16), preferred_element_type=F32))
        num = nd[:, :v_dim]
        den = nd[:, v_dim:]
        h_ref[:, h * v_dim:(h + 1) * v_dim] = num / jnp.maximum(jnp.abs(den), jnp.exp(-mt_col))
        kw_t = (kt_h * wk[h:h + 1, :]).astype(BF16)
        s_sc[h] = decay[h:h + 1, :] * state + jnp.dot(kw_t, v_aug, preferred_element_type=F32)

    return head


def _ml_scan_kernel(*refs, n_batch):
    n_in = 8 * n_batch
    hf_ref, hb_ref, s_sc, m_sc = refs[n_in:]

    @pl.when(pl.program_id(0) == 0)
    def _():
        s_sc[...] = jnp.zeros_like(s_sc)
        m_sc[...] = jnp.zeros_like(m_sc)

    heads = []
    for b in range(n_batch):
        qf, kf, vf, gf, qb, kb, vb, gb = refs[8 * b:8 * b + 8]
        heads.append(_ml_chunk(qf, kf, vf, gf, hf_ref.at[b], s_sc.at[b, 0], m_sc.at[b, 0], False))
        heads.append(_ml_chunk(qb, kb, vb, gb, hb_ref.at[b], s_sc.at[b, 1], m_sc.at[b, 1], True))
    for h in range(ML_HEADS):
        for head in heads:
            head(h)


def _ml_scan(q, k, v, g, dims):
    B, S, C, D = dims
    L = ML_CHUNK
    assert L == LANES
    nc_ctx = C // L
    nc_lat = S // L
    qk_dim = q.shape[1] // ML_HEADS
    v_dim = v.shape[1] // ML_HEADS

    def in_index(b, reverse):
        def idx(c):
            in_ctx = c < nc_ctx
            cl = c - nc_ctx
            if reverse:
                ctx_blk = (B * S + b * C) // L + (nc_ctx - 1 - c)
                lat_blk = (b * S) // L + (nc_lat - 1 - cl)
            else:
                ctx_blk = (B * S + b * C) // L + c
                lat_blk = (b * S) // L + cl
            return (jnp.where(in_ctx, ctx_blk, lat_blk), 0)
        return idx

    def out_index(reverse):
        def idx(c):
            cl = jnp.maximum(c - nc_ctx, 0)
            return (0, nc_lat - 1 - cl if reverse else cl, 0)
        return idx

    widths = (q.shape[1], k.shape[1], v.shape[1], LANES)
    in_specs, args = [], []
    for b in range(B):
        for rev in (False, True):
            in_specs += [pl.BlockSpec((L, w), in_index(b, rev)) for w in widths]
            args += [q, k, v, g]
    out = jax.ShapeDtypeStruct((B, S, v.shape[1]), F32)
    return pl.pallas_call(
        functools.partial(_ml_scan_kernel, n_batch=B),
        grid=(nc_ctx + nc_lat,),
        in_specs=in_specs,
        out_specs=[pl.BlockSpec((B, L, v.shape[1]), out_index(False)),
                   pl.BlockSpec((B, L, v.shape[1]), out_index(True))],
        out_shape=[out, out],
        scratch_shapes=[pltpu.VMEM((B, 2, ML_HEADS, qk_dim, 2 * v_dim), F32),
                        pltpu.VMEM((B, 2, ML_HEADS, LANES), F32)],
        compiler_params=_params(1),
        name="mlstm_scan",
    )(*args)


def kernel(x, c, ctx, c_ctx, ada_w, ada_b, ln_g, ln_b, attn_w_qkv, attn_b_qkv, attn_sink, attn_w_o, attn_b_o,
           ml_w_in, ml_b_in, ml_conv_w, ml_conv_b, ml_norm_g, ml_w_out, router_w, router_b,
           exp_w_gu, exp_b_gu, exp_w_down, exp_b_down):
    B, S, D = x.shape
    C = ctx.shape[1]
    depth = ada_w.shape[0]
    dims = (B, S, C, D)
    alpha = (2.0 * depth) ** 0.25
    n_lat = B * S

    cvec = jnp.zeros((SUBLANES, D), F32).at[:B].set(c).at[B].set(c_ctx)
    mods = _adaln(cvec, ada_w, ada_b)
    x_lat, x_ctx = x.reshape(n_lat, D), ctx.reshape(B * C, D)

    q, k, v = _attn_qkv(x_lat, x_ctx, mods[0], attn_w_qkv[0], attn_b_qkv[0], dims)
    o = _attention(q, k, v, attn_sink[0], dims)
    x1, hx, topi, gates = _post_mixer((o,), None, (x_lat, x_ctx), mods[0], attn_w_o[0], attn_b_o[0],
                                      ln_g[0, 0], ln_b[0, 0], router_w[0], router_b[0], dims, n_lat + B * C, alpha)
    x_all = _moe(hx, topi, gates, x1, mods[0], ln_g[0, 1], ln_b[0, 1],
                 0, exp_w_gu, exp_b_gu, exp_w_down, exp_b_down, dims, alpha)

    q, k, v, og, g = _ml_in(x_all, mods[1], ml_w_in[0], ml_b_in[0], ml_conv_w[0], ml_conv_b[0], dims)
    hf, hb = (h.reshape(n_lat, -1) for h in _ml_scan(q, k, v, g, dims))
    zero_b = jnp.zeros((D,), F32)
    x1, hx, topi, gates = _post_mixer((hf, hb, og), ml_norm_g[0], x_all, mods[1], ml_w_out[0], zero_b,
                                      ln_g[1, 0], ln_b[1, 0], router_w[1], router_b[1], dims, n_lat, alpha)
    out = _moe(hx, topi, gates, x1, mods[1], ln_g[1, 1], ln_b[1, 1],
               1, exp_w_gu, exp_b_gu, exp_w_down, exp_b_down, dims, alpha)
    return out.reshape(B, S, D)
```

```python
import functools

import jax
import jax.numpy as jnp
from jax import lax
from jax.experimental import pallas as pl
from jax.experimental.pallas import tpu as pltpu

F32 = jnp.float32
BF16 = jnp.bfloat16
I32 = jnp.int32
HIGHEST = lax.Precision.HIGHEST

GRID_W = 64
ATTN_HEAD_DIM = 64
ATTN_KV_HEADS = 4
WINDOW = 128
ATTN_BLOCK = 128
ROPE_THETA = 10000.0
ML_HEADS = 8
ML_CONV_W = 5
GATE_CAP = 15.0
TOP_K = 4
SWIGLU_ALPHA = 1.702
SWIGLU_LIMIT = 7.0
LN_EPS = 1e-5

LANES = 128
SUBLANES = 8
VMEM_LIMIT = 56 * 1024 * 1024
EXPERT_TILE = 512
ROUTE_TILE = 256
ML_CHUNK = 128
STAGE_ROWS = ROUTE_TILE * TOP_K + 2 * LANES
BIG_UNIT = 4 * SUBLANES
TABLE_W = 4 * LANES
ZERO_ROWS = 64


def _params(n_axes, vmem=VMEM_LIMIT):
    return pltpu.CompilerParams(dimension_semantics=("arbitrary",) * n_axes, vmem_limit_bytes=vmem)


def _layer_norm(r, g, b):
    mu = jnp.mean(r, axis=-1, keepdims=True)
    rc = r - mu
    var = jnp.mean(rc * rc, axis=-1, keepdims=True)
    return rc * lax.rsqrt(var + LN_EPS) * g + b


def _sigmoid(x):
    return 1.0 / (1.0 + jnp.exp(-x))


def _mod_row(i, n_lat_tiles, tiles_per_batch, ctx_row):
    return jnp.where(i < n_lat_tiles, i // tiles_per_batch, ctx_row)


def _adaln_kernel(c_ref, w_ref, b_ref, o_ref):
    c = c_ref[...]
    s = c * _sigmoid(c)
    o_ref[0] = jnp.dot(s, w_ref[0], precision=HIGHEST, preferred_element_type=F32) + b_ref[0]


def _adaln(cvec, ada_w, ada_b):
    depth, d, n = ada_w.shape
    tn = 1536
    return pl.pallas_call(
        _adaln_kernel,
        grid=(depth, n // tn),
        in_specs=[pl.BlockSpec((SUBLANES, d), lambda l, j: (0, 0)),
                  pl.BlockSpec((1, d, tn), lambda l, j: (l, 0, j)),
                  pl.BlockSpec((1, 1, tn), lambda l, j: (l, 0, j))],
        out_specs=pl.BlockSpec((1, SUBLANES, tn), lambda l, j: (l, 0, j)),
        out_shape=jax.ShapeDtypeStruct((depth, SUBLANES, n), F32),
        compiler_params=_params(2),
        name="adaln",
    )(cvec, ada_w, ada_b.reshape(depth, 1, n))


def _token_specs(tm, d, n_lat_tiles):
    return [pl.BlockSpec((tm, d), lambda i, *_: (jnp.minimum(i, n_lat_tiles - 1), 0)),
            pl.BlockSpec((tm, d), lambda i, *_: (jnp.maximum(i - n_lat_tiles, 0), 0))]


def _qkv_kernel(x_ref, c_ref, mod_ref, w_ref, b_ref, cos_ref, sin_ref, q_ref, k_ref, vt_ref, *,
                n_lat_tiles, tiles_per_batch, ctx_row, d, qd, kvd):
    i = pl.program_id(0)
    row = _mod_row(i, n_lat_tiles, tiles_per_batch, ctx_row)
    shift = mod_ref[pl.ds(row, 1), 0:d]
    scale = mod_ref[pl.ds(row, 1), d:2 * d]
    x = jnp.where(i < n_lat_tiles, x_ref[...], c_ref[...])
    h = x * (1.0 + scale) + shift
    z = jnp.dot(h.astype(BF16), w_ref[...], preferred_element_type=F32) + b_ref[...]
    nrot = qd + kvd
    qk = z[:, :nrot]
    reps = nrot // LANES
    cos = jnp.concatenate([cos_ref[...]] * reps, axis=1)
    sin = jnp.concatenate([sin_ref[...]] * reps, axis=1)
    lane = lax.broadcasted_iota(I32, qk.shape, 1)
    low_half = (lane & 16) == 0
    partner = jnp.where(low_half, pltpu.roll(qk, nrot - 16, 1), pltpu.roll(qk, 16, 1))
    qk = qk * cos + partner * sin
    q_ref[...] = (qk[:, :qd] * (ATTN_HEAD_DIM ** -0.5)).astype(BF16)
    k_ref[...] = qk[:, qd:].astype(BF16)
    vt_ref[...] = z[:, nrot:].T.astype(BF16)


def _rope_tables(s_len, tm):
    half = ATTN_HEAD_DIM // 4
    freqs = ROPE_THETA ** (-jnp.arange(half, dtype=F32) / half)
    t = jnp.arange(s_len)
    rows = (t // GRID_W).astype(F32)[:, None] * freqs[None, :]
    cols = (t % GRID_W).astype(F32)[:, None] * freqs[None, :]
    ang = jnp.concatenate([rows, rows, cols, cols], axis=1)
    sign = jnp.tile(jnp.concatenate([-jnp.ones((half,), F32), jnp.ones((half,), F32)]), 2)
    cos = jnp.cos(ang)
    sin = jnp.sin(ang) * sign[None, :]
    reps = LANES // ATTN_HEAD_DIM
    cos = jnp.concatenate([jnp.tile(cos, (1, reps)), jnp.ones((tm, LANES), F32)], axis=0)
    sin = jnp.concatenate([jnp.tile(sin, (1, reps)), jnp.zeros((tm, LANES), F32)], axis=0)
    return cos, sin


def _attn_qkv(x_lat, x_ctx, mod, w_qkv, b_qkv, dims):
    B, S, C, D = dims
    t_all = x_lat.shape[0] + x_ctx.shape[0]
    tm = 512
    n_lat_tiles = B * S // tm
    tiles_per_batch = S // tm
    ncols = w_qkv.shape[1]
    kvd = ATTN_KV_HEADS * ATTN_HEAD_DIM
    qd = ncols - 2 * kvd
    cos, sin = _rope_tables(S, tm)

    def tab_idx(i):
        return (jnp.where(i < n_lat_tiles, i % tiles_per_batch, tiles_per_batch), 0)

    kern = functools.partial(_qkv_kernel, n_lat_tiles=n_lat_tiles, tiles_per_batch=tiles_per_batch,
                             ctx_row=B, d=D, qd=qd, kvd=kvd)
    return pl.pallas_call(
        kern,
        grid=(t_all // tm,),
        in_specs=_token_specs(tm, D, n_lat_tiles) + [
                  pl.BlockSpec(mod.shape, lambda i: (0, 0)),
                  pl.BlockSpec((D, ncols), lambda i: (0, 0)),
                  pl.BlockSpec((1, ncols), lambda i: (0, 0)),
                  pl.BlockSpec((tm, LANES), tab_idx),
                  pl.BlockSpec((tm, LANES), tab_idx)],
        out_specs=[pl.BlockSpec((tm, qd), lambda i: (i, 0)),
                   pl.BlockSpec((tm, kvd), lambda i: (i, 0)),
                   pl.BlockSpec((kvd, tm), lambda i: (0, i))],
        out_shape=[jax.ShapeDtypeStruct((t_all, qd), BF16),
                   jax.ShapeDtypeStruct((t_all, kvd), BF16),
                   jax.ShapeDtypeStruct((kvd, t_all), BF16)],
        compiler_params=_params(1),
        name="attn_qkv",
    )(x_lat, x_ctx, mod, w_qkv.astype(BF16), b_qkv.reshape(1, ncols), cos, sin)


def _attn_kernel(sink_ref, q_ref, kp_ref, ko_ref, kn_ref, kc_ref, vp_ref, vo_ref, vn_ref, vc_ref, o_ref,
                 bias_sc, s_sc, p_sc, ot_sc, *,
                 n_lat_steps, nb, s_len, c_len):
    j = pl.program_id(0)
    is_lat = j < n_lat_steps
    n = j % nb
    blk = ATTN_BLOCK
    nloc = 3 * blk
    nk = nloc + c_len
    ki = lax.broadcasted_iota(I32, (nloc, blk), 0)
    qj = lax.broadcasted_iota(I32, (nloc, blk), 1)
    kpos = n * blk - WINDOW + ki
    qpos = n * blk + qj
    local_ok = (jnp.abs(kpos - qpos) <= WINDOW) & (kpos >= 0) & (kpos < s_len) & is_lat
    bias_sc[...] = jnp.where(local_ok, 0.0, -jnp.inf)
    hd = ATTN_HEAD_DIM
    group = q_ref.shape[1] // (ATTN_KV_HEADS * hd)
    kcat = jnp.concatenate([kp_ref[...], ko_ref[...], kn_ref[...], kc_ref[...]], axis=0)
    vcat_t = jnp.concatenate([vp_ref[...], vo_ref[...], vn_ref[...], vc_ref[...]], axis=1)
    def score_matmul(kh):
        q_grp = jnp.concatenate([q_ref[:, (kh * group + g) * hd:(kh * group + g + 1) * hd]
                                 for g in range(group)], axis=0)
        s_sc[kh] = lax.dot_general(kcat[:, kh * hd:(kh + 1) * hd], q_grp, (((1,), (1,)), ((), ())),
                                   preferred_element_type=F32)

    def weighted_values(kh, sink_terms):
        v_ones = jnp.concatenate([vcat_t[kh * hd:(kh + 1) * hd, :], jnp.ones((SUBLANES, nk), BF16)], axis=0)
        o_aug = jnp.dot(v_ones, p_sc[kh], preferred_element_type=F32)
        l = o_aug[hd:hd + 1, :] + jnp.concatenate(sink_terms, axis=1)
        o_t = o_aug[0:hd, :] * (1.0 / l)
        for g in range(group):
            h = kh * group + g
            ot_sc[h * hd:(h + 1) * hd, :] = o_t[:, g * blk:(g + 1) * blk]

    score_matmul(0)
    for kh in range(ATTN_KV_HEADS):
        if kh + 1 < ATTN_KV_HEADS:
            score_matmul(kh + 1)
        sink_terms = []
        for g in range(group):
            cols = slice(g * blk, (g + 1) * blk)
            def scores(a):
                s = s_sc[kh, a:a + blk, cols]
                return s + bias_sc[a:a + blk, :] if a < nloc else s

            sk = sink_ref[kh * group + g]
            top = scores(0)
            for a in range(blk, nk, blk):
                top = jnp.maximum(top, scores(a))
            m = jnp.maximum(jnp.max(top, axis=0, keepdims=True), sk)
            for a in range(0, nk, blk):
                p_sc[kh, a:a + blk, cols] = jnp.exp((scores(a) - m).astype(BF16))
            sink_terms.append(jnp.exp(sk - m))
        weighted_values(kh, sink_terms)
    o_ref[...] = ot_sc[...].T.astype(BF16)


def _attention(q_all, k_all, vt_all, sink, dims):
    B, S, C, D = dims
    blk = ATTN_BLOCK
    nb = S // blk
    n_lat_steps = B * nb
    ctx_steps_per_batch = C // blk
    n_steps = n_lat_steps + B * ctx_steps_per_batch
    qd = q_all.shape[1]
    kvd = k_all.shape[1]

    def local_idx(off):
        def idx(j):
            b = j // nb
            nn = jnp.clip(j % nb + off, 0, nb - 1)
            return (jnp.where(j < n_lat_steps, b * nb + nn, j), 0)
        return idx

    def ctx_idx(j):
        b = jnp.where(j < n_lat_steps, j // nb, (j - n_lat_steps) // ctx_steps_per_batch)
        return (B * S // C + b, 0)

    swap = lambda f: (lambda j: f(j)[::-1])
    loc = lambda off: pl.BlockSpec((blk, kvd), local_idx(off))
    ctxs = pl.BlockSpec((C, kvd), ctx_idx)
    loc_t = lambda off: pl.BlockSpec((kvd, blk), swap(local_idx(off)))
    ctxs_t = pl.BlockSpec((kvd, C), swap(ctx_idx))
    group = qd // kvd
    nk = 3 * blk + C
    kern = functools.partial(_attn_kernel, n_lat_steps=n_lat_steps, nb=nb, s_len=S, c_len=C)
    return pl.pallas_call(
        kern,
        grid=(n_steps,),
        in_specs=[pl.BlockSpec(memory_space=pltpu.SMEM),
                  pl.BlockSpec((blk, qd), lambda j: (j, 0)),
                  loc(-1), loc(0), loc(1), ctxs,
                  loc_t(-1), loc_t(0), loc_t(1), ctxs_t],
        out_specs=pl.BlockSpec((blk, qd), lambda j: (j, 0)),
        out_shape=jax.ShapeDtypeStruct((q_all.shape[0], qd), BF16),
        scratch_shapes=[pltpu.VMEM((3 * blk, blk), F32),
                        pltpu.VMEM((ATTN_KV_HEADS, nk, group * blk), F32),
                        pltpu.VMEM((ATTN_KV_HEADS, nk, group * blk), BF16),
                        pltpu.VMEM((qd, blk), F32)],
        compiler_params=_params(1),
        name="attn_core",
    )(sink, q_all, k_all, k_all, k_all, k_all, vt_all, vt_all, vt_all, vt_all)


def _split_bf16(a):
    hi = a.astype(BF16)
    return hi, (a - hi.astype(F32)).astype(BF16)


def _split_weight(w_f32):
    hi = w_f32.astype(BF16)
    return jnp.concatenate([hi, (w_f32 - hi.astype(F32)).astype(BF16)], axis=1)


def _dot_split(a, w_ref):
    a_hi, a_lo = _split_bf16(a)
    n = w_ref.shape[1] // 2
    both = jnp.dot(a_hi, w_ref[...], preferred_element_type=F32)
    return both[:, :n] + both[:, n:] + jnp.dot(a_lo, w_ref[:, :n], preferred_element_type=F32)


def _route(hx, rw_ref, rb_ref, topi_ref, gate_ref, sizes_ref):
    logits = _dot_split(hx, rw_ref) + rb_ref[...]
    lane = lax.broadcasted_iota(I32, logits.shape, 1)
    lanef = lane.astype(F32)
    vals, idxs = [], []
    l = logits
    for _ in range(TOP_K):
        m = jnp.max(l, axis=1, keepdims=True)
        idx = jnp.min(jnp.where(l == m, lanef, float(LANES)), axis=1, keepdims=True)
        vals.append(m)
        idxs.append(idx)
        l = jnp.where(lanef == idx, -jnp.inf, l)
    es = [jnp.exp(v - vals[0]) for v in vals]
    denom = es[0]
    for e in es[1:]:
        denom = denom + e
    topi = jnp.zeros(logits.shape, F32)
    gates = jnp.zeros(logits.shape, F32)
    for k in range(TOP_K):
        topi = jnp.where(lane == k, idxs[k], topi)
        gates = jnp.where(lane == k, es[k] / denom, gates)
    topi_ref[...] = topi.astype(I32)
    gate_ref[...] = gates
    chosen = jnp.zeros(logits.shape, F32)
    for idx in idxs:
        chosen = chosen + jnp.where(lanef == idx, 1.0, 0.0)
    sizes = [jnp.ceil(jnp.sum(chosen[r:r + ROUTE_TILE], axis=0, keepdims=True) / SUBLANES) * SUBLANES
             for r in range(0, logits.shape[0], ROUTE_TILE)]
    sizes_ref[0] = jnp.concatenate(sizes + [jnp.zeros((SUBLANES - len(sizes), LANES), F32)], axis=0)


def _post_common(a, x, mod_ref, w_ref, b_ref, lng_ref, lnb_ref, rw_ref, rb_ref,
                 x1_ref, hx_ref, topi_ref, gate_ref, sizes_ref, row, d, alpha):
    y = jnp.dot(a, w_ref[...], preferred_element_type=F32) + b_ref[...]
    gate_mix = mod_ref[pl.ds(row, 1), 2 * d:3 * d]
    shift = mod_ref[pl.ds(row, 1), 3 * d:4 * d]
    scale = mod_ref[pl.ds(row, 1), 4 * d:5 * d]
    x1 = _layer_norm(alpha * x + gate_mix * y, lng_ref[...], lnb_ref[...])
    hx = x1 * (1.0 + scale) + shift
    x1_ref[...] = x1
    hx_ref[...] = hx.astype(BF16)
    _route(hx, rw_ref, rb_ref, topi_ref, gate_ref, sizes_ref)


def _post_attn_kernel(o_ref, x_ref, c_ref, mod_ref, w_ref, b_ref, lng_ref, lnb_ref, rw_ref, rb_ref,
                      x1_ref, hx_ref, topi_ref, gate_ref, sizes_ref, *, n_lat_tiles, tiles_per_batch, ctx_row, d, alpha):
    i = pl.program_id(0)
    row = _mod_row(i, n_lat_tiles, tiles_per_batch, ctx_row)
    x = jnp.where(i < n_lat_tiles, x_ref[...], c_ref[...])
    _post_common(o_ref[...], x, mod_ref, w_ref, b_ref, lng_ref, lnb_ref, rw_ref, rb_ref,
                 x1_ref, hx_ref, topi_ref, gate_ref, sizes_ref, row, d, alpha)


def _post_mlstm_kernel(hf_ref, hb_ref, og_ref, ng_ref, x_ref, mod_ref, w_ref, b_ref, lng_ref, lnb_ref,
                       rw_ref, rb_ref, x1_ref, hx_ref, topi_ref, gate_ref, sizes_ref, *,
                       n_lat_tiles, tiles_per_batch, ctx_row, d, alpha):
    row = _mod_row(pl.program_id(0), n_lat_tiles, tiles_per_batch, ctx_row)
    hsum = hf_ref[...] + hb_ref[...]
    vdim = hsum.shape[1] // ML_HEADS
    parts = []
    for h in range(ML_HEADS):
        seg = hsum[:, h * vdim:(h + 1) * vdim]
        mu = jnp.mean(seg, axis=1, keepdims=True)
        sc = seg - mu
        var = jnp.mean(sc * sc, axis=1, keepdims=True)
        parts.append(sc * lax.rsqrt(var + LN_EPS))
    y = jnp.concatenate(parts, axis=1) * ng_ref[...]
    a = (og_ref[...].astype(F32) * y).astype(BF16)
    _post_common(a, x_ref[...], mod_ref, w_ref, b_ref, lng_ref, lnb_ref, rw_ref, rb_ref,
                 x1_ref, hx_ref, topi_ref, gate_ref, sizes_ref, row, d, alpha)


def _post_mixer(mixer_inputs, norm_g, x_all, mod, w_o, b_o, ln_g, ln_b, router_w, router_b, dims, n_rows, alpha):
    B, S, C, D = dims
    tm = 512
    n_lat_tiles = B * S // tm
    tiles_per_batch = S // tm
    n_exp = router_w.shape[1]
    rw = _split_weight(jnp.zeros((D, LANES), F32).at[:, :n_exp].set(router_w))
    rb =jnp.full((1, LANES), -1e30, F32).at[0, :n_exp].set(router_b)
    row_spec = lambda w: pl.BlockSpec((tm, w), lambda i: (i, 0))
    full = lambda a: pl.BlockSpec(a.shape, lambda i: (0,) * a.ndim)
    common = dict(n_lat_tiles=n_lat_tiles, tiles_per_batch=tiles_per_batch, ctx_row=B, d=D, alpha=alpha)
    w_bf = w_o.astype(BF16)
    b2 = b_o.reshape(1, D)
    if isinstance(x_all, tuple):
        x_args, x_specs = list(x_all), _token_specs(tm, D, n_lat_tiles)
    else:
        x_args, x_specs = [x_all], [row_spec(D)]
    tail = x_args + [mod, w_bf, b2, ln_g.reshape(1, D), ln_b.reshape(1, D), rw, rb]
    tail_specs = x_specs + [full(mod), full(w_bf), full(b2), pl.BlockSpec((1, D), lambda i: (0, 0)),
                            pl.BlockSpec((1, D), lambda i: (0, 0)), full(rw), full(rb)]
    if norm_g is None:
        kern = functools.partial(_post_attn_kernel, **common)
        args = list(mixer_inputs) + tail
        specs = [row_spec(mixer_inputs[0].shape[1])] + tail_specs
        name = "post_attn"
    else:
        kern = functools.partial(_post_mlstm_kernel, **common)
        ng = norm_g.reshape(1, -1)
        args = list(mixer_inputs) + [ng] + tail
        specs = [row_spec(a.shape[1]) for a in mixer_inputs] + [full(ng)] + tail_specs
        name = "post_mlstm"
    return pl.pallas_call(
        kern,
        grid=(n_rows // tm,),
        in_specs=specs,
        out_specs=[row_spec(D), row_spec(D), row_spec(LANES), row_spec(LANES),
                   pl.BlockSpec((1, SUBLANES, LANES), lambda i: (i, 0, 0))],
        out_shape=[jax.ShapeDtypeStruct((n_rows, D), F32), jax.ShapeDtypeStruct((n_rows, D), BF16),
                   jax.ShapeDtypeStruct((n_rows, LANES), I32), jax.ShapeDtypeStruct((n_rows, LANES), F32),
                   jax.ShapeDtypeStruct((n_rows // tm, SUBLANES, LANES), F32)],
        compiler_params=_params(1),
        name=name,
    )(*args)


def _exclusive_lane_cumsum(row):
    r = lax.broadcasted_iota(I32, (LANES, LANES), 0)
    c = lax.broadcasted_iota(I32, (LANES, LANES), 1)
    before = jnp.where(r < c, 1.0, 0.0)
    return jnp.dot(jnp.broadcast_to(row, (SUBLANES, LANES)), before,
                   precision=HIGHEST, preferred_element_type=F32)[0:1]


def _for_each_unit(tab_ref, tile, fn):
    base = tile * TABLE_W
    for blk, rows in ((0, BIG_UNIT), (2, SUBLANES)):
        def body(j, carry, blk=blk, rows=rows):
            slot_row = pl.multiple_of(tab_ref[base + blk * LANES + j], SUBLANES)
            stage_row = pl.multiple_of(tab_ref[base + (blk + 1) * LANES + j], SUBLANES)
            fn(slot_row, stage_row, rows)
            return carry
        lax.fori_loop(0, tab_ref[base + blk * LANES + LANES - 1], body, 0)


def _slots_kernel(topi_ref, sizes_ref, col_ref, colt_ref, tab_ref, meta_ref, base_sc, carry_sc, *, tile):
    i = pl.program_id(0)
    tm = topi_ref.shape[0]
    topi = topi_ref[...]
    lane = lax.broadcasted_iota(I32, topi.shape, 1)
    sel = [lane == topi[:, k:k + 1] for k in range(TOP_K)]
    maskf = jnp.where(sel[0], 1.0, 0.0)
    for s in sel[1:]:
        maskf = maskf + jnp.where(s, 1.0, 0.0)
    n8 = jnp.ceil(jnp.sum(maskf, axis=0, keepdims=True) / SUBLANES) * SUBLANES

    @pl.when(i == 0)
    def _():
        tot = jnp.sum(jnp.sum(sizes_ref[...], axis=0), axis=0, keepdims=True)
        padded = jnp.ceil(tot / tile) * tile
        base = _exclusive_lane_cumsum(padded)
        base_sc[...] = base
        carry_sc[...] = jnp.zeros_like(carry_sc)
        rowi = lax.broadcasted_iota(I32, meta_ref.shape, 0)
        meta_ref[...] = jnp.where(rowi == 0, tot, jnp.where(rowi == 1, base, padded)).astype(I32)

    start = base_sc[...] + carry_sc[...]
    off = _exclusive_lane_cumsum(n8)
    r = lax.broadcasted_iota(I32, (tm, tm), 0)
    c = lax.broadcasted_iota(I32, (tm, tm), 1)
    earlier = jnp.where(c < r, 1.0, 0.0).astype(BF16)
    rank = jnp.dot(earlier, maskf.astype(BF16), preferred_element_type=F32)
    stage_row = rank + off
    out = jnp.zeros(topi.shape, F32)
    for k in range(TOP_K):
        pk = jnp.sum(jnp.where(sel[k], stage_row, 0.0), axis=1, keepdims=True)
        out = jnp.where(lane == k, pk, out)
    col_ref[...] = out.astype(I32)
    colt_ref[0] = out.T[0:SUBLANES].astype(I32)

    n_big = jnp.floor(n8 / BIG_UNIT)
    n_small = (n8 - BIG_UNIT * n_big) / SUBLANES
    first_big = _exclusive_lane_cumsum(n_big)
    first_small = _exclusive_lane_cumsum(n_small)
    rows = jnp.concatenate([off, start, first_big, n_big, first_small, n_small,
                            jnp.zeros((LANES - 6, LANES), F32)], axis=0)
    per_expert = rows.T
    off_c, start_c, fb_c, nb_c, fs_c, ns_c = (per_expert[:, k:k + 1] for k in range(6))
    u = lax.broadcasted_iota(I32, (LANES, LANES), 1).astype(F32)
    lane_t = lax.broadcasted_iota(I32, (1, LANES), 1)

    def unit_list(first_c, count_c, rel, count_row):
        inside = (u >= first_c) & (u < first_c + count_c)
        src = jnp.sum(jnp.where(inside, start_c + rel, 0.0), axis=0, keepdims=True)
        dst = jnp.sum(jnp.where(inside, off_c + rel, 0.0), axis=0, keepdims=True)
        total = jnp.sum(count_row, axis=1, keepdims=True)
        return [jnp.where(lane_t == LANES - 1, total, src), dst]

    lists = (unit_list(fb_c, nb_c, BIG_UNIT * (u - fb_c), n_big)
             + unit_list(fs_c, ns_c, BIG_UNIT * nb_c + SUBLANES * (u - fs_c), n_small))
    tab_ref[0] = jnp.concatenate(lists, axis=1).astype(I32)
    carry_sc[...] = carry_sc[...] + n8


def _slots(topi, sizes, tile):
    t = topi.shape[0]
    tm = ROUTE_TILE
    nt = t // tm
    return pl.pallas_call(
        functools.partial(_slots_kernel, tile=float(tile)),
        grid=(nt,),
        in_specs=[pl.BlockSpec((tm, LANES), lambda i: (i, 0)),
                  pl.BlockSpec(sizes.shape, lambda i: (0, 0, 0))],
        out_specs=[pl.BlockSpec((tm, LANES), lambda i: (i, 0)),
                   pl.BlockSpec((1, SUBLANES, tm), lambda i: (i, 0, 0)),
                   pl.BlockSpec((1, 1, TABLE_W), lambda i: (i, 0, 0)),
                   pl.BlockSpec((SUBLANES, LANES), lambda i: (0, 0))],
        out_shape=[jax.ShapeDtypeStruct((t, LANES), I32),
                   jax.ShapeDtypeStruct((nt, SUBLANES, tm), I32),
                   jax.ShapeDtypeStruct((nt, 1, TABLE_W), I32),
                   jax.ShapeDtypeStruct((SUBLANES, LANES), I32)],
        scratch_shapes=[pltpu.VMEM((1, LANES), F32), pltpu.VMEM((1, LANES), F32)],
        compiler_params=_params(1),
        name="moe_slots",
    )(topi, sizes)


def _dispatch_kernel(tab_ref, pad_start_ref, pad_units_ref, tail_ref, colt_ref, hx_ref, xs_ref, stage, zeros, sem, zsem):
    i = pl.program_id(0)
    nt = pl.num_programs(0)
    slot = i % 2
    kb = stage.shape[1]
    tm = hx_ref.shape[0]

    def scatter(buf_slot):
        def copy(slot_row, stage_row, rows):
            return pltpu.make_async_copy(stage.at[buf_slot, pl.ds(stage_row, rows)],
                                         xs_ref.at[pl.ds(slot_row, rows)], sem.at[buf_slot])
        return copy

    def drain(tile, buf_slot):
        _for_each_unit(tab_ref, tile, lambda *u: scatter(buf_slot)(*u).wait())

    @pl.when(i >= 2)
    def _():
        drain(i - 2, slot)

    colt = colt_ref[0]
    c = lax.broadcasted_iota(I32, (kb, tm), 0)
    onehot = jnp.zeros((kb, tm), F32)
    for k in range(TOP_K):
        onehot = jnp.where(c == colt[k:k + 1, :], 1.0, onehot)
    stage[slot] = jnp.dot(onehot.astype(BF16), hx_ref[...], preferred_element_type=F32)

    _for_each_unit(tab_ref, i, lambda *u: scatter(slot)(*u).start())

    @pl.when(i == nt - 1)
    def _():
        zeros[...] = jnp.zeros_like(zeros)
        n_exp = pad_start_ref.shape[0]

        def zero_copy(e, u):
            dst = pl.multiple_of(pad_start_ref[e] + u * SUBLANES, SUBLANES)
            return pltpu.make_async_copy(zeros.at[pl.ds(0, SUBLANES)], xs_ref.at[pl.ds(dst, SUBLANES)], zsem)

        def per_expert(fn):
            def outer(e, carry):
                def inner(u, carry2):
                    fn(e, u)
                    return carry2
                lax.fori_loop(0, pad_units_ref[e], inner, 0)
                return carry
            lax.fori_loop(0, n_exp, outer, 0)

        per_expert(lambda e, u: zero_copy(e, u).start())
        per_expert(lambda e, u: zero_copy(e, u).wait())

        zrows = zeros.shape[0]

        def tail_copy(u):
            dst = pl.multiple_of(tail_ref[0] + u * zrows, zrows)
            return pltpu.make_async_copy(zeros, xs_ref.at[pl.ds(dst, zrows)], zsem)

        def tail_loop(fn):
            def body(u, carry):
                fn(u)
                return carry
            lax.fori_loop(0, tail_ref[1], body, 0)

        tail_loop(lambda u: tail_copy(u).start())
        tail_loop(lambda u: tail_copy(u).wait())

        @pl.when(nt >= 2)
        def _():
            drain(i - 1, 1 - slot)
        drain(i, slot)


def _dispatch(tab, pad_start, pad_units, tail, colt, hx, n_slots):
    t, d = hx.shape
    tm = ROUTE_TILE
    return pl.pallas_call(
        _dispatch_kernel,
        grid_spec=pltpu.PrefetchScalarGridSpec(
            num_scalar_prefetch=4,
            grid=(t // tm,),
            in_specs=[pl.BlockSpec((1, SUBLANES, tm), lambda i, *_: (i, 0, 0)),
                      pl.BlockSpec((tm, d), lambda i, *_: (i, 0))],
            out_specs=pl.BlockSpec(memory_space=pl.ANY),
            scratch_shapes=[pltpu.VMEM((2, STAGE_ROWS, d), F32), pltpu.VMEM((ZERO_ROWS, d), F32),
                            pltpu.SemaphoreType.DMA((2,)), pltpu.SemaphoreType.DMA(())]),
        out_shape=jax.ShapeDtypeStruct((n_slots, d), F32),
        compiler_params=_params(1),
        name="moe_dispatch",
    )(tab, pad_start, pad_units, tail, colt, hx)


def _expert_kernel(te_ref, rows_ref, next_ref, nu_ref, xs_ref, wgu_hbm, bgu_ref, wd_hbm, bd_ref, ys_ref,
                   wgu_f32, wd_f32, wgu_sc, wd_sc, wsem, *, layer):
    j = pl.program_id(0)
    active = j < nu_ref[0]
    changed = (j == 0) | (te_ref[j] != te_ref[jnp.maximum(j - 1, 0)])
    ff = wd_sc.shape[0]
    te = xs_ref.shape[0]
    half = te // 2

    def weight_copies(e):
        return (pltpu.make_async_copy(wgu_hbm.at[layer, e], wgu_f32, wsem.at[0]),
                pltpu.make_async_copy(wd_hbm.at[layer, e], wd_f32, wsem.at[1]))

    @pl.when(active & changed)
    def _():
        @pl.when(j == 0)
        def _():
            for cp in weight_copies(te_ref[0]):
                cp.start()

        for cp in weight_copies(te_ref[j]):
            cp.wait()
        wgu_sc[...] = wgu_f32[...].astype(BF16)
        wd_sc[...] = wd_f32[...].astype(BF16)

        @pl.when(next_ref[j] >= 0)
        def _():
            for cp in weight_copies(next_ref[j]):
                cp.start()

    def run(rows):
        x = xs_ref[0:rows, :].astype(BF16)
        nblk = 2
        fb = ff // nblk

        def up(c):
            gl = jnp.dot(x, wgu_sc[:, c * fb:(c + 1) * fb], preferred_element_type=F32) + bgu_ref[:, c * fb:(c + 1) * fb]
            lin = (jnp.dot(x, wgu_sc[:, ff + c * fb:ff + (c + 1) * fb], preferred_element_type=F32)
                   + bgu_ref[:, ff + c * fb:ff + (c + 1) * fb])
            return gl, lin

        y = bd_ref[...]
        nxt = up(0)
        for c in range(nblk):
            gl, lin = nxt
            if c + 1 < nblk:
                nxt = up(c + 1)
            gl = jnp.minimum(gl, SWIGLU_LIMIT)
            lin = jnp.clip(lin, -SWIGLU_LIMIT, SWIGLU_LIMIT)
            act = gl * _sigmoid(SWIGLU_ALPHA * gl) * (lin + 1.0)
            y = y + jnp.dot(act.astype(BF16), wd_sc[c * fb:(c + 1) * fb, :], preferred_element_type=F32)
        ys_ref[0:rows, :] = y
        if rows < te:
            ys_ref[rows:te, :] = jnp.zeros((te - rows, ys_ref.shape[1]), F32)

    valid_rows = rows_ref[j]

    @pl.when(active & (valid_rows > half))
    def _():
        run(te)

    @pl.when(active & (valid_rows <= half))
    def _():
        run(half)

    @pl.when(jnp.logical_not(active))
    def _():
        ys_ref[...] = jnp.zeros_like(ys_ref)


def _experts(tile_expert, tile_rows, next_expert, n_used, xs, layer, w_gu, b_gu, w_down, b_down):
    n_slots, d = xs.shape
    depth, n_exp, _, ff2 = w_gu.shape
    ff = w_down.shape[2]
    te = EXPERT_TILE
    row_idx = lambda j, te_r, rows_r, nx_r, nu: (jnp.maximum(jnp.minimum(j, nu[0] - 1), 0), 0)
    b_idx = lambda j, te_r, rows_r, nx_r, nu: (layer, te_r[j], 0, 0)
    return pl.pallas_call(
        functools.partial(_expert_kernel, layer=layer),
        grid_spec=pltpu.PrefetchScalarGridSpec(
            num_scalar_prefetch=4,
            grid=(n_slots // te,),
            in_specs=[pl.BlockSpec((te, d), row_idx),
                      pl.BlockSpec(memory_space=pl.ANY),
                      pl.BlockSpec((None, None, 1, ff2), b_idx),
                      pl.BlockSpec(memory_space=pl.ANY),
                      pl.BlockSpec((None, None, 1, d), b_idx)],
            out_specs=pl.BlockSpec((te, d), lambda j, te_r, rows_r, nx_r, nu: (j, 0)),
            scratch_shapes=[pltpu.VMEM((d, ff2), F32), pltpu.VMEM((ff, d), F32),
                            pltpu.VMEM((d, ff2), BF16), pltpu.VMEM((ff, d), BF16),
                            pltpu.SemaphoreType.DMA((2,))]),
        out_shape=jax.ShapeDtypeStruct((n_slots, d), F32),
        compiler_params=_params(1),
        name="moe_experts",
    )(tile_expert, tile_rows, next_expert, n_used, xs, w_gu, b_gu.reshape(depth, n_exp, 1, ff2),
      w_down, b_down.reshape(depth, n_exp, 1, d))


def _combine_kernel(tab_ref, col_ref, gate_ref, ys_ref, x_ref, mod_ref, lng_ref, lnb_ref, out_ref, stage, sem, *,
                    n_lat_tiles, tiles_per_batch, ctx_row, d, alpha):
    i = pl.program_id(0)
    nt = pl.num_programs(0)
    slot = i % 2
    kb = stage.shape[1]
    tm = x_ref.shape[0]

    def gather(buf_slot):
        def copy(slot_row, stage_row, rows):
            return pltpu.make_async_copy(ys_ref.at[pl.ds(slot_row, rows)],
                                         stage.at[buf_slot, pl.ds(stage_row, rows)], sem.at[buf_slot])
        return copy

    @pl.when(i == 0)
    def _():
        stage[...] = jnp.zeros_like(stage)
        _for_each_unit(tab_ref, 0, lambda *u: gather(0)(*u).start())

    @pl.when(i + 1 < nt)
    def _():
        _for_each_unit(tab_ref, i + 1, lambda *u: gather(1 - slot)(*u).start())

    _for_each_unit(tab_ref, i, lambda *u: gather(slot)(*u).wait())

    col = col_ref[...]
    gates = gate_ref[...]
    kblk = 2 * LANES

    def gate_weights(r0):
        c = r0 + lax.broadcasted_iota(I32, (tm, kblk), 1)
        w = jnp.zeros((tm, kblk), F32)
        for k in range(TOP_K):
            w = jnp.where(c == col[:, k:k + 1], gates[:, k:k + 1], w)
        return w.astype(BF16)

    y = jnp.zeros((tm, d), F32)
    nxt = gate_weights(0)
    for r0 in range(0, kb, kblk):
        w = nxt
        if r0 + kblk < kb:
            nxt = gate_weights(r0 + kblk)
        y = y + jnp.dot(w, stage[slot, r0:r0 + kblk, :].astype(BF16), preferred_element_type=F32)
    row = _mod_row(i, n_lat_tiles, tiles_per_batch, ctx_row)
    gate_mlp = mod_ref[pl.ds(row, 1), 5 * d:6 * d]
    out_ref[...] = _layer_norm(alpha * x_ref[...] + gate_mlp * y, lng_ref[...], lnb_ref[...])


def _combine(tab, col, ys, gates, x1, mod, ln_g, ln_b, dims, alpha):
    B, S, C, D = dims
    t = x1.shape[0]
    tm = ROUTE_TILE
    kern = functools.partial(_combine_kernel, n_lat_tiles=B * S // tm, tiles_per_batch=S // tm, ctx_row=B,
                             d=D, alpha=alpha)
    return pl.pallas_call(
        kern,
        grid_spec=pltpu.PrefetchScalarGridSpec(
            num_scalar_prefetch=1,
            grid=(t // tm,),
            in_specs=[pl.BlockSpec((tm, LANES), lambda i, tab_r: (i, 0)),
                      pl.BlockSpec((tm, LANES), lambda i, tab_r: (i, 0)),
                      pl.BlockSpec(memory_space=pl.ANY),
                      pl.BlockSpec((tm, D), lambda i, tab_r: (i, 0)),
                      pl.BlockSpec(mod.shape, lambda i, tab_r: (0, 0)),
                      pl.BlockSpec((1, D), lambda i, tab_r: (0, 0)),
                      pl.BlockSpec((1, D), lambda i, tab_r: (0, 0))],
            out_specs=pl.BlockSpec((tm, D), lambda i, tab_r: (i, 0)),
            scratch_shapes=[pltpu.VMEM((2, STAGE_ROWS, D), F32), pltpu.SemaphoreType.DMA((2,))]),
        out_shape=jax.ShapeDtypeStruct((t, D), F32),
        compiler_params=_params(1),
        name="moe_combine",
    )(tab, col, gates, ys, x1, mod, ln_g.reshape(1, D), ln_b.reshape(1, D))


def _moe(hx, topi, gates, sizes, x1, mod, ln_g, ln_b, layer, w_gu, b_gu, w_down, b_down, dims, alpha):
    t = hx.shape[0]
    n_exp = w_gu.shape[1]
    te = EXPERT_TILE
    n_route_tiles = t // ROUTE_TILE
    max_rows = t * TOP_K + (SUBLANES - 1) * n_exp * n_route_tiles
    n_tiles = -(-max_rows // te) + n_exp
    col, colt, tab, meta = _slots(topi, sizes, te)
    tab = tab.reshape(-1)
    tot, base, padded = meta[0, :n_exp], meta[1, :n_exp], meta[2, :n_exp]
    ends = jnp.cumsum(padded // te)
    n_used = ends[-1:].astype(I32)
    tile_ids = jnp.minimum(jnp.arange(n_tiles, dtype=I32), n_used[0] - 1)
    tile_expert = jnp.sum((tile_ids[:, None] >= ends[None, :]).astype(I32), axis=1)
    tile_expert = jnp.minimum(tile_expert, n_exp - 1).astype(I32)
    pad_start = (base + tot).astype(I32)
    pad_units = ((padded - tot) // SUBLANES).astype(I32)
    used_rows = n_used[0] * te
    tail = jnp.stack([used_rows, (n_tiles * te - used_rows) // ZERO_ROWS]).astype(I32)
    xs = _dispatch(tab, pad_start, pad_units, tail, colt, hx, n_tiles * te)
    experts = jnp.arange(n_exp, dtype=I32)
    of_tile = tile_expert[:, None] == experts[None, :]
    pick = lambda per_expert: jnp.sum(jnp.where(of_tile, per_expert[None, :], 0), axis=1).astype(I32)
    tile_rows = jnp.clip(pick(base + tot) - tile_ids * te, 0, te).astype(I32)
    later = (padded[None, :] > 0) & (experts[None, :] > experts[:, None])
    next_of = jnp.min(jnp.where(later, experts[None, :], n_exp), axis=1)
    next_expert = pick(jnp.where(next_of == n_exp, -1, next_of))
    ys = _experts(tile_expert, tile_rows, next_expert, n_used, xs, layer, w_gu, b_gu, w_down, b_down)
    return _combine(tab, col, ys, gates, x1, mod, ln_g, ln_b, dims, alpha)


def _ml_in_kernel(xp_ref, x_ref, xn_ref, mod_ref, wqk_ref, bqk_ref, wvo_ref, bvo_ref, wg_ref, bg_ref,
                  cw_ref, cb_ref, q_ref, k_ref, v_ref, og_ref, g_ref, *,
                  n_lat_tiles, tiles_per_batch, ctx_tiles_per_seq, ctx_row, d):
    i = pl.program_id(0)
    tm = x_ref.shape[0]
    halo = SUBLANES
    row = _mod_row(i, n_lat_tiles, tiles_per_batch, ctx_row)
    shift = mod_ref[pl.ds(row, 1), 0:d]
    scale = mod_ref[pl.ds(row, 1), d:2 * d]
    is_lat = i < n_lat_tiles
    seq_tile = jnp.where(is_lat, i % tiles_per_batch, (i - n_lat_tiles) % ctx_tiles_per_seq)
    seq_tiles = jnp.where(is_lat, tiles_per_batch, ctx_tiles_per_seq)
    first = seq_tile == 0
    last = seq_tile == seq_tiles - 1

    h = x_ref[...] * (1.0 + scale) + shift
    h_ext = jnp.concatenate([xp_ref[...] * (1.0 + scale) + shift, h, xn_ref[...] * (1.0 + scale) + shift], axis=0)
    z = jnp.dot(h_ext.astype(BF16), wqk_ref[...], preferred_element_type=F32) + bqk_ref[...]
    n_ext = tm + 2 * halo
    cw = cw_ref[...]
    r = lax.broadcasted_iota(I32, z.shape, 0)
    z = jnp.where(((r < halo) & first) | ((r >= halo + tm) & last), 0.0, z)
    acc = None
    for j in range(ML_CONV_W):
        sh = (ML_CONV_W // 2 - j) % n_ext
        zj = z if sh == 0 else pltpu.roll(z, sh, 0)
        term = zj[halo:halo + tm] * cw[j:j + 1]
        acc = term if acc is None else acc + term
    qk = acc + cb_ref[...]
    qk = qk * _sigmoid(qk)
    nqk = qk.shape[1] // 2
    qk_dim = nqk // ML_HEADS
    q_ref[...] = (qk[:, :nqk] * (qk_dim ** -0.5)).astype(BF16)
    k_ref[...] = qk[:, nqk:].astype(BF16)

    vo = jnp.dot(h.astype(BF16), wvo_ref[...], preferred_element_type=F32) + bvo_ref[...]
    nv = vo.shape[1] // 2
    v_ref[...] = vo[:, :nv].astype(BF16)
    og_ref[...] = _sigmoid(vo[:, nv:]).astype(BF16)

    zg = _dot_split(h, wg_ref) + bg_ref[...]
    g = GATE_CAP * jnp.tanh(zg / GATE_CAP)
    log_sig = jnp.minimum(g, 0.0) - jnp.log(1.0 + jnp.exp(-jnp.abs(g)))
    lane = lax.broadcasted_iota(I32, g.shape, 1)
    is_forget = ((lane // ML_HEADS) % 2) == 1
    g_ref[...] = jnp.where(is_forget, log_sig, g)


def _ml_in(x_all, mod, w_in, b_in, conv_w, conv_b, dims):
    B, S, C, D = dims
    t_all = x_all.shape[0]
    tm = 256
    assert C % tm == 0 and S % tm == 0
    n_lat_tiles = B * S // tm
    nqk2 = conv_w.shape[1]
    nv = (w_in.shape[1] - nqk2 - 4 * ML_HEADS) // 2
    ng = 4 * ML_HEADS
    w_qk = w_in[:, :nqk2].astype(BF16)
    w_vo = w_in[:, nqk2:nqk2 + 2 * nv].astype(BF16)
    w_g = _split_weight(jnp.zeros((D, LANES), F32).at[:, :ng].set(w_in[:, nqk2 + 2 * nv:]))
    b_qk = b_in[:nqk2].reshape(1, -1)
    b_vo = b_in[nqk2:nqk2 + 2 * nv].reshape(1, -1)
    b_g = jnp.zeros((1, LANES), F32).at[0, :ng].set(b_in[nqk2 + 2 * nv:])
    cw = jnp.zeros((SUBLANES, nqk2), F32).at[:ML_CONV_W].set(conv_w)
    cb = conv_b.reshape(1, -1)
    hb = tm // SUBLANES
    n_hblk = t_all // SUBLANES
    full = lambda a: pl.BlockSpec(a.shape, lambda i: (0,) * a.ndim)
    row_spec = lambda w: pl.BlockSpec((tm, w), lambda i: (i, 0))
    kern = functools.partial(_ml_in_kernel, n_lat_tiles=n_lat_tiles, tiles_per_batch=S // tm,
                             ctx_tiles_per_seq=C // tm, ctx_row=B, d=D)
    return pl.pallas_call(
        kern,
        grid=(t_all // tm,),
        in_specs=[pl.BlockSpec((SUBLANES, D), lambda i: (jnp.maximum(i * hb - 1, 0), 0)),
                  row_spec(D),
                  pl.BlockSpec((SUBLANES, D), lambda i: (jnp.minimum((i + 1) * hb, n_hblk - 1), 0)),
                  full(mod), full(w_qk), full(b_qk), full(w_vo), full(b_vo), full(w_g), full(b_g),
                  full(cw), full(cb)],
        out_specs=[row_spec(nqk2 // 2), row_spec(nqk2 // 2), row_spec(nv), row_spec(nv), row_spec(LANES)],
        out_shape=[jax.ShapeDtypeStruct((t_all, nqk2 // 2), BF16), jax.ShapeDtypeStruct((t_all, nqk2 // 2), BF16),
                   jax.ShapeDtypeStruct((t_all, nv), BF16), jax.ShapeDtypeStruct((t_all, nv), BF16),
                   jax.ShapeDtypeStruct((t_all, LANES), F32)],
        compiler_params=_params(1),
        name="mlstm_in",
    )(x_all, x_all, x_all, mod, w_qk, b_qk, w_vo, b_vo, w_g, b_g, cw, cb)


def _ml_chunk(q_ref, k_ref, v_ref, g_ref, h_ref, s_sc, m_sc, reverse):
    L = q_ref.shape[0]
    qk_dim = q_ref.shape[1] // ML_HEADS
    v_dim = v_ref.shape[1] // ML_HEADS
    nh = ML_HEADS
    gates_t = g_ref[...].T
    sr = lax.broadcasted_iota(I32, (L, L), 0)
    lc = lax.broadcasted_iota(I32, (L, L), 1)
    upto = (sr >= lc) if reverse else (sr <= lc)
    cum_t = jnp.dot(gates_t, jnp.where(upto, 1.0, 0.0), precision=HIGHEST, preferred_element_type=F32)
    off = 2 * nh if reverse else 0
    li = gates_t[off:off + nh]
    b = cum_t[off + nh:off + 2 * nh]
    c = li - b
    lane = lax.broadcasted_iota(I32, c.shape, 1)
    mu = c
    d = 1
    while d < L:
        if reverse:
            shifted = jnp.where(lane < L - d, pltpu.roll(mu, L - d, 1), -jnp.inf)
        else:
            shifted = jnp.where(lane >= d, pltpu.roll(mu, d, 1), -jnp.inf)
        mu = jnp.maximum(mu, shifted)
        d *= 2
    m_prev = m_sc[...]
    mu = jnp.maximum(mu, m_prev)
    m_t = b + mu
    end = 0 if reverse else L - 1
    mu_end = mu[:, end:end + 1]
    decay = jnp.exp(m_prev[:, 0:1] - mu_end)
    wk = jnp.exp(c - mu_end)
    m_sc[...] = jnp.broadcast_to(b[:, end:end + 1] + mu_end, m_prev.shape)
    cols = jnp.concatenate([mu, m_t, jnp.zeros((L - 2 * nh, L), F32)], axis=0).T
    k_t = k_ref[...].astype(F32).T
    rl = lax.broadcasted_iota(I32, (L, L), 0)
    cs = lax.broadcasted_iota(I32, (L, L), 1)
    allowed = (cs >= rl) if reverse else (cs <= rl)
    ones = jnp.ones((L, v_dim), BF16)

    def head(h):
        qh = q_ref[:, h * qk_dim:(h + 1) * qk_dim]
        kt_h = k_t[h * qk_dim:(h + 1) * qk_dim, :]
        qk = jnp.dot(qh, kt_h.astype(BF16), preferred_element_type=F32)
        state = s_sc[h]
        q_state = jnp.dot(qh, state.astype(BF16), preferred_element_type=F32)
        yield
        mu_col = jnp.broadcast_to(cols[:, h:h + 1], (L, L))
        p = jnp.exp(jnp.where(allowed, c[h:h + 1, :] - mu_col, -jnp.inf))
        w_inter = jnp.exp(m_prev[h:h + 1, :] - mu_col)
        s = (qk * p).astype(BF16)
        yield
        v_aug = jnp.concatenate([v_ref[:, h * v_dim:(h + 1) * v_dim], ones], axis=1)
        nd = jnp.dot(s, v_aug, preferred_element_type=F32)
        kw_t = (kt_h * wk[h:h + 1, :]).astype(BF16)
        new_state = decay[h:h + 1, :] * state + jnp.dot(kw_t, v_aug, preferred_element_type=F32)
        yield
        mt_col = jnp.broadcast_to(cols[:, nh + h:nh + h + 1], (L, v_dim))
        nd = nd + jnp.concatenate([w_inter, w_inter], axis=1) * q_state
        num = nd[:, :v_dim]
        den = nd[:, v_dim:]
        h_ref[:, h * v_dim:(h + 1) * v_dim] = num / jnp.maximum(jnp.abs(den), jnp.exp(-mt_col))
        s_sc[h] = new_state

    return head


def _ml_scan_kernel(*refs, n_batch):
    n_in = 8 * n_batch
    hf_ref, hb_ref, s_sc, m_sc = refs[n_in:]

    @pl.when(pl.program_id(0) == 0)
    def _():
        s_sc[...] = jnp.zeros_like(s_sc)
        m_sc[...] = jnp.zeros_like(m_sc)

    heads = []
    for b in range(n_batch):
        qf, kf, vf, gf, qb, kb, vb, gb = refs[8 * b:8 * b + 8]
        heads.append(_ml_chunk(qf, kf, vf, gf, hf_ref.at[b], s_sc.at[b, 0], m_sc.at[b, 0], False))
        heads.append(_ml_chunk(qb, kb, vb, gb, hb_ref.at[b], s_sc.at[b, 1], m_sc.at[b, 1], True))
    for h in range(ML_HEADS):
        running = [head(h) for head in heads]
        while running:
            running = [g for g in running if next(g, True) is None]


def _ml_scan(q, k, v, g, dims):
    B, S, C, D = dims
    L = ML_CHUNK
    assert L == LANES
    nc_ctx = C // L
    nc_lat = S // L
    qk_dim = q.shape[1] // ML_HEADS
    v_dim = v.shape[1] // ML_HEADS

    def in_index(b, reverse):
        def idx(c):
            in_ctx = c < nc_ctx
            cl = c - nc_ctx
            if reverse:
                ctx_blk = (B * S + b * C) // L + (nc_ctx - 1 - c)
                lat_blk = (b * S) // L + (nc_lat - 1 - cl)
            else:
                ctx_blk = (B * S + b * C) // L + c
                lat_blk = (b * S) // L + cl
            return (jnp.where(in_ctx, ctx_blk, lat_blk), 0)
        return idx

    def out_index(reverse):
        def idx(c):
            cl = jnp.maximum(c - nc_ctx, 0)
            return (0, nc_lat - 1 - cl if reverse else cl, 0)
        return idx

    widths = (q.shape[1], k.shape[1], v.shape[1], LANES)
    in_specs, args = [], []
    for b in range(B):
        for rev in (False, True):
            in_specs += [pl.BlockSpec((L, w), in_index(b, rev)) for w in widths]
            args += [q, k, v, g]
    out = jax.ShapeDtypeStruct((B, S, v.shape[1]), F32)
    return pl.pallas_call(
        functools.partial(_ml_scan_kernel, n_batch=B),
        grid=(nc_ctx + nc_lat,),
        in_specs=in_specs,
        out_specs=[pl.BlockSpec((B, L, v.shape[1]), out_index(False)),
                   pl.BlockSpec((B, L, v.shape[1]), out_index(True))],
        out_shape=[out, out],
        scratch_shapes=[pltpu.VMEM((B, 2, ML_HEADS, qk_dim, 2 * v_dim), F32),
                        pltpu.VMEM((B, 2, ML_HEADS, LANES), F32)],
        compiler_params=_params(1),
        name="mlstm_scan",
    )(*args)


def kernel(x, c, ctx, c_ctx, ada_w, ada_b, ln_g, ln_b, attn_w_qkv, attn_b_qkv, attn_sink, attn_w_o, attn_b_o,
           ml_w_in, ml_b_in, ml_conv_w, ml_conv_b, ml_norm_g, ml_w_out, router_w, router_b,
           exp_w_gu, exp_b_gu, exp_w_down, exp_b_down):
    B, S, D = x.shape
    C = ctx.shape[1]
    depth = ada_w.shape[0]
    dims = (B, S, C, D)
    alpha = (2.0 * depth) ** 0.25
    n_lat = B * S

    cvec = jnp.zeros((SUBLANES, D), F32).at[:B].set(c).at[B].set(c_ctx)
    mods = _adaln(cvec, ada_w, ada_b)
    x_lat, x_ctx = x.reshape(n_lat, D), ctx.reshape(B * C, D)

    q, k, v = _attn_qkv(x_lat, x_ctx, mods[0], attn_w_qkv[0], attn_b_qkv[0], dims)
    o = _attention(q, k, v, attn_sink[0], dims)
    x1, hx, topi, gates, sizes = _post_mixer((o,), None, (x_lat, x_ctx), mods[0], attn_w_o[0], attn_b_o[0],
                                      ln_g[0, 0], ln_b[0, 0], router_w[0], router_b[0], dims, n_lat + B * C, alpha)
    x_all = _moe(hx, topi, gates, sizes, x1, mods[0], ln_g[0, 1], ln_b[0, 1],
                 0, exp_w_gu, exp_b_gu, exp_w_down, exp_b_down, dims, alpha)

    q, k, v, og, g = _ml_in(x_all, mods[1], ml_w_in[0], ml_b_in[0], ml_conv_w[0], ml_conv_b[0], dims)
    hf, hb = (h.reshape(n_lat, -1) for h in _ml_scan(q, k, v, g, dims))
    zero_b = jnp.zeros((D,), F32)
    x1, hx, topi, gates, sizes = _post_mixer((hf, hb, og), ml_norm_g[0], x_all, mods[1], ml_w_out[0], zero_b,
                                      ln_g[1, 0], ln_b[1, 0], router_w[1], router_b[1], dims, n_lat, alpha)
    out = _moe(hx, topi, gates, sizes, x1, mods[1], ln_g[1, 1], ln_b[1, 1],
               1, exp_w_gu, exp_b_gu, exp_w_down, exp_b_down, dims, alpha)
    return out.reshape(B, S, D)
```

```python
import functools

import jax
import jax.numpy as jnp
from jax import lax
from jax.experimental import pallas as pl
from jax.experimental.pallas import tpu as pltpu

F32 = jnp.float32
BF16 = jnp.bfloat16
I32 = jnp.int32
HIGHEST = lax.Precision.HIGHEST

GRID_W = 64
ATTN_HEAD_DIM = 64
ATTN_KV_HEADS = 4
WINDOW = 128
ATTN_BLOCK = 128
ROPE_THETA = 10000.0
ML_HEADS = 8
ML_CONV_W = 5
GATE_CAP = 15.0
TOP_K = 4
SWIGLU_ALPHA = 1.702
SWIGLU_LIMIT = 7.0
LN_EPS = 1e-5

LANES = 128
SUBLANES = 8
VMEM_LIMIT = 56 * 1024 * 1024
EXPERT_TILE = 512
ROUTE_TILE = 256
ML_CHUNK = 128
GROUP_ALIGN = 16
BIG_UNIT = 2 * GROUP_ALIGN
STAGE_ROWS = ROUTE_TILE * TOP_K + 4 * LANES
TABLE_W = 4 * LANES
ZERO_ROWS = 64


def _params(n_axes, vmem=VMEM_LIMIT):
    return pltpu.CompilerParams(dimension_semantics=("arbitrary",) * n_axes, vmem_limit_bytes=vmem)


def _layer_norm(r, g, b):
    mu = jnp.mean(r, axis=-1, keepdims=True)
    rc = r - mu
    var = jnp.mean(rc * rc, axis=-1, keepdims=True)
    return rc * lax.rsqrt(var + LN_EPS) * g + b


def _sigmoid(x):
    return 1.0 / (1.0 + jnp.exp(-x))


def _mod_row(i, n_lat_tiles, tiles_per_batch, ctx_row):
    return jnp.where(i < n_lat_tiles, i // tiles_per_batch, ctx_row)


def _adaln_kernel(c_ref, w_ref, b_ref, o_ref):
    c = c_ref[...]
    s = c * _sigmoid(c)
    o_ref[0] = jnp.dot(s, w_ref[0], precision=HIGHEST, preferred_element_type=F32) + b_ref[0]


def _adaln(cvec, ada_w, ada_b):
    depth, d, n = ada_w.shape
    tn = 1536
    return pl.pallas_call(
        _adaln_kernel,
        grid=(depth, n // tn),
        in_specs=[pl.BlockSpec((SUBLANES, d), lambda l, j: (0, 0)),
                  pl.BlockSpec((1, d, tn), lambda l, j: (l, 0, j)),
                  pl.BlockSpec((1, 1, tn), lambda l, j: (l, 0, j))],
        out_specs=pl.BlockSpec((1, SUBLANES, tn), lambda l, j: (l, 0, j)),
        out_shape=jax.ShapeDtypeStruct((depth, SUBLANES, n), F32),
        compiler_params=_params(2),
        name="adaln",
    )(cvec, ada_w, ada_b.reshape(depth, 1, n))


def _token_specs(tm, d, n_lat_tiles):
    return [pl.BlockSpec((tm, d), lambda i, *_: (jnp.minimum(i, n_lat_tiles - 1), 0)),
            pl.BlockSpec((tm, d), lambda i, *_: (jnp.maximum(i - n_lat_tiles, 0), 0))]


def _qkv_kernel(x_ref, c_ref, mod_ref, w_ref, b_ref, cos_ref, sin_ref, q_ref, k_ref, vt_ref, *,
                n_lat_tiles, tiles_per_batch, ctx_row, d, qd, kvd):
    i = pl.program_id(0)
    row = _mod_row(i, n_lat_tiles, tiles_per_batch, ctx_row)
    shift = mod_ref[pl.ds(row, 1), 0:d]
    scale = mod_ref[pl.ds(row, 1), d:2 * d]
    x = jnp.where(i < n_lat_tiles, x_ref[...], c_ref[...])
    h = x * (1.0 + scale) + shift
    z = jnp.dot(h.astype(BF16), w_ref[...], preferred_element_type=F32) + b_ref[...]
    nrot = qd + kvd
    qk = z[:, :nrot]
    reps = nrot // LANES
    cos = jnp.concatenate([cos_ref[...]] * reps, axis=1)
    sin = jnp.concatenate([sin_ref[...]] * reps, axis=1)
    lane = lax.broadcasted_iota(I32, qk.shape, 1)
    low_half = (lane & 16) == 0
    partner = jnp.where(low_half, pltpu.roll(qk, nrot - 16, 1), pltpu.roll(qk, 16, 1))
    qk = qk * cos + partner * sin
    q_ref[...] = (qk[:, :qd] * (ATTN_HEAD_DIM ** -0.5)).astype(BF16)
    k_ref[...] = qk[:, qd:].astype(BF16)
    vt_ref[...] = z[:, nrot:].T.astype(BF16)


def _rope_tables(s_len, tm):
    half = ATTN_HEAD_DIM // 4
    freqs = ROPE_THETA ** (-jnp.arange(half, dtype=F32) / half)
    t = jnp.arange(s_len)
    rows = (t // GRID_W).astype(F32)[:, None] * freqs[None, :]
    cols = (t % GRID_W).astype(F32)[:, None] * freqs[None, :]
    ang = jnp.concatenate([rows, rows, cols, cols], axis=1)
    sign = jnp.tile(jnp.concatenate([-jnp.ones((half,), F32), jnp.ones((half,), F32)]), 2)
    cos = jnp.cos(ang)
    sin = jnp.sin(ang) * sign[None, :]
    reps = LANES // ATTN_HEAD_DIM
    cos = jnp.concatenate([jnp.tile(cos, (1, reps)), jnp.ones((tm, LANES), F32)], axis=0)
    sin = jnp.concatenate([jnp.tile(sin, (1, reps)), jnp.zeros((tm, LANES), F32)], axis=0)
    return cos, sin


def _attn_qkv(x_lat, x_ctx, mod, w_qkv, b_qkv, dims):
    B, S, C, D = dims
    t_all = x_lat.shape[0] + x_ctx.shape[0]
    tm = 512
    n_lat_tiles = B * S // tm
    tiles_per_batch = S // tm
    ncols = w_qkv.shape[1]
    kvd = ATTN_KV_HEADS * ATTN_HEAD_DIM
    qd = ncols - 2 * kvd
    cos, sin = _rope_tables(S, tm)

    def tab_idx(i):
        return (jnp.where(i < n_lat_tiles, i % tiles_per_batch, tiles_per_batch), 0)

    kern = functools.partial(_qkv_kernel, n_lat_tiles=n_lat_tiles, tiles_per_batch=tiles_per_batch,
                             ctx_row=B, d=D, qd=qd, kvd=kvd)
    return pl.pallas_call(
        kern,
        grid=(t_all // tm,),
        in_specs=_token_specs(tm, D, n_lat_tiles) + [
                  pl.BlockSpec(mod.shape, lambda i: (0, 0)),
                  pl.BlockSpec((D, ncols), lambda i: (0, 0)),
                  pl.BlockSpec((1, ncols), lambda i: (0, 0)),
                  pl.BlockSpec((tm, LANES), tab_idx),
                  pl.BlockSpec((tm, LANES), tab_idx)],
        out_specs=[pl.BlockSpec((tm, qd), lambda i: (i, 0)),
                   pl.BlockSpec((tm, kvd), lambda i: (i, 0)),
                   pl.BlockSpec((kvd, tm), lambda i: (0, i))],
        out_shape=[jax.ShapeDtypeStruct((t_all, qd), BF16),
                   jax.ShapeDtypeStruct((t_all, kvd), BF16),
                   jax.ShapeDtypeStruct((kvd, t_all), BF16)],
        compiler_params=_params(1),
        name="attn_qkv",
    )(x_lat, x_ctx, mod, w_qkv.astype(BF16), b_qkv.reshape(1, ncols), cos, sin)


def _attn_kernel(sink_ref, q_ref, kp_ref, ko_ref, kn_ref, kc_ref, vp_ref, vo_ref, vn_ref, vc_ref, o_ref,
                 bias_sc, s_sc, p_sc, ot_sc, *,
                 n_lat_steps, nb, s_len, c_len):
    j = pl.program_id(0)
    is_lat = j < n_lat_steps
    n = j % nb
    blk = ATTN_BLOCK
    nloc = 3 * blk
    nk = nloc + c_len
    ki = lax.broadcasted_iota(I32, (nloc, blk), 0)
    qj = lax.broadcasted_iota(I32, (nloc, blk), 1)
    kpos = n * blk - WINDOW + ki
    qpos = n * blk + qj
    local_ok = (jnp.abs(kpos - qpos) <= WINDOW) & (kpos >= 0) & (kpos < s_len) & is_lat
    bias_sc[...] = jnp.where(local_ok, 0.0, -jnp.inf)
    hd = ATTN_HEAD_DIM
    group = q_ref.shape[1] // (ATTN_KV_HEADS * hd)
    kcat = jnp.concatenate([kp_ref[...], ko_ref[...], kn_ref[...], kc_ref[...]], axis=0)
    vcat_t = jnp.concatenate([vp_ref[...], vo_ref[...], vn_ref[...], vc_ref[...]], axis=1)
    def score_matmul(kh):
        q_grp = jnp.concatenate([q_ref[:, (kh * group + g) * hd:(kh * group + g + 1) * hd]
                                 for g in range(group)], axis=0)
        s_sc[kh] = lax.dot_general(kcat[:, kh * hd:(kh + 1) * hd], q_grp, (((1,), (1,)), ((), ())),
                                   preferred_element_type=F32)

    def weighted_values(kh, sink_terms):
        v_ones = jnp.concatenate([vcat_t[kh * hd:(kh + 1) * hd, :], jnp.ones((SUBLANES, nk), BF16)], axis=0)
        o_aug = jnp.dot(v_ones, p_sc[kh], preferred_element_type=F32)
        l = o_aug[hd:hd + 1, :] + jnp.concatenate(sink_terms, axis=1)
        o_t = o_aug[0:hd, :] * (1.0 / l)
        for g in range(group):
            h = kh * group + g
            ot_sc[h * hd:(h + 1) * hd, :] = o_t[:, g * blk:(g + 1) * blk]

    score_matmul(0)
    for kh in range(ATTN_KV_HEADS):
        if kh + 1 < ATTN_KV_HEADS:
            score_matmul(kh + 1)
        sink_terms = []
        for g in range(group):
            cols = slice(g * blk, (g + 1) * blk)
            def scores(a):
                s = s_sc[kh, a:a + blk, cols]
                return s + bias_sc[a:a + blk, :] if a < nloc else s

            sk = sink_ref[kh * group + g]
            top = scores(0)
            for a in range(blk, nk, blk):
                top = jnp.maximum(top, scores(a))
            m = jnp.maximum(jnp.max(top, axis=0, keepdims=True), sk)
            for a in range(0, nk, blk):
                p_sc[kh, a:a + blk, cols] = jnp.exp((scores(a) - m).astype(BF16))
            sink_terms.append(jnp.exp(sk - m))
        weighted_values(kh, sink_terms)
    o_ref[...] = ot_sc[...].T.astype(BF16)


def _attention(q_all, k_all, vt_all, sink, dims):
    B, S, C, D = dims
    blk = ATTN_BLOCK
    nb = S // blk
    n_lat_steps = B * nb
    ctx_steps_per_batch = C // blk
    n_steps = n_lat_steps + B * ctx_steps_per_batch
    qd = q_all.shape[1]
    kvd = k_all.shape[1]

    def local_idx(off):
        def idx(j):
            b = j // nb
            nn = jnp.clip(j % nb + off, 0, nb - 1)
            return (jnp.where(j < n_lat_steps, b * nb + nn, j), 0)
        return idx

    def ctx_idx(j):
        b = jnp.where(j < n_lat_steps, j // nb, (j - n_lat_steps) // ctx_steps_per_batch)
        return (B * S // C + b, 0)

    swap = lambda f: (lambda j: f(j)[::-1])
    loc = lambda off: pl.BlockSpec((blk, kvd), local_idx(off))
    ctxs = pl.BlockSpec((C, kvd), ctx_idx)
    loc_t = lambda off: pl.BlockSpec((kvd, blk), swap(local_idx(off)))
    ctxs_t = pl.BlockSpec((kvd, C), swap(ctx_idx))
    group = qd // kvd
    nk = 3 * blk + C
    kern = functools.partial(_attn_kernel, n_lat_steps=n_lat_steps, nb=nb, s_len=S, c_len=C)
    return pl.pallas_call(
        kern,
        grid=(n_steps,),
        in_specs=[pl.BlockSpec(memory_space=pltpu.SMEM),
                  pl.BlockSpec((blk, qd), lambda j: (j, 0)),
                  loc(-1), loc(0), loc(1), ctxs,
                  loc_t(-1), loc_t(0), loc_t(1), ctxs_t],
        out_specs=pl.BlockSpec((blk, qd), lambda j: (j, 0)),
        out_shape=jax.ShapeDtypeStruct((q_all.shape[0], qd), BF16),
        scratch_shapes=[pltpu.VMEM((3 * blk, blk), F32),
                        pltpu.VMEM((ATTN_KV_HEADS, nk, group * blk), F32),
                        pltpu.VMEM((ATTN_KV_HEADS, nk, group * blk), BF16),
                        pltpu.VMEM((qd, blk), F32)],
        compiler_params=_params(1),
        name="attn_core",
    )(sink, q_all, k_all, k_all, k_all, k_all, vt_all, vt_all, vt_all, vt_all)


def _split_bf16(a):
    hi = a.astype(BF16)
    return hi, (a - hi.astype(F32)).astype(BF16)


def _split_weight(w_f32):
    hi = w_f32.astype(BF16)
    return jnp.concatenate([hi, (w_f32 - hi.astype(F32)).astype(BF16)], axis=1)


def _dot_split(a, w_ref):
    a_hi, a_lo = _split_bf16(a)
    n = w_ref.shape[1] // 2
    both = jnp.dot(a_hi, w_ref[...], preferred_element_type=F32)
    return both[:, :n] + both[:, n:] + jnp.dot(a_lo, w_ref[:, :n], preferred_element_type=F32)


def _route(hx, rw_ref, rb_ref, topi_ref, gate_ref, sizes_ref):
    logits = _dot_split(hx, rw_ref) + rb_ref[...]
    lane = lax.broadcasted_iota(I32, logits.shape, 1)
    lanef = lane.astype(F32)
    vals, idxs = [], []
    l = logits
    for _ in range(TOP_K):
        m = jnp.max(l, axis=1, keepdims=True)
        idx = jnp.min(jnp.where(l == m, lanef, float(LANES)), axis=1, keepdims=True)
        vals.append(m)
        idxs.append(idx)
        l = jnp.where(lanef == idx, -jnp.inf, l)
    es = [jnp.exp(v - vals[0]) for v in vals]
    denom = es[0]
    for e in es[1:]:
        denom = denom + e
    topi = jnp.zeros(logits.shape, F32)
    gates = jnp.zeros(logits.shape, F32)
    for k in range(TOP_K):
        topi = jnp.where(lane == k, idxs[k], topi)
        gates = jnp.where(lane == k, es[k] / denom, gates)
    topi_ref[...] = topi.astype(I32)
    gate_ref[...] = gates
    chosen = jnp.zeros(logits.shape, F32)
    for idx in idxs:
        chosen = chosen + jnp.where(lanef == idx, 1.0, 0.0)
    sizes = [jnp.ceil(jnp.sum(chosen[r:r + ROUTE_TILE], axis=0, keepdims=True) / GROUP_ALIGN) * GROUP_ALIGN
             for r in range(0, logits.shape[0], ROUTE_TILE)]
    sizes_ref[0] = jnp.concatenate(sizes + [jnp.zeros((SUBLANES - len(sizes), LANES), F32)], axis=0)


def _post_common(a, x, mod_ref, w_ref, b_ref, lng_ref, lnb_ref, rw_ref, rb_ref,
                 x1_ref, hx_ref, topi_ref, gate_ref, sizes_ref, row, d, alpha):
    y = jnp.dot(a, w_ref[...], preferred_element_type=F32) + b_ref[...]
    gate_mix = mod_ref[pl.ds(row, 1), 2 * d:3 * d]
    shift = mod_ref[pl.ds(row, 1), 3 * d:4 * d]
    scale = mod_ref[pl.ds(row, 1), 4 * d:5 * d]
    x1 = _layer_norm(alpha * x + gate_mix * y, lng_ref[...], lnb_ref[...])
    hx = x1 * (1.0 + scale) + shift
    x1_ref[...] = x1
    hx_ref[...] = hx.astype(BF16)
    _route(hx, rw_ref, rb_ref, topi_ref, gate_ref, sizes_ref)


def _post_attn_kernel(o_ref, x_ref, c_ref, mod_ref, w_ref, b_ref, lng_ref, lnb_ref, rw_ref, rb_ref,
                      x1_ref, hx_ref, topi_ref, gate_ref, sizes_ref, *, n_lat_tiles, tiles_per_batch, ctx_row, d, alpha):
    i = pl.program_id(0)
    row = _mod_row(i, n_lat_tiles, tiles_per_batch, ctx_row)
    x = jnp.where(i < n_lat_tiles, x_ref[...], c_ref[...])
    _post_common(o_ref[...], x, mod_ref, w_ref, b_ref, lng_ref, lnb_ref, rw_ref, rb_ref,
                 x1_ref, hx_ref, topi_ref, gate_ref, sizes_ref, row, d, alpha)


def _post_mlstm_kernel(hf_ref, hb_ref, og_ref, ng_ref, x_ref, mod_ref, w_ref, b_ref, lng_ref, lnb_ref,
                       rw_ref, rb_ref, x1_ref, hx_ref, topi_ref, gate_ref, sizes_ref, *,
                       n_lat_tiles, tiles_per_batch, ctx_row, d, alpha):
    row = _mod_row(pl.program_id(0), n_lat_tiles, tiles_per_batch, ctx_row)
    hsum = hf_ref[...] + hb_ref[...]
    vdim = hsum.shape[1] // ML_HEADS
    parts = []
    for h in range(ML_HEADS):
        seg = hsum[:, h * vdim:(h + 1) * vdim]
        mu = jnp.mean(seg, axis=1, keepdims=True)
        sc = seg - mu
        var = jnp.mean(sc * sc, axis=1, keepdims=True)
        parts.append(sc * lax.rsqrt(var + LN_EPS))
    y = jnp.concatenate(parts, axis=1) * ng_ref[...]
    a = (og_ref[...].astype(F32) * y).astype(BF16)
    _post_common(a, x_ref[...], mod_ref, w_ref, b_ref, lng_ref, lnb_ref, rw_ref, rb_ref,
                 x1_ref, hx_ref, topi_ref, gate_ref, sizes_ref, row, d, alpha)


def _post_mixer(mixer_inputs, norm_g, x_all, mod, w_o, b_o, ln_g, ln_b, router_w, router_b, dims, n_rows, alpha):
    B, S, C, D = dims
    tm = 512
    n_lat_tiles = B * S // tm
    tiles_per_batch = S // tm
    n_exp = router_w.shape[1]
    rw = _split_weight(jnp.zeros((D, LANES), F32).at[:, :n_exp].set(router_w))
    rb =jnp.full((1, LANES), -1e30, F32).at[0, :n_exp].set(router_b)
    row_spec = lambda w: pl.BlockSpec((tm, w), lambda i: (i, 0))
    full = lambda a: pl.BlockSpec(a.shape, lambda i: (0,) * a.ndim)
    common = dict(n_lat_tiles=n_lat_tiles, tiles_per_batch=tiles_per_batch, ctx_row=B, d=D, alpha=alpha)
    w_bf = w_o.astype(BF16)
    b2 = b_o.reshape(1, D)
    if isinstance(x_all, tuple):
        x_args, x_specs = list(x_all), _token_specs(tm, D, n_lat_tiles)
    else:
        x_args, x_specs = [x_all], [row_spec(D)]
    tail = x_args + [mod, w_bf, b2, ln_g.reshape(1, D), ln_b.reshape(1, D), rw, rb]
    tail_specs = x_specs + [full(mod), full(w_bf), full(b2), pl.BlockSpec((1, D), lambda i: (0, 0)),
                            pl.BlockSpec((1, D), lambda i: (0, 0)), full(rw), full(rb)]
    if norm_g is None:
        kern = functools.partial(_post_attn_kernel, **common)
        args = list(mixer_inputs) + tail
        specs = [row_spec(mixer_inputs[0].shape[1])] + tail_specs
        name = "post_attn"
    else:
        kern = functools.partial(_post_mlstm_kernel, **common)
        ng = norm_g.reshape(1, -1)
        args = list(mixer_inputs) + [ng] + tail
        specs = [row_spec(a.shape[1]) for a in mixer_inputs] + [full(ng)] + tail_specs
        name = "post_mlstm"
    return pl.pallas_call(
        kern,
        grid=(n_rows // tm,),
        in_specs=specs,
        out_specs=[row_spec(D), row_spec(D), row_spec(LANES), row_spec(LANES),
                   pl.BlockSpec((1, SUBLANES, LANES), lambda i: (i, 0, 0))],
        out_shape=[jax.ShapeDtypeStruct((n_rows, D), F32), jax.ShapeDtypeStruct((n_rows, D), BF16),
                   jax.ShapeDtypeStruct((n_rows, LANES), I32), jax.ShapeDtypeStruct((n_rows, LANES), F32),
                   jax.ShapeDtypeStruct((n_rows // tm, SUBLANES, LANES), F32)],
        compiler_params=_params(1),
        name=name,
    )(*args)


def _exclusive_lane_cumsum(row):
    r = lax.broadcasted_iota(I32, (LANES, LANES), 0)
    c = lax.broadcasted_iota(I32, (LANES, LANES), 1)
    before = jnp.where(r < c, 1.0, 0.0)
    return jnp.dot(jnp.broadcast_to(row, (SUBLANES, LANES)), before,
                   precision=HIGHEST, preferred_element_type=F32)[0:1]


def _for_each_unit(tab_ref, tile, fn):
    base = tile * TABLE_W
    for blk, rows in ((0, BIG_UNIT), (2, GROUP_ALIGN)):
        def body(j, carry, blk=blk, rows=rows):
            slot_row = pl.multiple_of(tab_ref[base + blk * LANES + j], GROUP_ALIGN)
            stage_row = pl.multiple_of(tab_ref[base + (blk + 1) * LANES + j], GROUP_ALIGN)
            fn(slot_row, stage_row, rows)
            return carry
        lax.fori_loop(0, tab_ref[base + blk * LANES + LANES - 1], body, 0)


def _slots_kernel(topi_ref, sizes_ref, col_ref, colt_ref, tab_ref, meta_ref, base_sc, carry_sc, *, tile):
    i = pl.program_id(0)
    tm = topi_ref.shape[0]
    topi = topi_ref[...]
    lane = lax.broadcasted_iota(I32, topi.shape, 1)
    sel = [lane == topi[:, k:k + 1] for k in range(TOP_K)]
    maskf = jnp.where(sel[0], 1.0, 0.0)
    for s in sel[1:]:
        maskf = maskf + jnp.where(s, 1.0, 0.0)
    n8 = jnp.ceil(jnp.sum(maskf, axis=0, keepdims=True) / GROUP_ALIGN) * GROUP_ALIGN

    @pl.when(i == 0)
    def _():
        tot = jnp.sum(jnp.sum(sizes_ref[...], axis=0), axis=0, keepdims=True)
        padded = jnp.ceil(tot / tile) * tile
        base = _exclusive_lane_cumsum(padded)
        base_sc[...] = base
        carry_sc[...] = jnp.zeros_like(carry_sc)
        rowi = lax.broadcasted_iota(I32, meta_ref.shape, 0)
        meta_ref[...] = jnp.where(rowi == 0, tot, jnp.where(rowi == 1, base, padded)).astype(I32)

    start = base_sc[...] + carry_sc[...]
    off = _exclusive_lane_cumsum(n8)
    r = lax.broadcasted_iota(I32, (tm, tm), 0)
    c = lax.broadcasted_iota(I32, (tm, tm), 1)
    earlier = jnp.where(c < r, 1.0, 0.0).astype(BF16)
    rank = jnp.dot(earlier, maskf.astype(BF16), preferred_element_type=F32)
    stage_row = rank + off
    out = jnp.zeros(topi.shape, F32)
    for k in range(TOP_K):
        pk = jnp.sum(jnp.where(sel[k], stage_row, 0.0), axis=1, keepdims=True)
        out = jnp.where(lane == k, pk, out)
    col_ref[...] = out.astype(I32)
    colt_ref[0] = out.T[0:SUBLANES].astype(I32)

    n_big = jnp.floor(n8 / BIG_UNIT)
    n_small = (n8 - BIG_UNIT * n_big) / GROUP_ALIGN
    first_big = _exclusive_lane_cumsum(n_big)
    first_small = _exclusive_lane_cumsum(n_small)
    rows = jnp.concatenate([off, start, first_big, n_big, first_small, n_small,
                            jnp.zeros((LANES - 6, LANES), F32)], axis=0)
    per_expert = rows.T
    off_c, start_c, fb_c, nb_c, fs_c, ns_c = (per_expert[:, k:k + 1] for k in range(6))
    u = lax.broadcasted_iota(I32, (LANES, LANES), 1).astype(F32)
    lane_t = lax.broadcasted_iota(I32, (1, LANES), 1)

    def unit_list(first_c, count_c, rel, count_row):
        inside = (u >= first_c) & (u < first_c + count_c)
        src = jnp.sum(jnp.where(inside, start_c + rel, 0.0), axis=0, keepdims=True)
        dst = jnp.sum(jnp.where(inside, off_c + rel, 0.0), axis=0, keepdims=True)
        total = jnp.sum(count_row, axis=1, keepdims=True)
        return [jnp.where(lane_t == LANES - 1, total, src), dst]

    lists = (unit_list(fb_c, nb_c, BIG_UNIT * (u - fb_c), n_big)
             + unit_list(fs_c, ns_c, BIG_UNIT * nb_c + GROUP_ALIGN * (u - fs_c), n_small))
    tab_ref[0] = jnp.concatenate(lists, axis=1).astype(I32)
    carry_sc[...] = carry_sc[...] + n8


def _slots(topi, sizes, tile):
    t = topi.shape[0]
    tm = ROUTE_TILE
    nt = t // tm
    return pl.pallas_call(
        functools.partial(_slots_kernel, tile=float(tile)),
        grid=(nt,),
        in_specs=[pl.BlockSpec((tm, LANES), lambda i: (i, 0)),
                  pl.BlockSpec(sizes.shape, lambda i: (0, 0, 0))],
        out_specs=[pl.BlockSpec((tm, LANES), lambda i: (i, 0)),
                   pl.BlockSpec((1, SUBLANES, tm), lambda i: (i, 0, 0)),
                   pl.BlockSpec((1, 1, TABLE_W), lambda i: (i, 0, 0)),
                   pl.BlockSpec((SUBLANES, LANES), lambda i: (0, 0))],
        out_shape=[jax.ShapeDtypeStruct((t, LANES), I32),
                   jax.ShapeDtypeStruct((nt, SUBLANES, tm), I32),
                   jax.ShapeDtypeStruct((nt, 1, TABLE_W), I32),
                   jax.ShapeDtypeStruct((SUBLANES, LANES), I32)],
        scratch_shapes=[pltpu.VMEM((1, LANES), F32), pltpu.VMEM((1, LANES), F32)],
        compiler_params=_params(1),
        name="moe_slots",
    )(topi, sizes)


def _dispatch_kernel(tab_ref, pad_start_ref, pad_units_ref, tail_ref, colt_ref, hx_ref, xs_ref, stage, zeros, sem, zsem):
    i = pl.program_id(0)
    nt = pl.num_programs(0)
    slot = i % 2
    kb = stage.shape[1]
    tm = hx_ref.shape[0]

    def scatter(buf_slot):
        def copy(slot_row, stage_row, rows):
            return pltpu.make_async_copy(stage.at[buf_slot, pl.ds(stage_row, rows)],
                                         xs_ref.at[pl.ds(slot_row, rows)], sem.at[buf_slot])
        return copy

    def drain(tile, buf_slot):
        _for_each_unit(tab_ref, tile, lambda *u: scatter(buf_slot)(*u).wait())

    @pl.when(i >= 2)
    def _():
        drain(i - 2, slot)

    colt = colt_ref[0]
    c = lax.broadcasted_iota(I32, (kb, tm), 0)
    onehot = jnp.zeros((kb, tm), F32)
    for k in range(TOP_K):
        onehot = jnp.where(c == colt[k:k + 1, :], 1.0, onehot)
    stage[slot] = jnp.dot(onehot.astype(BF16), hx_ref[...], preferred_element_type=F32).astype(BF16)

    _for_each_unit(tab_ref, i, lambda *u: scatter(slot)(*u).start())

    @pl.when(i == nt - 1)
    def _():
        zeros[...] = jnp.zeros_like(zeros)
        n_exp = pad_start_ref.shape[0]

        def zero_copy(e, u):
            dst = pl.multiple_of(pad_start_ref[e] + u * GROUP_ALIGN, GROUP_ALIGN)
            return pltpu.make_async_copy(zeros.at[pl.ds(0, GROUP_ALIGN)], xs_ref.at[pl.ds(dst, GROUP_ALIGN)], zsem)

        def per_expert(fn):
            def outer(e, carry):
                def inner(u, carry2):
                    fn(e, u)
                    return carry2
                lax.fori_loop(0, pad_units_ref[e], inner, 0)
                return carry
            lax.fori_loop(0, n_exp, outer, 0)

        per_expert(lambda e, u: zero_copy(e, u).start())
        per_expert(lambda e, u: zero_copy(e, u).wait())

        zrows = zeros.shape[0]

        def tail_copy(u):
            dst = pl.multiple_of(tail_ref[0] + u * zrows, zrows)
            return pltpu.make_async_copy(zeros, xs_ref.at[pl.ds(dst, zrows)], zsem)

        def tail_loop(fn):
            def body(u, carry):
                fn(u)
                return carry
            lax.fori_loop(0, tail_ref[1], body, 0)

        tail_loop(lambda u: tail_copy(u).start())
        tail_loop(lambda u: tail_copy(u).wait())

        @pl.when(nt >= 2)
        def _():
            drain(i - 1, 1 - slot)
        drain(i, slot)


def _dispatch(tab, pad_start, pad_units, tail, colt, hx, n_slots):
    t, d = hx.shape
    tm = ROUTE_TILE
    return pl.pallas_call(
        _dispatch_kernel,
        grid_spec=pltpu.PrefetchScalarGridSpec(
            num_scalar_prefetch=4,
            grid=(t // tm,),
            in_specs=[pl.BlockSpec((1, SUBLANES, tm), lambda i, *_: (i, 0, 0)),
                      pl.BlockSpec((tm, d), lambda i, *_: (i, 0))],
            out_specs=pl.BlockSpec(memory_space=pl.ANY),
            scratch_shapes=[pltpu.VMEM((2, STAGE_ROWS, d), BF16), pltpu.VMEM((ZERO_ROWS, d), BF16),
                            pltpu.SemaphoreType.DMA((2,)), pltpu.SemaphoreType.DMA(())]),
        out_shape=jax.ShapeDtypeStruct((n_slots, d), BF16),
        compiler_params=_params(1),
        name="moe_dispatch",
    )(tab, pad_start, pad_units, tail, colt, hx)


def _expert_kernel(te_ref, rows_ref, next_ref, nu_ref, xs_ref, wgu_hbm, bgu_ref, wd_hbm, bd_ref, ys_ref,
                   wgu_f32, wd_f32, wgu_sc, wd_sc, wsem, *, layer):
    j = pl.program_id(0)
    active = j < nu_ref[0]
    changed = (j == 0) | (te_ref[j] != te_ref[jnp.maximum(j - 1, 0)])
    ff = wd_sc.shape[0]
    te = xs_ref.shape[0]
    half = te // 2

    def weight_copies(e):
        return (pltpu.make_async_copy(wgu_hbm.at[layer, e], wgu_f32, wsem.at[0]),
                pltpu.make_async_copy(wd_hbm.at[layer, e], wd_f32, wsem.at[1]))

    @pl.when(active & changed)
    def _():
        @pl.when(j == 0)
        def _():
            for cp in weight_copies(te_ref[0]):
                cp.start()

        for cp in weight_copies(te_ref[j]):
            cp.wait()
        wgu_sc[...] = wgu_f32[...].astype(BF16)
        wd_sc[...] = wd_f32[...].astype(BF16)

        @pl.when(next_ref[j] >= 0)
        def _():
            for cp in weight_copies(next_ref[j]):
                cp.start()

    def run(rows):
        x = xs_ref[0:rows, :]
        nblk = 2
        fb = ff // nblk

        def up(c):
            gl = jnp.dot(x, wgu_sc[:, c * fb:(c + 1) * fb], preferred_element_type=F32) + bgu_ref[:, c * fb:(c + 1) * fb]
            lin = (jnp.dot(x, wgu_sc[:, ff + c * fb:ff + (c + 1) * fb], preferred_element_type=F32)
                   + bgu_ref[:, ff + c * fb:ff + (c + 1) * fb])
            return gl, lin

        y = bd_ref[...]
        nxt = up(0)
        for c in range(nblk):
            gl, lin = nxt
            if c + 1 < nblk:
                nxt = up(c + 1)
            gl = jnp.minimum(gl, SWIGLU_LIMIT)
            lin = jnp.clip(lin, -SWIGLU_LIMIT, SWIGLU_LIMIT)
            act = gl * _sigmoid(SWIGLU_ALPHA * gl) * (lin + 1.0)
            y = y + jnp.dot(act.astype(BF16), wd_sc[c * fb:(c + 1) * fb, :], preferred_element_type=F32)
        ys_ref[0:rows, :] = y.astype(BF16)
        if rows < te:
            ys_ref[rows:te, :] = jnp.zeros((te - rows, ys_ref.shape[1]), BF16)

    valid_rows = rows_ref[j]

    @pl.when(active & (valid_rows > half))
    def _():
        run(te)

    @pl.when(active & (valid_rows <= half))
    def _():
        run(half)

    @pl.when(jnp.logical_not(active))
    def _():
        ys_ref[...] = jnp.zeros_like(ys_ref)


def _experts(tile_expert, tile_rows, next_expert, n_used, xs, layer, w_gu, b_gu, w_down, b_down):
    n_slots, d = xs.shape
    depth, n_exp, _, ff2 = w_gu.shape
    ff = w_down.shape[2]
    te = EXPERT_TILE
    row_idx = lambda j, te_r, rows_r, nx_r, nu: (jnp.maximum(jnp.minimum(j, nu[0] - 1), 0), 0)
    b_idx = lambda j, te_r, rows_r, nx_r, nu: (layer, te_r[j], 0, 0)
    return pl.pallas_call(
        functools.partial(_expert_kernel, layer=layer),
        grid_spec=pltpu.PrefetchScalarGridSpec(
            num_scalar_prefetch=4,
            grid=(n_slots // te,),
            in_specs=[pl.BlockSpec((te, d), row_idx),
                      pl.BlockSpec(memory_space=pl.ANY),
                      pl.BlockSpec((None, None, 1, ff2), b_idx),
                      pl.BlockSpec(memory_space=pl.ANY),
                      pl.BlockSpec((None, None, 1, d), b_idx)],
            out_specs=pl.BlockSpec((te, d), lambda j, te_r, rows_r, nx_r, nu: (j, 0)),
            scratch_shapes=[pltpu.VMEM((d, ff2), F32), pltpu.VMEM((ff, d), F32),
                            pltpu.VMEM((d, ff2), BF16), pltpu.VMEM((ff, d), BF16),
                            pltpu.SemaphoreType.DMA((2,))]),
        out_shape=jax.ShapeDtypeStruct((n_slots, d), BF16),
        compiler_params=_params(1),
        name="moe_experts",
    )(tile_expert, tile_rows, next_expert, n_used, xs, w_gu, b_gu.reshape(depth, n_exp, 1, ff2),
      w_down, b_down.reshape(depth, n_exp, 1, d))


def _combine_kernel(tab_ref, col_ref, gate_ref, ys_ref, x_ref, mod_ref, lng_ref, lnb_ref, out_ref, stage, sem, *,
                    n_lat_tiles, tiles_per_batch, ctx_row, d, alpha):
    i = pl.program_id(0)
    nt = pl.num_programs(0)
    slot = i % 2
    kb = stage.shape[1]
    tm = x_ref.shape[0]

    def gather(buf_slot):
        def copy(slot_row, stage_row, rows):
            return pltpu.make_async_copy(ys_ref.at[pl.ds(slot_row, rows)],
                                         stage.at[buf_slot, pl.ds(stage_row, rows)], sem.at[buf_slot])
        return copy

    @pl.when(i == 0)
    def _():
        stage[...] = jnp.zeros_like(stage)
        _for_each_unit(tab_ref, 0, lambda *u: gather(0)(*u).start())

    @pl.when(i + 1 < nt)
    def _():
        _for_each_unit(tab_ref, i + 1, lambda *u: gather(1 - slot)(*u).start())

    _for_each_unit(tab_ref, i, lambda *u: gather(slot)(*u).wait())

    col = col_ref[...]
    gates = gate_ref[...]
    kblk = 2 * LANES

    def gate_weights(r0):
        c = r0 + lax.broadcasted_iota(I32, (tm, kblk), 1)
        w = jnp.zeros((tm, kblk), F32)
        for k in range(TOP_K):
            w = jnp.where(c == col[:, k:k + 1], gates[:, k:k + 1], w)
        return w.astype(BF16)

    y = jnp.zeros((tm, d), F32)
    nxt = gate_weights(0)
    for r0 in range(0, kb, kblk):
        w = nxt
        if r0 + kblk < kb:
            nxt = gate_weights(r0 + kblk)
        y = y + jnp.dot(w, stage[slot, r0:r0 + kblk, :], preferred_element_type=F32)
    row = _mod_row(i, n_lat_tiles, tiles_per_batch, ctx_row)
    gate_mlp = mod_ref[pl.ds(row, 1), 5 * d:6 * d]
    out_ref[...] = _layer_norm(alpha * x_ref[...] + gate_mlp * y, lng_ref[...], lnb_ref[...])


def _combine(tab, col, ys, gates, x1, mod, ln_g, ln_b, dims, alpha):
    B, S, C, D = dims
    t = x1.shape[0]
    tm = ROUTE_TILE
    kern = functools.partial(_combine_kernel, n_lat_tiles=B * S // tm, tiles_per_batch=S // tm, ctx_row=B,
                             d=D, alpha=alpha)
    return pl.pallas_call(
        kern,
        grid_spec=pltpu.PrefetchScalarGridSpec(
            num_scalar_prefetch=1,
            grid=(t // tm,),
            in_specs=[pl.BlockSpec((tm, LANES), lambda i, tab_r: (i, 0)),
                      pl.BlockSpec((tm, LANES), lambda i, tab_r: (i, 0)),
                      pl.BlockSpec(memory_space=pl.ANY),
                      pl.BlockSpec((tm, D), lambda i, tab_r: (i, 0)),
                      pl.BlockSpec(mod.shape, lambda i, tab_r: (0, 0)),
                      pl.BlockSpec((1, D), lambda i, tab_r: (0, 0)),
                      pl.BlockSpec((1, D), lambda i, tab_r: (0, 0))],
            out_specs=pl.BlockSpec((tm, D), lambda i, tab_r: (i, 0)),
            scratch_shapes=[pltpu.VMEM((2, STAGE_ROWS, D), BF16), pltpu.SemaphoreType.DMA((2,))]),
        out_shape=jax.ShapeDtypeStruct((t, D), F32),
        compiler_params=_params(1),
        name="moe_combine",
    )(tab, col, gates, ys, x1, mod, ln_g.reshape(1, D), ln_b.reshape(1, D))


def _moe(hx, topi, gates, sizes, x1, mod, ln_g, ln_b, layer, w_gu, b_gu, w_down, b_down, dims, alpha):
    t = hx.shape[0]
    n_exp = w_gu.shape[1]
    te = EXPERT_TILE
    n_route_tiles = t // ROUTE_TILE
    max_rows = t * TOP_K + (GROUP_ALIGN - 1) * n_exp * n_route_tiles
    n_tiles = -(-max_rows // te) + n_exp
    col, colt, tab, meta = _slots(topi, sizes, te)
    tab = tab.reshape(-1)
    tot, base, padded = meta[0, :n_exp], meta[1, :n_exp], meta[2, :n_exp]
    ends = jnp.cumsum(padded // te)
    n_used = ends[-1:].astype(I32)
    tile_ids = jnp.minimum(jnp.arange(n_tiles, dtype=I32), n_used[0] - 1)
    tile_expert = jnp.sum((tile_ids[:, None] >= ends[None, :]).astype(I32), axis=1)
    tile_expert = jnp.minimum(tile_expert, n_exp - 1).astype(I32)
    pad_start = (base + tot).astype(I32)
    pad_units = ((padded - tot) // GROUP_ALIGN).astype(I32)
    used_rows = n_used[0] * te
    tail = jnp.stack([used_rows, (n_tiles * te - used_rows) // ZERO_ROWS]).astype(I32)
    xs = _dispatch(tab, pad_start, pad_units, tail, colt, hx, n_tiles * te)
    experts = jnp.arange(n_exp, dtype=I32)
    of_tile = tile_expert[:, None] == experts[None, :]
    pick = lambda per_expert: jnp.sum(jnp.where(of_tile, per_expert[None, :], 0), axis=1).astype(I32)
    tile_rows = jnp.clip(pick(base + tot) - tile_ids * te, 0, te).astype(I32)
    later = (padded[None, :] > 0) & (experts[None, :] > experts[:, None])
    next_of = jnp.min(jnp.where(later, experts[None, :], n_exp), axis=1)
    next_expert = pick(jnp.where(next_of == n_exp, -1, next_of))
    ys = _experts(tile_expert, tile_rows, next_expert, n_used, xs, layer, w_gu, b_gu, w_down, b_down)
    return _combine(tab, col, ys, gates, x1, mod, ln_g, ln_b, dims, alpha)


def _ml_in_kernel(xp_ref, x_ref, xn_ref, mod_ref, wqk_ref, bqk_ref, wvo_ref, bvo_ref, wg_ref, bg_ref,
                  cw_ref, cb_ref, q_ref, k_ref, v_ref, og_ref, g_ref, *,
                  n_lat_tiles, tiles_per_batch, ctx_tiles_per_seq, ctx_row, d):
    i = pl.program_id(0)
    tm = x_ref.shape[0]
    halo = SUBLANES
    row = _mod_row(i, n_lat_tiles, tiles_per_batch, ctx_row)
    shift = mod_ref[pl.ds(row, 1), 0:d]
    scale = mod_ref[pl.ds(row, 1), d:2 * d]
    is_lat = i < n_lat_tiles
    seq_tile = jnp.where(is_lat, i % tiles_per_batch, (i - n_lat_tiles) % ctx_tiles_per_seq)
    seq_tiles = jnp.where(is_lat, tiles_per_batch, ctx_tiles_per_seq)
    first = seq_tile == 0
    last = seq_tile == seq_tiles - 1

    h = x_ref[...] * (1.0 + scale) + shift
    h_ext = jnp.concatenate([xp_ref[...] * (1.0 + scale) + shift, h, xn_ref[...] * (1.0 + scale) + shift], axis=0)
    z = jnp.dot(h_ext.astype(BF16), wqk_ref[...], preferred_element_type=F32) + bqk_ref[...]
    n_ext = tm + 2 * halo
    cw = cw_ref[...]
    r = lax.broadcasted_iota(I32, z.shape, 0)
    z = jnp.where(((r < halo) & first) | ((r >= halo + tm) & last), 0.0, z)
    acc = None
    for j in range(ML_CONV_W):
        sh = (ML_CONV_W // 2 - j) % n_ext
        zj = z if sh == 0 else pltpu.roll(z, sh, 0)
        term = zj[halo:halo + tm] * cw[j:j + 1]
        acc = term if acc is None else acc + term
    qk = acc + cb_ref[...]
    qk = qk * _sigmoid(qk)
    nqk = qk.shape[1] // 2
    qk_dim = nqk // ML_HEADS
    q_ref[...] = (qk[:, :nqk] * (qk_dim ** -0.5)).astype(BF16)
    k_ref[...] = qk[:, nqk:].astype(BF16)

    vo = jnp.dot(h.astype(BF16), wvo_ref[...], preferred_element_type=F32) + bvo_ref[...]
    nv = vo.shape[1] // 2
    v_ref[...] = vo[:, :nv].astype(BF16)
    og_ref[...] = _sigmoid(vo[:, nv:]).astype(BF16)

    zg = _dot_split(h, wg_ref) + bg_ref[...]
    g = GATE_CAP * jnp.tanh(zg / GATE_CAP)
    log_sig = jnp.minimum(g, 0.0) - jnp.log(1.0 + jnp.exp(-jnp.abs(g)))
    lane = lax.broadcasted_iota(I32, g.shape, 1)
    is_forget = ((lane // ML_HEADS) % 2) == 1
    g_ref[...] = jnp.where(is_forget, log_sig, g)


def _ml_in(x_all, mod, w_in, b_in, conv_w, conv_b, dims):
    B, S, C, D = dims
    t_all = x_all.shape[0]
    tm = 256
    assert C % tm == 0 and S % tm == 0
    n_lat_tiles = B * S // tm
    nqk2 = conv_w.shape[1]
    nv = (w_in.shape[1] - nqk2 - 4 * ML_HEADS) // 2
    ng = 4 * ML_HEADS
    w_qk = w_in[:, :nqk2].astype(BF16)
    w_vo = w_in[:, nqk2:nqk2 + 2 * nv].astype(BF16)
    w_g = _split_weight(jnp.zeros((D, LANES), F32).at[:, :ng].set(w_in[:, nqk2 + 2 * nv:]))
    b_qk = b_in[:nqk2].reshape(1, -1)
    b_vo = b_in[nqk2:nqk2 + 2 * nv].reshape(1, -1)
    b_g = jnp.zeros((1, LANES), F32).at[0, :ng].set(b_in[nqk2 + 2 * nv:])
    cw = jnp.zeros((SUBLANES, nqk2), F32).at[:ML_CONV_W].set(conv_w)
    cb = conv_b.reshape(1, -1)
    hb = tm // SUBLANES
    n_hblk = t_all // SUBLANES
    full = lambda a: pl.BlockSpec(a.shape, lambda i: (0,) * a.ndim)
    row_spec = lambda w: pl.BlockSpec((tm, w), lambda i: (i, 0))
    kern = functools.partial(_ml_in_kernel, n_lat_tiles=n_lat_tiles, tiles_per_batch=S // tm,
                             ctx_tiles_per_seq=C // tm, ctx_row=B, d=D)
    return pl.pallas_call(
        kern,
        grid=(t_all // tm,),
        in_specs=[pl.BlockSpec((SUBLANES, D), lambda i: (jnp.maximum(i * hb - 1, 0), 0)),
                  row_spec(D),
                  pl.BlockSpec((SUBLANES, D), lambda i: (jnp.minimum((i + 1) * hb, n_hblk - 1), 0)),
                  full(mod), full(w_qk), full(b_qk), full(w_vo), full(b_vo), full(w_g), full(b_g),
                  full(cw), full(cb)],
        out_specs=[row_spec(nqk2 // 2), row_spec(nqk2 // 2), row_spec(nv), row_spec(nv), row_spec(LANES)],
        out_shape=[jax.ShapeDtypeStruct((t_all, nqk2 // 2), BF16), jax.ShapeDtypeStruct((t_all, nqk2 // 2), BF16),
                   jax.ShapeDtypeStruct((t_all, nv), BF16), jax.ShapeDtypeStruct((t_all, nv), BF16),
                   jax.ShapeDtypeStruct((t_all, LANES), F32)],
        compiler_params=_params(1),
        name="mlstm_in",
    )(x_all, x_all, x_all, mod, w_qk, b_qk, w_vo, b_vo, w_g, b_g, cw, cb)


def _ml_chunk(q_ref, k_ref, v_ref, g_ref, h_ref, s_sc, m_sc, reverse):
    L = q_ref.shape[0]
    qk_dim = q_ref.shape[1] // ML_HEADS
    v_dim = v_ref.shape[1] // ML_HEADS
    nh = ML_HEADS
    gates_t = g_ref[...].T
    sr = lax.broadcasted_iota(I32, (L, L), 0)
    lc = lax.broadcasted_iota(I32, (L, L), 1)
    upto = (sr >= lc) if reverse else (sr <= lc)
    cum_t = jnp.dot(gates_t, jnp.where(upto, 1.0, 0.0), precision=HIGHEST, preferred_element_type=F32)
    off = 2 * nh if reverse else 0
    li = gates_t[off:off + nh]
    b = cum_t[off + nh:off + 2 * nh]
    c = li - b
    lane = lax.broadcasted_iota(I32, c.shape, 1)
    mu = c
    d = 1
    while d < L:
        if reverse:
            shifted = jnp.where(lane < L - d, pltpu.roll(mu, L - d, 1), -jnp.inf)
        else:
            shifted = jnp.where(lane >= d, pltpu.roll(mu, d, 1), -jnp.inf)
        mu = jnp.maximum(mu, shifted)
        d *= 2
    m_prev = m_sc[...]
    mu = jnp.maximum(mu, m_prev)
    m_t = b + mu
    end = 0 if reverse else L - 1
    mu_end = mu[:, end:end + 1]
    decay = jnp.exp(m_prev[:, 0:1] - mu_end)
    wk = jnp.exp(c - mu_end)
    m_sc[...] = jnp.broadcast_to(b[:, end:end + 1] + mu_end, m_prev.shape)
    cols = jnp.concatenate([mu, m_t, jnp.zeros((L - 2 * nh, L), F32)], axis=0).T
    k_t = k_ref[...].astype(F32).T
    rl = lax.broadcasted_iota(I32, (L, L), 0)
    cs = lax.broadcasted_iota(I32, (L, L), 1)
    allowed = (cs >= rl) if reverse else (cs <= rl)
    ones = jnp.ones((L, v_dim), BF16)

    def head(h):
        qh = q_ref[:, h * qk_dim:(h + 1) * qk_dim]
        kt_h = k_t[h * qk_dim:(h + 1) * qk_dim, :]
        qk = jnp.dot(qh, kt_h.astype(BF16), preferred_element_type=F32)
        state = s_sc[h]
        q_state = jnp.dot(qh, state.astype(BF16), preferred_element_type=F32)
        yield
        mu_col = jnp.broadcast_to(cols[:, h:h + 1], (L, L))
        p = jnp.exp(jnp.where(allowed, c[h:h + 1, :] - mu_col, -jnp.inf))
        w_inter = jnp.exp(m_prev[h:h + 1, :] - mu_col)
        s = (qk * p).astype(BF16)
        yield
        v_aug = jnp.concatenate([v_ref[:, h * v_dim:(h + 1) * v_dim], ones], axis=1)
        nd = jnp.dot(s, v_aug, preferred_element_type=F32)
        kw_t = (kt_h * wk[h:h + 1, :]).astype(BF16)
        new_state = decay[h:h + 1, :] * state + jnp.dot(kw_t, v_aug, preferred_element_type=F32)
        yield
        mt_col = jnp.broadcast_to(cols[:, nh + h:nh + h + 1], (L, v_dim))
        nd = nd + jnp.concatenate([w_inter, w_inter], axis=1) * q_state
        num = nd[:, :v_dim]
        den = nd[:, v_dim:]
        h_ref[:, h * v_dim:(h + 1) * v_dim] = num / jnp.maximum(jnp.abs(den), jnp.exp(-mt_col))
        s_sc[h] = new_state

    return head


def _ml_scan_kernel(*refs, n_batch):
    n_in = 8 * n_batch
    hf_ref, hb_ref, s_sc, m_sc = refs[n_in:]

    @pl.when(pl.program_id(0) == 0)
    def _():
        s_sc[...] = jnp.zeros_like(s_sc)
        m_sc[...] = jnp.zeros_like(m_sc)

    heads = []
    for b in range(n_batch):
        qf, kf, vf, gf, qb, kb, vb, gb = refs[8 * b:8 * b + 8]
        heads.append(_ml_chunk(qf, kf, vf, gf, hf_ref.at[b], s_sc.at[b, 0], m_sc.at[b, 0], False))
        heads.append(_ml_chunk(qb, kb, vb, gb, hb_ref.at[b], s_sc.at[b, 1], m_sc.at[b, 1], True))
    for h in range(ML_HEADS):
        running = [head(h) for head in heads]
        while running:
            running = [g for g in running if next(g, True) is None]


def _ml_scan(q, k, v, g, dims):
    B, S, C, D = dims
    L = ML_CHUNK
    assert L == LANES
    nc_ctx = C // L
    nc_lat = S // L
    qk_dim = q.shape[1] // ML_HEADS
    v_dim = v.shape[1] // ML_HEADS

    def in_index(b, reverse):
        def idx(c):
            in_ctx = c < nc_ctx
            cl = c - nc_ctx
            if reverse:
                ctx_blk = (B * S + b * C) // L + (nc_ctx - 1 - c)
                lat_blk = (b * S) // L + (nc_lat - 1 - cl)
            else:
                ctx_blk = (B * S + b * C) // L + c
                lat_blk = (b * S) // L + cl
            return (jnp.where(in_ctx, ctx_blk, lat_blk), 0)
        return idx

    def out_index(reverse):
        def idx(c):
            cl = jnp.maximum(c - nc_ctx, 0)
            return (0, nc_lat - 1 - cl if reverse else cl, 0)
        return idx

    widths = (q.shape[1], k.shape[1], v.shape[1], LANES)
    in_specs, args = [], []
    for b in range(B):
        for rev in (False, True):
            in_specs += [pl.BlockSpec((L, w), in_index(b, rev)) for w in widths]
            args += [q, k, v, g]
    out = jax.ShapeDtypeStruct((B, S, v.shape[1]), F32)
    return pl.pallas_call(
        functools.partial(_ml_scan_kernel, n_batch=B),
        grid=(nc_ctx + nc_lat,),
        in_specs=in_specs,
        out_specs=[pl.BlockSpec((B, L, v.shape[1]), out_index(False)),
                   pl.BlockSpec((B, L, v.shape[1]), out_index(True))],
        out_shape=[out, out],
        scratch_shapes=[pltpu.VMEM((B, 2, ML_HEADS, qk_dim, 2 * v_dim), F32),
                        pltpu.VMEM((B, 2, ML_HEADS, LANES), F32)],
        compiler_params=_params(1),
        name="mlstm_scan",
    )(*args)


def kernel(x, c, ctx, c_ctx, ada_w, ada_b, ln_g, ln_b, attn_w_qkv, attn_b_qkv, attn_sink, attn_w_o, attn_b_o,
           ml_w_in, ml_b_in, ml_conv_w, ml_conv_b, ml_norm_g, ml_w_out, router_w, router_b,
           exp_w_gu, exp_b_gu, exp_w_down, exp_b_down):
    B, S, D = x.shape
    C = ctx.shape[1]
    depth = ada_w.shape[0]
    dims = (B, S, C, D)
    alpha = (2.0 * depth) ** 0.25
    n_lat = B * S

    cvec = jnp.zeros((SUBLANES, D), F32).at[:B].set(c).at[B].set(c_ctx)
    mods = _adaln(cvec, ada_w, ada_b)
    x_lat, x_ctx = x.reshape(n_lat, D), ctx.reshape(B * C, D)

    q, k, v = _attn_qkv(x_lat, x_ctx, mods[0], attn_w_qkv[0], attn_b_qkv[0], dims)
    o = _attention(q, k, v, attn_sink[0], dims)
    x1, hx, topi, gates, sizes = _post_mixer((o,), None, (x_lat, x_ctx), mods[0], attn_w_o[0], attn_b_o[0],
                                      ln_g[0, 0], ln_b[0, 0], router_w[0], router_b[0], dims, n_lat + B * C, alpha)
    x_all = _moe(hx, topi, gates, sizes, x1, mods[0], ln_g[0, 1], ln_b[0, 1],
                 0, exp_w_gu, exp_b_gu, exp_w_down, exp_b_down, dims, alpha)

    q, k, v, og, g = _ml_in(x_all, mods[1], ml_w_in[0], ml_b_in[0], ml_conv_w[0], ml_conv_b[0], dims)
    hf, hb = (h.reshape(n_lat, -1) for h in _ml_scan(q, k, v, g, dims))
    zero_b = jnp.zeros((D,), F32)
    x1, hx, topi, gates, sizes = _post_mixer((hf, hb, og), ml_norm_g[0], x_all, mods[1], ml_w_out[0], zero_b,
                                      ln_g[1, 0], ln_b[1, 0], router_w[1], router_b[1], dims, n_lat, alpha)
    out = _moe(hx, topi, gates, sizes, x1, mods[1], ln_g[1, 1], ln_b[1, 1],
               1, exp_w_gu, exp_b_gu, exp_w_down, exp_b_down, dims, alpha)
    return out.reshape(B, S, D)
```

```python
import functools

import jax
import jax.numpy as jnp
from jax import lax
from jax.experimental import pallas as pl
from jax.experimental.pallas import tpu as pltpu

F32 = jnp.float32
BF16 = jnp.bfloat16
I32 = jnp.int32
HIGHEST = lax.Precision.HIGHEST

GRID_W = 64
ATTN_HEAD_DIM = 64
ATTN_KV_HEADS = 4
WINDOW = 128
ATTN_BLOCK = 128
ROPE_THETA = 10000.0
ML_HEADS = 8
ML_CONV_W = 5
GATE_CAP = 15.0
TOP_K = 4
SWIGLU_ALPHA = 1.702
SWIGLU_LIMIT = 7.0
LN_EPS = 1e-5

LANES = 128
SUBLANES = 8
VMEM_LIMIT = 56 * 1024 * 1024
EXPERT_TILE = 512
ROUTE_TILE = 512
POST_PART = 256
ML_CHUNK = 128
GROUP_ALIGN = 16
BIG_UNIT = 2 * GROUP_ALIGN
STAGE_ROWS = ROUTE_TILE * TOP_K + 4 * LANES
TABLE_W = 4 * LANES
ZERO_ROWS = 64


def _params(n_axes, vmem=VMEM_LIMIT):
    return pltpu.CompilerParams(dimension_semantics=("arbitrary",) * n_axes, vmem_limit_bytes=vmem)


def _layer_norm(r, g, b):
    mu = jnp.mean(r, axis=-1, keepdims=True)
    rc = r - mu
    var = jnp.mean(rc * rc, axis=-1, keepdims=True)
    return rc * lax.rsqrt(var + LN_EPS) * g + b


def _sigmoid(x):
    return 1.0 / (1.0 + jnp.exp(-x))


def _mod_row(i, n_lat_tiles, tiles_per_batch, ctx_row):
    return jnp.where(i < n_lat_tiles, i // tiles_per_batch, ctx_row)


def _adaln_kernel(c_ref, w_ref, b_ref, o_ref):
    c = c_ref[...]
    s = c * _sigmoid(c)
    o_ref[0] = jnp.dot(s, w_ref[0], precision=HIGHEST, preferred_element_type=F32) + b_ref[0]


def _adaln(cvec, ada_w, ada_b):
    depth, d, n = ada_w.shape
    tn = 1536
    return pl.pallas_call(
        _adaln_kernel,
        grid=(depth, n // tn),
        in_specs=[pl.BlockSpec((SUBLANES, d), lambda l, j: (0, 0)),
                  pl.BlockSpec((1, d, tn), lambda l, j: (l, 0, j)),
                  pl.BlockSpec((1, 1, tn), lambda l, j: (l, 0, j))],
        out_specs=pl.BlockSpec((1, SUBLANES, tn), lambda l, j: (l, 0, j)),
        out_shape=jax.ShapeDtypeStruct((depth, SUBLANES, n), F32),
        compiler_params=_params(2),
        name="adaln",
    )(cvec, ada_w, ada_b.reshape(depth, 1, n))


def _token_specs(tm, d, n_lat_tiles):
    return [pl.BlockSpec((tm, d), lambda i, *_: (jnp.minimum(i, n_lat_tiles - 1), 0)),
            pl.BlockSpec((tm, d), lambda i, *_: (jnp.maximum(i - n_lat_tiles, 0), 0))]


def _qkv_kernel(x_ref, c_ref, mod_ref, w_ref, b_ref, cos_ref, sin_ref, q_ref, k_ref, vt_ref, *,
                n_lat_tiles, tiles_per_batch, ctx_row, d, qd, kvd):
    i = pl.program_id(0)
    row = _mod_row(i, n_lat_tiles, tiles_per_batch, ctx_row)
    shift = mod_ref[pl.ds(row, 1), 0:d]
    scale = mod_ref[pl.ds(row, 1), d:2 * d]
    x = jnp.where(i < n_lat_tiles, x_ref[...], c_ref[...])
    h = x * (1.0 + scale) + shift
    z = jnp.dot(h.astype(BF16), w_ref[...], preferred_element_type=F32) + b_ref[...]
    nrot = qd + kvd
    qk = z[:, :nrot]
    reps = nrot // LANES
    cos = jnp.concatenate([cos_ref[...]] * reps, axis=1)
    sin = jnp.concatenate([sin_ref[...]] * reps, axis=1)
    lane = lax.broadcasted_iota(I32, qk.shape, 1)
    low_half = (lane & 16) == 0
    partner = jnp.where(low_half, pltpu.roll(qk, nrot - 16, 1), pltpu.roll(qk, 16, 1))
    qk = qk * cos + partner * sin
    q_ref[...] = (qk[:, :qd] * (ATTN_HEAD_DIM ** -0.5)).astype(BF16)
    k_ref[...] = qk[:, qd:].astype(BF16)
    vt_ref[...] = z[:, nrot:].T.astype(BF16)


def _rope_tables(s_len, tm):
    half = ATTN_HEAD_DIM // 4
    freqs = ROPE_THETA ** (-jnp.arange(half, dtype=F32) / half)
    t = jnp.arange(s_len)
    rows = (t // GRID_W).astype(F32)[:, None] * freqs[None, :]
    cols = (t % GRID_W).astype(F32)[:, None] * freqs[None, :]
    ang = jnp.concatenate([rows, rows, cols, cols], axis=1)
    sign = jnp.tile(jnp.concatenate([-jnp.ones((half,), F32), jnp.ones((half,), F32)]), 2)
    cos = jnp.cos(ang)
    sin = jnp.sin(ang) * sign[None, :]
    reps = LANES // ATTN_HEAD_DIM
    cos = jnp.concatenate([jnp.tile(cos, (1, reps)), jnp.ones((tm, LANES), F32)], axis=0)
    sin = jnp.concatenate([jnp.tile(sin, (1, reps)), jnp.zeros((tm, LANES), F32)], axis=0)
    return cos, sin


def _attn_qkv(x_lat, x_ctx, mod, w_qkv, b_qkv, dims):
    B, S, C, D = dims
    t_all = x_lat.shape[0] + x_ctx.shape[0]
    tm = 512
    n_lat_tiles = B * S // tm
    tiles_per_batch = S // tm
    ncols = w_qkv.shape[1]
    kvd = ATTN_KV_HEADS * ATTN_HEAD_DIM
    qd = ncols - 2 * kvd
    cos, sin = _rope_tables(S, tm)

    def tab_idx(i):
        return (jnp.where(i < n_lat_tiles, i % tiles_per_batch, tiles_per_batch), 0)

    kern = functools.partial(_qkv_kernel, n_lat_tiles=n_lat_tiles, tiles_per_batch=tiles_per_batch,
                             ctx_row=B, d=D, qd=qd, kvd=kvd)
    return pl.pallas_call(
        kern,
        grid=(t_all // tm,),
        in_specs=_token_specs(tm, D, n_lat_tiles) + [
                  pl.BlockSpec(mod.shape, lambda i: (0, 0)),
                  pl.BlockSpec((D, ncols), lambda i: (0, 0)),
                  pl.BlockSpec((1, ncols), lambda i: (0, 0)),
                  pl.BlockSpec((tm, LANES), tab_idx),
                  pl.BlockSpec((tm, LANES), tab_idx)],
        out_specs=[pl.BlockSpec((tm, qd), lambda i: (i, 0)),
                   pl.BlockSpec((tm, kvd), lambda i: (i, 0)),
                   pl.BlockSpec((kvd, tm), lambda i: (0, i))],
        out_shape=[jax.ShapeDtypeStruct((t_all, qd), BF16),
                   jax.ShapeDtypeStruct((t_all, kvd), BF16),
                   jax.ShapeDtypeStruct((kvd, t_all), BF16)],
        compiler_params=_params(1),
        name="attn_qkv",
    )(x_lat, x_ctx, mod, w_qkv.astype(BF16), b_qkv.reshape(1, ncols), cos, sin)


def _attn_kernel(sink_ref, q_ref, kp_ref, ko_ref, kn_ref, kc_ref, vp_ref, vo_ref, vn_ref, vc_ref, o_ref,
                 bias_sc, s_sc, p_sc, ot_sc, *,
                 n_lat_steps, nb, s_len, c_len):
    j = pl.program_id(0)
    is_lat = j < n_lat_steps
    n = j % nb
    blk = ATTN_BLOCK
    nloc = 3 * blk
    nk = nloc + c_len
    ki = lax.broadcasted_iota(I32, (nloc, blk), 0)
    qj = lax.broadcasted_iota(I32, (nloc, blk), 1)
    kpos = n * blk - WINDOW + ki
    qpos = n * blk + qj
    local_ok = (jnp.abs(kpos - qpos) <= WINDOW) & (kpos >= 0) & (kpos < s_len) & is_lat
    bias_sc[...] = jnp.where(local_ok, 0.0, -jnp.inf)
    hd = ATTN_HEAD_DIM
    group = q_ref.shape[1] // (ATTN_KV_HEADS * hd)
    kcat = jnp.concatenate([kp_ref[...], ko_ref[...], kn_ref[...], kc_ref[...]], axis=0)
    vcat_t = jnp.concatenate([vp_ref[...], vo_ref[...], vn_ref[...], vc_ref[...]], axis=1)
    def score_matmul(kh):
        q_grp = jnp.concatenate([q_ref[:, (kh * group + g) * hd:(kh * group + g + 1) * hd]
                                 for g in range(group)], axis=0)
        s_sc[kh] = lax.dot_general(kcat[:, kh * hd:(kh + 1) * hd], q_grp, (((1,), (1,)), ((), ())),
                                   preferred_element_type=F32)

    def weighted_values(kh, sink_terms):
        v_ones = jnp.concatenate([vcat_t[kh * hd:(kh + 1) * hd, :], jnp.ones((SUBLANES, nk), BF16)], axis=0)
        o_aug = jnp.dot(v_ones, p_sc[kh], preferred_element_type=F32)
        l = o_aug[hd:hd + 1, :] + jnp.concatenate(sink_terms, axis=1)
        o_t = o_aug[0:hd, :] * (1.0 / l)
        for g in range(group):
            h = kh * group + g
            ot_sc[h * hd:(h + 1) * hd, :] = o_t[:, g * blk:(g + 1) * blk]

    score_matmul(0)
    for kh in range(ATTN_KV_HEADS):
        if kh + 1 < ATTN_KV_HEADS:
            score_matmul(kh + 1)
        sink_terms = []
        for g in range(group):
            cols = slice(g * blk, (g + 1) * blk)
            def scores(a):
                s = s_sc[kh, a:a + blk, cols]
                return s + bias_sc[a:a + blk, :] if a < nloc else s

            sk = sink_ref[kh * group + g]
            top = scores(0)
            for a in range(blk, nk, blk):
                top = jnp.maximum(top, scores(a))
            m = jnp.maximum(jnp.max(top, axis=0, keepdims=True), sk)
            for a in range(0, nk, blk):
                p_sc[kh, a:a + blk, cols] = jnp.exp((scores(a) - m).astype(BF16))
            sink_terms.append(jnp.exp(sk - m))
        weighted_values(kh, sink_terms)
    o_ref[...] = ot_sc[...].T.astype(BF16)


def _attention(q_all, k_all, vt_all, sink, dims):
    B, S, C, D = dims
    blk = ATTN_BLOCK
    nb = S // blk
    n_lat_steps = B * nb
    ctx_steps_per_batch = C // blk
    n_steps = n_lat_steps + B * ctx_steps_per_batch
    qd = q_all.shape[1]
    kvd = k_all.shape[1]

    def local_idx(off):
        def idx(j):
            b = j // nb
            nn = jnp.clip(j % nb + off, 0, nb - 1)
            return (jnp.where(j < n_lat_steps, b * nb + nn, j), 0)
        return idx

    def ctx_idx(j):
        b = jnp.where(j < n_lat_steps, j // nb, (j - n_lat_steps) // ctx_steps_per_batch)
        return (B * S // C + b, 0)

    swap = lambda f: (lambda j: f(j)[::-1])
    loc = lambda off: pl.BlockSpec((blk, kvd), local_idx(off))
    ctxs = pl.BlockSpec((C, kvd), ctx_idx)
    loc_t = lambda off: pl.BlockSpec((kvd, blk), swap(local_idx(off)))
    ctxs_t = pl.BlockSpec((kvd, C), swap(ctx_idx))
    group = qd // kvd
    nk = 3 * blk + C
    kern = functools.partial(_attn_kernel, n_lat_steps=n_lat_steps, nb=nb, s_len=S, c_len=C)
    return pl.pallas_call(
        kern,
        grid=(n_steps,),
        in_specs=[pl.BlockSpec(memory_space=pltpu.SMEM),
                  pl.BlockSpec((blk, qd), lambda j: (j, 0)),
                  loc(-1), loc(0), loc(1), ctxs,
                  loc_t(-1), loc_t(0), loc_t(1), ctxs_t],
        out_specs=pl.BlockSpec((blk, qd), lambda j: (j, 0)),
        out_shape=jax.ShapeDtypeStruct((q_all.shape[0], qd), BF16),
        scratch_shapes=[pltpu.VMEM((3 * blk, blk), F32),
                        pltpu.VMEM((ATTN_KV_HEADS, nk, group * blk), F32),
                        pltpu.VMEM((ATTN_KV_HEADS, nk, group * blk), BF16),
                        pltpu.VMEM((qd, blk), F32)],
        compiler_params=_params(1),
        name="attn_core",
    )(sink, q_all, k_all, k_all, k_all, k_all, vt_all, vt_all, vt_all, vt_all)


def _split_bf16(a):
    hi = a.astype(BF16)
    return hi, (a - hi.astype(F32)).astype(BF16)


def _split_weight(w_f32):
    hi = w_f32.astype(BF16)
    return jnp.concatenate([hi, (w_f32 - hi.astype(F32)).astype(BF16)], axis=1)


def _dot_split(a, w_ref):
    a_hi, a_lo = _split_bf16(a)
    n = w_ref.shape[1] // 2
    both = jnp.dot(a_hi, w_ref[...], preferred_element_type=F32)
    return both[:, :n] + both[:, n:] + jnp.dot(a_lo, w_ref[:, :n], preferred_element_type=F32)


def _top_k_route(logits, rows, topi_ref, gate_ref, sizes_ref, part):
    lane = lax.broadcasted_iota(I32, logits.shape, 1)
    lanef = lane.astype(F32)
    vals, idxs = [], []
    l = logits
    for _ in range(TOP_K):
        m = jnp.max(l, axis=1, keepdims=True)
        idx = jnp.min(jnp.where(l == m, lanef, float(LANES)), axis=1, keepdims=True)
        vals.append(m)
        idxs.append(idx)
        l = jnp.where(lanef == idx, -jnp.inf, l)
    es = [jnp.exp(v - vals[0]) for v in vals]
    denom = es[0]
    for e in es[1:]:
        denom = denom + e
    topi = jnp.zeros(logits.shape, F32)
    gates = jnp.zeros(logits.shape, F32)
    chosen = jnp.zeros(logits.shape, F32)
    for k in range(TOP_K):
        topi = jnp.where(lane == k, idxs[k], topi)
        gates = jnp.where(lane == k, es[k] / denom, gates)
        chosen = chosen + jnp.where(lanef == idxs[k], 1.0, 0.0)
    topi_ref[rows, :] = topi.astype(I32)
    gate_ref[rows, :] = gates
    sizes_ref[0, part:part + 1, :] = jnp.sum(chosen, axis=0, keepdims=True)


def _post_common(a_fn, x_fn, mod_ref, w_ref, b_ref, lng_ref, lnb_ref, rw_ref, rb_ref,
                 x1_ref, hx_ref, topi_ref, gate_ref, sizes_ref, row, d, alpha):
    gate_mix = mod_ref[pl.ds(row, 1), 2 * d:3 * d]
    shift = mod_ref[pl.ds(row, 1), 3 * d:4 * d]
    scale = mod_ref[pl.ds(row, 1), 4 * d:5 * d]
    sizes_ref[...] = jnp.zeros_like(sizes_ref)

    def part_stages(part):
        rows = slice(part * POST_PART, (part + 1) * POST_PART)
        y = jnp.dot(a_fn(rows), w_ref[...], preferred_element_type=F32) + b_ref[...]
        yield
        x1 = _layer_norm(alpha * x_fn(rows) + gate_mix * y, lng_ref[...], lnb_ref[...])
        hx = x1 * (1.0 + scale) + shift
        x1_ref[rows, :] = x1
        hx_ref[rows, :] = hx.astype(BF16)
        logits = _dot_split(hx, rw_ref) + rb_ref[...]
        yield
        _top_k_route(logits, rows, topi_ref, gate_ref, sizes_ref, part)

    running = [part_stages(p) for p in range(x1_ref.shape[0] // POST_PART)]
    while running:
        running = [g for g in running if next(g, True) is None]


def _post_attn_kernel(o_ref, x_ref, c_ref, mod_ref, w_ref, b_ref, lng_ref, lnb_ref, rw_ref, rb_ref,
                      x1_ref, hx_ref, topi_ref, gate_ref, sizes_ref, *, n_lat_tiles, tiles_per_batch, ctx_row, d, alpha):
    i = pl.program_id(0)
    row = _mod_row(i, n_lat_tiles, tiles_per_batch, ctx_row)
    is_lat = i < n_lat_tiles
    _post_common(lambda rows: o_ref[rows, :],
                 lambda rows: jnp.where(is_lat, x_ref[rows, :], c_ref[rows, :]),
                 mod_ref, w_ref, b_ref, lng_ref, lnb_ref, rw_ref, rb_ref,
                 x1_ref, hx_ref, topi_ref, gate_ref, sizes_ref, row, d, alpha)


def _post_mlstm_kernel(hf_ref, hb_ref, og_ref, ng_ref, x_ref, mod_ref, w_ref, b_ref, lng_ref, lnb_ref,
                       rw_ref, rb_ref, x1_ref, hx_ref, topi_ref, gate_ref, sizes_ref, *,
                       n_lat_tiles, tiles_per_batch, ctx_row, d, alpha):
    row = _mod_row(pl.program_id(0), n_lat_tiles, tiles_per_batch, ctx_row)

    def gated_head_norm(rows):
        hsum = hf_ref[rows, :] + hb_ref[rows, :]
        vdim = hsum.shape[1] // ML_HEADS
        parts = []
        for h in range(ML_HEADS):
            seg = hsum[:, h * vdim:(h + 1) * vdim]
            mu = jnp.mean(seg, axis=1, keepdims=True)
            sc = seg - mu
            var = jnp.mean(sc * sc, axis=1, keepdims=True)
            parts.append(sc * lax.rsqrt(var + LN_EPS))
        y = jnp.concatenate(parts, axis=1) * ng_ref[...]
        return (og_ref[rows, :].astype(F32) * y).astype(BF16)

    _post_common(gated_head_norm, lambda rows: x_ref[rows, :],
                 mod_ref, w_ref, b_ref, lng_ref, lnb_ref, rw_ref, rb_ref,
                 x1_ref, hx_ref, topi_ref, gate_ref, sizes_ref, row, d, alpha)


def _post_mixer(mixer_inputs, norm_g, x_all, mod, w_o, b_o, ln_g, ln_b, router_w, router_b, dims, n_rows, alpha):
    B, S, C, D = dims
    tm = ROUTE_TILE
    n_lat_tiles = B * S // tm
    tiles_per_batch = S // tm
    n_exp = router_w.shape[1]
    rw = _split_weight(jnp.zeros((D, LANES), F32).at[:, :n_exp].set(router_w))
    rb = jnp.full((1, LANES), -1e30, F32).at[0, :n_exp].set(router_b)
    row_spec = lambda w: pl.BlockSpec((tm, w), lambda i: (i, 0))
    full = lambda a: pl.BlockSpec(a.shape, lambda i: (0,) * a.ndim)
    common = dict(n_lat_tiles=n_lat_tiles, tiles_per_batch=tiles_per_batch, ctx_row=B, d=D, alpha=alpha)
    w_bf = w_o.astype(BF16)
    b2 = b_o.reshape(1, D)
    if isinstance(x_all, tuple):
        x_args, x_specs = list(x_all), _token_specs(tm, D, n_lat_tiles)
    else:
        x_args, x_specs = [x_all], [row_spec(D)]
    tail = x_args + [mod, w_bf, b2, ln_g.reshape(1, D), ln_b.reshape(1, D), rw, rb]
    tail_specs = x_specs + [full(mod), full(w_bf), full(b2), pl.BlockSpec((1, D), lambda i: (0, 0)),
                            pl.BlockSpec((1, D), lambda i: (0, 0)), full(rw), full(rb)]
    if norm_g is None:
        kern = functools.partial(_post_attn_kernel, **common)
        args = list(mixer_inputs) + tail
        specs = [row_spec(mixer_inputs[0].shape[1])] + tail_specs
        name = "post_attn"
    else:
        kern = functools.partial(_post_mlstm_kernel, **common)
        ng = norm_g.reshape(1, -1)
        args = list(mixer_inputs) + [ng] + tail
        specs = [row_spec(a.shape[1]) for a in mixer_inputs] + [full(ng)] + tail_specs
        name = "post_mlstm"
    return pl.pallas_call(
        kern,
        grid=(n_rows // tm,),
        in_specs=specs,
        out_specs=[row_spec(D), row_spec(D), row_spec(LANES), row_spec(LANES),
                   pl.BlockSpec((1, SUBLANES, LANES), lambda i: (i, 0, 0))],
        out_shape=[jax.ShapeDtypeStruct((n_rows, D), F32), jax.ShapeDtypeStruct((n_rows, D), BF16),
                   jax.ShapeDtypeStruct((n_rows, LANES), I32), jax.ShapeDtypeStruct((n_rows, LANES), F32),
                   jax.ShapeDtypeStruct((n_rows // tm, SUBLANES, LANES), F32)],
        compiler_params=_params(1),
        name=name,
    )(*args)


def _exclusive_lane_cumsum(row):
    r = lax.broadcasted_iota(I32, (LANES, LANES), 0)
    c = lax.broadcasted_iota(I32, (LANES, LANES), 1)
    before = jnp.where(r < c, 1.0, 0.0)
    return jnp.dot(jnp.broadcast_to(row, (SUBLANES, LANES)), before,
                   precision=HIGHEST, preferred_element_type=F32)[0:1]


def _for_each_unit(tab_ref, tile, fn):
    base = tile * TABLE_W
    for blk, rows in ((0, BIG_UNIT), (2, GROUP_ALIGN)):
        def body(j, carry, blk=blk, rows=rows):
            slot_row = pl.multiple_of(tab_ref[base + blk * LANES + j], GROUP_ALIGN)
            stage_row = pl.multiple_of(tab_ref[base + (blk + 1) * LANES + j], GROUP_ALIGN)
            fn(slot_row, stage_row, rows)
            return carry
        lax.fori_loop(0, tab_ref[base + blk * LANES + LANES - 1], body, 0)


def _slots_kernel(topi_ref, sizes_ref, col_ref, colt_ref, tab_ref, meta_ref, base_sc, carry_sc, *, tile):
    i = pl.program_id(0)
    tm = topi_ref.shape[0]
    topi = topi_ref[...]
    lane = lax.broadcasted_iota(I32, topi.shape, 1)
    sel = [lane == topi[:, k:k + 1] for k in range(TOP_K)]
    maskf = jnp.where(sel[0], 1.0, 0.0)
    for s in sel[1:]:
        maskf = maskf + jnp.where(s, 1.0, 0.0)
    n8 = jnp.ceil(jnp.sum(maskf, axis=0, keepdims=True) / GROUP_ALIGN) * GROUP_ALIGN

    @pl.when(i == 0)
    def _():
        groups = jnp.ceil(jnp.sum(sizes_ref[...], axis=1) / GROUP_ALIGN) * GROUP_ALIGN
        tot = jnp.sum(groups, axis=0, keepdims=True)
        padded = jnp.ceil(tot / tile) * tile
        base = _exclusive_lane_cumsum(padded)
        base_sc[...] = base
        carry_sc[...] = jnp.zeros_like(carry_sc)
        rowi = lax.broadcasted_iota(I32, meta_ref.shape, 0)
        meta_ref[...] = jnp.where(rowi == 0, tot, jnp.where(rowi == 1, base, padded)).astype(I32)

    start = base_sc[...] + carry_sc[...]
    off = _exclusive_lane_cumsum(n8)
    r = lax.broadcasted_iota(I32, (tm, tm), 0)
    c = lax.broadcasted_iota(I32, (tm, tm), 1)
    earlier = jnp.where(c < r, 1.0, 0.0).astype(BF16)
    rank = jnp.dot(earlier, maskf.astype(BF16), preferred_element_type=F32)
    stage_row = rank + off
    out = jnp.zeros(topi.shape, F32)
    for k in range(TOP_K):
        pk = jnp.sum(jnp.where(sel[k], stage_row, 0.0), axis=1, keepdims=True)
        out = jnp.where(lane == k, pk, out)
    col_ref[...] = out.astype(I32)
    colt_ref[0] = out.T[0:SUBLANES].astype(I32)

    n_big = jnp.floor(n8 / BIG_UNIT)
    n_small = (n8 - BIG_UNIT * n_big) / GROUP_ALIGN
    first_big = _exclusive_lane_cumsum(n_big)
    first_small = _exclusive_lane_cumsum(n_small)
    rows = jnp.concatenate([off, start, first_big, n_big, first_small, n_small,
                            jnp.zeros((LANES - 6, LANES), F32)], axis=0)
    per_expert = rows.T
    off_c, start_c, fb_c, nb_c, fs_c, ns_c = (per_expert[:, k:k + 1] for k in range(6))
    u = lax.broadcasted_iota(I32, (LANES, LANES), 1).astype(F32)
    lane_t = lax.broadcasted_iota(I32, (1, LANES), 1)

    def unit_list(first_c, count_c, rel, count_row):
        inside = (u >= first_c) & (u < first_c + count_c)
        src = jnp.sum(jnp.where(inside, start_c + rel, 0.0), axis=0, keepdims=True)
        dst = jnp.sum(jnp.where(inside, off_c + rel, 0.0), axis=0, keepdims=True)
        total = jnp.sum(count_row, axis=1, keepdims=True)
        return [jnp.where(lane_t == LANES - 1, total, src), dst]

    lists = (unit_list(fb_c, nb_c, BIG_UNIT * (u - fb_c), n_big)
             + unit_list(fs_c, ns_c, BIG_UNIT * nb_c + GROUP_ALIGN * (u - fs_c), n_small))
    tab_ref[0] = jnp.concatenate(lists, axis=1).astype(I32)
    carry_sc[...] = carry_sc[...] + n8


def _slots(topi, sizes, tile):
    t = topi.shape[0]
    tm = ROUTE_TILE
    nt = t // tm
    return pl.pallas_call(
        functools.partial(_slots_kernel, tile=float(tile)),
        grid=(nt,),
        in_specs=[pl.BlockSpec((tm, LANES), lambda i: (i, 0)),
                  pl.BlockSpec(sizes.shape, lambda i: (0, 0, 0))],
        out_specs=[pl.BlockSpec((tm, LANES), lambda i: (i, 0)),
                   pl.BlockSpec((1, SUBLANES, tm), lambda i: (i, 0, 0)),
                   pl.BlockSpec((1, 1, TABLE_W), lambda i: (i, 0, 0)),
                   pl.BlockSpec((SUBLANES, LANES), lambda i: (0, 0))],
        out_shape=[jax.ShapeDtypeStruct((t, LANES), I32),
                   jax.ShapeDtypeStruct((nt, SUBLANES, tm), I32),
                   jax.ShapeDtypeStruct((nt, 1, TABLE_W), I32),
                   jax.ShapeDtypeStruct((SUBLANES, LANES), I32)],
        scratch_shapes=[pltpu.VMEM((1, LANES), F32), pltpu.VMEM((1, LANES), F32)],
        compiler_params=_params(1),
        name="moe_slots",
    )(topi, sizes)


def _dispatch_kernel(tab_ref, pad_start_ref, pad_units_ref, tail_ref, colt_ref, hx_ref, xs_ref, stage, zeros, sem, zsem):
    i = pl.program_id(0)
    nt = pl.num_programs(0)
    slot = i % 2
    kb = stage.shape[1]
    tm = hx_ref.shape[0]

    def scatter(buf_slot):
        def copy(slot_row, stage_row, rows):
            return pltpu.make_async_copy(stage.at[buf_slot, pl.ds(stage_row, rows)],
                                         xs_ref.at[pl.ds(slot_row, rows)], sem.at[buf_slot])
        return copy

    def drain(tile, buf_slot):
        _for_each_unit(tab_ref, tile, lambda *u: scatter(buf_slot)(*u).wait())

    @pl.when(i >= 2)
    def _():
        drain(i - 2, slot)

    colt = colt_ref[0]
    c = lax.broadcasted_iota(I32, (kb, tm), 0)
    onehot = jnp.zeros((kb, tm), F32)
    for k in range(TOP_K):
        onehot = jnp.where(c == colt[k:k + 1, :], 1.0, onehot)
    stage[slot] = jnp.dot(onehot.astype(BF16), hx_ref[...], preferred_element_type=F32).astype(BF16)

    _for_each_unit(tab_ref, i, lambda *u: scatter(slot)(*u).start())

    @pl.when(i == nt - 1)
    def _():
        zeros[...] = jnp.zeros_like(zeros)
        n_exp = pad_start_ref.shape[0]

        def zero_copy(e, u):
            dst = pl.multiple_of(pad_start_ref[e] + u * GROUP_ALIGN, GROUP_ALIGN)
            return pltpu.make_async_copy(zeros.at[pl.ds(0, GROUP_ALIGN)], xs_ref.at[pl.ds(dst, GROUP_ALIGN)], zsem)

        def per_expert(fn):
            def outer(e, carry):
                def inner(u, carry2):
                    fn(e, u)
                    return carry2
                lax.fori_loop(0, pad_units_ref[e], inner, 0)
                return carry
            lax.fori_loop(0, n_exp, outer, 0)

        per_expert(lambda e, u: zero_copy(e, u).start())
        per_expert(lambda e, u: zero_copy(e, u).wait())

        zrows = zeros.shape[0]

        def tail_copy(u):
            dst = pl.multiple_of(tail_ref[0] + u * zrows, zrows)
            return pltpu.make_async_copy(zeros, xs_ref.at[pl.ds(dst, zrows)], zsem)

        def tail_loop(fn):
            def body(u, carry):
                fn(u)
                return carry
            lax.fori_loop(0, tail_ref[1], body, 0)

        tail_loop(lambda u: tail_copy(u).start())
        tail_loop(lambda u: tail_copy(u).wait())

        @pl.when(nt >= 2)
        def _():
            drain(i - 1, 1 - slot)
        drain(i, slot)


def _dispatch(tab, pad_start, pad_units, tail, colt, hx, n_slots):
    t, d = hx.shape
    tm = ROUTE_TILE
    return pl.pallas_call(
        _dispatch_kernel,
        grid_spec=pltpu.PrefetchScalarGridSpec(
            num_scalar_prefetch=4,
            grid=(t // tm,),
            in_specs=[pl.BlockSpec((1, SUBLANES, tm), lambda i, *_: (i, 0, 0)),
                      pl.BlockSpec((tm, d), lambda i, *_: (i, 0))],
            out_specs=pl.BlockSpec(memory_space=pl.ANY),
            scratch_shapes=[pltpu.VMEM((2, STAGE_ROWS, d), BF16), pltpu.VMEM((ZERO_ROWS, d), BF16),
                            pltpu.SemaphoreType.DMA((2,)), pltpu.SemaphoreType.DMA(())]),
        out_shape=jax.ShapeDtypeStruct((n_slots, d), BF16),
        compiler_params=_params(1),
        name="moe_dispatch",
    )(tab, pad_start, pad_units, tail, colt, hx)


def _expert_kernel(te_ref, rows_ref, next_ref, nu_ref, xs_ref, wgu_hbm, bgu_ref, wd_hbm, bd_ref, ys_ref,
                   wgu_f32, wd_f32, wgu_sc, wd_sc, wsem, *, layer):
    j = pl.program_id(0)
    active = j < nu_ref[0]
    changed = (j == 0) | (te_ref[j] != te_ref[jnp.maximum(j - 1, 0)])
    ff = wd_sc.shape[0]
    te = xs_ref.shape[0]
    half = te // 2

    def weight_copies(e):
        return (pltpu.make_async_copy(wgu_hbm.at[layer, e], wgu_f32, wsem.at[0]),
                pltpu.make_async_copy(wd_hbm.at[layer, e], wd_f32, wsem.at[1]))

    @pl.when(active & changed)
    def _():
        @pl.when(j == 0)
        def _():
            for cp in weight_copies(te_ref[0]):
                cp.start()

        for cp in weight_copies(te_ref[j]):
            cp.wait()
        wgu_sc[...] = wgu_f32[...].astype(BF16)
        wd_sc[...] = wd_f32[...].astype(BF16)

        @pl.when(next_ref[j] >= 0)
        def _():
            for cp in weight_copies(next_ref[j]):
                cp.start()

    def run(rows):
        x = xs_ref[0:rows, :]
        nblk = 2
        fb = ff // nblk

        def up(c):
            gl = jnp.dot(x, wgu_sc[:, c * fb:(c + 1) * fb], preferred_element_type=F32) + bgu_ref[:, c * fb:(c + 1) * fb]
            lin = (jnp.dot(x, wgu_sc[:, ff + c * fb:ff + (c + 1) * fb], preferred_element_type=F32)
                   + bgu_ref[:, ff + c * fb:ff + (c + 1) * fb])
            return gl, lin

        y = bd_ref[...]
        nxt = up(0)
        for c in range(nblk):
            gl, lin = nxt
            if c + 1 < nblk:
                nxt = up(c + 1)
            gl = jnp.minimum(gl, SWIGLU_LIMIT)
            lin = jnp.clip(lin, -SWIGLU_LIMIT, SWIGLU_LIMIT)
            act = gl * _sigmoid(SWIGLU_ALPHA * gl) * (lin + 1.0)
            y = y + jnp.dot(act.astype(BF16), wd_sc[c * fb:(c + 1) * fb, :], preferred_element_type=F32)
        ys_ref[0:rows, :] = y.astype(BF16)
        if rows < te:
            ys_ref[rows:te, :] = jnp.zeros((te - rows, ys_ref.shape[1]), BF16)

    valid_rows = rows_ref[j]

    @pl.when(active & (valid_rows > half))
    def _():
        run(te)

    @pl.when(active & (valid_rows <= half))
    def _():
        run(half)

    @pl.when(jnp.logical_not(active))
    def _():
        ys_ref[...] = jnp.zeros_like(ys_ref)


def _experts(tile_expert, tile_rows, next_expert, n_used, xs, layer, w_gu, b_gu, w_down, b_down):
    n_slots, d = xs.shape
    depth, n_exp, _, ff2 = w_gu.shape
    ff = w_down.shape[2]
    te = EXPERT_TILE
    row_idx = lambda j, te_r, rows_r, nx_r, nu: (jnp.maximum(jnp.minimum(j, nu[0] - 1), 0), 0)
    b_idx = lambda j, te_r, rows_r, nx_r, nu: (layer, te_r[j], 0, 0)
    return pl.pallas_call(
        functools.partial(_expert_kernel, layer=layer),
        grid_spec=pltpu.PrefetchScalarGridSpec(
            num_scalar_prefetch=4,
            grid=(n_slots // te,),
            in_specs=[pl.BlockSpec((te, d), row_idx),
                      pl.BlockSpec(memory_space=pl.ANY),
                      pl.BlockSpec((None, None, 1, ff2), b_idx),
                      pl.BlockSpec(memory_space=pl.ANY),
                      pl.BlockSpec((None, None, 1, d), b_idx)],
            out_specs=pl.BlockSpec((te, d), lambda j, te_r, rows_r, nx_r, nu: (j, 0)),
            scratch_shapes=[pltpu.VMEM((d, ff2), F32), pltpu.VMEM((ff, d), F32),
                            pltpu.VMEM((d, ff2), BF16), pltpu.VMEM((ff, d), BF16),
                            pltpu.SemaphoreType.DMA((2,))]),
        out_shape=jax.ShapeDtypeStruct((n_slots, d), BF16),
        compiler_params=_params(1),
        name="moe_experts",
    )(tile_expert, tile_rows, next_expert, n_used, xs, w_gu, b_gu.reshape(depth, n_exp, 1, ff2),
      w_down, b_down.reshape(depth, n_exp, 1, d))


def _combine_kernel(tab_ref, col_ref, gate_ref, ys_ref, x_ref, mod_ref, lng_ref, lnb_ref, out_ref, stage, sem, *,
                    n_lat_tiles, tiles_per_batch, ctx_row, d, alpha):
    i = pl.program_id(0)
    nt = pl.num_programs(0)
    slot = i % 2
    kb = stage.shape[1]
    tm = x_ref.shape[0]

    def gather(buf_slot):
        def copy(slot_row, stage_row, rows):
            return pltpu.make_async_copy(ys_ref.at[pl.ds(slot_row, rows)],
                                         stage.at[buf_slot, pl.ds(stage_row, rows)], sem.at[buf_slot])
        return copy

    @pl.when(i == 0)
    def _():
        stage[...] = jnp.zeros_like(stage)
        _for_each_unit(tab_ref, 0, lambda *u: gather(0)(*u).start())

    @pl.when(i + 1 < nt)
    def _():
        _for_each_unit(tab_ref, i + 1, lambda *u: gather(1 - slot)(*u).start())

    _for_each_unit(tab_ref, i, lambda *u: gather(slot)(*u).wait())

    col = col_ref[...]
    gates = gate_ref[...]
    kblk = 2 * LANES

    def gate_weights(r0):
        c = r0 + lax.broadcasted_iota(I32, (tm, kblk), 1)
        w = jnp.zeros((tm, kblk), F32)
        for k in range(TOP_K):
            w = jnp.where(c == col[:, k:k + 1], gates[:, k:k + 1], w)
        return w.astype(BF16)

    y = jnp.zeros((tm, d), F32)
    nxt = gate_weights(0)
    for r0 in range(0, kb, kblk):
        w = nxt
        if r0 + kblk < kb:
            nxt = gate_weights(r0 + kblk)
        y = y + jnp.dot(w, stage[slot, r0:r0 + kblk, :], preferred_element_type=F32)
    row = _mod_row(i, n_lat_tiles, tiles_per_batch, ctx_row)
    gate_mlp = mod_ref[pl.ds(row, 1), 5 * d:6 * d]
    out_ref[...] = _layer_norm(alpha * x_ref[...] + gate_mlp * y, lng_ref[...], lnb_ref[...])


def _combine(tab, col, ys, gates, x1, mod, ln_g, ln_b, dims, alpha):
    B, S, C, D = dims
    t = x1.shape[0]
    tm = ROUTE_TILE
    kern = functools.partial(_combine_kernel, n_lat_tiles=B * S // tm, tiles_per_batch=S // tm, ctx_row=B,
                             d=D, alpha=alpha)
    return pl.pallas_call(
        kern,
        grid_spec=pltpu.PrefetchScalarGridSpec(
            num_scalar_prefetch=1,
            grid=(t // tm,),
            in_specs=[pl.BlockSpec((tm, LANES), lambda i, tab_r: (i, 0)),
                      pl.BlockSpec((tm, LANES), lambda i, tab_r: (i, 0)),
                      pl.BlockSpec(memory_space=pl.ANY),
                      pl.BlockSpec((tm, D), lambda i, tab_r: (i, 0)),
                      pl.BlockSpec(mod.shape, lambda i, tab_r: (0, 0)),
                      pl.BlockSpec((1, D), lambda i, tab_r: (0, 0)),
                      pl.BlockSpec((1, D), lambda i, tab_r: (0, 0))],
            out_specs=pl.BlockSpec((tm, D), lambda i, tab_r: (i, 0)),
            scratch_shapes=[pltpu.VMEM((2, STAGE_ROWS, D), BF16), pltpu.SemaphoreType.DMA((2,))]),
        out_shape=jax.ShapeDtypeStruct((t, D), F32),
        compiler_params=_params(1),
        name="moe_combine",
    )(tab, col, gates, ys, x1, mod, ln_g.reshape(1, D), ln_b.reshape(1, D))


def _moe(hx, topi, gates, sizes, x1, mod, ln_g, ln_b, layer, w_gu, b_gu, w_down, b_down, dims, alpha):
    t = hx.shape[0]
    n_exp = w_gu.shape[1]
    te = EXPERT_TILE
    n_route_tiles = t // ROUTE_TILE
    max_rows = t * TOP_K + (GROUP_ALIGN - 1) * n_exp * n_route_tiles
    n_tiles = -(-max_rows // te) + n_exp
    col, colt, tab, meta = _slots(topi, sizes, te)
    tab = tab.reshape(-1)
    tot, base, padded = meta[0, :n_exp], meta[1, :n_exp], meta[2, :n_exp]
    ends = jnp.cumsum(padded // te)
    n_used = ends[-1:].astype(I32)
    tile_ids = jnp.minimum(jnp.arange(n_tiles, dtype=I32), n_used[0] - 1)
    tile_expert = jnp.sum((tile_ids[:, None] >= ends[None, :]).astype(I32), axis=1)
    tile_expert = jnp.minimum(tile_expert, n_exp - 1).astype(I32)
    pad_start = (base + tot).astype(I32)
    pad_units = ((padded - tot) // GROUP_ALIGN).astype(I32)
    used_rows = n_used[0] * te
    tail = jnp.stack([used_rows, (n_tiles * te - used_rows) // ZERO_ROWS]).astype(I32)
    xs = _dispatch(tab, pad_start, pad_units, tail, colt, hx, n_tiles * te)
    experts = jnp.arange(n_exp, dtype=I32)
    of_tile = tile_expert[:, None] == experts[None, :]
    pick = lambda per_expert: jnp.sum(jnp.where(of_tile, per_expert[None, :], 0), axis=1).astype(I32)
    tile_rows = jnp.clip(pick(base + tot) - tile_ids * te, 0, te).astype(I32)
    later = (padded[None, :] > 0) & (experts[None, :] > experts[:, None])
    next_of = jnp.min(jnp.where(later, experts[None, :], n_exp), axis=1)
    next_expert = pick(jnp.where(next_of == n_exp, -1, next_of))
    ys = _experts(tile_expert, tile_rows, next_expert, n_used, xs, layer, w_gu, b_gu, w_down, b_down)
    return _combine(tab, col, ys, gates, x1, mod, ln_g, ln_b, dims, alpha)


def _ml_in_kernel(xp_ref, x_ref, xn_ref, mod_ref, wqk_ref, bqk_ref, wvo_ref, bvo_ref, wg_ref, bg_ref,
                  cw_ref, cb_ref, q_ref, k_ref, v_ref, og_ref, g_ref, *,
                  n_lat_tiles, tiles_per_batch, ctx_tiles_per_seq, ctx_row, d):
    i = pl.program_id(0)
    tm = x_ref.shape[0]
    halo = SUBLANES
    row = _mod_row(i, n_lat_tiles, tiles_per_batch, ctx_row)
    shift = mod_ref[pl.ds(row, 1), 0:d]
    scale = mod_ref[pl.ds(row, 1), d:2 * d]
    is_lat = i < n_lat_tiles
    seq_tile = jnp.where(is_lat, i % tiles_per_batch, (i - n_lat_tiles) % ctx_tiles_per_seq)
    seq_tiles = jnp.where(is_lat, tiles_per_batch, ctx_tiles_per_seq)
    first = seq_tile == 0
    last = seq_tile == seq_tiles - 1

    h = x_ref[...] * (1.0 + scale) + shift
    h_ext = jnp.concatenate([xp_ref[...] * (1.0 + scale) + shift, h, xn_ref[...] * (1.0 + scale) + shift], axis=0)
    z = jnp.dot(h_ext.astype(BF16), wqk_ref[...], preferred_element_type=F32) + bqk_ref[...]
    nv = wvo_ref.shape[1] // 2
    h_bf = h.astype(BF16)
    v_pre = jnp.dot(h_bf, wvo_ref[:, :nv], preferred_element_type=F32) + bvo_ref[:, :nv]
    o_pre = jnp.dot(h_bf, wvo_ref[:, nv:], preferred_element_type=F32) + bvo_ref[:, nv:]
    zg = _dot_split(h, wg_ref) + bg_ref[...]
    n_ext = tm + 2 * halo
    cw = cw_ref[...]
    r = lax.broadcasted_iota(I32, z.shape, 0)
    z = jnp.where(((r < halo) & first) | ((r >= halo + tm) & last), 0.0, z)
    acc = None
    for j in range(ML_CONV_W):
        sh = (ML_CONV_W // 2 - j) % n_ext
        zj = z if sh == 0 else pltpu.roll(z, sh, 0)
        term = zj[halo:halo + tm] * cw[j:j + 1]
        acc = term if acc is None else acc + term
    qk = acc + cb_ref[...]
    qk = qk * _sigmoid(qk)
    nqk = qk.shape[1] // 2
    qk_dim = nqk // ML_HEADS
    q_ref[...] = (qk[:, :nqk] * (qk_dim ** -0.5)).astype(BF16)
    k_ref[...] = qk[:, nqk:].astype(BF16)

    v_ref[...] = v_pre.astype(BF16)
    og_ref[...] = _sigmoid(o_pre).astype(BF16)

    g = GATE_CAP * jnp.tanh(zg / GATE_CAP)
    log_sig = jnp.minimum(g, 0.0) - jnp.log(1.0 + jnp.exp(-jnp.abs(g)))
    lane = lax.broadcasted_iota(I32, g.shape, 1)
    is_forget = ((lane // ML_HEADS) % 2) == 1
    g_ref[...] = jnp.where(is_forget, log_sig, g)


def _ml_in(x_all, mod, w_in, b_in, conv_w, conv_b, dims):
    B, S, C, D = dims
    t_all = x_all.shape[0]
    tm = 256
    assert C % tm == 0 and S % tm == 0
    n_lat_tiles = B * S // tm
    nqk2 = conv_w.shape[1]
    nv = (w_in.shape[1] - nqk2 - 4 * ML_HEADS) // 2
    ng = 4 * ML_HEADS
    w_qk = w_in[:, :nqk2].astype(BF16)
    w_vo = w_in[:, nqk2:nqk2 + 2 * nv].astype(BF16)
    w_g = _split_weight(jnp.zeros((D, LANES), F32).at[:, :ng].set(w_in[:, nqk2 + 2 * nv:]))
    b_qk = b_in[:nqk2].reshape(1, -1)
    b_vo = b_in[nqk2:nqk2 + 2 * nv].reshape(1, -1)
    b_g = jnp.zeros((1, LANES), F32).at[0, :ng].set(b_in[nqk2 + 2 * nv:])
    cw = jnp.zeros((SUBLANES, nqk2), F32).at[:ML_CONV_W].set(conv_w)
    cb = conv_b.reshape(1, -1)
    hb = tm // SUBLANES
    n_hblk = t_all // SUBLANES
    full = lambda a: pl.BlockSpec(a.shape, lambda i: (0,) * a.ndim)
    row_spec = lambda w: pl.BlockSpec((tm, w), lambda i: (i, 0))
    kern = functools.partial(_ml_in_kernel, n_lat_tiles=n_lat_tiles, tiles_per_batch=S // tm,
                             ctx_tiles_per_seq=C // tm, ctx_row=B, d=D)
    return pl.pallas_call(
        kern,
        grid=(t_all // tm,),
        in_specs=[pl.BlockSpec((SUBLANES, D), lambda i: (jnp.maximum(i * hb - 1, 0), 0)),
                  row_spec(D),
                  pl.BlockSpec((SUBLANES, D), lambda i: (jnp.minimum((i + 1) * hb, n_hblk - 1), 0)),
                  full(mod), full(w_qk), full(b_qk), full(w_vo), full(b_vo), full(w_g), full(b_g),
                  full(cw), full(cb)],
        out_specs=[row_spec(nqk2 // 2), row_spec(nqk2 // 2), row_spec(nv), row_spec(nv), row_spec(LANES)],
        out_shape=[jax.ShapeDtypeStruct((t_all, nqk2 // 2), BF16), jax.ShapeDtypeStruct((t_all, nqk2 // 2), BF16),
                   jax.ShapeDtypeStruct((t_all, nv), BF16), jax.ShapeDtypeStruct((t_all, nv), BF16),
                   jax.ShapeDtypeStruct((t_all, LANES), F32)],
        compiler_params=_params(1),
        name="mlstm_in",
    )(x_all, x_all, x_all, mod, w_qk, b_qk, w_vo, b_vo, w_g, b_g, cw, cb)


def _ml_chunk(q_ref, k_ref, v_ref, g_ref, h_ref, s_sc, m_sc, reverse):
    L = q_ref.shape[0]
    qk_dim = q_ref.shape[1] // ML_HEADS
    v_dim = v_ref.shape[1] // ML_HEADS
    nh = ML_HEADS
    gates_t = g_ref[...].T
    sr = lax.broadcasted_iota(I32, (L, L), 0)
    lc = lax.broadcasted_iota(I32, (L, L), 1)
    upto = (sr >= lc) if reverse else (sr <= lc)
    cum_t = jnp.dot(gates_t, jnp.where(upto, 1.0, 0.0), precision=HIGHEST, preferred_element_type=F32)
    off = 2 * nh if reverse else 0
    li = gates_t[off:off + nh]
    b = cum_t[off + nh:off + 2 * nh]
    c = li - b
    lane = lax.broadcasted_iota(I32, c.shape, 1)
    mu = c
    d = 1
    while d < L:
        if reverse:
            shifted = jnp.where(lane < L - d, pltpu.roll(mu, L - d, 1), -jnp.inf)
        else:
            shifted = jnp.where(lane >= d, pltpu.roll(mu, d, 1), -jnp.inf)
        mu = jnp.maximum(mu, shifted)
        d *= 2
    m_prev = m_sc[...]
    mu = jnp.maximum(mu, m_prev)
    m_t = b + mu
    end = 0 if reverse else L - 1
    mu_end = mu[:, end:end + 1]
    decay = jnp.exp(m_prev[:, 0:1] - mu_end)
    wk = jnp.exp(c - mu_end)
    m_sc[...] = jnp.broadcast_to(b[:, end:end + 1] + mu_end, m_prev.shape)
    cols = jnp.concatenate([mu, m_t, jnp.zeros((L - 2 * nh, L), F32)], axis=0).T
    k_t = k_ref[...].astype(F32).T
    rl = lax.broadcasted_iota(I32, (L, L), 0)
    cs = lax.broadcasted_iota(I32, (L, L), 1)
    allowed = (cs >= rl) if reverse else (cs <= rl)
    ones = jnp.ones((L, v_dim), BF16)

    def head(h):
        qh = q_ref[:, h * qk_dim:(h + 1) * qk_dim]
        kt_h = k_t[h * qk_dim:(h + 1) * qk_dim, :]
        qk = jnp.dot(qh, kt_h.astype(BF16), preferred_element_type=F32)
        state = s_sc[h]
        q_state = jnp.dot(qh, state.astype(BF16), preferred_element_type=F32)
        yield
        mu_col = jnp.broadcast_to(cols[:, h:h + 1], (L, L))
        p = jnp.exp(jnp.where(allowed, c[h:h + 1, :] - mu_col, -jnp.inf))
        w_inter = jnp.exp(m_prev[h:h + 1, :] - mu_col)
        s = (qk * p).astype(BF16)
        yield
        v_aug = jnp.concatenate([v_ref[:, h * v_dim:(h + 1) * v_dim], ones], axis=1)
        nd = jnp.dot(s, v_aug, preferred_element_type=F32)
        kw_t = (kt_h * wk[h:h + 1, :]).astype(BF16)
        new_state = decay[h:h + 1, :] * state + jnp.dot(kw_t, v_aug, preferred_element_type=F32)
        yield
        mt_col = jnp.broadcast_to(cols[:, nh + h:nh + h + 1], (L, v_dim))
        nd = nd + jnp.concatenate([w_inter, w_inter], axis=1) * q_state
        num = nd[:, :v_dim]
        den = nd[:, v_dim:]
        h_ref[:, h * v_dim:(h + 1) * v_dim] = num / jnp.maximum(jnp.abs(den), jnp.exp(-mt_col))
        s_sc[h] = new_state

    return head


def _ml_scan_kernel(*refs, n_batch):
    n_in = 8 * n_batch
    hf_ref, hb_ref, s_sc, m_sc = refs[n_in:]

    @pl.when(pl.program_id(0) == 0)
    def _():
        s_sc[...] = jnp.zeros_like(s_sc)
        m_sc[...] = jnp.zeros_like(m_sc)

    heads = []
    for b in range(n_batch):
        qf, kf, vf, gf, qb, kb, vb, gb = refs[8 * b:8 * b + 8]
        heads.append(_ml_chunk(qf, kf, vf, gf, hf_ref.at[b], s_sc.at[b, 0], m_sc.at[b, 0], False))
        heads.append(_ml_chunk(qb, kb, vb, gb, hb_ref.at[b], s_sc.at[b, 1], m_sc.at[b, 1], True))
    for h in range(ML_HEADS):
        running = [head(h) for head in heads]
        while running:
            running = [g for g in running if next(g, True) is None]


def _ml_scan(q, k, v, g, dims):
    B, S, C, D = dims
    L = ML_CHUNK
    assert L == LANES
    nc_ctx = C // L
    nc_lat = S // L
    qk_dim = q.shape[1] // ML_HEADS
    v_dim = v.shape[1] // ML_HEADS

    def in_index(b, reverse):
        def idx(c):
            in_ctx = c < nc_ctx
            cl = c - nc_ctx
            if reverse:
                ctx_blk = (B * S + b * C) // L + (nc_ctx - 1 - c)
                lat_blk = (b * S) // L + (nc_lat - 1 - cl)
            else:
                ctx_blk = (B * S + b * C) // L + c
                lat_blk = (b * S) // L + cl
            return (jnp.where(in_ctx, ctx_blk, lat_blk), 0)
        return idx

    def out_index(reverse):
        def idx(c):
            cl = jnp.maximum(c - nc_ctx, 0)
            return (0, nc_lat - 1 - cl if reverse else cl, 0)
        return idx

    widths = (q.shape[1], k.shape[1], v.shape[1], LANES)
    in_specs, args = [], []
    for b in range(B):
        for rev in (False, True):
            in_specs += [pl.BlockSpec((L, w), in_index(b, rev)) for w in widths]
            args += [q, k, v, g]
    out = jax.ShapeDtypeStruct((B, S, v.shape[1]), F32)
    return pl.pallas_call(
        functools.partial(_ml_scan_kernel, n_batch=B),
        grid=(nc_ctx + nc_lat,),
        in_specs=in_specs,
        out_specs=[pl.BlockSpec((B, L, v.shape[1]), out_index(False)),
                   pl.BlockSpec((B, L, v.shape[1]), out_index(True))],
        out_shape=[out, out],
        scratch_shapes=[pltpu.VMEM((B, 2, ML_HEADS, qk_dim, 2 * v_dim), F32),
                        pltpu.VMEM((B, 2, ML_HEADS, LANES), F32)],
        compiler_params=_params(1),
        name="mlstm_scan",
    )(*args)


def kernel(x, c, ctx, c_ctx, ada_w, ada_b, ln_g, ln_b, attn_w_qkv, attn_b_qkv, attn_sink, attn_w_o, attn_b_o,
           ml_w_in, ml_b_in, ml_conv_w, ml_conv_b, ml_norm_g, ml_w_out, router_w, router_b,
           exp_w_gu, exp_b_gu, exp_w_down, exp_b_down):
    B, S, D = x.shape
    C = ctx.shape[1]
    depth = ada_w.shape[0]
    dims = (B, S, C, D)
    alpha = (2.0 * depth) ** 0.25
    n_lat = B * S

    cvec = jnp.zeros((SUBLANES, D), F32).at[:B].set(c).at[B].set(c_ctx)
    mods = _adaln(cvec, ada_w, ada_b)
    x_lat, x_ctx = x.reshape(n_lat, D), ctx.reshape(B * C, D)

    q, k, v = _attn_qkv(x_lat, x_ctx, mods[0], attn_w_qkv[0], attn_b_qkv[0], dims)
    o = _attention(q, k, v, attn_sink[0], dims)
    x1, hx, topi, gates, sizes = _post_mixer((o,), None, (x_lat, x_ctx), mods[0], attn_w_o[0], attn_b_o[0],
                                      ln_g[0, 0], ln_b[0, 0], router_w[0], router_b[0], dims, n_lat + B * C, alpha)
    x_all = _moe(hx, topi, gates, sizes, x1, mods[0], ln_g[0, 1], ln_b[0, 1],
                 0, exp_w_gu, exp_b_gu, exp_w_down, exp_b_down, dims, alpha)

    q, k, v, og, g = _ml_in(x_all, mods[1], ml_w_in[0], ml_b_in[0], ml_conv_w[0], ml_conv_b[0], dims)
    hf, hb = (h.reshape(n_lat, -1) for h in _ml_scan(q, k, v, g, dims))
    zero_b = jnp.zeros((D,), F32)
    x1, hx, topi, gates, sizes = _post_mixer((hf, hb, og), ml_norm_g[0], x_all, mods[1], ml_w_out[0], zero_b,
                                      ln_g[1, 0], ln_b[1, 0], router_w[1], router_b[1], dims, n_lat, alpha)
    out = _moe(hx, topi, gates, sizes, x1, mods[1], ln_g[1, 1], ln_b[1, 1],
               1, exp_w_gu, exp_b_gu, exp_w_down, exp_b_down, dims, alpha)
    return out.reshape(B, S, D)
```

```python
import functools

import jax
import jax.numpy as jnp
from jax import lax
from jax.experimental import pallas as pl
from jax.experimental.pallas import tpu as pltpu

F32 = jnp.float32
BF16 = jnp.bfloat16
I32 = jnp.int32
HIGHEST = lax.Precision.HIGHEST

GRID_W = 64
ATTN_HEAD_DIM = 64
ATTN_KV_HEADS = 4
WINDOW = 128
ATTN_BLOCK = 128
ROPE_THETA = 10000.0
ML_HEADS = 8
ML_CONV_W = 5
GATE_CAP = 15.0
TOP_K = 4
SWIGLU_ALPHA = 1.702
SWIGLU_LIMIT = 7.0
LN_EPS = 1e-5

LANES = 128
SUBLANES = 8
VMEM_LIMIT = 56 * 1024 * 1024
EXPERT_TILE = 512
ROUTE_TILE = 512
POST_PART = 256
ML_CHUNK = 128
GROUP_ALIGN = 16
BIG_UNIT = 2 * GROUP_ALIGN
STAGE_ROWS = ROUTE_TILE * TOP_K + 4 * LANES
TABLE_W = 4 * LANES
ZERO_ROWS = 512
ML_IN_PART = 256


def _params(n_axes, vmem=VMEM_LIMIT):
    return pltpu.CompilerParams(dimension_semantics=("arbitrary",) * n_axes, vmem_limit_bytes=vmem)


def _layer_norm(r, g, b):
    mu = jnp.mean(r, axis=-1, keepdims=True)
    rc = r - mu
    var = jnp.mean(rc * rc, axis=-1, keepdims=True)
    return rc * lax.rsqrt(var + LN_EPS) * g + b


def _sigmoid(x):
    return 1.0 / (1.0 + jnp.exp(-x))


def _mod_row(i, n_lat_tiles, tiles_per_batch, ctx_row):
    return jnp.where(i < n_lat_tiles, i // tiles_per_batch, ctx_row)


def _adaln_kernel(c_ref, w_ref, b_ref, o_ref):
    c = c_ref[...]
    s = c * _sigmoid(c)
    o_ref[0] = jnp.dot(s, w_ref[0], precision=HIGHEST, preferred_element_type=F32) + b_ref[0]


def _adaln(cvec, ada_w, ada_b):
    depth, d, n = ada_w.shape
    tn = 1536
    return pl.pallas_call(
        _adaln_kernel,
        grid=(depth, n // tn),
        in_specs=[pl.BlockSpec((SUBLANES, d), lambda l, j: (0, 0)),
                  pl.BlockSpec((1, d, tn), lambda l, j: (l, 0, j)),
                  pl.BlockSpec((1, 1, tn), lambda l, j: (l, 0, j))],
        out_specs=pl.BlockSpec((1, SUBLANES, tn), lambda l, j: (l, 0, j)),
        out_shape=jax.ShapeDtypeStruct((depth, SUBLANES, n), F32),
        compiler_params=_params(2),
        name="adaln",
    )(cvec, ada_w, ada_b.reshape(depth, 1, n))


def _token_specs(tm, d, n_lat_tiles):
    return [pl.BlockSpec((tm, d), lambda i, *_: (jnp.minimum(i, n_lat_tiles - 1), 0)),
            pl.BlockSpec((tm, d), lambda i, *_: (jnp.maximum(i - n_lat_tiles, 0), 0))]


def _qkv_kernel(x_ref, c_ref, mod_ref, w_ref, b_ref, cos_ref, sin_ref, q_ref, k_ref, vt_ref, *,
                n_lat_tiles, tiles_per_batch, ctx_row, d, qd, kvd):
    i = pl.program_id(0)
    row = _mod_row(i, n_lat_tiles, tiles_per_batch, ctx_row)
    shift = mod_ref[pl.ds(row, 1), 0:d]
    scale = mod_ref[pl.ds(row, 1), d:2 * d]
    x = jnp.where(i < n_lat_tiles, x_ref[...], c_ref[...])
    h = x * (1.0 + scale) + shift
    z = jnp.dot(h.astype(BF16), w_ref[...], preferred_element_type=F32) + b_ref[...]
    nrot = qd + kvd
    qk = z[:, :nrot]
    reps = nrot // LANES
    cos = jnp.concatenate([cos_ref[...]] * reps, axis=1)
    sin = jnp.concatenate([sin_ref[...]] * reps, axis=1)
    lane = lax.broadcasted_iota(I32, qk.shape, 1)
    low_half = (lane & 16) == 0
    partner = jnp.where(low_half, pltpu.roll(qk, nrot - 16, 1), pltpu.roll(qk, 16, 1))
    qk = qk * cos + partner * sin
    q_ref[...] = (qk[:, :qd] * (ATTN_HEAD_DIM ** -0.5)).astype(BF16)
    k_ref[...] = qk[:, qd:].astype(BF16)
    vt_ref[...] = z[:, nrot:].T.astype(BF16)


def _rope_tables(s_len, tm):
    half = ATTN_HEAD_DIM // 4
    freqs = ROPE_THETA ** (-jnp.arange(half, dtype=F32) / half)
    t = jnp.arange(s_len)
    rows = (t // GRID_W).astype(F32)[:, None] * freqs[None, :]
    cols = (t % GRID_W).astype(F32)[:, None] * freqs[None, :]
    ang = jnp.concatenate([rows, rows, cols, cols], axis=1)
    sign = jnp.tile(jnp.concatenate([-jnp.ones((half,), F32), jnp.ones((half,), F32)]), 2)
    cos = jnp.cos(ang)
    sin = jnp.sin(ang) * sign[None, :]
    reps = LANES // ATTN_HEAD_DIM
    cos = jnp.concatenate([jnp.tile(cos, (1, reps)), jnp.ones((tm, LANES), F32)], axis=0)
    sin = jnp.concatenate([jnp.tile(sin, (1, reps)), jnp.zeros((tm, LANES), F32)], axis=0)
    return cos, sin


def _attn_qkv(x_lat, x_ctx, mod, w_qkv, b_qkv, dims):
    B, S, C, D = dims
    t_all = x_lat.shape[0] + x_ctx.shape[0]
    tm = 512
    n_lat_tiles = B * S // tm
    tiles_per_batch = S // tm
    ncols = w_qkv.shape[1]
    kvd = ATTN_KV_HEADS * ATTN_HEAD_DIM
    qd = ncols - 2 * kvd
    cos, sin = _rope_tables(S, tm)

    def tab_idx(i):
        return (jnp.where(i < n_lat_tiles, i % tiles_per_batch, tiles_per_batch), 0)

    kern = functools.partial(_qkv_kernel, n_lat_tiles=n_lat_tiles, tiles_per_batch=tiles_per_batch,
                             ctx_row=B, d=D, qd=qd, kvd=kvd)
    return pl.pallas_call(
        kern,
        grid=(t_all // tm,),
        in_specs=_token_specs(tm, D, n_lat_tiles) + [
                  pl.BlockSpec(mod.shape, lambda i: (0, 0)),
                  pl.BlockSpec((D, ncols), lambda i: (0, 0)),
                  pl.BlockSpec((1, ncols), lambda i: (0, 0)),
                  pl.BlockSpec((tm, LANES), tab_idx),
                  pl.BlockSpec((tm, LANES), tab_idx)],
        out_specs=[pl.BlockSpec((tm, qd), lambda i: (i, 0)),
                   pl.BlockSpec((tm, kvd), lambda i: (i, 0)),
                   pl.BlockSpec((kvd, tm), lambda i: (0, i))],
        out_shape=[jax.ShapeDtypeStruct((t_all, qd), BF16),
                   jax.ShapeDtypeStruct((t_all, kvd), BF16),
                   jax.ShapeDtypeStruct((kvd, t_all), BF16)],
        compiler_params=_params(1),
        name="attn_qkv",
    )(x_lat, x_ctx, mod, w_qkv.astype(BF16), b_qkv.reshape(1, ncols), cos, sin)


def _attn_kernel(sink_ref, q_ref, kp_ref, ko_ref, kn_ref, kc_ref, vp_ref, vo_ref, vn_ref, vc_ref, o_ref,
                 bias_sc, s_sc, p_sc, ot_sc, *,
                 n_lat_steps, nb, s_len, c_len):
    j = pl.program_id(0)
    is_lat = j < n_lat_steps
    n = j % nb
    blk = ATTN_BLOCK
    nloc = 3 * blk
    nk = nloc + c_len
    ki = lax.broadcasted_iota(I32, (nloc, blk), 0)
    qj = lax.broadcasted_iota(I32, (nloc, blk), 1)
    kpos = n * blk - WINDOW + ki
    qpos = n * blk + qj
    local_ok = (jnp.abs(kpos - qpos) <= WINDOW) & (kpos >= 0) & (kpos < s_len) & is_lat
    bias_sc[...] = jnp.where(local_ok, 0.0, -jnp.inf)
    hd = ATTN_HEAD_DIM
    group = q_ref.shape[1] // (ATTN_KV_HEADS * hd)
    kcat = jnp.concatenate([kp_ref[...], ko_ref[...], kn_ref[...], kc_ref[...]], axis=0)
    vcat_t = jnp.concatenate([vp_ref[...], vo_ref[...], vn_ref[...], vc_ref[...]], axis=1)
    def score_matmul(kh):
        q_grp = jnp.concatenate([q_ref[:, (kh * group + g) * hd:(kh * group + g + 1) * hd]
                                 for g in range(group)], axis=0)
        s_sc[kh] = lax.dot_general(kcat[:, kh * hd:(kh + 1) * hd], q_grp, (((1,), (1,)), ((), ())),
                                   preferred_element_type=F32)

    def weighted_values(kh, sink_terms):
        v_ones = jnp.concatenate([vcat_t[kh * hd:(kh + 1) * hd, :], jnp.ones((SUBLANES, nk), BF16)], axis=0)
        o_aug = jnp.dot(v_ones, p_sc[kh], preferred_element_type=F32)
        l = o_aug[hd:hd + 1, :] + jnp.concatenate(sink_terms, axis=1)
        o_t = o_aug[0:hd, :] * (1.0 / l)
        for g in range(group):
            h = kh * group + g
            ot_sc[h * hd:(h + 1) * hd, :] = o_t[:, g * blk:(g + 1) * blk]

    score_matmul(0)
    for kh in range(ATTN_KV_HEADS):
        if kh + 1 < ATTN_KV_HEADS:
            score_matmul(kh + 1)
        sink_terms = []
        for g in range(group):
            cols = slice(g * blk, (g + 1) * blk)
            def scores(a):
                s = s_sc[kh, a:a + blk, cols]
                return s + bias_sc[a:a + blk, :] if a < nloc else s

            sk = sink_ref[kh * group + g]
            top = scores(0)
            for a in range(blk, nk, blk):
                top = jnp.maximum(top, scores(a))
            m = jnp.maximum(jnp.max(top, axis=0, keepdims=True), sk)
            for a in range(0, nk, blk):
                p_sc[kh, a:a + blk, cols] = jnp.exp((scores(a) - m).astype(BF16))
            sink_terms.append(jnp.exp(sk - m))
        weighted_values(kh, sink_terms)
    o_ref[...] = ot_sc[...].T.astype(BF16)


def _attention(q_all, k_all, vt_all, sink, dims):
    B, S, C, D = dims
    blk = ATTN_BLOCK
    nb = S // blk
    n_lat_steps = B * nb
    ctx_steps_per_batch = C // blk
    n_steps = n_lat_steps + B * ctx_steps_per_batch
    qd = q_all.shape[1]
    kvd = k_all.shape[1]

    def local_idx(off):
        def idx(j):
            b = j // nb
            nn = jnp.clip(j % nb + off, 0, nb - 1)
            return (jnp.where(j < n_lat_steps, b * nb + nn, j), 0)
        return idx

    def ctx_idx(j):
        b = jnp.where(j < n_lat_steps, j // nb, (j - n_lat_steps) // ctx_steps_per_batch)
        return (B * S // C + b, 0)

    swap = lambda f: (lambda j: f(j)[::-1])
    loc = lambda off: pl.BlockSpec((blk, kvd), local_idx(off))
    ctxs = pl.BlockSpec((C, kvd), ctx_idx)
    loc_t = lambda off: pl.BlockSpec((kvd, blk), swap(local_idx(off)))
    ctxs_t = pl.BlockSpec((kvd, C), swap(ctx_idx))
    group = qd // kvd
    nk = 3 * blk + C
    kern = functools.partial(_attn_kernel, n_lat_steps=n_lat_steps, nb=nb, s_len=S, c_len=C)
    return pl.pallas_call(
        kern,
        grid=(n_steps,),
        in_specs=[pl.BlockSpec(memory_space=pltpu.SMEM),
                  pl.BlockSpec((blk, qd), lambda j: (j, 0)),
                  loc(-1), loc(0), loc(1), ctxs,
                  loc_t(-1), loc_t(0), loc_t(1), ctxs_t],
        out_specs=pl.BlockSpec((blk, qd), lambda j: (j, 0)),
        out_shape=jax.ShapeDtypeStruct((q_all.shape[0], qd), BF16),
        scratch_shapes=[pltpu.VMEM((3 * blk, blk), F32),
                        pltpu.VMEM((ATTN_KV_HEADS, nk, group * blk), F32),
                        pltpu.VMEM((ATTN_KV_HEADS, nk, group * blk), BF16),
                        pltpu.VMEM((qd, blk), F32)],
        compiler_params=_params(1),
        name="attn_core",
    )(sink, q_all, k_all, k_all, k_all, k_all, vt_all, vt_all, vt_all, vt_all)


def _split_bf16(a):
    hi = a.astype(BF16)
    return hi, (a - hi.astype(F32)).astype(BF16)


def _split_weight(w_f32):
    hi = w_f32.astype(BF16)
    return jnp.concatenate([hi, (w_f32 - hi.astype(F32)).astype(BF16)], axis=1)


def _dot_split(a, w_ref):
    a_hi, a_lo = _split_bf16(a)
    n = w_ref.shape[1] // 2
    both = jnp.dot(a_hi, w_ref[...], preferred_element_type=F32)
    return both[:, :n] + both[:, n:] + jnp.dot(a_lo, w_ref[:, :n], preferred_element_type=F32)


def _top_k_route(logits, rows, topi_ref, gate_ref, sizes_ref, part):
    lane = lax.broadcasted_iota(I32, logits.shape, 1)
    lanef = lane.astype(F32)
    vals, idxs = [], []
    l = logits
    for _ in range(TOP_K):
        m = jnp.max(l, axis=1, keepdims=True)
        idx = jnp.min(jnp.where(l == m, lanef, float(LANES)), axis=1, keepdims=True)
        vals.append(m)
        idxs.append(idx)
        l = jnp.where(lanef == idx, -jnp.inf, l)
    es = [jnp.exp(v - vals[0]) for v in vals]
    denom = es[0]
    for e in es[1:]:
        denom = denom + e
    topi = jnp.zeros(logits.shape, F32)
    gates = jnp.zeros(logits.shape, F32)
    chosen = jnp.zeros(logits.shape, F32)
    for k in range(TOP_K):
        topi = jnp.where(lane == k, idxs[k], topi)
        gates = jnp.where(lane == k, es[k] / denom, gates)
        chosen = chosen + jnp.where(lanef == idxs[k], 1.0, 0.0)
    topi_ref[rows, :] = topi.astype(I32)
    gate_ref[rows, :] = gates
    sizes_ref[0, part:part + 1, :] = jnp.sum(chosen, axis=0, keepdims=True)


def _post_common(a_fn, x_fn, mod_ref, w_ref, b_ref, lng_ref, lnb_ref, rw_ref, rb_ref,
                 x1_ref, hx_ref, topi_ref, gate_ref, sizes_ref, row, d, alpha):
    gate_mix = mod_ref[pl.ds(row, 1), 2 * d:3 * d]
    shift = mod_ref[pl.ds(row, 1), 3 * d:4 * d]
    scale = mod_ref[pl.ds(row, 1), 4 * d:5 * d]
    sizes_ref[...] = jnp.zeros_like(sizes_ref)

    def part_stages(part):
        rows = slice(part * POST_PART, (part + 1) * POST_PART)
        y = jnp.dot(a_fn(rows), w_ref[...], preferred_element_type=F32) + b_ref[...]
        yield
        x1 = _layer_norm(alpha * x_fn(rows) + gate_mix * y, lng_ref[...], lnb_ref[...])
        hx = x1 * (1.0 + scale) + shift
        x1_ref[rows, :] = x1
        hx_ref[rows, :] = hx.astype(BF16)
        logits = _dot_split(hx, rw_ref) + rb_ref[...]
        yield
        _top_k_route(logits, rows, topi_ref, gate_ref, sizes_ref, part)

    running = [part_stages(p) for p in range(x1_ref.shape[0] // POST_PART)]
    while running:
        running = [g for g in running if next(g, True) is None]


def _post_attn_kernel(o_ref, x_ref, c_ref, mod_ref, w_ref, b_ref, lng_ref, lnb_ref, rw_ref, rb_ref,
                      x1_ref, hx_ref, topi_ref, gate_ref, sizes_ref, *, n_lat_tiles, tiles_per_batch, ctx_row, d, alpha):
    i = pl.program_id(0)
    row = _mod_row(i, n_lat_tiles, tiles_per_batch, ctx_row)
    is_lat = i < n_lat_tiles
    _post_common(lambda rows: o_ref[rows, :],
                 lambda rows: jnp.where(is_lat, x_ref[rows, :], c_ref[rows, :]),
                 mod_ref, w_ref, b_ref, lng_ref, lnb_ref, rw_ref, rb_ref,
                 x1_ref, hx_ref, topi_ref, gate_ref, sizes_ref, row, d, alpha)


def _post_mlstm_kernel(hf_ref, hb_ref, og_ref, ng_ref, x_ref, mod_ref, w_ref, b_ref, lng_ref, lnb_ref,
                       rw_ref, rb_ref, x1_ref, hx_ref, topi_ref, gate_ref, sizes_ref, *,
                       n_lat_tiles, tiles_per_batch, ctx_row, d, alpha):
    row = _mod_row(pl.program_id(0), n_lat_tiles, tiles_per_batch, ctx_row)

    def gated_head_norm(rows):
        hsum = hf_ref[rows, :] + hb_ref[rows, :]
        vdim = hsum.shape[1] // ML_HEADS
        parts = []
        for h in range(ML_HEADS):
            seg = hsum[:, h * vdim:(h + 1) * vdim]
            mu = jnp.mean(seg, axis=1, keepdims=True)
            sc = seg - mu
            var = jnp.mean(sc * sc, axis=1, keepdims=True)
            parts.append(sc * lax.rsqrt(var + LN_EPS))
        y = jnp.concatenate(parts, axis=1) * ng_ref[...]
        return (og_ref[rows, :].astype(F32) * y).astype(BF16)

    _post_common(gated_head_norm, lambda rows: x_ref[rows, :],
                 mod_ref, w_ref, b_ref, lng_ref, lnb_ref, rw_ref, rb_ref,
                 x1_ref, hx_ref, topi_ref, gate_ref, sizes_ref, row, d, alpha)


def _post_mixer(mixer_inputs, norm_g, x_all, mod, w_o, b_o, ln_g, ln_b, router_w, router_b, dims, n_rows, alpha):
    B, S, C, D = dims
    tm = ROUTE_TILE
    n_lat_tiles = B * S // tm
    tiles_per_batch = S // tm
    n_exp = router_w.shape[1]
    rw = _split_weight(jnp.zeros((D, LANES), F32).at[:, :n_exp].set(router_w))
    rb = jnp.full((1, LANES), -1e30, F32).at[0, :n_exp].set(router_b)
    row_spec = lambda w: pl.BlockSpec((tm, w), lambda i: (i, 0))
    full = lambda a: pl.BlockSpec(a.shape, lambda i: (0,) * a.ndim)
    common = dict(n_lat_tiles=n_lat_tiles, tiles_per_batch=tiles_per_batch, ctx_row=B, d=D, alpha=alpha)
    w_bf = w_o.astype(BF16)
    b2 = b_o.reshape(1, D)
    if isinstance(x_all, tuple):
        x_args, x_specs = list(x_all), _token_specs(tm, D, n_lat_tiles)
    else:
        x_args, x_specs = [x_all], [row_spec(D)]
    tail = x_args + [mod, w_bf, b2, ln_g.reshape(1, D), ln_b.reshape(1, D), rw, rb]
    tail_specs = x_specs + [full(mod), full(w_bf), full(b2), pl.BlockSpec((1, D), lambda i: (0, 0)),
                            pl.BlockSpec((1, D), lambda i: (0, 0)), full(rw), full(rb)]
    if norm_g is None:
        kern = functools.partial(_post_attn_kernel, **common)
        args = list(mixer_inputs) + tail
        specs = [row_spec(mixer_inputs[0].shape[1])] + tail_specs
        name = "post_attn"
    else:
        kern = functools.partial(_post_mlstm_kernel, **common)
        ng = norm_g.reshape(1, -1)
        args = list(mixer_inputs) + [ng] + tail
        specs = [row_spec(a.shape[1]) for a in mixer_inputs] + [full(ng)] + tail_specs
        name = "post_mlstm"
    return pl.pallas_call(
        kern,
        grid=(n_rows // tm,),
        in_specs=specs,
        out_specs=[row_spec(D), row_spec(D), row_spec(LANES), row_spec(LANES),
                   pl.BlockSpec((1, SUBLANES, LANES), lambda i: (i, 0, 0))],
        out_shape=[jax.ShapeDtypeStruct((n_rows, D), F32), jax.ShapeDtypeStruct((n_rows, D), BF16),
                   jax.ShapeDtypeStruct((n_rows, LANES), I32), jax.ShapeDtypeStruct((n_rows, LANES), F32),
                   jax.ShapeDtypeStruct((n_rows // tm, SUBLANES, LANES), F32)],
        compiler_params=_params(1),
        name=name,
    )(*args)


def _exclusive_lane_cumsum(row):
    r = lax.broadcasted_iota(I32, (LANES, LANES), 0)
    c = lax.broadcasted_iota(I32, (LANES, LANES), 1)
    before = jnp.where(r < c, 1.0, 0.0)
    return jnp.dot(jnp.broadcast_to(row, (SUBLANES, LANES)), before,
                   precision=HIGHEST, preferred_element_type=F32)[0:1]


def _for_each_unit(tab_ref, tile, fn):
    base = tile * TABLE_W
    for blk, rows in ((0, BIG_UNIT), (2, GROUP_ALIGN)):
        def body(j, carry, blk=blk, rows=rows):
            slot_row = pl.multiple_of(tab_ref[base + blk * LANES + j], GROUP_ALIGN)
            stage_row = pl.multiple_of(tab_ref[base + (blk + 1) * LANES + j], GROUP_ALIGN)
            fn(slot_row, stage_row, rows)
            return carry
        lax.fori_loop(0, tab_ref[base + blk * LANES + LANES - 1], body, 0)


def _slots_kernel(topi_ref, sizes_ref, col_ref, colt_ref, tab_ref, meta_ref, base_sc, carry_sc, *, tile):
    i = pl.program_id(0)
    tm = topi_ref.shape[0]
    topi = topi_ref[...]
    lane = lax.broadcasted_iota(I32, topi.shape, 1)
    sel = [lane == topi[:, k:k + 1] for k in range(TOP_K)]
    maskf = jnp.where(sel[0], 1.0, 0.0)
    for s in sel[1:]:
        maskf = maskf + jnp.where(s, 1.0, 0.0)
    n8 = jnp.ceil(jnp.sum(maskf, axis=0, keepdims=True) / GROUP_ALIGN) * GROUP_ALIGN

    @pl.when(i == 0)
    def _():
        groups = jnp.ceil(jnp.sum(sizes_ref[...], axis=1) / GROUP_ALIGN) * GROUP_ALIGN
        tot = jnp.sum(groups, axis=0, keepdims=True)
        padded = jnp.ceil(tot / tile) * tile
        base = _exclusive_lane_cumsum(padded)
        base_sc[...] = base
        carry_sc[...] = jnp.zeros_like(carry_sc)
        rowi = lax.broadcasted_iota(I32, meta_ref.shape, 0)
        meta_ref[...] = jnp.where(rowi == 0, tot, jnp.where(rowi == 1, base, padded)).astype(I32)

    start = base_sc[...] + carry_sc[...]
    off = _exclusive_lane_cumsum(n8)
    r = lax.broadcasted_iota(I32, (tm, tm), 0)
    c = lax.broadcasted_iota(I32, (tm, tm), 1)
    earlier = jnp.where(c < r, 1.0, 0.0).astype(BF16)
    rank = jnp.dot(earlier, maskf.astype(BF16), preferred_element_type=F32)
    stage_row = rank + off
    out = jnp.zeros(topi.shape, F32)
    for k in range(TOP_K):
        pk = jnp.sum(jnp.where(sel[k], stage_row, 0.0), axis=1, keepdims=True)
        out = jnp.where(lane == k, pk, out)
    col_ref[...] = out.astype(I32)
    colt_ref[0] = out.T[0:SUBLANES].astype(I32)

    n_big = jnp.floor(n8 / BIG_UNIT)
    n_small = (n8 - BIG_UNIT * n_big) / GROUP_ALIGN
    first_big = _exclusive_lane_cumsum(n_big)
    first_small = _exclusive_lane_cumsum(n_small)
    rows = jnp.concatenate([off, start, first_big, n_big, first_small, n_small,
                            jnp.zeros((LANES - 6, LANES), F32)], axis=0)
    per_expert = rows.T
    off_c, start_c, fb_c, nb_c, fs_c, ns_c = (per_expert[:, k:k + 1] for k in range(6))
    u = lax.broadcasted_iota(I32, (LANES, LANES), 1).astype(F32)
    lane_t = lax.broadcasted_iota(I32, (1, LANES), 1)

    def unit_list(first_c, count_c, rel, count_row):
        inside = (u >= first_c) & (u < first_c + count_c)
        src = jnp.sum(jnp.where(inside, start_c + rel, 0.0), axis=0, keepdims=True)
        dst = jnp.sum(jnp.where(inside, off_c + rel, 0.0), axis=0, keepdims=True)
        total = jnp.sum(count_row, axis=1, keepdims=True)
        return [jnp.where(lane_t == LANES - 1, total, src), dst]

    lists = (unit_list(fb_c, nb_c, BIG_UNIT * (u - fb_c), n_big)
             + unit_list(fs_c, ns_c, BIG_UNIT * nb_c + GROUP_ALIGN * (u - fs_c), n_small))
    tab_ref[0] = jnp.concatenate(lists, axis=1).astype(I32)
    carry_sc[...] = carry_sc[...] + n8


def _slots(topi, sizes, tile):
    t = topi.shape[0]
    tm = ROUTE_TILE
    nt = t // tm
    return pl.pallas_call(
        functools.partial(_slots_kernel, tile=float(tile)),
        grid=(nt,),
        in_specs=[pl.BlockSpec((tm, LANES), lambda i: (i, 0)),
                  pl.BlockSpec(sizes.shape, lambda i: (0, 0, 0))],
        out_specs=[pl.BlockSpec((tm, LANES), lambda i: (i, 0)),
                   pl.BlockSpec((1, SUBLANES, tm), lambda i: (i, 0, 0)),
                   pl.BlockSpec((1, 1, TABLE_W), lambda i: (i, 0, 0)),
                   pl.BlockSpec((SUBLANES, LANES), lambda i: (0, 0))],
        out_shape=[jax.ShapeDtypeStruct((t, LANES), I32),
                   jax.ShapeDtypeStruct((nt, SUBLANES, tm), I32),
                   jax.ShapeDtypeStruct((nt, 1, TABLE_W), I32),
                   jax.ShapeDtypeStruct((SUBLANES, LANES), I32)],
        scratch_shapes=[pltpu.VMEM((1, LANES), F32), pltpu.VMEM((1, LANES), F32)],
        compiler_params=_params(1),
        name="moe_slots",
    )(topi, sizes)


def _dispatch_kernel(tab_ref, pad_start_ref, pad_units_ref, tail_ref, colt_ref, hx_ref, xs_ref, stage, zeros, sem, zsem):
    i = pl.program_id(0)
    nt = pl.num_programs(0)
    slot = i % 2
    kb = stage.shape[1]
    tm = hx_ref.shape[0]

    def scatter(buf_slot):
        def copy(slot_row, stage_row, rows):
            return pltpu.make_async_copy(stage.at[buf_slot, pl.ds(stage_row, rows)],
                                         xs_ref.at[pl.ds(slot_row, rows)], sem.at[buf_slot])
        return copy

    def drain(tile, buf_slot):
        _for_each_unit(tab_ref, tile, lambda *u: scatter(buf_slot)(*u).wait())

    @pl.when(i >= 2)
    def _():
        drain(i - 2, slot)

    colt = colt_ref[0]
    c = lax.broadcasted_iota(I32, (kb, tm), 0)
    onehot = jnp.zeros((kb, tm), F32)
    for k in range(TOP_K):
        onehot = jnp.where(c == colt[k:k + 1, :], 1.0, onehot)
    stage[slot] = jnp.dot(onehot.astype(BF16), hx_ref[...], preferred_element_type=F32).astype(BF16)

    _for_each_unit(tab_ref, i, lambda *u: scatter(slot)(*u).start())

    @pl.when(i == nt - 1)
    def _():
        zeros[...] = jnp.zeros_like(zeros)
        n_exp = pad_start_ref.shape[0]

        def zero_copy(e, u):
            dst = pl.multiple_of(pad_start_ref[e] + u * GROUP_ALIGN, GROUP_ALIGN)
            return pltpu.make_async_copy(zeros.at[pl.ds(0, GROUP_ALIGN)], xs_ref.at[pl.ds(dst, GROUP_ALIGN)], zsem)

        def per_expert(fn):
            def outer(e, carry):
                def inner(u, carry2):
                    fn(e, u)
                    return carry2
                lax.fori_loop(0, pad_units_ref[e], inner, 0)
                return carry
            lax.fori_loop(0, n_exp, outer, 0)

        per_expert(lambda e, u: zero_copy(e, u).start())
        per_expert(lambda e, u: zero_copy(e, u).wait())

        zrows = zeros.shape[0]

        def tail_copy(u):
            dst = pl.multiple_of(tail_ref[0] + u * zrows, zrows)
            return pltpu.make_async_copy(zeros, xs_ref.at[pl.ds(dst, zrows)], zsem)

        def tail_loop(fn):
            def body(u, carry):
                fn(u)
                return carry
            lax.fori_loop(0, tail_ref[1], body, 0)

        tail_loop(lambda u: tail_copy(u).start())
        tail_loop(lambda u: tail_copy(u).wait())

        @pl.when(nt >= 2)
        def _():
            drain(i - 1, 1 - slot)
        drain(i, slot)


def _dispatch(tab, pad_start, pad_units, tail, colt, hx, n_slots):
    t, d = hx.shape
    tm = ROUTE_TILE
    return pl.pallas_call(
        _dispatch_kernel,
        grid_spec=pltpu.PrefetchScalarGridSpec(
            num_scalar_prefetch=4,
            grid=(t // tm,),
            in_specs=[pl.BlockSpec((1, SUBLANES, tm), lambda i, *_: (i, 0, 0)),
                      pl.BlockSpec((tm, d), lambda i, *_: (i, 0))],
            out_specs=pl.BlockSpec(memory_space=pl.ANY),
            scratch_shapes=[pltpu.VMEM((2, STAGE_ROWS, d), BF16), pltpu.VMEM((ZERO_ROWS, d), BF16),
                            pltpu.SemaphoreType.DMA((2,)), pltpu.SemaphoreType.DMA(())]),
        out_shape=jax.ShapeDtypeStruct((n_slots, d), BF16),
        compiler_params=_params(1),
        name="moe_dispatch",
    )(tab, pad_start, pad_units, tail, colt, hx)


def _expert_kernel(te_ref, rows_ref, next_ref, nu_ref, xs_ref, wgu_hbm, bgu_ref, wd_hbm, bd_ref, ys_ref,
                   wgu_f32, wd_f32, wgu_sc, wd_sc, wsem, *, layer):
    j = pl.program_id(0)
    active = j < nu_ref[0]
    changed = (j == 0) | (te_ref[j] != te_ref[jnp.maximum(j - 1, 0)])
    ff = wd_sc.shape[0]
    te = xs_ref.shape[0]
    half = te // 2

    def weight_copies(e):
        return (pltpu.make_async_copy(wgu_hbm.at[layer, e], wgu_f32, wsem.at[0]),
                pltpu.make_async_copy(wd_hbm.at[layer, e], wd_f32, wsem.at[1]))

    @pl.when(active & changed)
    def _():
        @pl.when(j == 0)
        def _():
            for cp in weight_copies(te_ref[0]):
                cp.start()

        for cp in weight_copies(te_ref[j]):
            cp.wait()
        wgu_sc[...] = wgu_f32[...].astype(BF16)
        wd_sc[...] = wd_f32[...].astype(BF16)

        @pl.when(next_ref[j] >= 0)
        def _():
            for cp in weight_copies(next_ref[j]):
                cp.start()

    def run(rows):
        x = xs_ref[0:rows, :]
        nblk = 2
        fb = ff // nblk

        def up(c):
            gl = jnp.dot(x, wgu_sc[:, c * fb:(c + 1) * fb], preferred_element_type=F32) + bgu_ref[:, c * fb:(c + 1) * fb]
            lin = (jnp.dot(x, wgu_sc[:, ff + c * fb:ff + (c + 1) * fb], preferred_element_type=F32)
                   + bgu_ref[:, ff + c * fb:ff + (c + 1) * fb])
            return gl, lin

        y = bd_ref[...]
        nxt = up(0)
        for c in range(nblk):
            gl, lin = nxt
            if c + 1 < nblk:
                nxt = up(c + 1)
            gl = jnp.minimum(gl, SWIGLU_LIMIT)
            lin = jnp.clip(lin, -SWIGLU_LIMIT, SWIGLU_LIMIT)
            act = gl * _sigmoid(SWIGLU_ALPHA * gl) * (lin + 1.0)
            y = y + jnp.dot(act.astype(BF16), wd_sc[c * fb:(c + 1) * fb, :], preferred_element_type=F32)
        ys_ref[0:rows, :] = y.astype(BF16)
        if rows < te:
            ys_ref[rows:te, :] = jnp.zeros((te - rows, ys_ref.shape[1]), BF16)

    valid_rows = rows_ref[j]

    @pl.when(active & (valid_rows > half))
    def _():
        run(te)

    @pl.when(active & (valid_rows <= half))
    def _():
        run(half)

    @pl.when(jnp.logical_not(active))
    def _():
        ys_ref[...] = jnp.zeros_like(ys_ref)


def _experts(tile_expert, tile_rows, next_expert, n_used, xs, layer, w_gu, b_gu, w_down, b_down):
    n_slots, d = xs.shape
    depth, n_exp, _, ff2 = w_gu.shape
    ff = w_down.shape[2]
    te = EXPERT_TILE
    row_idx = lambda j, te_r, rows_r, nx_r, nu: (jnp.maximum(jnp.minimum(j, nu[0] - 1), 0), 0)
    b_idx = lambda j, te_r, rows_r, nx_r, nu: (layer, te_r[j], 0, 0)
    return pl.pallas_call(
        functools.partial(_expert_kernel, layer=layer),
        grid_spec=pltpu.PrefetchScalarGridSpec(
            num_scalar_prefetch=4,
            grid=(n_slots // te,),
            in_specs=[pl.BlockSpec((te, d), row_idx),
                      pl.BlockSpec(memory_space=pl.ANY),
                      pl.BlockSpec((None, None, 1, ff2), b_idx),
                      pl.BlockSpec(memory_space=pl.ANY),
                      pl.BlockSpec((None, None, 1, d), b_idx)],
            out_specs=pl.BlockSpec((te, d), lambda j, te_r, rows_r, nx_r, nu: (j, 0)),
            scratch_shapes=[pltpu.VMEM((d, ff2), F32), pltpu.VMEM((ff, d), F32),
                            pltpu.VMEM((d, ff2), BF16), pltpu.VMEM((ff, d), BF16),
                            pltpu.SemaphoreType.DMA((2,))]),
        out_shape=jax.ShapeDtypeStruct((n_slots, d), BF16),
        compiler_params=_params(1),
        name="moe_experts",
    )(tile_expert, tile_rows, next_expert, n_used, xs, w_gu, b_gu.reshape(depth, n_exp, 1, ff2),
      w_down, b_down.reshape(depth, n_exp, 1, d))


def _combine_kernel(tab_ref, col_ref, gate_ref, ys_ref, x_ref, mod_ref, lng_ref, lnb_ref, out_ref, stage, sem, *,
                    n_lat_tiles, tiles_per_batch, ctx_row, d, alpha):
    i = pl.program_id(0)
    nt = pl.num_programs(0)
    slot = i % 2
    kb = stage.shape[1]
    tm = x_ref.shape[0]

    def gather(buf_slot):
        def copy(slot_row, stage_row, rows):
            return pltpu.make_async_copy(ys_ref.at[pl.ds(slot_row, rows)],
                                         stage.at[buf_slot, pl.ds(stage_row, rows)], sem.at[buf_slot])
        return copy

    @pl.when(i == 0)
    def _():
        stage[...] = jnp.zeros_like(stage)
        _for_each_unit(tab_ref, 0, lambda *u: gather(0)(*u).start())

    @pl.when(i + 1 < nt)
    def _():
        _for_each_unit(tab_ref, i + 1, lambda *u: gather(1 - slot)(*u).start())

    _for_each_unit(tab_ref, i, lambda *u: gather(slot)(*u).wait())

    col = col_ref[...]
    gates = gate_ref[...]
    kblk = 2 * LANES

    def gate_weights(r0):
        c = r0 + lax.broadcasted_iota(I32, (tm, kblk), 1)
        w = jnp.zeros((tm, kblk), F32)
        for k in range(TOP_K):
            w = jnp.where(c == col[:, k:k + 1], gates[:, k:k + 1], w)
        return w.astype(BF16)

    y = jnp.zeros((tm, d), F32)
    nxt = gate_weights(0)
    for r0 in range(0, kb, kblk):
        w = nxt
        if r0 + kblk < kb:
            nxt = gate_weights(r0 + kblk)
        y = y + jnp.dot(w, stage[slot, r0:r0 + kblk, :], preferred_element_type=F32)
    row = _mod_row(i, n_lat_tiles, tiles_per_batch, ctx_row)
    gate_mlp = mod_ref[pl.ds(row, 1), 5 * d:6 * d]
    out_ref[...] = _layer_norm(alpha * x_ref[...] + gate_mlp * y, lng_ref[...], lnb_ref[...])


def _combine(tab, col, ys, gates, x1, mod, ln_g, ln_b, dims, alpha):
    B, S, C, D = dims
    t = x1.shape[0]
    tm = ROUTE_TILE
    kern = functools.partial(_combine_kernel, n_lat_tiles=B * S // tm, tiles_per_batch=S // tm, ctx_row=B,
                             d=D, alpha=alpha)
    return pl.pallas_call(
        kern,
        grid_spec=pltpu.PrefetchScalarGridSpec(
            num_scalar_prefetch=1,
            grid=(t // tm,),
            in_specs=[pl.BlockSpec((tm, LANES), lambda i, tab_r: (i, 0)),
                      pl.BlockSpec((tm, LANES), lambda i, tab_r: (i, 0)),
                      pl.BlockSpec(memory_space=pl.ANY),
                      pl.BlockSpec((tm, D), lambda i, tab_r: (i, 0)),
                      pl.BlockSpec(mod.shape, lambda i, tab_r: (0, 0)),
                      pl.BlockSpec((1, D), lambda i, tab_r: (0, 0)),
                      pl.BlockSpec((1, D), lambda i, tab_r: (0, 0))],
            out_specs=pl.BlockSpec((tm, D), lambda i, tab_r: (i, 0)),
            scratch_shapes=[pltpu.VMEM((2, STAGE_ROWS, D), BF16), pltpu.SemaphoreType.DMA((2,))]),
        out_shape=jax.ShapeDtypeStruct((t, D), F32),
        compiler_params=_params(1),
        name="moe_combine",
    )(tab, col, gates, ys, x1, mod, ln_g.reshape(1, D), ln_b.reshape(1, D))


def _moe(hx, topi, gates, sizes, x1, mod, ln_g, ln_b, layer, w_gu, b_gu, w_down, b_down, dims, alpha):
    t = hx.shape[0]
    n_exp = w_gu.shape[1]
    te = EXPERT_TILE
    n_route_tiles = t // ROUTE_TILE
    max_rows = t * TOP_K + (GROUP_ALIGN - 1) * n_exp * n_route_tiles
    n_tiles = -(-max_rows // te) + n_exp
    col, colt, tab, meta = _slots(topi, sizes, te)
    tab = tab.reshape(-1)
    tot, base, padded = meta[0, :n_exp], meta[1, :n_exp], meta[2, :n_exp]
    ends = jnp.cumsum(padded // te)
    n_used = ends[-1:].astype(I32)
    tile_ids = jnp.minimum(jnp.arange(n_tiles, dtype=I32), n_used[0] - 1)
    tile_expert = jnp.sum((tile_ids[:, None] >= ends[None, :]).astype(I32), axis=1)
    tile_expert = jnp.minimum(tile_expert, n_exp - 1).astype(I32)
    pad_start = (base + tot).astype(I32)
    pad_units = ((padded - tot) // GROUP_ALIGN).astype(I32)
    used_rows = n_used[0] * te
    tail = jnp.stack([used_rows, (n_tiles * te - used_rows) // ZERO_ROWS]).astype(I32)
    xs = _dispatch(tab, pad_start, pad_units, tail, colt, hx, n_tiles * te)
    experts = jnp.arange(n_exp, dtype=I32)
    of_tile = tile_expert[:, None] == experts[None, :]
    pick = lambda per_expert: jnp.sum(jnp.where(of_tile, per_expert[None, :], 0), axis=1).astype(I32)
    tile_rows = jnp.clip(pick(base + tot) - tile_ids * te, 0, te).astype(I32)
    later = (padded[None, :] > 0) & (experts[None, :] > experts[:, None])
    next_of = jnp.min(jnp.where(later, experts[None, :], n_exp), axis=1)
    next_expert = pick(jnp.where(next_of == n_exp, -1, next_of))
    ys = _experts(tile_expert, tile_rows, next_expert, n_used, xs, layer, w_gu, b_gu, w_down, b_down)
    return _combine(tab, col, ys, gates, x1, mod, ln_g, ln_b, dims, alpha)


def _ml_in_kernel(xp_ref, x_ref, xn_ref, mod_ref, wqk_ref, bqk_ref, wvo_ref, bvo_ref, wg_ref, bg_ref,
                  cw_ref, cb_ref, q_ref, k_ref, v_ref, og_ref, g_ref, *,
                  n_lat_tiles, tiles_per_batch, ctx_parts_per_seq, ctx_row, d):
    i = pl.program_id(0)
    tm = x_ref.shape[0]
    part = ML_IN_PART
    n_parts = tm // part
    halo = SUBLANES
    row = _mod_row(i, n_lat_tiles, tiles_per_batch, ctx_row)
    shift = mod_ref[pl.ds(row, 1), 0:d]
    scale = mod_ref[pl.ds(row, 1), d:2 * d]
    is_lat = i < n_lat_tiles
    nv = wvo_ref.shape[1] // 2
    n_ext = part + 2 * halo
    cw = cw_ref[...]

    def modulated(rows_ref, rows=slice(None)):
        return rows_ref[rows, :] * (1.0 + scale) + shift

    def part_stages(p):
        rows = slice(p * part, (p + 1) * part)
        seq_part = jnp.where(is_lat, (i % tiles_per_batch) * n_parts + p, p % ctx_parts_per_seq)
        seq_parts = jnp.where(is_lat, tiles_per_batch * n_parts, ctx_parts_per_seq)
        first = seq_part == 0
        last = seq_part == seq_parts - 1
        before = modulated(xp_ref) if p == 0 else modulated(x_ref, slice(p * part - halo, p * part))
        after = (modulated(xn_ref) if p == n_parts - 1
                 else modulated(x_ref, slice((p + 1) * part, (p + 1) * part + halo)))
        h = modulated(x_ref, rows)
        h_ext = jnp.concatenate([before, h, after], axis=0)
        z = jnp.dot(h_ext.astype(BF16), wqk_ref[...], preferred_element_type=F32) + bqk_ref[...]
        h_bf = h.astype(BF16)
        v_pre = jnp.dot(h_bf, wvo_ref[:, :nv], preferred_element_type=F32) + bvo_ref[:, :nv]
        o_pre = jnp.dot(h_bf, wvo_ref[:, nv:], preferred_element_type=F32) + bvo_ref[:, nv:]
        zg = _dot_split(h, wg_ref) + bg_ref[...]
        yield
        r = lax.broadcasted_iota(I32, z.shape, 0)
        z = jnp.where(((r < halo) & first) | ((r >= halo + part) & last), 0.0, z)
        acc = None
        for j in range(ML_CONV_W):
            sh = (ML_CONV_W // 2 - j) % n_ext
            zj = z if sh == 0 else pltpu.roll(z, sh, 0)
            term = zj[halo:halo + part] * cw[j:j + 1]
            acc = term if acc is None else acc + term
        qk = acc + cb_ref[...]
        qk = qk * _sigmoid(qk)
        nqk = qk.shape[1] // 2
        qk_dim = nqk // ML_HEADS
        q_ref[rows, :] = (qk[:, :nqk] * (qk_dim ** -0.5)).astype(BF16)
        k_ref[rows, :] = qk[:, nqk:].astype(BF16)
        yield
        v_ref[rows, :] = v_pre.astype(BF16)
        og_ref[rows, :] = _sigmoid(o_pre).astype(BF16)
        g = GATE_CAP * jnp.tanh(zg / GATE_CAP)
        log_sig = jnp.minimum(g, 0.0) - jnp.log(1.0 + jnp.exp(-jnp.abs(g)))
        lane = lax.broadcasted_iota(I32, g.shape, 1)
        is_forget = ((lane // ML_HEADS) % 2) == 1
        g_ref[rows, :] = jnp.where(is_forget, log_sig, g)

    running = [part_stages(p) for p in range(n_parts)]
    while running:
        running = [s for s in running if next(s, True) is None]


def _ml_in(x_all, mod, w_in, b_in, conv_w, conv_b, dims):
    B, S, C, D = dims
    t_all = x_all.shape[0]
    tm = 512
    part = ML_IN_PART
    assert C % part == 0 and S % tm == 0 and (B * C) % tm == 0
    n_lat_tiles = B * S // tm
    nqk2 = conv_w.shape[1]
    nv = (w_in.shape[1] - nqk2 - 4 * ML_HEADS) // 2
    ng = 4 * ML_HEADS
    w_qk = w_in[:, :nqk2].astype(BF16)
    w_vo = w_in[:, nqk2:nqk2 + 2 * nv].astype(BF16)
    w_g = _split_weight(jnp.zeros((D, LANES), F32).at[:, :ng].set(w_in[:, nqk2 + 2 * nv:]))
    b_qk = b_in[:nqk2].reshape(1, -1)
    b_vo = b_in[nqk2:nqk2 + 2 * nv].reshape(1, -1)
    b_g = jnp.zeros((1, LANES), F32).at[0, :ng].set(b_in[nqk2 + 2 * nv:])
    cw = jnp.zeros((SUBLANES, nqk2), F32).at[:ML_CONV_W].set(conv_w)
    cb = conv_b.reshape(1, -1)
    hb = tm // SUBLANES
    n_hblk = t_all // SUBLANES
    full = lambda a: pl.BlockSpec(a.shape, lambda i: (0,) * a.ndim)
    row_spec = lambda w: pl.BlockSpec((tm, w), lambda i: (i, 0))
    kern = functools.partial(_ml_in_kernel, n_lat_tiles=n_lat_tiles, tiles_per_batch=S // tm,
                             ctx_parts_per_seq=C // part, ctx_row=B, d=D)
    return pl.pallas_call(
        kern,
        grid=(t_all // tm,),
        in_specs=[pl.BlockSpec((SUBLANES, D), lambda i: (jnp.maximum(i * hb - 1, 0), 0)),
                  row_spec(D),
                  pl.BlockSpec((SUBLANES, D), lambda i: (jnp.minimum((i + 1) * hb, n_hblk - 1), 0)),
                  full(mod), full(w_qk), full(b_qk), full(w_vo), full(b_vo), full(w_g), full(b_g),
                  full(cw), full(cb)],
        out_specs=[row_spec(nqk2 // 2), row_spec(nqk2 // 2), row_spec(nv), row_spec(nv), row_spec(LANES)],
        out_shape=[jax.ShapeDtypeStruct((t_all, nqk2 // 2), BF16), jax.ShapeDtypeStruct((t_all, nqk2 // 2), BF16),
                   jax.ShapeDtypeStruct((t_all, nv), BF16), jax.ShapeDtypeStruct((t_all, nv), BF16),
                   jax.ShapeDtypeStruct((t_all, LANES), F32)],
        compiler_params=_params(1),
        name="mlstm_in",
    )(x_all, x_all, x_all, mod, w_qk, b_qk, w_vo, b_vo, w_g, b_g, cw, cb)


def _ml_chunk(q_ref, k_ref, v_ref, g_ref, h_ref, s_sc, m_sc, reverse):
    L = q_ref.shape[0]
    qk_dim = q_ref.shape[1] // ML_HEADS
    v_dim = v_ref.shape[1] // ML_HEADS
    nh = ML_HEADS
    gates_t = g_ref[...].T
    sr = lax.broadcasted_iota(I32, (L, L), 0)
    lc = lax.broadcasted_iota(I32, (L, L), 1)
    upto = (sr >= lc) if reverse else (sr <= lc)
    cum_t = jnp.dot(gates_t, jnp.where(upto, 1.0, 0.0), precision=HIGHEST, preferred_element_type=F32)
    off = 2 * nh if reverse else 0
    li = gates_t[off:off + nh]
    b = cum_t[off + nh:off + 2 * nh]
    c = li - b
    lane = lax.broadcasted_iota(I32, c.shape, 1)
    mu = c
    d = 1
    while d < L:
        if reverse:
            shifted = jnp.where(lane < L - d, pltpu.roll(mu, L - d, 1), -jnp.inf)
        else:
            shifted = jnp.where(lane >= d, pltpu.roll(mu, d, 1), -jnp.inf)
        mu = jnp.maximum(mu, shifted)
        d *= 2
    m_prev = m_sc[...]
    mu = jnp.maximum(mu, m_prev)
    m_t = b + mu
    end = 0 if reverse else L - 1
    mu_end = mu[:, end:end + 1]
    decay = jnp.exp(m_prev[:, 0:1] - mu_end)
    wk = jnp.exp(c - mu_end)
    m_sc[...] = jnp.broadcast_to(b[:, end:end + 1] + mu_end, m_prev.shape)
    cols = jnp.concatenate([mu, m_t, jnp.zeros((L - 2 * nh, L), F32)], axis=0).T
    k_t = k_ref[...].astype(F32).T
    rl = lax.broadcasted_iota(I32, (L, L), 0)
    cs = lax.broadcasted_iota(I32, (L, L), 1)
    allowed = (cs >= rl) if reverse else (cs <= rl)
    ones = jnp.ones((L, v_dim), BF16)

    def head(h):
        qh = q_ref[:, h * qk_dim:(h + 1) * qk_dim]
        kt_h = k_t[h * qk_dim:(h + 1) * qk_dim, :]
        qk = jnp.dot(qh, kt_h.astype(BF16), preferred_element_type=F32)
        state = s_sc[h]
        q_state = jnp.dot(qh, state.astype(BF16), preferred_element_type=F32)
        yield
        mu_col = jnp.broadcast_to(cols[:, h:h + 1], (L, L))
        p = jnp.exp(jnp.where(allowed, c[h:h + 1, :] - mu_col, -jnp.inf))
        w_inter = jnp.exp(m_prev[h:h + 1, :] - mu_col)
        s = (qk * p).astype(BF16)
        yield
        v_aug = jnp.concatenate([v_ref[:, h * v_dim:(h + 1) * v_dim], ones], axis=1)
        nd = jnp.dot(s, v_aug, preferred_element_type=F32)
        kw_t = (kt_h * wk[h:h + 1, :]).astype(BF16)
        new_state = decay[h:h + 1, :] * state + jnp.dot(kw_t, v_aug, preferred_element_type=F32)
        yield
        mt_col = jnp.broadcast_to(cols[:, nh + h:nh + h + 1], (L, v_dim))
        nd = nd + jnp.concatenate([w_inter, w_inter], axis=1) * q_state
        num = nd[:, :v_dim]
        den = nd[:, v_dim:]
        h_ref[:, h * v_dim:(h + 1) * v_dim] = num / jnp.maximum(jnp.abs(den), jnp.exp(-mt_col))
        s_sc[h] = new_state

    return head


def _ml_scan_kernel(*refs, n_batch):
    n_in = 8 * n_batch
    hf_ref, hb_ref, s_sc, m_sc = refs[n_in:]

    @pl.when(pl.program_id(0) == 0)
    def _():
        s_sc[...] = jnp.zeros_like(s_sc)
        m_sc[...] = jnp.zeros_like(m_sc)

    heads = []
    for b in range(n_batch):
        qf, kf, vf, gf, qb, kb, vb, gb = refs[8 * b:8 * b + 8]
        heads.append(_ml_chunk(qf, kf, vf, gf, hf_ref.at[b], s_sc.at[b, 0], m_sc.at[b, 0], False))
        heads.append(_ml_chunk(qb, kb, vb, gb, hb_ref.at[b], s_sc.at[b, 1], m_sc.at[b, 1], True))
    for h in range(ML_HEADS):
        running = [head(h) for head in heads]
        while running:
            running = [g for g in running if next(g, True) is None]


def _ml_scan(q, k, v, g, dims):
    B, S, C, D = dims
    L = ML_CHUNK
    assert L == LANES
    nc_ctx = C // L
    nc_lat = S // L
    qk_dim = q.shape[1] // ML_HEADS
    v_dim = v.shape[1] // ML_HEADS

    def in_index(b, reverse):
        def idx(c):
            in_ctx = c < nc_ctx
            cl = c - nc_ctx
            if reverse:
                ctx_blk = (B * S + b * C) // L + (nc_ctx - 1 - c)
                lat_blk = (b * S) // L + (nc_lat - 1 - cl)
            else:
                ctx_blk = (B * S + b * C) // L + c
                lat_blk = (b * S) // L + cl
            return (jnp.where(in_ctx, ctx_blk, lat_blk), 0)
        return idx

    def out_index(reverse):
        def idx(c):
            cl = jnp.maximum(c - nc_ctx, 0)
            return (0, nc_lat - 1 - cl if reverse else cl, 0)
        return idx

    widths = (q.shape[1], k.shape[1], v.shape[1], LANES)
    in_specs, args = [], []
    for b in range(B):
        for rev in (False, True):
            in_specs += [pl.BlockSpec((L, w), in_index(b, rev)) for w in widths]
            args += [q, k, v, g]
    out = jax.ShapeDtypeStruct((B, S, v.shape[1]), F32)
    return pl.pallas_call(
        functools.partial(_ml_scan_kernel, n_batch=B),
        grid=(nc_ctx + nc_lat,),
        in_specs=in_specs,
        out_specs=[pl.BlockSpec((B, L, v.shape[1]), out_index(False)),
                   pl.BlockSpec((B, L, v.shape[1]), out_index(True))],
        out_shape=[out, out],
        scratch_shapes=[pltpu.VMEM((B, 2, ML_HEADS, qk_dim, 2 * v_dim), F32),
                        pltpu.VMEM((B, 2, ML_HEADS, LANES), F32)],
        compiler_params=_params(1),
        name="mlstm_scan",
    )(*args)


def kernel(x, c, ctx, c_ctx, ada_w, ada_b, ln_g, ln_b, attn_w_qkv, attn_b_qkv, attn_sink, attn_w_o, attn_b_o,
           ml_w_in, ml_b_in, ml_conv_w, ml_conv_b, ml_norm_g, ml_w_out, router_w, router_b,
           exp_w_gu, exp_b_gu, exp_w_down, exp_b_down):
    B, S, D = x.shape
    C = ctx.shape[1]
    depth = ada_w.shape[0]
    dims = (B, S, C, D)
    alpha = (2.0 * depth) ** 0.25
    n_lat = B * S

    cvec = jnp.zeros((SUBLANES, D), F32).at[:B].set(c).at[B].set(c_ctx)
    mods = _adaln(cvec, ada_w, ada_b)
    x_lat, x_ctx = x.reshape(n_lat, D), ctx.reshape(B * C, D)

    q, k, v = _attn_qkv(x_lat, x_ctx, mods[0], attn_w_qkv[0], attn_b_qkv[0], dims)
    o = _attention(q, k, v, attn_sink[0], dims)
    x1, hx, topi, gates, sizes = _post_mixer((o,), None, (x_lat, x_ctx), mods[0], attn_w_o[0], attn_b_o[0],
                                      ln_g[0, 0], ln_b[0, 0], router_w[0], router_b[0], dims, n_lat + B * C, alpha)
    x_all = _moe(hx, topi, gates, sizes, x1, mods[0], ln_g[0, 1], ln_b[0, 1],
                 0, exp_w_gu, exp_b_gu, exp_w_down, exp_b_down, dims, alpha)

    q, k, v, og, g = _ml_in(x_all, mods[1], ml_w_in[0], ml_b_in[0], ml_conv_w[0], ml_conv_b[0], dims)
    hf, hb = (h.reshape(n_lat, -1) for h in _ml_scan(q, k, v, g, dims))
    zero_b = jnp.zeros((D,), F32)
    x1, hx, topi, gates, sizes = _post_mixer((hf, hb, og), ml_norm_g[0], x_all, mods[1], ml_w_out[0], zero_b,
                                      ln_g[1, 0], ln_b[1, 0], router_w[1], router_b[1], dims, n_lat, alpha)
    out = _moe(hx, topi, gates, sizes, x1, mods[1], ln_g[1, 1], ln_b[1, 1],
               1, exp_w_gu, exp_b_gu, exp_w_down, exp_b_down, dims, alpha)
    return out.reshape(B, S, D)
```

```python
import functools

import jax
import jax.numpy as jnp
from jax import lax
from jax.experimental import pallas as pl
from jax.experimental.pallas import tpu as pltpu

F32 = jnp.float32
BF16 = jnp.bfloat16
I32 = jnp.int32
HIGHEST = lax.Precision.HIGHEST

GRID_W = 64
ATTN_HEAD_DIM = 64
ATTN_KV_HEADS = 4
WINDOW = 128
ATTN_BLOCK = 128
ROPE_THETA = 10000.0
ML_HEADS = 8
ML_CONV_W = 5
GATE_CAP = 15.0
TOP_K = 4
SWIGLU_ALPHA = 1.702
SWIGLU_LIMIT = 7.0
LN_EPS = 1e-5

LANES = 128
SUBLANES = 8
VMEM_LIMIT = 56 * 1024 * 1024
EXPERT_TILE = 1024
ROUTE_TILE = 512
POST_PART = 256
ML_CHUNK = 128
GROUP_ALIGN = 16
BIG_UNIT = 2 * GROUP_ALIGN
STAGE_ROWS = ROUTE_TILE * TOP_K + 4 * LANES
TABLE_W = 4 * LANES
ZERO_ROWS = 512
ML_IN_PART = 256


def _params(n_axes, vmem=VMEM_LIMIT):
    return pltpu.CompilerParams(dimension_semantics=("arbitrary",) * n_axes, vmem_limit_bytes=vmem)


def _layer_norm(r, g, b):
    mu = jnp.mean(r, axis=-1, keepdims=True)
    rc = r - mu
    var = jnp.mean(rc * rc, axis=-1, keepdims=True)
    return rc * lax.rsqrt(var + LN_EPS) * g + b


def _sigmoid(x):
    return 1.0 / (1.0 + jnp.exp(-x))


def _mod_row(i, n_lat_tiles, tiles_per_batch, ctx_row):
    return jnp.where(i < n_lat_tiles, i // tiles_per_batch, ctx_row)


def _adaln_kernel(c_ref, w_ref, b_ref, o_ref):
    c = c_ref[...]
    s = c * _sigmoid(c)
    o_ref[0] = jnp.dot(s, w_ref[0], precision=HIGHEST, preferred_element_type=F32) + b_ref[0]


def _adaln(cvec, ada_w, ada_b):
    depth, d, n = ada_w.shape
    tn = 1536
    return pl.pallas_call(
        _adaln_kernel,
        grid=(depth, n // tn),
        in_specs=[pl.BlockSpec((SUBLANES, d), lambda l, j: (0, 0)),
                  pl.BlockSpec((1, d, tn), lambda l, j: (l, 0, j)),
                  pl.BlockSpec((1, 1, tn), lambda l, j: (l, 0, j))],
        out_specs=pl.BlockSpec((1, SUBLANES, tn), lambda l, j: (l, 0, j)),
        out_shape=jax.ShapeDtypeStruct((depth, SUBLANES, n), F32),
        compiler_params=_params(2),
        name="adaln",
    )(cvec, ada_w, ada_b.reshape(depth, 1, n))


def _token_specs(tm, d, n_lat_tiles):
    return [pl.BlockSpec((tm, d), lambda i, *_: (jnp.minimum(i, n_lat_tiles - 1), 0)),
            pl.BlockSpec((tm, d), lambda i, *_: (jnp.maximum(i - n_lat_tiles, 0), 0))]


def _qkv_kernel(x_ref, c_ref, mod_ref, w_ref, b_ref, cos_ref, sin_ref, q_ref, k_ref, vt_ref, *,
                n_lat_tiles, tiles_per_batch, ctx_row, d, qd, kvd):
    i = pl.program_id(0)
    row = _mod_row(i, n_lat_tiles, tiles_per_batch, ctx_row)
    shift = mod_ref[pl.ds(row, 1), 0:d]
    scale = mod_ref[pl.ds(row, 1), d:2 * d]
    x = jnp.where(i < n_lat_tiles, x_ref[...], c_ref[...])
    h = x * (1.0 + scale) + shift
    z = jnp.dot(h.astype(BF16), w_ref[...], preferred_element_type=F32) + b_ref[...]
    nrot = qd + kvd
    qk = z[:, :nrot]
    reps = nrot // LANES
    cos = jnp.concatenate([cos_ref[...]] * reps, axis=1)
    sin = jnp.concatenate([sin_ref[...]] * reps, axis=1)
    lane = lax.broadcasted_iota(I32, qk.shape, 1)
    low_half = (lane & 16) == 0
    partner = jnp.where(low_half, pltpu.roll(qk, nrot - 16, 1), pltpu.roll(qk, 16, 1))
    qk = qk * cos + partner * sin
    q_ref[...] = (qk[:, :qd] * (ATTN_HEAD_DIM ** -0.5)).astype(BF16)
    k_ref[...] = qk[:, qd:].astype(BF16)
    vt_ref[...] = z[:, nrot:].T.astype(BF16)


def _rope_tables(s_len, tm):
    half = ATTN_HEAD_DIM // 4
    freqs = ROPE_THETA ** (-jnp.arange(half, dtype=F32) / half)
    t = jnp.arange(s_len)
    rows = (t // GRID_W).astype(F32)[:, None] * freqs[None, :]
    cols = (t % GRID_W).astype(F32)[:, None] * freqs[None, :]
    ang = jnp.concatenate([rows, rows, cols, cols], axis=1)
    sign = jnp.tile(jnp.concatenate([-jnp.ones((half,), F32), jnp.ones((half,), F32)]), 2)
    cos = jnp.cos(ang)
    sin = jnp.sin(ang) * sign[None, :]
    reps = LANES // ATTN_HEAD_DIM
    cos = jnp.concatenate([jnp.tile(cos, (1, reps)), jnp.ones((tm, LANES), F32)], axis=0)
    sin = jnp.concatenate([jnp.tile(sin, (1, reps)), jnp.zeros((tm, LANES), F32)], axis=0)
    return cos, sin


def _attn_qkv(x_lat, x_ctx, mod, w_qkv, b_qkv, dims):
    B, S, C, D = dims
    t_all = x_lat.shape[0] + x_ctx.shape[0]
    tm = 512
    n_lat_tiles = B * S // tm
    tiles_per_batch = S // tm
    ncols = w_qkv.shape[1]
    kvd = ATTN_KV_HEADS * ATTN_HEAD_DIM
    qd = ncols - 2 * kvd
    cos, sin = _rope_tables(S, tm)

    def tab_idx(i):
        return (jnp.where(i < n_lat_tiles, i % tiles_per_batch, tiles_per_batch), 0)

    kern = functools.partial(_qkv_kernel, n_lat_tiles=n_lat_tiles, tiles_per_batch=tiles_per_batch,
                             ctx_row=B, d=D, qd=qd, kvd=kvd)
    return pl.pallas_call(
        kern,
        grid=(t_all // tm,),
        in_specs=_token_specs(tm, D, n_lat_tiles) + [
                  pl.BlockSpec(mod.shape, lambda i: (0, 0)),
                  pl.BlockSpec((D, ncols), lambda i: (0, 0)),
                  pl.BlockSpec((1, ncols), lambda i: (0, 0)),
                  pl.BlockSpec((tm, LANES), tab_idx),
                  pl.BlockSpec((tm, LANES), tab_idx)],
        out_specs=[pl.BlockSpec((tm, qd), lambda i: (i, 0)),
                   pl.BlockSpec((tm, kvd), lambda i: (i, 0)),
                   pl.BlockSpec((kvd, tm), lambda i: (0, i))],
        out_shape=[jax.ShapeDtypeStruct((t_all, qd), BF16),
                   jax.ShapeDtypeStruct((t_all, kvd), BF16),
                   jax.ShapeDtypeStruct((kvd, t_all), BF16)],
        compiler_params=_params(1),
        name="attn_qkv",
    )(x_lat, x_ctx, mod, w_qkv.astype(BF16), b_qkv.reshape(1, ncols), cos, sin)


def _attn_kernel(sink_ref, q_ref, kp_ref, ko_ref, kn_ref, kc_ref, vp_ref, vo_ref, vn_ref, vc_ref, o_ref,
                 bias_sc, s_sc, p_sc, ot_sc, *,
                 n_lat_steps, nb, s_len, c_len):
    j = pl.program_id(0)
    is_lat = j < n_lat_steps
    n = j % nb
    blk = ATTN_BLOCK
    nloc = 3 * blk
    nk = nloc + c_len
    ki = lax.broadcasted_iota(I32, (nloc, blk), 0)
    qj = lax.broadcasted_iota(I32, (nloc, blk), 1)
    kpos = n * blk - WINDOW + ki
    qpos = n * blk + qj
    local_ok = (jnp.abs(kpos - qpos) <= WINDOW) & (kpos >= 0) & (kpos < s_len) & is_lat
    bias_sc[...] = jnp.where(local_ok, 0.0, -jnp.inf)
    hd = ATTN_HEAD_DIM
    group = q_ref.shape[1] // (ATTN_KV_HEADS * hd)
    kcat = jnp.concatenate([kp_ref[...], ko_ref[...], kn_ref[...], kc_ref[...]], axis=0)
    vcat_t = jnp.concatenate([vp_ref[...], vo_ref[...], vn_ref[...], vc_ref[...]], axis=1)
    def score_matmul(kh):
        q_grp = jnp.concatenate([q_ref[:, (kh * group + g) * hd:(kh * group + g + 1) * hd]
                                 for g in range(group)], axis=0)
        s_sc[kh] = lax.dot_general(kcat[:, kh * hd:(kh + 1) * hd], q_grp, (((1,), (1,)), ((), ())),
                                   preferred_element_type=F32)

    def weighted_values(kh, sink_terms):
        v_ones = jnp.concatenate([vcat_t[kh * hd:(kh + 1) * hd, :], jnp.ones((SUBLANES, nk), BF16)], axis=0)
        o_aug = jnp.dot(v_ones, p_sc[kh], preferred_element_type=F32)
        l = o_aug[hd:hd + 1, :] + jnp.concatenate(sink_terms, axis=1)
        o_t = o_aug[0:hd, :] * (1.0 / l)
        for g in range(group):
            h = kh * group + g
            ot_sc[h * hd:(h + 1) * hd, :] = o_t[:, g * blk:(g + 1) * blk]

    score_matmul(0)
    for kh in range(ATTN_KV_HEADS):
        if kh + 1 < ATTN_KV_HEADS:
            score_matmul(kh + 1)
        sink_terms = []
        for g in range(group):
            cols = slice(g * blk, (g + 1) * blk)
            def scores(a):
                s = s_sc[kh, a:a + blk, cols]
                return s + bias_sc[a:a + blk, :] if a < nloc else s

            sk = sink_ref[kh * group + g]
            top = scores(0)
            for a in range(blk, nk, blk):
                top = jnp.maximum(top, scores(a))
            m = jnp.maximum(jnp.max(top, axis=0, keepdims=True), sk)
            for a in range(0, nk, blk):
                p_sc[kh, a:a + blk, cols] = jnp.exp((scores(a) - m).astype(BF16))
            sink_terms.append(jnp.exp(sk - m))
        weighted_values(kh, sink_terms)
    o_ref[...] = ot_sc[...].T.astype(BF16)


def _attention(q_all, k_all, vt_all, sink, dims):
    B, S, C, D = dims
    blk = ATTN_BLOCK
    nb = S // blk
    n_lat_steps = B * nb
    ctx_steps_per_batch = C // blk
    n_steps = n_lat_steps + B * ctx_steps_per_batch
    qd = q_all.shape[1]
    kvd = k_all.shape[1]

    def local_idx(off):
        def idx(j):
            b = j // nb
            nn = jnp.clip(j % nb + off, 0, nb - 1)
            return (jnp.where(j < n_lat_steps, b * nb + nn, j), 0)
        return idx

    def ctx_idx(j):
        b = jnp.where(j < n_lat_steps, j // nb, (j - n_lat_steps) // ctx_steps_per_batch)
        return (B * S // C + b, 0)

    swap = lambda f: (lambda j: f(j)[::-1])
    loc = lambda off: pl.BlockSpec((blk, kvd), local_idx(off))
    ctxs = pl.BlockSpec((C, kvd), ctx_idx)
    loc_t = lambda off: pl.BlockSpec((kvd, blk), swap(local_idx(off)))
    ctxs_t = pl.BlockSpec((kvd, C), swap(ctx_idx))
    group = qd // kvd
    nk = 3 * blk + C
    kern = functools.partial(_attn_kernel, n_lat_steps=n_lat_steps, nb=nb, s_len=S, c_len=C)
    return pl.pallas_call(
        kern,
        grid=(n_steps,),
        in_specs=[pl.BlockSpec(memory_space=pltpu.SMEM),
                  pl.BlockSpec((blk, qd), lambda j: (j, 0)),
                  loc(-1), loc(0), loc(1), ctxs,
                  loc_t(-1), loc_t(0), loc_t(1), ctxs_t],
        out_specs=pl.BlockSpec((blk, qd), lambda j: (j, 0)),
        out_shape=jax.ShapeDtypeStruct((q_all.shape[0], qd), BF16),
        scratch_shapes=[pltpu.VMEM((3 * blk, blk), F32),
                        pltpu.VMEM((ATTN_KV_HEADS, nk, group * blk), F32),
                        pltpu.VMEM((ATTN_KV_HEADS, nk, group * blk), BF16),
                        pltpu.VMEM((qd, blk), F32)],
        compiler_params=_params(1),
        name="attn_core",
    )(sink, q_all, k_all, k_all, k_all, k_all, vt_all, vt_all, vt_all, vt_all)


def _split_bf16(a):
    hi = a.astype(BF16)
    return hi, (a - hi.astype(F32)).astype(BF16)


def _split_weight(w_f32):
    hi = w_f32.astype(BF16)
    return jnp.concatenate([hi, (w_f32 - hi.astype(F32)).astype(BF16)], axis=1)


def _dot_split(a, w_ref):
    a_hi, a_lo = _split_bf16(a)
    n = w_ref.shape[1] // 2
    both = jnp.dot(a_hi, w_ref[...], preferred_element_type=F32)
    return both[:, :n] + both[:, n:] + jnp.dot(a_lo, w_ref[:, :n], preferred_element_type=F32)


def _top_k_route(logits, rows, topi_ref, gate_ref, sizes_ref, part):
    lane = lax.broadcasted_iota(I32, logits.shape, 1)
    lanef = lane.astype(F32)
    vals, idxs = [], []
    l = logits
    for _ in range(TOP_K):
        m = jnp.max(l, axis=1, keepdims=True)
        idx = jnp.min(jnp.where(l == m, lanef, float(LANES)), axis=1, keepdims=True)
        vals.append(m)
        idxs.append(idx)
        l = jnp.where(lanef == idx, -jnp.inf, l)
    es = [jnp.exp(v - vals[0]) for v in vals]
    denom = es[0]
    for e in es[1:]:
        denom = denom + e
    topi = jnp.zeros(logits.shape, F32)
    gates = jnp.zeros(logits.shape, F32)
    chosen = jnp.zeros(logits.shape, F32)
    for k in range(TOP_K):
        topi = jnp.where(lane == k, idxs[k], topi)
        gates = jnp.where(lane == k, es[k] / denom, gates)
        chosen = chosen + jnp.where(lanef == idxs[k], 1.0, 0.0)
    topi_ref[rows, :] = topi.astype(I32)
    gate_ref[rows, :] = gates
    sizes_ref[0, part:part + 1, :] = jnp.sum(chosen, axis=0, keepdims=True)


def _post_common(a_fn, x_fn, mod_ref, w_ref, b_ref, lng_ref, lnb_ref, rw_ref, rb_ref,
                 x1_ref, hx_ref, topi_ref, gate_ref, sizes_ref, row, d, alpha):
    gate_mix = mod_ref[pl.ds(row, 1), 2 * d:3 * d]
    shift = mod_ref[pl.ds(row, 1), 3 * d:4 * d]
    scale = mod_ref[pl.ds(row, 1), 4 * d:5 * d]
    sizes_ref[...] = jnp.zeros_like(sizes_ref)

    def part_stages(part):
        rows = slice(part * POST_PART, (part + 1) * POST_PART)
        y = jnp.dot(a_fn(rows), w_ref[...], preferred_element_type=F32) + b_ref[...]
        yield
        x1 = _layer_norm(alpha * x_fn(rows) + gate_mix * y, lng_ref[...], lnb_ref[...])
        hx = x1 * (1.0 + scale) + shift
        x1_ref[rows, :] = x1
        hx_ref[rows, :] = hx.astype(BF16)
        logits = _dot_split(hx, rw_ref) + rb_ref[...]
        yield
        _top_k_route(logits, rows, topi_ref, gate_ref, sizes_ref, part)

    running = [part_stages(p) for p in range(x1_ref.shape[0] // POST_PART)]
    while running:
        running = [g for g in running if next(g, True) is None]


def _post_attn_kernel(o_ref, x_ref, c_ref, mod_ref, w_ref, b_ref, lng_ref, lnb_ref, rw_ref, rb_ref,
                      x1_ref, hx_ref, topi_ref, gate_ref, sizes_ref, *, n_lat_tiles, tiles_per_batch, ctx_row, d, alpha):
    i = pl.program_id(0)
    row = _mod_row(i, n_lat_tiles, tiles_per_batch, ctx_row)
    is_lat = i < n_lat_tiles
    _post_common(lambda rows: o_ref[rows, :],
                 lambda rows: jnp.where(is_lat, x_ref[rows, :], c_ref[rows, :]),
                 mod_ref, w_ref, b_ref, lng_ref, lnb_ref, rw_ref, rb_ref,
                 x1_ref, hx_ref, topi_ref, gate_ref, sizes_ref, row, d, alpha)


def _post_mlstm_kernel(hf_ref, hb_ref, og_ref, ng_ref, x_ref, mod_ref, w_ref, b_ref, lng_ref, lnb_ref,
                       rw_ref, rb_ref, x1_ref, hx_ref, topi_ref, gate_ref, sizes_ref, *,
                       n_lat_tiles, tiles_per_batch, ctx_row, d, alpha):
    row = _mod_row(pl.program_id(0), n_lat_tiles, tiles_per_batch, ctx_row)

    def gated_head_norm(rows):
        hsum = hf_ref[rows, :] + hb_ref[rows, :]
        vdim = hsum.shape[1] // ML_HEADS
        parts = []
        for h in range(ML_HEADS):
            seg = hsum[:, h * vdim:(h + 1) * vdim]
            mu = jnp.mean(seg, axis=1, keepdims=True)
            sc = seg - mu
            var = jnp.mean(sc * sc, axis=1, keepdims=True)
            parts.append(sc * lax.rsqrt(var + LN_EPS))
        y = jnp.concatenate(parts, axis=1) * ng_ref[...]
        return (og_ref[rows, :].astype(F32) * y).astype(BF16)

    _post_common(gated_head_norm, lambda rows: x_ref[rows, :],
                 mod_ref, w_ref, b_ref, lng_ref, lnb_ref, rw_ref, rb_ref,
                 x1_ref, hx_ref, topi_ref, gate_ref, sizes_ref, row, d, alpha)


def _post_mixer(mixer_inputs, norm_g, x_all, mod, w_o, b_o, ln_g, ln_b, router_w, router_b, dims, n_rows, alpha):
    B, S, C, D = dims
    tm = ROUTE_TILE
    n_lat_tiles = B * S // tm
    tiles_per_batch = S // tm
    n_exp = router_w.shape[1]
    rw = _split_weight(jnp.zeros((D, LANES), F32).at[:, :n_exp].set(router_w))
    rb = jnp.full((1, LANES), -1e30, F32).at[0, :n_exp].set(router_b)
    row_spec = lambda w: pl.BlockSpec((tm, w), lambda i: (i, 0))
    full = lambda a: pl.BlockSpec(a.shape, lambda i: (0,) * a.ndim)
    common = dict(n_lat_tiles=n_lat_tiles, tiles_per_batch=tiles_per_batch, ctx_row=B, d=D, alpha=alpha)
    w_bf = w_o.astype(BF16)
    b2 = b_o.reshape(1, D)
    if isinstance(x_all, tuple):
        x_args, x_specs = list(x_all), _token_specs(tm, D, n_lat_tiles)
    else:
        x_args, x_specs = [x_all], [row_spec(D)]
    tail = x_args + [mod, w_bf, b2, ln_g.reshape(1, D), ln_b.reshape(1, D), rw, rb]
    tail_specs = x_specs + [full(mod), full(w_bf), full(b2), pl.BlockSpec((1, D), lambda i: (0, 0)),
                            pl.BlockSpec((1, D), lambda i: (0, 0)), full(rw), full(rb)]
    if norm_g is None:
        kern = functools.partial(_post_attn_kernel, **common)
        args = list(mixer_inputs) + tail
        specs = [row_spec(mixer_inputs[0].shape[1])] + tail_specs
        name = "post_attn"
    else:
        kern = functools.partial(_post_mlstm_kernel, **common)
        ng = norm_g.reshape(1, -1)
        args = list(mixer_inputs) + [ng] + tail
        specs = [row_spec(a.shape[1]) for a in mixer_inputs] + [full(ng)] + tail_specs
        name = "post_mlstm"
    return pl.pallas_call(
        kern,
        grid=(n_rows // tm,),
        in_specs=specs,
        out_specs=[row_spec(D), row_spec(D), row_spec(LANES), row_spec(LANES),
                   pl.BlockSpec((1, SUBLANES, LANES), lambda i: (i, 0, 0))],
        out_shape=[jax.ShapeDtypeStruct((n_rows, D), F32), jax.ShapeDtypeStruct((n_rows, D), BF16),
                   jax.ShapeDtypeStruct((n_rows, LANES), I32), jax.ShapeDtypeStruct((n_rows, LANES), F32),
                   jax.ShapeDtypeStruct((n_rows // tm, SUBLANES, LANES), F32)],
        compiler_params=_params(1),
        name=name,
    )(*args)


def _exclusive_lane_cumsum(row):
    r = lax.broadcasted_iota(I32, (LANES, LANES), 0)
    c = lax.broadcasted_iota(I32, (LANES, LANES), 1)
    before = jnp.where(r < c, 1.0, 0.0)
    return jnp.dot(jnp.broadcast_to(row, (SUBLANES, LANES)), before,
                   precision=HIGHEST, preferred_element_type=F32)[0:1]


def _for_each_unit(tab_ref, tile, fn):
    base = tile * TABLE_W
    for blk, rows in ((0, BIG_UNIT), (2, GROUP_ALIGN)):
        def body(j, carry, blk=blk, rows=rows):
            slot_row = pl.multiple_of(tab_ref[base + blk * LANES + j], GROUP_ALIGN)
            stage_row = pl.multiple_of(tab_ref[base + (blk + 1) * LANES + j], GROUP_ALIGN)
            fn(slot_row, stage_row, rows)
            return carry
        lax.fori_loop(0, tab_ref[base + blk * LANES + LANES - 1], body, 0)


def _slots_kernel(topi_ref, sizes_ref, col_ref, colt_ref, tab_ref, meta_ref, base_sc, carry_sc, *, tile):
    i = pl.program_id(0)
    tm = topi_ref.shape[0]
    topi = topi_ref[...]
    lane = lax.broadcasted_iota(I32, topi.shape, 1)
    sel = [lane == topi[:, k:k + 1] for k in range(TOP_K)]
    maskf = jnp.where(sel[0], 1.0, 0.0)
    for s in sel[1:]:
        maskf = maskf + jnp.where(s, 1.0, 0.0)
    n8 = jnp.ceil(jnp.sum(maskf, axis=0, keepdims=True) / GROUP_ALIGN) * GROUP_ALIGN

    @pl.when(i == 0)
    def _():
        groups = jnp.ceil(jnp.sum(sizes_ref[...], axis=1) / GROUP_ALIGN) * GROUP_ALIGN
        tot = jnp.sum(groups, axis=0, keepdims=True)
        padded = jnp.ceil(tot / tile) * tile
        base = _exclusive_lane_cumsum(padded)
        base_sc[...] = base
        carry_sc[...] = jnp.zeros_like(carry_sc)
        rowi = lax.broadcasted_iota(I32, meta_ref.shape, 0)
        meta_ref[...] = jnp.where(rowi == 0, tot, jnp.where(rowi == 1, base, padded)).astype(I32)

    start = base_sc[...] + carry_sc[...]
    off = _exclusive_lane_cumsum(n8)
    r = lax.broadcasted_iota(I32, (tm, tm), 0)
    c = lax.broadcasted_iota(I32, (tm, tm), 1)
    earlier = jnp.where(c < r, 1.0, 0.0).astype(BF16)
    rank = jnp.dot(earlier, maskf.astype(BF16), preferred_element_type=F32)
    stage_row = rank + off
    out = jnp.zeros(topi.shape, F32)
    for k in range(TOP_K):
        pk = jnp.sum(jnp.where(sel[k], stage_row, 0.0), axis=1, keepdims=True)
        out = jnp.where(lane == k, pk, out)
    col_ref[...] = out.astype(I32)
    colt_ref[0] = out.T[0:SUBLANES].astype(I32)

    n_big = jnp.floor(n8 / BIG_UNIT)
    n_small = (n8 - BIG_UNIT * n_big) / GROUP_ALIGN
    first_big = _exclusive_lane_cumsum(n_big)
    first_small = _exclusive_lane_cumsum(n_small)
    rows = jnp.concatenate([off, start, first_big, n_big, first_small, n_small,
                            jnp.zeros((LANES - 6, LANES), F32)], axis=0)
    per_expert = rows.T
    off_c, start_c, fb_c, nb_c, fs_c, ns_c = (per_expert[:, k:k + 1] for k in range(6))
    u = lax.broadcasted_iota(I32, (LANES, LANES), 1).astype(F32)
    lane_t = lax.broadcasted_iota(I32, (1, LANES), 1)

    def unit_list(first_c, count_c, rel, count_row):
        inside = (u >= first_c) & (u < first_c + count_c)
        src = jnp.sum(jnp.where(inside, start_c + rel, 0.0), axis=0, keepdims=True)
        dst = jnp.sum(jnp.where(inside, off_c + rel, 0.0), axis=0, keepdims=True)
        total = jnp.sum(count_row, axis=1, keepdims=True)
        return [jnp.where(lane_t == LANES - 1, total, src), dst]

    lists = (unit_list(fb_c, nb_c, BIG_UNIT * (u - fb_c), n_big)
             + unit_list(fs_c, ns_c, BIG_UNIT * nb_c + GROUP_ALIGN * (u - fs_c), n_small))
    tab_ref[0] = jnp.concatenate(lists, axis=1).astype(I32)
    carry_sc[...] = carry_sc[...] + n8


def _slots(topi, sizes, tile):
    t = topi.shape[0]
    tm = ROUTE_TILE
    nt = t // tm
    return pl.pallas_call(
        functools.partial(_slots_kernel, tile=float(tile)),
        grid=(nt,),
        in_specs=[pl.BlockSpec((tm, LANES), lambda i: (i, 0)),
                  pl.BlockSpec(sizes.shape, lambda i: (0, 0, 0))],
        out_specs=[pl.BlockSpec((tm, LANES), lambda i: (i, 0)),
                   pl.BlockSpec((1, SUBLANES, tm), lambda i: (i, 0, 0)),
                   pl.BlockSpec((1, 1, TABLE_W), lambda i: (i, 0, 0)),
                   pl.BlockSpec((SUBLANES, LANES), lambda i: (0, 0))],
        out_shape=[jax.ShapeDtypeStruct((t, LANES), I32),
                   jax.ShapeDtypeStruct((nt, SUBLANES, tm), I32),
                   jax.ShapeDtypeStruct((nt, 1, TABLE_W), I32),
                   jax.ShapeDtypeStruct((SUBLANES, LANES), I32)],
        scratch_shapes=[pltpu.VMEM((1, LANES), F32), pltpu.VMEM((1, LANES), F32)],
        compiler_params=_params(1),
        name="moe_slots",
    )(topi, sizes)


def _dispatch_kernel(tab_ref, pad_start_ref, pad_units_ref, tail_ref, colt_ref, hx_ref, xs_ref, stage, zeros, sem, zsem):
    i = pl.program_id(0)
    nt = pl.num_programs(0)
    slot = i % 2
    kb = stage.shape[1]
    tm = hx_ref.shape[0]

    def scatter(buf_slot):
        def copy(slot_row, stage_row, rows):
            return pltpu.make_async_copy(stage.at[buf_slot, pl.ds(stage_row, rows)],
                                         xs_ref.at[pl.ds(slot_row, rows)], sem.at[buf_slot])
        return copy

    def drain(tile, buf_slot):
        _for_each_unit(tab_ref, tile, lambda *u: scatter(buf_slot)(*u).wait())

    @pl.when(i >= 2)
    def _():
        drain(i - 2, slot)

    colt = colt_ref[0]
    c = lax.broadcasted_iota(I32, (kb, tm), 0)
    onehot = jnp.zeros((kb, tm), F32)
    for k in range(TOP_K):
        onehot = jnp.where(c == colt[k:k + 1, :], 1.0, onehot)
    stage[slot] = jnp.dot(onehot.astype(BF16), hx_ref[...], preferred_element_type=F32).astype(BF16)

    _for_each_unit(tab_ref, i, lambda *u: scatter(slot)(*u).start())

    @pl.when(i == nt - 1)
    def _():
        zeros[...] = jnp.zeros_like(zeros)
        n_exp = pad_start_ref.shape[0]

        def zero_copy(e, u):
            dst = pl.multiple_of(pad_start_ref[e] + u * GROUP_ALIGN, GROUP_ALIGN)
            return pltpu.make_async_copy(zeros.at[pl.ds(0, GROUP_ALIGN)], xs_ref.at[pl.ds(dst, GROUP_ALIGN)], zsem)

        def per_expert(fn):
            def outer(e, carry):
                def inner(u, carry2):
                    fn(e, u)
                    return carry2
                lax.fori_loop(0, pad_units_ref[e], inner, 0)
                return carry
            lax.fori_loop(0, n_exp, outer, 0)

        per_expert(lambda e, u: zero_copy(e, u).start())
        per_expert(lambda e, u: zero_copy(e, u).wait())

        zrows = zeros.shape[0]

        def tail_copy(u):
            dst = pl.multiple_of(tail_ref[0] + u * zrows, zrows)
            return pltpu.make_async_copy(zeros, xs_ref.at[pl.ds(dst, zrows)], zsem)

        def tail_loop(fn):
            def body(u, carry):
                fn(u)
                return carry
            lax.fori_loop(0, tail_ref[1], body, 0)

        tail_loop(lambda u: tail_copy(u).start())
        tail_loop(lambda u: tail_copy(u).wait())

        @pl.when(nt >= 2)
        def _():
            drain(i - 1, 1 - slot)
        drain(i, slot)


def _dispatch(tab, pad_start, pad_units, tail, colt, hx, n_slots):
    t, d = hx.shape
    tm = ROUTE_TILE
    return pl.pallas_call(
        _dispatch_kernel,
        grid_spec=pltpu.PrefetchScalarGridSpec(
            num_scalar_prefetch=4,
            grid=(t // tm,),
            in_specs=[pl.BlockSpec((1, SUBLANES, tm), lambda i, *_: (i, 0, 0)),
                      pl.BlockSpec((tm, d), lambda i, *_: (i, 0))],
            out_specs=pl.BlockSpec(memory_space=pl.ANY),
            scratch_shapes=[pltpu.VMEM((2, STAGE_ROWS, d), BF16), pltpu.VMEM((ZERO_ROWS, d), BF16),
                            pltpu.SemaphoreType.DMA((2,)), pltpu.SemaphoreType.DMA(())]),
        out_shape=jax.ShapeDtypeStruct((n_slots, d), BF16),
        compiler_params=_params(1),
        name="moe_dispatch",
    )(tab, pad_start, pad_units, tail, colt, hx)


def _expert_kernel(te_ref, rows_ref, next_ref, nu_ref, xs_ref, wgu_hbm, bgu_ref, wd_hbm, bd_ref, ys_ref,
                   wgu_f32, wd_f32, wgu_sc, wd_sc, wsem, *, layer):
    j = pl.program_id(0)
    active = j < nu_ref[0]
    changed = (j == 0) | (te_ref[j] != te_ref[jnp.maximum(j - 1, 0)])
    ff = wd_sc.shape[0]
    te = xs_ref.shape[0]

    def weight_copies(e):
        return (pltpu.make_async_copy(wgu_hbm.at[layer, e], wgu_f32, wsem.at[0]),
                pltpu.make_async_copy(wd_hbm.at[layer, e], wd_f32, wsem.at[1]))

    @pl.when(active & changed)
    def _():
        @pl.when(j == 0)
        def _():
            for cp in weight_copies(te_ref[0]):
                cp.start()

        for cp in weight_copies(te_ref[j]):
            cp.wait()
        wgu_sc[...] = wgu_f32[...].astype(BF16)
        wd_sc[...] = wd_f32[...].astype(BF16)

        @pl.when(next_ref[j] >= 0)
        def _():
            for cp in weight_copies(next_ref[j]):
                cp.start()

    def run(rows):
        x = xs_ref[0:rows, :]
        nblk = 2
        fb = ff // nblk

        def up(c):
            gl = jnp.dot(x, wgu_sc[:, c * fb:(c + 1) * fb], preferred_element_type=F32) + bgu_ref[:, c * fb:(c + 1) * fb]
            lin = (jnp.dot(x, wgu_sc[:, ff + c * fb:ff + (c + 1) * fb], preferred_element_type=F32)
                   + bgu_ref[:, ff + c * fb:ff + (c + 1) * fb])
            return gl, lin

        y = bd_ref[...]
        nxt = up(0)
        for c in range(nblk):
            gl, lin = nxt
            if c + 1 < nblk:
                nxt = up(c + 1)
            gl = jnp.minimum(gl, SWIGLU_LIMIT)
            lin = jnp.clip(lin, -SWIGLU_LIMIT, SWIGLU_LIMIT)
            act = gl * _sigmoid(SWIGLU_ALPHA * gl) * (lin + 1.0)
            y = y + jnp.dot(act.astype(BF16), wd_sc[c * fb:(c + 1) * fb, :], preferred_element_type=F32)
        ys_ref[0:rows, :] = y.astype(BF16)
        if rows < te:
            ys_ref[rows:te, :] = jnp.zeros((te - rows, ys_ref.shape[1]), BF16)

    valid_rows = rows_ref[j]
    quarter = te // 4
    n_quarters = jnp.clip((valid_rows + quarter - 1) // quarter, 1, 4)
    for nq in range(1, 5):
        @pl.when(active & (n_quarters == nq))
        def _(nq=nq):
            run(nq * quarter)

    @pl.when(jnp.logical_not(active))
    def _():
        ys_ref[...] = jnp.zeros_like(ys_ref)


def _experts(tile_expert, tile_rows, next_expert, n_used, xs, layer, w_gu, b_gu, w_down, b_down):
    n_slots, d = xs.shape
    depth, n_exp, _, ff2 = w_gu.shape
    ff = w_down.shape[2]
    te = EXPERT_TILE
    row_idx = lambda j, te_r, rows_r, nx_r, nu: (jnp.maximum(jnp.minimum(j, nu[0] - 1), 0), 0)
    b_idx = lambda j, te_r, rows_r, nx_r, nu: (layer, te_r[j], 0, 0)
    return pl.pallas_call(
        functools.partial(_expert_kernel, layer=layer),
        grid_spec=pltpu.PrefetchScalarGridSpec(
            num_scalar_prefetch=4,
            grid=(n_slots // te,),
            in_specs=[pl.BlockSpec((te, d), row_idx),
                      pl.BlockSpec(memory_space=pl.ANY),
                      pl.BlockSpec((None, None, 1, ff2), b_idx),
                      pl.BlockSpec(memory_space=pl.ANY),
                      pl.BlockSpec((None, None, 1, d), b_idx)],
            out_specs=pl.BlockSpec((te, d), lambda j, te_r, rows_r, nx_r, nu: (j, 0)),
            scratch_shapes=[pltpu.VMEM((d, ff2), F32), pltpu.VMEM((ff, d), F32),
                            pltpu.VMEM((d, ff2), BF16), pltpu.VMEM((ff, d), BF16),
                            pltpu.SemaphoreType.DMA((2,))]),
        out_shape=jax.ShapeDtypeStruct((n_slots, d), BF16),
        compiler_params=_params(1),
        name="moe_experts",
    )(tile_expert, tile_rows, next_expert, n_used, xs, w_gu, b_gu.reshape(depth, n_exp, 1, ff2),
      w_down, b_down.reshape(depth, n_exp, 1, d))


def _combine_kernel(tab_ref, col_ref, gate_ref, ys_ref, x_ref, mod_ref, lng_ref, lnb_ref, out_ref, stage, sem, *,
                    n_lat_tiles, tiles_per_batch, ctx_row, d, alpha):
    i = pl.program_id(0)
    nt = pl.num_programs(0)
    slot = i % 2
    kb = stage.shape[1]
    tm = x_ref.shape[0]

    def gather(buf_slot):
        def copy(slot_row, stage_row, rows):
            return pltpu.make_async_copy(ys_ref.at[pl.ds(slot_row, rows)],
                                         stage.at[buf_slot, pl.ds(stage_row, rows)], sem.at[buf_slot])
        return copy

    @pl.when(i == 0)
    def _():
        stage[...] = jnp.zeros_like(stage)
        _for_each_unit(tab_ref, 0, lambda *u: gather(0)(*u).start())

    @pl.when(i + 1 < nt)
    def _():
        _for_each_unit(tab_ref, i + 1, lambda *u: gather(1 - slot)(*u).start())

    _for_each_unit(tab_ref, i, lambda *u: gather(slot)(*u).wait())

    col = col_ref[...]
    gates = gate_ref[...]
    kblk = 2 * LANES

    def gate_weights(r0):
        c = r0 + lax.broadcasted_iota(I32, (tm, kblk), 1)
        w = jnp.zeros((tm, kblk), F32)
        for k in range(TOP_K):
            w = jnp.where(c == col[:, k:k + 1], gates[:, k:k + 1], w)
        return w.astype(BF16)

    y = jnp.zeros((tm, d), F32)
    nxt = gate_weights(0)
    for r0 in range(0, kb, kblk):
        w = nxt
        if r0 + kblk < kb:
            nxt = gate_weights(r0 + kblk)
        y = y + jnp.dot(w, stage[slot, r0:r0 + kblk, :], preferred_element_type=F32)
    row = _mod_row(i, n_lat_tiles, tiles_per_batch, ctx_row)
    gate_mlp = mod_ref[pl.ds(row, 1), 5 * d:6 * d]
    out_ref[...] = _layer_norm(alpha * x_ref[...] + gate_mlp * y, lng_ref[...], lnb_ref[...])


def _combine(tab, col, ys, gates, x1, mod, ln_g, ln_b, dims, alpha):
    B, S, C, D = dims
    t = x1.shape[0]
    tm = ROUTE_TILE
    kern = functools.partial(_combine_kernel, n_lat_tiles=B * S // tm, tiles_per_batch=S // tm, ctx_row=B,
                             d=D, alpha=alpha)
    return pl.pallas_call(
        kern,
        grid_spec=pltpu.PrefetchScalarGridSpec(
            num_scalar_prefetch=1,
            grid=(t // tm,),
            in_specs=[pl.BlockSpec((tm, LANES), lambda i, tab_r: (i, 0)),
                      pl.BlockSpec((tm, LANES), lambda i, tab_r: (i, 0)),
                      pl.BlockSpec(memory_space=pl.ANY),
                      pl.BlockSpec((tm, D), lambda i, tab_r: (i, 0)),
                      pl.BlockSpec(mod.shape, lambda i, tab_r: (0, 0)),
                      pl.BlockSpec((1, D), lambda i, tab_r: (0, 0)),
                      pl.BlockSpec((1, D), lambda i, tab_r: (0, 0))],
            out_specs=pl.BlockSpec((tm, D), lambda i, tab_r: (i, 0)),
            scratch_shapes=[pltpu.VMEM((2, STAGE_ROWS, D), BF16), pltpu.SemaphoreType.DMA((2,))]),
        out_shape=jax.ShapeDtypeStruct((t, D), F32),
        compiler_params=_params(1),
        name="moe_combine",
    )(tab, col, gates, ys, x1, mod, ln_g.reshape(1, D), ln_b.reshape(1, D))


def _moe(hx, topi, gates, sizes, x1, mod, ln_g, ln_b, layer, w_gu, b_gu, w_down, b_down, dims, alpha):
    t = hx.shape[0]
    n_exp = w_gu.shape[1]
    te = EXPERT_TILE
    n_route_tiles = t // ROUTE_TILE
    max_rows = t * TOP_K + (GROUP_ALIGN - 1) * n_exp * n_route_tiles
    n_tiles = -(-max_rows // te) + n_exp
    col, colt, tab, meta = _slots(topi, sizes, te)
    tab = tab.reshape(-1)
    tot, base, padded = meta[0, :n_exp], meta[1, :n_exp], meta[2, :n_exp]
    ends = jnp.cumsum(padded // te)
    n_used = ends[-1:].astype(I32)
    tile_ids = jnp.minimum(jnp.arange(n_tiles, dtype=I32), n_used[0] - 1)
    tile_expert = jnp.sum((tile_ids[:, None] >= ends[None, :]).astype(I32), axis=1)
    tile_expert = jnp.minimum(tile_expert, n_exp - 1).astype(I32)
    pad_start = (base + tot).astype(I32)
    pad_units = ((padded - tot) // GROUP_ALIGN).astype(I32)
    used_rows = n_used[0] * te
    tail = jnp.stack([used_rows, (n_tiles * te - used_rows) // ZERO_ROWS]).astype(I32)
    xs = _dispatch(tab, pad_start, pad_units, tail, colt, hx, n_tiles * te)
    experts = jnp.arange(n_exp, dtype=I32)
    of_tile = tile_expert[:, None] == experts[None, :]
    pick = lambda per_expert: jnp.sum(jnp.where(of_tile, per_expert[None, :], 0), axis=1).astype(I32)
    tile_rows = jnp.clip(pick(base + tot) - tile_ids * te, 0, te).astype(I32)
    later = (padded[None, :] > 0) & (experts[None, :] > experts[:, None])
    next_of = jnp.min(jnp.where(later, experts[None, :], n_exp), axis=1)
    next_expert = pick(jnp.where(next_of == n_exp, -1, next_of))
    ys = _experts(tile_expert, tile_rows, next_expert, n_used, xs, layer, w_gu, b_gu, w_down, b_down)
    return _combine(tab, col, ys, gates, x1, mod, ln_g, ln_b, dims, alpha)


def _ml_in_kernel(xp_ref, x_ref, xn_ref, mod_ref, wqk_ref, bqk_ref, wvo_ref, bvo_ref, wg_ref, bg_ref,
                  cw_ref, cb_ref, q_ref, k_ref, v_ref, og_ref, g_ref, *,
                  n_lat_tiles, tiles_per_batch, ctx_parts_per_seq, ctx_row, d):
    i = pl.program_id(0)
    tm = x_ref.shape[0]
    part = ML_IN_PART
    n_parts = tm // part
    halo = SUBLANES
    row = _mod_row(i, n_lat_tiles, tiles_per_batch, ctx_row)
    shift = mod_ref[pl.ds(row, 1), 0:d]
    scale = mod_ref[pl.ds(row, 1), d:2 * d]
    is_lat = i < n_lat_tiles
    nv = wvo_ref.shape[1] // 2
    n_ext = part + 2 * halo
    cw = cw_ref[...]

    def modulated(rows_ref, rows=slice(None)):
        return rows_ref[rows, :] * (1.0 + scale) + shift

    def part_stages(p):
        rows = slice(p * part, (p + 1) * part)
        seq_part = jnp.where(is_lat, (i % tiles_per_batch) * n_parts + p, p % ctx_parts_per_seq)
        seq_parts = jnp.where(is_lat, tiles_per_batch * n_parts, ctx_parts_per_seq)
        first = seq_part == 0
        last = seq_part == seq_parts - 1
        before = modulated(xp_ref) if p == 0 else modulated(x_ref, slice(p * part - halo, p * part))
        after = (modulated(xn_ref) if p == n_parts - 1
                 else modulated(x_ref, slice((p + 1) * part, (p + 1) * part + halo)))
        h = modulated(x_ref, rows)
        h_ext = jnp.concatenate([before, h, after], axis=0)
        z = jnp.dot(h_ext.astype(BF16), wqk_ref[...], preferred_element_type=F32) + bqk_ref[...]
        h_bf = h.astype(BF16)
        v_pre = jnp.dot(h_bf, wvo_ref[:, :nv], preferred_element_type=F32) + bvo_ref[:, :nv]
        o_pre = jnp.dot(h_bf, wvo_ref[:, nv:], preferred_element_type=F32) + bvo_ref[:, nv:]
        zg = _dot_split(h, wg_ref) + bg_ref[...]
        yield
        r = lax.broadcasted_iota(I32, z.shape, 0)
        z = jnp.where(((r < halo) & first) | ((r >= halo + part) & last), 0.0, z)
        acc = None
        for j in range(ML_CONV_W):
            sh = (ML_CONV_W // 2 - j) % n_ext
            zj = z if sh == 0 else pltpu.roll(z, sh, 0)
            term = zj[halo:halo + part] * cw[j:j + 1]
            acc = term if acc is None else acc + term
        qk = acc + cb_ref[...]
        qk = qk * _sigmoid(qk)
        nqk = qk.shape[1] // 2
        qk_dim = nqk // ML_HEADS
        q_ref[rows, :] = (qk[:, :nqk] * (qk_dim ** -0.5)).astype(BF16)
        k_ref[rows, :] = qk[:, nqk:].astype(BF16)
        yield
        v_ref[rows, :] = v_pre.astype(BF16)
        og_ref[rows, :] = _sigmoid(o_pre).astype(BF16)
        g = GATE_CAP * jnp.tanh(zg / GATE_CAP)
        log_sig = jnp.minimum(g, 0.0) - jnp.log(1.0 + jnp.exp(-jnp.abs(g)))
        lane = lax.broadcasted_iota(I32, g.shape, 1)
        is_forget = ((lane // ML_HEADS) % 2) == 1
        g_ref[rows, :] = jnp.where(is_forget, log_sig, g)

    running = [part_stages(p) for p in range(n_parts)]
    while running:
        running = [s for s in running if next(s, True) is None]


def _ml_in(x_all, mod, w_in, b_in, conv_w, conv_b, dims):
    B, S, C, D = dims
    t_all = x_all.shape[0]
    tm = 512
    part = ML_IN_PART
    assert C % part == 0 and S % tm == 0 and (B * C) % tm == 0
    n_lat_tiles = B * S // tm
    nqk2 = conv_w.shape[1]
    nv = (w_in.shape[1] - nqk2 - 4 * ML_HEADS) // 2
    ng = 4 * ML_HEADS
    w_qk = w_in[:, :nqk2].astype(BF16)
    w_vo = w_in[:, nqk2:nqk2 + 2 * nv].astype(BF16)
    w_g = _split_weight(jnp.zeros((D, LANES), F32).at[:, :ng].set(w_in[:, nqk2 + 2 * nv:]))
    b_qk = b_in[:nqk2].reshape(1, -1)
    b_vo = b_in[nqk2:nqk2 + 2 * nv].reshape(1, -1)
    b_g = jnp.zeros((1, LANES), F32).at[0, :ng].set(b_in[nqk2 + 2 * nv:])
    cw = jnp.zeros((SUBLANES, nqk2), F32).at[:ML_CONV_W].set(conv_w)
    cb = conv_b.reshape(1, -1)
    hb = tm // SUBLANES
    n_hblk = t_all // SUBLANES
    full = lambda a: pl.BlockSpec(a.shape, lambda i: (0,) * a.ndim)
    row_spec = lambda w: pl.BlockSpec((tm, w), lambda i: (i, 0))
    kern = functools.partial(_ml_in_kernel, n_lat_tiles=n_lat_tiles, tiles_per_batch=S // tm,
                             ctx_parts_per_seq=C // part, ctx_row=B, d=D)
    return pl.pallas_call(
        kern,
        grid=(t_all // tm,),
        in_specs=[pl.BlockSpec((SUBLANES, D), lambda i: (jnp.maximum(i * hb - 1, 0), 0)),
                  row_spec(D),
                  pl.BlockSpec((SUBLANES, D), lambda i: (jnp.minimum((i + 1) * hb, n_hblk - 1), 0)),
                  full(mod), full(w_qk), full(b_qk), full(w_vo), full(b_vo), full(w_g), full(b_g),
                  full(cw), full(cb)],
        out_specs=[row_spec(nqk2 // 2), row_spec(nqk2 // 2), row_spec(nv), row_spec(nv), row_spec(LANES)],
        out_shape=[jax.ShapeDtypeStruct((t_all, nqk2 // 2), BF16), jax.ShapeDtypeStruct((t_all, nqk2 // 2), BF16),
                   jax.ShapeDtypeStruct((t_all, nv), BF16), jax.ShapeDtypeStruct((t_all, nv), BF16),
                   jax.ShapeDtypeStruct((t_all, LANES), F32)],
        compiler_params=_params(1),
        name="mlstm_in",
    )(x_all, x_all, x_all, mod, w_qk, b_qk, w_vo, b_vo, w_g, b_g, cw, cb)


def _ml_chunk(q_ref, k_ref, v_ref, g_ref, h_ref, s_sc, m_sc, reverse):
    L = q_ref.shape[0]
    qk_dim = q_ref.shape[1] // ML_HEADS
    v_dim = v_ref.shape[1] // ML_HEADS
    nh = ML_HEADS
    gates_t = g_ref[...].T
    sr = lax.broadcasted_iota(I32, (L, L), 0)
    lc = lax.broadcasted_iota(I32, (L, L), 1)
    upto = (sr >= lc) if reverse else (sr <= lc)
    cum_t = jnp.dot(gates_t, jnp.where(upto, 1.0, 0.0), precision=HIGHEST, preferred_element_type=F32)
    off = 2 * nh if reverse else 0
    li = gates_t[off:off + nh]
    b = cum_t[off + nh:off + 2 * nh]
    c = li - b
    lane = lax.broadcasted_iota(I32, c.shape, 1)
    mu = c
    d = 1
    while d < L:
        if reverse:
            shifted = jnp.where(lane < L - d, pltpu.roll(mu, L - d, 1), -jnp.inf)
        else:
            shifted = jnp.where(lane >= d, pltpu.roll(mu, d, 1), -jnp.inf)
        mu = jnp.maximum(mu, shifted)
        d *= 2
    m_prev = m_sc[...]
    mu = jnp.maximum(mu, m_prev)
    m_t = b + mu
    end = 0 if reverse else L - 1
    mu_end = mu[:, end:end + 1]
    decay = jnp.exp(m_prev[:, 0:1] - mu_end)
    wk = jnp.exp(c - mu_end)
    m_sc[...] = jnp.broadcast_to(b[:, end:end + 1] + mu_end, m_prev.shape)
    cols = jnp.concatenate([mu, m_t, jnp.zeros((L - 2 * nh, L), F32)], axis=0).T
    k_t = k_ref[...].astype(F32).T
    rl = lax.broadcasted_iota(I32, (L, L), 0)
    cs = lax.broadcasted_iota(I32, (L, L), 1)
    allowed = (cs >= rl) if reverse else (cs <= rl)
    ones = jnp.ones((L, v_dim), BF16)

    def head(h):
        qh = q_ref[:, h * qk_dim:(h + 1) * qk_dim]
        kt_h = k_t[h * qk_dim:(h + 1) * qk_dim, :]
        qk = jnp.dot(qh, kt_h.astype(BF16), preferred_element_type=F32)
        state = s_sc[h]
        q_state = jnp.dot(qh, state.astype(BF16), preferred_element_type=F32)
        yield
        mu_col = jnp.broadcast_to(cols[:, h:h + 1], (L, L))
        p = jnp.exp(jnp.where(allowed, c[h:h + 1, :] - mu_col, -jnp.inf))
        w_inter = jnp.exp(m_prev[h:h + 1, :] - mu_col)
        s = (qk * p).astype(BF16)
        yield
        v_aug = jnp.concatenate([v_ref[:, h * v_dim:(h + 1) * v_dim], ones], axis=1)
        nd = jnp.dot(s, v_aug, preferred_element_type=F32)
        kw_t = (kt_h * wk[h:h + 1, :]).astype(BF16)
        new_state = decay[h:h + 1, :] * state + jnp.dot(kw_t, v_aug, preferred_element_type=F32)
        yield
        mt_col = jnp.broadcast_to(cols[:, nh + h:nh + h + 1], (L, v_dim))
        nd = nd + jnp.concatenate([w_inter, w_inter], axis=1) * q_state
        num = nd[:, :v_dim]
        den = nd[:, v_dim:]
        h_ref[:, h * v_dim:(h + 1) * v_dim] = num / jnp.maximum(jnp.abs(den), jnp.exp(-mt_col))
        s_sc[h] = new_state

    return head


def _ml_scan_kernel(*refs, n_batch):
    n_in = 8 * n_batch
    hf_ref, hb_ref, s_sc, m_sc = refs[n_in:]

    @pl.when(pl.program_id(0) == 0)
    def _():
        s_sc[...] = jnp.zeros_like(s_sc)
        m_sc[...] = jnp.zeros_like(m_sc)

    heads = []
    for b in range(n_batch):
        qf, kf, vf, gf, qb, kb, vb, gb = refs[8 * b:8 * b + 8]
        heads.append(_ml_chunk(qf, kf, vf, gf, hf_ref.at[b], s_sc.at[b, 0], m_sc.at[b, 0], False))
        heads.append(_ml_chunk(qb, kb, vb, gb, hb_ref.at[b], s_sc.at[b, 1], m_sc.at[b, 1], True))
    for h in range(ML_HEADS):
        running = [head(h) for head in heads]
        while running:
            running = [g for g in running if next(g, True) is None]


def _ml_scan(q, k, v, g, dims):
    B, S, C, D = dims
    L = ML_CHUNK
    assert L == LANES
    nc_ctx = C // L
    nc_lat = S // L
    qk_dim = q.shape[1] // ML_HEADS
    v_dim = v.shape[1] // ML_HEADS

    def in_index(b, reverse):
        def idx(c):
            in_ctx = c < nc_ctx
            cl = c - nc_ctx
            if reverse:
                ctx_blk = (B * S + b * C) // L + (nc_ctx - 1 - c)
                lat_blk = (b * S) // L + (nc_lat - 1 - cl)
            else:
                ctx_blk = (B * S + b * C) // L + c
                lat_blk = (b * S) // L + cl
            return (jnp.where(in_ctx, ctx_blk, lat_blk), 0)
        return idx

    def out_index(reverse):
        def idx(c):
            cl = jnp.maximum(c - nc_ctx, 0)
            return (0, nc_lat - 1 - cl if reverse else cl, 0)
        return idx

    widths = (q.shape[1], k.shape[1], v.shape[1], LANES)
    in_specs, args = [], []
    for b in range(B):
        for rev in (False, True):
            in_specs += [pl.BlockSpec((L, w), in_index(b, rev)) for w in widths]
            args += [q, k, v, g]
    out = jax.ShapeDtypeStruct((B, S, v.shape[1]), F32)
    return pl.pallas_call(
        functools.partial(_ml_scan_kernel, n_batch=B),
        grid=(nc_ctx + nc_lat,),
        in_specs=in_specs,
        out_specs=[pl.BlockSpec((B, L, v.shape[1]), out_index(False)),
                   pl.BlockSpec((B, L, v.shape[1]), out_index(True))],
        out_shape=[out, out],
        scratch_shapes=[pltpu.VMEM((B, 2, ML_HEADS, qk_dim, 2 * v_dim), F32),
                        pltpu.VMEM((B, 2, ML_HEADS, LANES), F32)],
        compiler_params=_params(1),
        name="mlstm_scan",
    )(*args)


def kernel(x, c, ctx, c_ctx, ada_w, ada_b, ln_g, ln_b, attn_w_qkv, attn_b_qkv, attn_sink, attn_w_o, attn_b_o,
           ml_w_in, ml_b_in, ml_conv_w, ml_conv_b, ml_norm_g, ml_w_out, router_w, router_b,
           exp_w_gu, exp_b_gu, exp_w_down, exp_b_down):
    B, S, D = x.shape
    C = ctx.shape[1]
    depth = ada_w.shape[0]
    dims = (B, S, C, D)
    alpha = (2.0 * depth) ** 0.25
    n_lat = B * S

    cvec = jnp.zeros((SUBLANES, D), F32).at[:B].set(c).at[B].set(c_ctx)
    mods = _adaln(cvec, ada_w, ada_b)
    x_lat, x_ctx = x.reshape(n_lat, D), ctx.reshape(B * C, D)

    q, k, v = _attn_qkv(x_lat, x_ctx, mods[0], attn_w_qkv[0], attn_b_qkv[0], dims)
    o = _attention(q, k, v, attn_sink[0], dims)
    x1, hx, topi, gates, sizes = _post_mixer((o,), None, (x_lat, x_ctx), mods[0], attn_w_o[0], attn_b_o[0],
                                      ln_g[0, 0], ln_b[0, 0], router_w[0], router_b[0], dims, n_lat + B * C, alpha)
    x_all = _moe(hx, topi, gates, sizes, x1, mods[0], ln_g[0, 1], ln_b[0, 1],
                 0, exp_w_gu, exp_b_gu, exp_w_down, exp_b_down, dims, alpha)

    q, k, v, og, g = _ml_in(x_all, mods[1], ml_w_in[0], ml_b_in[0], ml_conv_w[0], ml_conv_b[0], dims)
    hf, hb = (h.reshape(n_lat, -1) for h in _ml_scan(q, k, v, g, dims))
    zero_b = jnp.zeros((D,), F32)
    x1, hx, topi, gates, sizes = _post_mixer((hf, hb, og), ml_norm_g[0], x_all, mods[1], ml_w_out[0], zero_b,
                                      ln_g[1, 0], ln_b[1, 0], router_w[1], router_b[1], dims, n_lat, alpha)
    out = _moe(hx, topi, gates, sizes, x1, mods[1], ln_g[1, 1], ln_b[1, 1],
               1, exp_w_gu, exp_b_gu, exp_w_down, exp_b_down, dims, alpha)
    return out.reshape(B, S, D)
```

```python
import functools

import jax
import jax.numpy as jnp
from jax import lax
from jax.experimental import pallas as pl
from jax.experimental.pallas import tpu as pltpu

F32 = jnp.float32
BF16 = jnp.bfloat16
I32 = jnp.int32
HIGHEST = lax.Precision.HIGHEST

GRID_W = 64
ATTN_HEAD_DIM = 64
ATTN_KV_HEADS = 4
WINDOW = 128
ATTN_BLOCK = 128
ROPE_THETA = 10000.0
ML_HEADS = 8
ML_CONV_W = 5
GATE_CAP = 15.0
TOP_K = 4
SWIGLU_ALPHA = 1.702
SWIGLU_LIMIT = 7.0
LN_EPS = 1e-5

LANES = 128
SUBLANES = 8
VMEM_LIMIT = 56 * 1024 * 1024
EXPERT_TILE = 512
ROUTE_TILE = 512
POST_PART = 256
ML_CHUNK = 128
GROUP_ALIGN = 16
BIG_UNIT = 2 * GROUP_ALIGN
STAGE_ROWS = ROUTE_TILE * TOP_K + 4 * LANES
TABLE_W = 4 * LANES
ZERO_ROWS = 512
ML_IN_PART = 256


def _params(n_axes, vmem=VMEM_LIMIT):
    return pltpu.CompilerParams(dimension_semantics=("arbitrary",) * n_axes, vmem_limit_bytes=vmem)


def _layer_norm(r, g, b):
    mu = jnp.mean(r, axis=-1, keepdims=True)
    rc = r - mu
    var = jnp.mean(rc * rc, axis=-1, keepdims=True)
    return rc * lax.rsqrt(var + LN_EPS) * g + b


def _sigmoid(x):
    return 1.0 / (1.0 + jnp.exp(-x))


def _mod_row(i, n_lat_tiles, tiles_per_batch, ctx_row):
    return jnp.where(i < n_lat_tiles, i // tiles_per_batch, ctx_row)


def _adaln_kernel(c_ref, w_ref, b_ref, o_ref):
    c = c_ref[...]
    s = c * _sigmoid(c)
    o_ref[0] = jnp.dot(s, w_ref[0], precision=HIGHEST, preferred_element_type=F32) + b_ref[0]


def _adaln(cvec, ada_w, ada_b):
    depth, d, n = ada_w.shape
    tn = 1536
    return pl.pallas_call(
        _adaln_kernel,
        grid=(depth, n // tn),
        in_specs=[pl.BlockSpec((SUBLANES, d), lambda l, j: (0, 0)),
                  pl.BlockSpec((1, d, tn), lambda l, j: (l, 0, j)),
                  pl.BlockSpec((1, 1, tn), lambda l, j: (l, 0, j))],
        out_specs=pl.BlockSpec((1, SUBLANES, tn), lambda l, j: (l, 0, j)),
        out_shape=jax.ShapeDtypeStruct((depth, SUBLANES, n), F32),
        compiler_params=_params(2),
        name="adaln",
    )(cvec, ada_w, ada_b.reshape(depth, 1, n))


def _token_specs(tm, d, n_lat_tiles):
    return [pl.BlockSpec((tm, d), lambda i, *_: (jnp.minimum(i, n_lat_tiles - 1), 0)),
            pl.BlockSpec((tm, d), lambda i, *_: (jnp.maximum(i - n_lat_tiles, 0), 0))]


def _qkv_kernel(x_ref, c_ref, mod_ref, w_ref, b_ref, cos_ref, sin_ref, q_ref, k_ref, vt_ref, *,
                n_lat_tiles, tiles_per_batch, ctx_row, d, qd, kvd):
    i = pl.program_id(0)
    row = _mod_row(i, n_lat_tiles, tiles_per_batch, ctx_row)
    shift = mod_ref[pl.ds(row, 1), 0:d]
    scale = mod_ref[pl.ds(row, 1), d:2 * d]
    x = jnp.where(i < n_lat_tiles, x_ref[...], c_ref[...])
    h = x * (1.0 + scale) + shift
    z = jnp.dot(h.astype(BF16), w_ref[...], preferred_element_type=F32) + b_ref[...]
    nrot = qd + kvd
    qk = z[:, :nrot]
    reps = nrot // LANES
    cos = jnp.concatenate([cos_ref[...]] * reps, axis=1)
    sin = jnp.concatenate([sin_ref[...]] * reps, axis=1)
    lane = lax.broadcasted_iota(I32, qk.shape, 1)
    low_half = (lane & 16) == 0
    partner = jnp.where(low_half, pltpu.roll(qk, nrot - 16, 1), pltpu.roll(qk, 16, 1))
    qk = qk * cos + partner * sin
    q_ref[...] = (qk[:, :qd] * (ATTN_HEAD_DIM ** -0.5)).astype(BF16)
    k_ref[...] = qk[:, qd:].astype(BF16)
    vt_ref[...] = z[:, nrot:].T.astype(BF16)


def _rope_tables(s_len, tm):
    half = ATTN_HEAD_DIM // 4
    freqs = ROPE_THETA ** (-jnp.arange(half, dtype=F32) / half)
    t = jnp.arange(s_len)
    rows = (t // GRID_W).astype(F32)[:, None] * freqs[None, :]
    cols = (t % GRID_W).astype(F32)[:, None] * freqs[None, :]
    ang = jnp.concatenate([rows, rows, cols, cols], axis=1)
    sign = jnp.tile(jnp.concatenate([-jnp.ones((half,), F32), jnp.ones((half,), F32)]), 2)
    cos = jnp.cos(ang)
    sin = jnp.sin(ang) * sign[None, :]
    reps = LANES // ATTN_HEAD_DIM
    cos = jnp.concatenate([jnp.tile(cos, (1, reps)), jnp.ones((tm, LANES), F32)], axis=0)
    sin = jnp.concatenate([jnp.tile(sin, (1, reps)), jnp.zeros((tm, LANES), F32)], axis=0)
    return cos, sin


def _attn_qkv(x_lat, x_ctx, mod, w_qkv, b_qkv, dims):
    B, S, C, D = dims
    t_all = x_lat.shape[0] + x_ctx.shape[0]
    tm = 512
    n_lat_tiles = B * S // tm
    tiles_per_batch = S // tm
    ncols = w_qkv.shape[1]
    kvd = ATTN_KV_HEADS * ATTN_HEAD_DIM
    qd = ncols - 2 * kvd
    cos, sin = _rope_tables(S, tm)

    def tab_idx(i):
        return (jnp.where(i < n_lat_tiles, i % tiles_per_batch, tiles_per_batch), 0)

    kern = functools.partial(_qkv_kernel, n_lat_tiles=n_lat_tiles, tiles_per_batch=tiles_per_batch,
                             ctx_row=B, d=D, qd=qd, kvd=kvd)
    return pl.pallas_call(
        kern,
        grid=(t_all // tm,),
        in_specs=_token_specs(tm, D, n_lat_tiles) + [
                  pl.BlockSpec(mod.shape, lambda i: (0, 0)),
                  pl.BlockSpec((D, ncols), lambda i: (0, 0)),
                  pl.BlockSpec((1, ncols), lambda i: (0, 0)),
                  pl.BlockSpec((tm, LANES), tab_idx),
                  pl.BlockSpec((tm, LANES), tab_idx)],
        out_specs=[pl.BlockSpec((tm, qd), lambda i: (i, 0)),
                   pl.BlockSpec((tm, kvd), lambda i: (i, 0)),
                   pl.BlockSpec((kvd, tm), lambda i: (0, i))],
        out_shape=[jax.ShapeDtypeStruct((t_all, qd), BF16),
                   jax.ShapeDtypeStruct((t_all, kvd), BF16),
                   jax.ShapeDtypeStruct((kvd, t_all), BF16)],
        compiler_params=_params(1),
        name="attn_qkv",
    )(x_lat, x_ctx, mod, w_qkv.astype(BF16), b_qkv.reshape(1, ncols), cos, sin)


def _attn_kernel(sink_ref, q_ref, kp_ref, ko_ref, kn_ref, kc_ref, vp_ref, vo_ref, vn_ref, vc_ref, o_ref,
                 bias_sc, s_sc, p_sc, ot_sc, *,
                 n_lat_steps, nb, s_len, c_len):
    j = pl.program_id(0)
    is_lat = j < n_lat_steps
    n = j % nb
    blk = ATTN_BLOCK
    nloc = 3 * blk
    nk = nloc + c_len
    ki = lax.broadcasted_iota(I32, (nloc, blk), 0)
    qj = lax.broadcasted_iota(I32, (nloc, blk), 1)
    kpos = n * blk - WINDOW + ki
    qpos = n * blk + qj
    local_ok = (jnp.abs(kpos - qpos) <= WINDOW) & (kpos >= 0) & (kpos < s_len) & is_lat
    bias_sc[...] = jnp.where(local_ok, 0.0, -jnp.inf)
    hd = ATTN_HEAD_DIM
    group = q_ref.shape[1] // (ATTN_KV_HEADS * hd)
    kcat = jnp.concatenate([kp_ref[...], ko_ref[...], kn_ref[...], kc_ref[...]], axis=0)
    vcat_t = jnp.concatenate([vp_ref[...], vo_ref[...], vn_ref[...], vc_ref[...]], axis=1)
    def score_matmul(kh):
        q_grp = jnp.concatenate([q_ref[:, (kh * group + g) * hd:(kh * group + g + 1) * hd]
                                 for g in range(group)], axis=0)
        s_sc[kh] = lax.dot_general(kcat[:, kh * hd:(kh + 1) * hd], q_grp, (((1,), (1,)), ((), ())),
                                   preferred_element_type=F32)

    def weighted_values(kh, sink_terms):
        v_ones = jnp.concatenate([vcat_t[kh * hd:(kh + 1) * hd, :], jnp.ones((SUBLANES, nk), BF16)], axis=0)
        o_aug = jnp.dot(v_ones, p_sc[kh], preferred_element_type=F32)
        l = o_aug[hd:hd + 1, :] + jnp.concatenate(sink_terms, axis=1)
        o_t = o_aug[0:hd, :] * (1.0 / l)
        for g in range(group):
            h = kh * group + g
            ot_sc[h * hd:(h + 1) * hd, :] = o_t[:, g * blk:(g + 1) * blk]

    score_matmul(0)
    for kh in range(ATTN_KV_HEADS):
        if kh + 1 < ATTN_KV_HEADS:
            score_matmul(kh + 1)
        sink_terms = []
        for g in range(group):
            cols = slice(g * blk, (g + 1) * blk)
            def scores(a):
                s = s_sc[kh, a:a + blk, cols]
                return s + bias_sc[a:a + blk, :] if a < nloc else s

            sk = sink_ref[kh * group + g]
            top = scores(0)
            for a in range(blk, nk, blk):
                top = jnp.maximum(top, scores(a))
            m = jnp.maximum(jnp.max(top, axis=0, keepdims=True), sk)
            for a in range(0, nk, blk):
                p_sc[kh, a:a + blk, cols] = jnp.exp((scores(a) - m).astype(BF16))
            sink_terms.append(jnp.exp(sk - m))
        weighted_values(kh, sink_terms)
    o_ref[...] = ot_sc[...].T.astype(BF16)


def _attention(q_all, k_all, vt_all, sink, dims):
    B, S, C, D = dims
    blk = ATTN_BLOCK
    nb = S // blk
    n_lat_steps = B * nb
    ctx_steps_per_batch = C // blk
    n_steps = n_lat_steps + B * ctx_steps_per_batch
    qd = q_all.shape[1]
    kvd = k_all.shape[1]

    def local_idx(off):
        def idx(j):
            b = j // nb
            nn = jnp.clip(j % nb + off, 0, nb - 1)
            return (jnp.where(j < n_lat_steps, b * nb + nn, j), 0)
        return idx

    def ctx_idx(j):
        b = jnp.where(j < n_lat_steps, j // nb, (j - n_lat_steps) // ctx_steps_per_batch)
        return (B * S // C + b, 0)

    swap = lambda f: (lambda j: f(j)[::-1])
    loc = lambda off: pl.BlockSpec((blk, kvd), local_idx(off))
    ctxs = pl.BlockSpec((C, kvd), ctx_idx)
    loc_t = lambda off: pl.BlockSpec((kvd, blk), swap(local_idx(off)))
    ctxs_t = pl.BlockSpec((kvd, C), swap(ctx_idx))
    group = qd // kvd
    nk = 3 * blk + C
    kern = functools.partial(_attn_kernel, n_lat_steps=n_lat_steps, nb=nb, s_len=S, c_len=C)
    return pl.pallas_call(
        kern,
        grid=(n_steps,),
        in_specs=[pl.BlockSpec(memory_space=pltpu.SMEM),
                  pl.BlockSpec((blk, qd), lambda j: (j, 0)),
                  loc(-1), loc(0), loc(1), ctxs,
                  loc_t(-1), loc_t(0), loc_t(1), ctxs_t],
        out_specs=pl.BlockSpec((blk, qd), lambda j: (j, 0)),
        out_shape=jax.ShapeDtypeStruct((q_all.shape[0], qd), BF16),
        scratch_shapes=[pltpu.VMEM((3 * blk, blk), F32),
                        pltpu.VMEM((ATTN_KV_HEADS, nk, group * blk), F32),
                        pltpu.VMEM((ATTN_KV_HEADS, nk, group * blk), BF16),
                        pltpu.VMEM((qd, blk), F32)],
        compiler_params=_params(1),
        name="attn_core",
    )(sink, q_all, k_all, k_all, k_all, k_all, vt_all, vt_all, vt_all, vt_all)


def _split_bf16(a):
    hi = a.astype(BF16)
    return hi, (a - hi.astype(F32)).astype(BF16)


def _split_weight(w_f32):
    hi = w_f32.astype(BF16)
    return jnp.concatenate([hi, (w_f32 - hi.astype(F32)).astype(BF16)], axis=1)


def _dot_split(a, w_ref):
    a_hi, a_lo = _split_bf16(a)
    n = w_ref.shape[1] // 2
    both = jnp.dot(a_hi, w_ref[...], preferred_element_type=F32)
    return both[:, :n] + both[:, n:] + jnp.dot(a_lo, w_ref[:, :n], preferred_element_type=F32)


def _top_k_route(logits, rows, topi_ref, gate_ref, sizes_ref, part):
    lane = lax.broadcasted_iota(I32, logits.shape, 1)
    lanef = lane.astype(F32)
    vals, idxs = [], []
    l = logits
    for _ in range(TOP_K):
        m = jnp.max(l, axis=1, keepdims=True)
        idx = jnp.min(jnp.where(l == m, lanef, float(LANES)), axis=1, keepdims=True)
        vals.append(m)
        idxs.append(idx)
        l = jnp.where(lanef == idx, -jnp.inf, l)
    es = [jnp.exp(v - vals[0]) for v in vals]
    denom = es[0]
    for e in es[1:]:
        denom = denom + e
    topi = jnp.zeros(logits.shape, F32)
    gates = jnp.zeros(logits.shape, F32)
    chosen = jnp.zeros(logits.shape, F32)
    for k in range(TOP_K):
        topi = jnp.where(lane == k, idxs[k], topi)
        gates = jnp.where(lane == k, es[k] / denom, gates)
        chosen = chosen + jnp.where(lanef == idxs[k], 1.0, 0.0)
    topi_ref[rows, :] = topi.astype(I32)
    gate_ref[rows, :] = gates
    sizes_ref[0, part:part + 1, :] = jnp.sum(chosen, axis=0, keepdims=True)


def _post_common(a_fn, x_fn, mod_ref, w_ref, b_ref, lng_ref, lnb_ref, rw_ref, rb_ref,
                 x1_ref, hx_ref, topi_ref, gate_ref, sizes_ref, row, d, alpha):
    gate_mix = mod_ref[pl.ds(row, 1), 2 * d:3 * d]
    shift = mod_ref[pl.ds(row, 1), 3 * d:4 * d]
    scale = mod_ref[pl.ds(row, 1), 4 * d:5 * d]
    sizes_ref[...] = jnp.zeros_like(sizes_ref)

    def part_stages(part):
        rows = slice(part * POST_PART, (part + 1) * POST_PART)
        y = jnp.dot(a_fn(rows), w_ref[...], preferred_element_type=F32) + b_ref[...]
        yield
        x1 = _layer_norm(alpha * x_fn(rows) + gate_mix * y, lng_ref[...], lnb_ref[...])
        hx = x1 * (1.0 + scale) + shift
        x1_ref[rows, :] = x1
        hx_ref[rows, :] = hx.astype(BF16)
        logits = _dot_split(hx, rw_ref) + rb_ref[...]
        yield
        _top_k_route(logits, rows, topi_ref, gate_ref, sizes_ref, part)

    running = [part_stages(p) for p in range(x1_ref.shape[0] // POST_PART)]
    while running:
        running = [g for g in running if next(g, True) is None]


def _post_attn_kernel(o_ref, x_ref, c_ref, mod_ref, w_ref, b_ref, lng_ref, lnb_ref, rw_ref, rb_ref,
                      x1_ref, hx_ref, topi_ref, gate_ref, sizes_ref, *, n_lat_tiles, tiles_per_batch, ctx_row, d, alpha):
    i = pl.program_id(0)
    row = _mod_row(i, n_lat_tiles, tiles_per_batch, ctx_row)
    is_lat = i < n_lat_tiles
    _post_common(lambda rows: o_ref[rows, :],
                 lambda rows: jnp.where(is_lat, x_ref[rows, :], c_ref[rows, :]),
                 mod_ref, w_ref, b_ref, lng_ref, lnb_ref, rw_ref, rb_ref,
                 x1_ref, hx_ref, topi_ref, gate_ref, sizes_ref, row, d, alpha)


def _post_mlstm_kernel(hf_ref, hb_ref, og_ref, ng_ref, x_ref, mod_ref, w_ref, b_ref, lng_ref, lnb_ref,
                       rw_ref, rb_ref, x1_ref, hx_ref, topi_ref, gate_ref, sizes_ref, *,
                       n_lat_tiles, tiles_per_batch, ctx_row, d, alpha):
    row = _mod_row(pl.program_id(0), n_lat_tiles, tiles_per_batch, ctx_row)

    def gated_head_norm(rows):
        hsum = hf_ref[rows, :] + hb_ref[rows, :]
        vdim = hsum.shape[1] // ML_HEADS
        parts = []
        for h in range(ML_HEADS):
            seg = hsum[:, h * vdim:(h + 1) * vdim]
            mu = jnp.mean(seg, axis=1, keepdims=True)
            sc = seg - mu
            var = jnp.mean(sc * sc, axis=1, keepdims=True)
            parts.append(sc * lax.rsqrt(var + LN_EPS))
        y = jnp.concatenate(parts, axis=1) * ng_ref[...]
        return (og_ref[rows, :].astype(F32) * y).astype(BF16)

    _post_common(gated_head_norm, lambda rows: x_ref[rows, :],
                 mod_ref, w_ref, b_ref, lng_ref, lnb_ref, rw_ref, rb_ref,
                 x1_ref, hx_ref, topi_ref, gate_ref, sizes_ref, row, d, alpha)


def _post_mixer(mixer_inputs, norm_g, x_all, mod, w_o, b_o, ln_g, ln_b, router_w, router_b, dims, n_rows, alpha):
    B, S, C, D = dims
    tm = ROUTE_TILE
    n_lat_tiles = B * S // tm
    tiles_per_batch = S // tm
    n_exp = router_w.shape[1]
    rw = _split_weight(jnp.zeros((D, LANES), F32).at[:, :n_exp].set(router_w))
    rb = jnp.full((1, LANES), -1e30, F32).at[0, :n_exp].set(router_b)
    row_spec = lambda w: pl.BlockSpec((tm, w), lambda i: (i, 0))
    full = lambda a: pl.BlockSpec(a.shape, lambda i: (0,) * a.ndim)
    common = dict(n_lat_tiles=n_lat_tiles, tiles_per_batch=tiles_per_batch, ctx_row=B, d=D, alpha=alpha)
    w_bf = w_o.astype(BF16)
    b2 = b_o.reshape(1, D)
    if isinstance(x_all, tuple):
        x_args, x_specs = list(x_all), _token_specs(tm, D, n_lat_tiles)
    else:
        x_args, x_specs = [x_all], [row_spec(D)]
    tail = x_args + [mod, w_bf, b2, ln_g.reshape(1, D), ln_b.reshape(1, D), rw, rb]
    tail_specs = x_specs + [full(mod), full(w_bf), full(b2), pl.BlockSpec((1, D), lambda i: (0, 0)),
                            pl.BlockSpec((1, D), lambda i: (0, 0)), full(rw), full(rb)]
    if norm_g is None:
        kern = functools.partial(_post_attn_kernel, **common)
        args = list(mixer_inputs) + tail
        specs = [row_spec(mixer_inputs[0].shape[1])] + tail_specs
        name = "post_attn"
    else:
        kern = functools.partial(_post_mlstm_kernel, **common)
        ng = norm_g.reshape(1, -1)
        args = list(mixer_inputs) + [ng] + tail
        specs = [row_spec(a.shape[1]) for a in mixer_inputs] + [full(ng)] + tail_specs
        name = "post_mlstm"
    return pl.pallas_call(
        kern,
        grid=(n_rows // tm,),
        in_specs=specs,
        out_specs=[row_spec(D), row_spec(D), row_spec(LANES), row_spec(LANES),
                   pl.BlockSpec((1, SUBLANES, LANES), lambda i: (i, 0, 0))],
        out_shape=[jax.ShapeDtypeStruct((n_rows, D), F32), jax.ShapeDtypeStruct((n_rows, D), BF16),
                   jax.ShapeDtypeStruct((n_rows, LANES), I32), jax.ShapeDtypeStruct((n_rows, LANES), F32),
                   jax.ShapeDtypeStruct((n_rows // tm, SUBLANES, LANES), F32)],
        compiler_params=_params(1),
        name=name,
    )(*args)


def _exclusive_lane_cumsum(row):
    r = lax.broadcasted_iota(I32, (LANES, LANES), 0)
    c = lax.broadcasted_iota(I32, (LANES, LANES), 1)
    before = jnp.where(r < c, 1.0, 0.0)
    return jnp.dot(jnp.broadcast_to(row, (SUBLANES, LANES)), before,
                   precision=HIGHEST, preferred_element_type=F32)[0:1]


def _for_each_unit(tab_ref, tile, fn):
    base = tile * TABLE_W
    for blk, rows in ((0, BIG_UNIT), (2, GROUP_ALIGN)):
        def body(j, carry, blk=blk, rows=rows):
            slot_row = pl.multiple_of(tab_ref[base + blk * LANES + j], GROUP_ALIGN)
            stage_row = pl.multiple_of(tab_ref[base + (blk + 1) * LANES + j], GROUP_ALIGN)
            fn(slot_row, stage_row, rows)
            return carry
        lax.fori_loop(0, tab_ref[base + blk * LANES + LANES - 1], body, 0)


def _slots_kernel(topi_ref, sizes_ref, col_ref, colt_ref, tab_ref, meta_ref, base_sc, carry_sc, *, tile):
    i = pl.program_id(0)
    tm = topi_ref.shape[0]
    topi = topi_ref[...]
    lane = lax.broadcasted_iota(I32, topi.shape, 1)
    sel = [lane == topi[:, k:k + 1] for k in range(TOP_K)]
    maskf = jnp.where(sel[0], 1.0, 0.0)
    for s in sel[1:]:
        maskf = maskf + jnp.where(s, 1.0, 0.0)
    n8 = jnp.ceil(jnp.sum(maskf, axis=0, keepdims=True) / GROUP_ALIGN) * GROUP_ALIGN

    @pl.when(i == 0)
    def _():
        groups = jnp.ceil(jnp.sum(sizes_ref[...], axis=1) / GROUP_ALIGN) * GROUP_ALIGN
        tot = jnp.sum(groups, axis=0, keepdims=True)
        padded = jnp.ceil(tot / tile) * tile
        base = _exclusive_lane_cumsum(padded)
        base_sc[...] = base
        carry_sc[...] = jnp.zeros_like(carry_sc)
        rowi = lax.broadcasted_iota(I32, meta_ref.shape, 0)
        meta_ref[...] = jnp.where(rowi == 0, tot, jnp.where(rowi == 1, base, padded)).astype(I32)

    start = base_sc[...] + carry_sc[...]
    off = _exclusive_lane_cumsum(n8)
    r = lax.broadcasted_iota(I32, (tm, tm), 0)
    c = lax.broadcasted_iota(I32, (tm, tm), 1)
    earlier = jnp.where(c < r, 1.0, 0.0).astype(BF16)
    rank = jnp.dot(earlier, maskf.astype(BF16), preferred_element_type=F32)
    stage_row = rank + off
    out = jnp.zeros(topi.shape, F32)
    for k in range(TOP_K):
        pk = jnp.sum(jnp.where(sel[k], stage_row, 0.0), axis=1, keepdims=True)
        out = jnp.where(lane == k, pk, out)
    col_ref[...] = out.astype(I32)
    colt_ref[0] = out.T[0:SUBLANES].astype(I32)

    n_big = jnp.floor(n8 / BIG_UNIT)
    n_small = (n8 - BIG_UNIT * n_big) / GROUP_ALIGN
    first_big = _exclusive_lane_cumsum(n_big)
    first_small = _exclusive_lane_cumsum(n_small)
    rows = jnp.concatenate([off, start, first_big, n_big, first_small, n_small,
                            jnp.zeros((LANES - 6, LANES), F32)], axis=0)
    per_expert = rows.T
    off_c, start_c, fb_c, nb_c, fs_c, ns_c = (per_expert[:, k:k + 1] for k in range(6))
    u = lax.broadcasted_iota(I32, (LANES, LANES), 1).astype(F32)
    lane_t = lax.broadcasted_iota(I32, (1, LANES), 1)

    def unit_list(first_c, count_c, rel, count_row):
        inside = (u >= first_c) & (u < first_c + count_c)
        src = jnp.sum(jnp.where(inside, start_c + rel, 0.0), axis=0, keepdims=True)
        dst = jnp.sum(jnp.where(inside, off_c + rel, 0.0), axis=0, keepdims=True)
        total = jnp.sum(count_row, axis=1, keepdims=True)
        return [jnp.where(lane_t == LANES - 1, total, src), dst]

    lists = (unit_list(fb_c, nb_c, BIG_UNIT * (u - fb_c), n_big)
             + unit_list(fs_c, ns_c, BIG_UNIT * nb_c + GROUP_ALIGN * (u - fs_c), n_small))
    tab_ref[0] = jnp.concatenate(lists, axis=1).astype(I32)
    carry_sc[...] = carry_sc[...] + n8


def _slots(topi, sizes, tile):
    t = topi.shape[0]
    tm = ROUTE_TILE
    nt = t // tm
    return pl.pallas_call(
        functools.partial(_slots_kernel, tile=float(tile)),
        grid=(nt,),
        in_specs=[pl.BlockSpec((tm, LANES), lambda i: (i, 0)),
                  pl.BlockSpec(sizes.shape, lambda i: (0, 0, 0))],
        out_specs=[pl.BlockSpec((tm, LANES), lambda i: (i, 0)),
                   pl.BlockSpec((1, SUBLANES, tm), lambda i: (i, 0, 0)),
                   pl.BlockSpec((1, 1, TABLE_W), lambda i: (i, 0, 0)),
                   pl.BlockSpec((SUBLANES, LANES), lambda i: (0, 0))],
        out_shape=[jax.ShapeDtypeStruct((t, LANES), I32),
                   jax.ShapeDtypeStruct((nt, SUBLANES, tm), I32),
                   jax.ShapeDtypeStruct((nt, 1, TABLE_W), I32),
                   jax.ShapeDtypeStruct((SUBLANES, LANES), I32)],
        scratch_shapes=[pltpu.VMEM((1, LANES), F32), pltpu.VMEM((1, LANES), F32)],
        compiler_params=_params(1),
        name="moe_slots",
    )(topi, sizes)


def _dispatch_kernel(tab_ref, pad_start_ref, pad_units_ref, tail_ref, colt_ref, hx_ref, xs_ref, stage, zeros, sem, zsem):
    i = pl.program_id(0)
    nt = pl.num_programs(0)
    slot = i % 2
    kb = stage.shape[1]
    tm = hx_ref.shape[0]

    def scatter(buf_slot):
        def copy(slot_row, stage_row, rows):
            return pltpu.make_async_copy(stage.at[buf_slot, pl.ds(stage_row, rows)],
                                         xs_ref.at[pl.ds(slot_row, rows)], sem.at[buf_slot])
        return copy

    def drain(tile, buf_slot):
        _for_each_unit(tab_ref, tile, lambda *u: scatter(buf_slot)(*u).wait())

    @pl.when(i >= 2)
    def _():
        drain(i - 2, slot)

    colt = colt_ref[0]
    c = lax.broadcasted_iota(I32, (kb, tm), 0)
    onehot = jnp.zeros((kb, tm), F32)
    for k in range(TOP_K):
        onehot = jnp.where(c == colt[k:k + 1, :], 1.0, onehot)
    stage[slot] = jnp.dot(onehot.astype(BF16), hx_ref[...], preferred_element_type=F32).astype(BF16)

    _for_each_unit(tab_ref, i, lambda *u: scatter(slot)(*u).start())

    @pl.when(i == nt - 1)
    def _():
        zeros[...] = jnp.zeros_like(zeros)
        n_exp = pad_start_ref.shape[0]

        def zero_copy(e, u):
            dst = pl.multiple_of(pad_start_ref[e] + u * GROUP_ALIGN, GROUP_ALIGN)
            return pltpu.make_async_copy(zeros.at[pl.ds(0, GROUP_ALIGN)], xs_ref.at[pl.ds(dst, GROUP_ALIGN)], zsem)

        def per_expert(fn):
            def outer(e, carry):
                def inner(u, carry2):
                    fn(e, u)
                    return carry2
                lax.fori_loop(0, pad_units_ref[e], inner, 0)
                return carry
            lax.fori_loop(0, n_exp, outer, 0)

        per_expert(lambda e, u: zero_copy(e, u).start())
        per_expert(lambda e, u: zero_copy(e, u).wait())

        zrows = zeros.shape[0]

        def tail_copy(u):
            dst = pl.multiple_of(tail_ref[0] + u * zrows, zrows)
            return pltpu.make_async_copy(zeros, xs_ref.at[pl.ds(dst, zrows)], zsem)

        def tail_loop(fn):
            def body(u, carry):
                fn(u)
                return carry
            lax.fori_loop(0, tail_ref[1], body, 0)

        tail_loop(lambda u: tail_copy(u).start())
        tail_loop(lambda u: tail_copy(u).wait())

        @pl.when(nt >= 2)
        def _():
            drain(i - 1, 1 - slot)
        drain(i, slot)


def _dispatch(tab, pad_start, pad_units, tail, colt, hx, n_slots):
    t, d = hx.shape
    tm = ROUTE_TILE
    return pl.pallas_call(
        _dispatch_kernel,
        grid_spec=pltpu.PrefetchScalarGridSpec(
            num_scalar_prefetch=4,
            grid=(t // tm,),
            in_specs=[pl.BlockSpec((1, SUBLANES, tm), lambda i, *_: (i, 0, 0)),
                      pl.BlockSpec((tm, d), lambda i, *_: (i, 0))],
            out_specs=pl.BlockSpec(memory_space=pl.ANY),
            scratch_shapes=[pltpu.VMEM((2, STAGE_ROWS, d), BF16), pltpu.VMEM((ZERO_ROWS, d), BF16),
                            pltpu.SemaphoreType.DMA((2,)), pltpu.SemaphoreType.DMA(())]),
        out_shape=jax.ShapeDtypeStruct((n_slots, d), BF16),
        compiler_params=_params(1),
        name="moe_dispatch",
    )(tab, pad_start, pad_units, tail, colt, hx)


def _expert_kernel(te_ref, rows_ref, next_ref, nu_ref, xs_ref, wgu_hbm, bgu_ref, wd_hbm, bd_ref, ys_ref,
                   wgu_f32, wd_f32, wgu_sc, wd_sc, wsem, *, layer):
    j = pl.program_id(0)
    active = j < nu_ref[0]
    changed = (j == 0) | (te_ref[j] != te_ref[jnp.maximum(j - 1, 0)])
    ff = wd_sc.shape[0]
    te = xs_ref.shape[0]
    half = te // 2

    def weight_copies(e):
        return (pltpu.make_async_copy(wgu_hbm.at[layer, e], wgu_f32, wsem.at[0]),
                pltpu.make_async_copy(wd_hbm.at[layer, e], wd_f32, wsem.at[1]))

    def run(rows, fresh):
        x = xs_ref[0:rows, :]
        nblk = 2
        fb = ff // nblk

        def up(c):
            gl_cols = slice(c * fb, (c + 1) * fb)
            lin_cols = slice(ff + c * fb, ff + (c + 1) * fb)
            if fresh:
                wgu_sc[:, gl_cols] = wgu_f32[:, gl_cols].astype(BF16)
                wgu_sc[:, lin_cols] = wgu_f32[:, lin_cols].astype(BF16)
            gl = jnp.dot(x, wgu_sc[:, gl_cols], preferred_element_type=F32) + bgu_ref[:, gl_cols]
            lin = jnp.dot(x, wgu_sc[:, lin_cols], preferred_element_type=F32) + bgu_ref[:, lin_cols]
            return gl, lin

        y = bd_ref[...]
        nxt = up(0)
        for c in range(nblk):
            gl, lin = nxt
            if c + 1 < nblk:
                nxt = up(c + 1)
            gl = jnp.minimum(gl, SWIGLU_LIMIT)
            lin = jnp.clip(lin, -SWIGLU_LIMIT, SWIGLU_LIMIT)
            act = gl * _sigmoid(SWIGLU_ALPHA * gl) * (lin + 1.0)
            if fresh:
                wd_sc[c * fb:(c + 1) * fb, :] = wd_f32[c * fb:(c + 1) * fb, :].astype(BF16)
            y = y + jnp.dot(act.astype(BF16), wd_sc[c * fb:(c + 1) * fb, :], preferred_element_type=F32)
        ys_ref[0:rows, :] = y.astype(BF16)
        if rows < te:
            ys_ref[rows:te, :] = jnp.zeros((te - rows, ys_ref.shape[1]), BF16)

    valid_rows = rows_ref[j]
    for fresh in (True, False):
        for rows in (te, half):
            size_ok = (valid_rows > half) if rows == te else (valid_rows <= half)
            first_ok = changed if fresh else jnp.logical_not(changed)

            @pl.when(active & size_ok & first_ok)
            def _(fresh=fresh, rows=rows):
                if fresh:
                    @pl.when(j == 0)
                    def _():
                        for cp in weight_copies(te_ref[0]):
                            cp.start()

                    for cp in weight_copies(te_ref[j]):
                        cp.wait()
                run(rows, fresh)
                if fresh:
                    @pl.when(next_ref[j] >= 0)
                    def _():
                        for cp in weight_copies(next_ref[j]):
                            cp.start()

    @pl.when(jnp.logical_not(active))
    def _():
        ys_ref[...] = jnp.zeros_like(ys_ref)


def _experts(tile_expert, tile_rows, next_expert, n_used, xs, layer, w_gu, b_gu, w_down, b_down):
    n_slots, d = xs.shape
    depth, n_exp, _, ff2 = w_gu.shape
    ff = w_down.shape[2]
    te = EXPERT_TILE
    row_idx = lambda j, te_r, rows_r, nx_r, nu: (jnp.maximum(jnp.minimum(j, nu[0] - 1), 0), 0)
    b_idx = lambda j, te_r, rows_r, nx_r, nu: (layer, te_r[j], 0, 0)
    return pl.pallas_call(
        functools.partial(_expert_kernel, layer=layer),
        grid_spec=pltpu.PrefetchScalarGridSpec(
            num_scalar_prefetch=4,
            grid=(n_slots // te,),
            in_specs=[pl.BlockSpec((te, d), row_idx),
                      pl.BlockSpec(memory_space=pl.ANY),
                      pl.BlockSpec((None, None, 1, ff2), b_idx),
                      pl.BlockSpec(memory_space=pl.ANY),
                      pl.BlockSpec((None, None, 1, d), b_idx)],
            out_specs=pl.BlockSpec((te, d), lambda j, te_r, rows_r, nx_r, nu: (j, 0)),
            scratch_shapes=[pltpu.VMEM((d, ff2), F32), pltpu.VMEM((ff, d), F32),
                            pltpu.VMEM((d, ff2), BF16), pltpu.VMEM((ff, d), BF16),
                            pltpu.SemaphoreType.DMA((2,))]),
        out_shape=jax.ShapeDtypeStruct((n_slots, d), BF16),
        compiler_params=_params(1),
        name="moe_experts",
    )(tile_expert, tile_rows, next_expert, n_used, xs, w_gu, b_gu.reshape(depth, n_exp, 1, ff2),
      w_down, b_down.reshape(depth, n_exp, 1, d))


def _combine_kernel(tab_ref, col_ref, gate_ref, ys_ref, x_ref, mod_ref, lng_ref, lnb_ref, out_ref, stage, sem, *,
                    n_lat_tiles, tiles_per_batch, ctx_row, d, alpha):
    i = pl.program_id(0)
    nt = pl.num_programs(0)
    slot = i % 2
    kb = stage.shape[1]
    tm = x_ref.shape[0]

    def gather(buf_slot):
        def copy(slot_row, stage_row, rows):
            return pltpu.make_async_copy(ys_ref.at[pl.ds(slot_row, rows)],
                                         stage.at[buf_slot, pl.ds(stage_row, rows)], sem.at[buf_slot])
        return copy

    @pl.when(i == 0)
    def _():
        stage[...] = jnp.zeros_like(stage)
        _for_each_unit(tab_ref, 0, lambda *u: gather(0)(*u).start())

    @pl.when(i + 1 < nt)
    def _():
        _for_each_unit(tab_ref, i + 1, lambda *u: gather(1 - slot)(*u).start())

    _for_each_unit(tab_ref, i, lambda *u: gather(slot)(*u).wait())

    col = col_ref[...]
    gates = gate_ref[...]
    kblk = 2 * LANES

    def gate_weights(r0):
        c = r0 + lax.broadcasted_iota(I32, (tm, kblk), 1)
        w = jnp.zeros((tm, kblk), F32)
        for k in range(TOP_K):
            w = jnp.where(c == col[:, k:k + 1], gates[:, k:k + 1], w)
        return w.astype(BF16)

    y = jnp.zeros((tm, d), F32)
    nxt = gate_weights(0)
    for r0 in range(0, kb, kblk):
        w = nxt
        if r0 + kblk < kb:
            nxt = gate_weights(r0 + kblk)
        y = y + jnp.dot(w, stage[slot, r0:r0 + kblk, :], preferred_element_type=F32)
    row = _mod_row(i, n_lat_tiles, tiles_per_batch, ctx_row)
    gate_mlp = mod_ref[pl.ds(row, 1), 5 * d:6 * d]
    out_ref[...] = _layer_norm(alpha * x_ref[...] + gate_mlp * y, lng_ref[...], lnb_ref[...])


def _combine(tab, col, ys, gates, x1, mod, ln_g, ln_b, dims, alpha):
    B, S, C, D = dims
    t = x1.shape[0]
    tm = ROUTE_TILE
    kern = functools.partial(_combine_kernel, n_lat_tiles=B * S // tm, tiles_per_batch=S // tm, ctx_row=B,
                             d=D, alpha=alpha)
    return pl.pallas_call(
        kern,
        grid_spec=pltpu.PrefetchScalarGridSpec(
            num_scalar_prefetch=1,
            grid=(t // tm,),
            in_specs=[pl.BlockSpec((tm, LANES), lambda i, tab_r: (i, 0)),
                      pl.BlockSpec((tm, LANES), lambda i, tab_r: (i, 0)),
                      pl.BlockSpec(memory_space=pl.ANY),
                      pl.BlockSpec((tm, D), lambda i, tab_r: (i, 0)),
                      pl.BlockSpec(mod.shape, lambda i, tab_r: (0, 0)),
                      pl.BlockSpec((1, D), lambda i, tab_r: (0, 0)),
                      pl.BlockSpec((1, D), lambda i, tab_r: (0, 0))],
            out_specs=pl.BlockSpec((tm, D), lambda i, tab_r: (i, 0)),
            scratch_shapes=[pltpu.VMEM((2, STAGE_ROWS, D), BF16), pltpu.SemaphoreType.DMA((2,))]),
        out_shape=jax.ShapeDtypeStruct((t, D), F32),
        compiler_params=_params(1),
        name="moe_combine",
    )(tab, col, gates, ys, x1, mod, ln_g.reshape(1, D), ln_b.reshape(1, D))


def _moe(hx, topi, gates, sizes, x1, mod, ln_g, ln_b, layer, w_gu, b_gu, w_down, b_down, dims, alpha):
    t = hx.shape[0]
    n_exp = w_gu.shape[1]
    te = EXPERT_TILE
    n_route_tiles = t // ROUTE_TILE
    max_rows = t * TOP_K + (GROUP_ALIGN - 1) * n_exp * n_route_tiles
    n_tiles = -(-max_rows // te) + n_exp
    col, colt, tab, meta = _slots(topi, sizes, te)
    tab = tab.reshape(-1)
    tot, base, padded = meta[0, :n_exp], meta[1, :n_exp], meta[2, :n_exp]
    ends = jnp.cumsum(padded // te)
    n_used = ends[-1:].astype(I32)
    tile_ids = jnp.minimum(jnp.arange(n_tiles, dtype=I32), n_used[0] - 1)
    tile_expert = jnp.sum((tile_ids[:, None] >= ends[None, :]).astype(I32), axis=1)
    tile_expert = jnp.minimum(tile_expert, n_exp - 1).astype(I32)
    pad_start = (base + tot).astype(I32)
    pad_units = ((padded - tot) // GROUP_ALIGN).astype(I32)
    used_rows = n_used[0] * te
    tail = jnp.stack([used_rows, (n_tiles * te - used_rows) // ZERO_ROWS]).astype(I32)
    xs = _dispatch(tab, pad_start, pad_units, tail, colt, hx, n_tiles * te)
    experts = jnp.arange(n_exp, dtype=I32)
    of_tile = tile_expert[:, None] == experts[None, :]
    pick = lambda per_expert: jnp.sum(jnp.where(of_tile, per_expert[None, :], 0), axis=1).astype(I32)
    tile_rows = jnp.clip(pick(base + tot) - tile_ids * te, 0, te).astype(I32)
    later = (padded[None, :] > 0) & (experts[None, :] > experts[:, None])
    next_of = jnp.min(jnp.where(later, experts[None, :], n_exp), axis=1)
    next_expert = pick(jnp.where(next_of == n_exp, -1, next_of))
    ys = _experts(tile_expert, tile_rows, next_expert, n_used, xs, layer, w_gu, b_gu, w_down, b_down)
    return _combine(tab, col, ys, gates, x1, mod, ln_g, ln_b, dims, alpha)


def _ml_in_kernel(xp_ref, x_ref, xn_ref, mod_ref, wqk_ref, bqk_ref, wvo_ref, bvo_ref, wg_ref, bg_ref,
                  cw_ref, cb_ref, q_ref, k_ref, v_ref, og_ref, g_ref, *,
                  n_lat_tiles, tiles_per_batch, ctx_parts_per_seq, ctx_row, d):
    i = pl.program_id(0)
    tm = x_ref.shape[0]
    part = ML_IN_PART
    n_parts = tm // part
    halo = SUBLANES
    row = _mod_row(i, n_lat_tiles, tiles_per_batch, ctx_row)
    shift = mod_ref[pl.ds(row, 1), 0:d]
    scale = mod_ref[pl.ds(row, 1), d:2 * d]
    is_lat = i < n_lat_tiles
    nv = wvo_ref.shape[1] // 2
    n_ext = part + 2 * halo
    cw = cw_ref[...]

    def modulated(rows_ref, rows=slice(None)):
        return rows_ref[rows, :] * (1.0 + scale) + shift

    def part_stages(p):
        rows = slice(p * part, (p + 1) * part)
        seq_part = jnp.where(is_lat, (i % tiles_per_batch) * n_parts + p, p % ctx_parts_per_seq)
        seq_parts = jnp.where(is_lat, tiles_per_batch * n_parts, ctx_parts_per_seq)
        first = seq_part == 0
        last = seq_part == seq_parts - 1
        before = modulated(xp_ref) if p == 0 else modulated(x_ref, slice(p * part - halo, p * part))
        after = (modulated(xn_ref) if p == n_parts - 1
                 else modulated(x_ref, slice((p + 1) * part, (p + 1) * part + halo)))
        h = modulated(x_ref, rows)
        h_ext = jnp.concatenate([before, h, after], axis=0)
        z = jnp.dot(h_ext.astype(BF16), wqk_ref[...], preferred_element_type=F32) + bqk_ref[...]
        h_bf = h.astype(BF16)
        v_pre = jnp.dot(h_bf, wvo_ref[:, :nv], preferred_element_type=F32) + bvo_ref[:, :nv]
        o_pre = jnp.dot(h_bf, wvo_ref[:, nv:], preferred_element_type=F32) + bvo_ref[:, nv:]
        zg = _dot_split(h, wg_ref) + bg_ref[...]
        yield
        r = lax.broadcasted_iota(I32, z.shape, 0)
        z = jnp.where(((r < halo) & first) | ((r >= halo + part) & last), 0.0, z)
        acc = None
        for j in range(ML_CONV_W):
            sh = (ML_CONV_W // 2 - j) % n_ext
            zj = z if sh == 0 else pltpu.roll(z, sh, 0)
            term = zj[halo:halo + part] * cw[j:j + 1]
            acc = term if acc is None else acc + term
        qk = acc + cb_ref[...]
        qk = qk * _sigmoid(qk)
        nqk = qk.shape[1] // 2
        qk_dim = nqk // ML_HEADS
        q_ref[rows, :] = (qk[:, :nqk] * (qk_dim ** -0.5)).astype(BF16)
        k_ref[rows, :] = qk[:, nqk:].astype(BF16)
        yield
        v_ref[rows, :] = v_pre.astype(BF16)
        og_ref[rows, :] = _sigmoid(o_pre).astype(BF16)
        g = GATE_CAP * jnp.tanh(zg / GATE_CAP)
        log_sig = jnp.minimum(g, 0.0) - jnp.log(1.0 + jnp.exp(-jnp.abs(g)))
        lane = lax.broadcasted_iota(I32, g.shape, 1)
        is_forget = ((lane // ML_HEADS) % 2) == 1
        g_ref[rows, :] = jnp.where(is_forget, log_sig, g)

    running = [part_stages(p) for p in range(n_parts)]
    while running:
        running = [s for s in running if next(s, True) is None]


def _ml_in(x_all, mod, w_in, b_in, conv_w, conv_b, dims):
    B, S, C, D = dims
    t_all = x_all.shape[0]
    tm = 512
    part = ML_IN_PART
    assert C % part == 0 and S % tm == 0 and (B * C) % tm == 0
    n_lat_tiles = B * S // tm
    nqk2 = conv_w.shape[1]
    nv = (w_in.shape[1] - nqk2 - 4 * ML_HEADS) // 2
    ng = 4 * ML_HEADS
    w_qk = w_in[:, :nqk2].astype(BF16)
    w_vo = w_in[:, nqk2:nqk2 + 2 * nv].astype(BF16)
    w_g = _split_weight(jnp.zeros((D, LANES), F32).at[:, :ng].set(w_in[:, nqk2 + 2 * nv:]))
    b_qk = b_in[:nqk2].reshape(1, -1)
    b_vo = b_in[nqk2:nqk2 + 2 * nv].reshape(1, -1)
    b_g = jnp.zeros((1, LANES), F32).at[0, :ng].set(b_in[nqk2 + 2 * nv:])
    cw = jnp.zeros((SUBLANES, nqk2), F32).at[:ML_CONV_W].set(conv_w)
    cb = conv_b.reshape(1, -1)
    hb = tm // SUBLANES
    n_hblk = t_all // SUBLANES
    full = lambda a: pl.BlockSpec(a.shape, lambda i: (0,) * a.ndim)
    row_spec = lambda w: pl.BlockSpec((tm, w), lambda i: (i, 0))
    kern = functools.partial(_ml_in_kernel, n_lat_tiles=n_lat_tiles, tiles_per_batch=S // tm,
                             ctx_parts_per_seq=C // part, ctx_row=B, d=D)
    return pl.pallas_call(
        kern,
        grid=(t_all // tm,),
        in_specs=[pl.BlockSpec((SUBLANES, D), lambda i: (jnp.maximum(i * hb - 1, 0), 0)),
                  row_spec(D),
                  pl.BlockSpec((SUBLANES, D), lambda i: (jnp.minimum((i + 1) * hb, n_hblk - 1), 0)),
                  full(mod), full(w_qk), full(b_qk), full(w_vo), full(b_vo), full(w_g), full(b_g),
                  full(cw), full(cb)],
        out_specs=[row_spec(nqk2 // 2), row_spec(nqk2 // 2), row_spec(nv), row_spec(nv), row_spec(LANES)],
        out_shape=[jax.ShapeDtypeStruct((t_all, nqk2 // 2), BF16), jax.ShapeDtypeStruct((t_all, nqk2 // 2), BF16),
                   jax.ShapeDtypeStruct((t_all, nv), BF16), jax.ShapeDtypeStruct((t_all, nv), BF16),
                   jax.ShapeDtypeStruct((t_all, LANES), F32)],
        compiler_params=_params(1),
        name="mlstm_in",
    )(x_all, x_all, x_all, mod, w_qk, b_qk, w_vo, b_vo, w_g, b_g, cw, cb)


def _ml_chunk(q_ref, k_ref, v_ref, g_ref, h_ref, s_sc, m_sc, reverse):
    L = q_ref.shape[0]
    qk_dim = q_ref.shape[1] // ML_HEADS
    v_dim = v_ref.shape[1] // ML_HEADS
    nh = ML_HEADS
    gates_t = g_ref[...].T
    sr = lax.broadcasted_iota(I32, (L, L), 0)
    lc = lax.broadcasted_iota(I32, (L, L), 1)
    upto = (sr >= lc) if reverse else (sr <= lc)
    cum_t = jnp.dot(gates_t, jnp.where(upto, 1.0, 0.0), precision=HIGHEST, preferred_element_type=F32)
    off = 2 * nh if reverse else 0
    li = gates_t[off:off + nh]
    b = cum_t[off + nh:off + 2 * nh]
    c = li - b
    lane = lax.broadcasted_iota(I32, c.shape, 1)
    mu = c
    d = 1
    while d < L:
        if reverse:
            shifted = jnp.where(lane < L - d, pltpu.roll(mu, L - d, 1), -jnp.inf)
        else:
            shifted = jnp.where(lane >= d, pltpu.roll(mu, d, 1), -jnp.inf)
        mu = jnp.maximum(mu, shifted)
        d *= 2
    m_prev = m_sc[...]
    mu = jnp.maximum(mu, m_prev)
    m_t = b + mu
    end = 0 if reverse else L - 1
    mu_end = mu[:, end:end + 1]
    decay = jnp.exp(m_prev[:, 0:1] - mu_end)
    wk = jnp.exp(c - mu_end)
    m_sc[...] = jnp.broadcast_to(b[:, end:end + 1] + mu_end, m_prev.shape)
    cols = jnp.concatenate([mu, m_t, jnp.zeros((L - 2 * nh, L), F32)], axis=0).T
    k_t = k_ref[...].astype(F32).T
    rl = lax.broadcasted_iota(I32, (L, L), 0)
    cs = lax.broadcasted_iota(I32, (L, L), 1)
    allowed = (cs >= rl) if reverse else (cs <= rl)
    ones = jnp.ones((L, v_dim), BF16)

    def head(h):
        qh = q_ref[:, h * qk_dim:(h + 1) * qk_dim]
        kt_h = k_t[h * qk_dim:(h + 1) * qk_dim, :]
        qk = jnp.dot(qh, kt_h.astype(BF16), preferred_element_type=F32)
        state = s_sc[h]
        q_state = jnp.dot(qh, state.astype(BF16), preferred_element_type=F32)
        yield
        mu_col = jnp.broadcast_to(cols[:, h:h + 1], (L, L))
        p = jnp.exp(jnp.where(allowed, c[h:h + 1, :] - mu_col, -jnp.inf))
        w_inter = jnp.exp(m_prev[h:h + 1, :] - mu_col)
        s = (qk * p).astype(BF16)
        yield
        v_aug = jnp.concatenate([v_ref[:, h * v_dim:(h + 1) * v_dim], ones], axis=1)
        nd = jnp.dot(s, v_aug, preferred_element_type=F32)
        kw_t = (kt_h * wk[h:h + 1, :]).astype(BF16)
        new_state = decay[h:h + 1, :] * state + jnp.dot(kw_t, v_aug, preferred_element_type=F32)
        yield
        mt_col = jnp.broadcast_to(cols[:, nh + h:nh + h + 1], (L, v_dim))
        nd = nd + jnp.concatenate([w_inter, w_inter], axis=1) * q_state
        num = nd[:, :v_dim]
        den = nd[:, v_dim:]
        h_ref[:, h * v_dim:(h + 1) * v_dim] = num / jnp.maximum(jnp.abs(den), jnp.exp(-mt_col))
        s_sc[h] = new_state

    return head


def _ml_scan_kernel(*refs, n_batch):
    n_in = 8 * n_batch
    hf_ref, hb_ref, s_sc, m_sc = refs[n_in:]

    @pl.when(pl.program_id(0) == 0)
    def _():
        s_sc[...] = jnp.zeros_like(s_sc)
        m_sc[...] = jnp.zeros_like(m_sc)

    heads = []
    for b in range(n_batch):
        qf, kf, vf, gf, qb, kb, vb, gb = refs[8 * b:8 * b + 8]
        heads.append(_ml_chunk(qf, kf, vf, gf, hf_ref.at[b], s_sc.at[b, 0], m_sc.at[b, 0], False))
        heads.append(_ml_chunk(qb, kb, vb, gb, hb_ref.at[b], s_sc.at[b, 1], m_sc.at[b, 1], True))
    for h in range(ML_HEADS):
        running = [head(h) for head in heads]
        while running:
            running = [g for g in running if next(g, True) is None]


def _ml_scan(q, k, v, g, dims):
    B, S, C, D = dims
    L = ML_CHUNK
    assert L == LANES
    nc_ctx = C // L
    nc_lat = S // L
    qk_dim = q.shape[1] // ML_HEADS
    v_dim = v.shape[1] // ML_HEADS

    def in_index(b, reverse):
        def idx(c):
            in_ctx = c < nc_ctx
            cl = c - nc_ctx
            if reverse:
                ctx_blk = (B * S + b * C) // L + (nc_ctx - 1 - c)
                lat_blk = (b * S) // L + (nc_lat - 1 - cl)
            else:
                ctx_blk = (B * S + b * C) // L + c
                lat_blk = (b * S) // L + cl
            return (jnp.where(in_ctx, ctx_blk, lat_blk), 0)
        return idx

    def out_index(reverse):
        def idx(c):
            cl = jnp.maximum(c - nc_ctx, 0)
            return (0, nc_lat - 1 - cl if reverse else cl, 0)
        return idx

    widths = (q.shape[1], k.shape[1], v.shape[1], LANES)
    in_specs, args = [], []
    for b in range(B):
        for rev in (False, True):
            in_specs += [pl.BlockSpec((L, w), in_index(b, rev)) for w in widths]
            args += [q, k, v, g]
    out = jax.ShapeDtypeStruct((B, S, v.shape[1]), F32)
    return pl.pallas_call(
        functools.partial(_ml_scan_kernel, n_batch=B),
        grid=(nc_ctx + nc_lat,),
        in_specs=in_specs,
        out_specs=[pl.BlockSpec((B, L, v.shape[1]), out_index(False)),
                   pl.BlockSpec((B, L, v.shape[1]), out_index(True))],
        out_shape=[out, out],
        scratch_shapes=[pltpu.VMEM((B, 2, ML_HEADS, qk_dim, 2 * v_dim), F32),
                        pltpu.VMEM((B, 2, ML_HEADS, LANES), F32)],
        compiler_params=_params(1),
        name="mlstm_scan",
    )(*args)


def kernel(x, c, ctx, c_ctx, ada_w, ada_b, ln_g, ln_b, attn_w_qkv, attn_b_qkv, attn_sink, attn_w_o, attn_b_o,
           ml_w_in, ml_b_in, ml_conv_w, ml_conv_b, ml_norm_g, ml_w_out, router_w, router_b,
           exp_w_gu, exp_b_gu, exp_w_down, exp_b_down):
    B, S, D = x.shape
    C = ctx.shape[1]
    depth = ada_w.shape[0]
    dims = (B, S, C, D)
    alpha = (2.0 * depth) ** 0.25
    n_lat = B * S

    cvec = jnp.zeros((SUBLANES, D), F32).at[:B].set(c).at[B].set(c_ctx)
    mods = _adaln(cvec, ada_w, ada_b)
    x_lat, x_ctx = x.reshape(n_lat, D), ctx.reshape(B * C, D)

    q, k, v = _attn_qkv(x_lat, x_ctx, mods[0], attn_w_qkv[0], attn_b_qkv[0], dims)
    o = _attention(q, k, v, attn_sink[0], dims)
    x1, hx, topi, gates, sizes = _post_mixer((o,), None, (x_lat, x_ctx), mods[0], attn_w_o[0], attn_b_o[0],
                                      ln_g[0, 0], ln_b[0, 0], router_w[0], router_b[0], dims, n_lat + B * C, alpha)
    x_all = _moe(hx, topi, gates, sizes, x1, mods[0], ln_g[0, 1], ln_b[0, 1],
                 0, exp_w_gu, exp_b_gu, exp_w_down, exp_b_down, dims, alpha)

    q, k, v, og, g = _ml_in(x_all, mods[1], ml_w_in[0], ml_b_in[0], ml_conv_w[0], ml_conv_b[0], dims)
    hf, hb = (h.reshape(n_lat, -1) for h in _ml_scan(q, k, v, g, dims))
    zero_b = jnp.zeros((D,), F32)
    x1, hx, topi, gates, sizes = _post_mixer((hf, hb, og), ml_norm_g[0], x_all, mods[1], ml_w_out[0], zero_b,
                                      ln_g[1, 0], ln_b[1, 0], router_w[1], router_b[1], dims, n_lat, alpha)
    out = _moe(hx, topi, gates, sizes, x1, mods[1], ln_g[1, 1], ln_b[1, 1],
               1, exp_w_gu, exp_b_gu, exp_w_down, exp_b_down, dims, alpha)
    return out.reshape(B, S, D)
```

```python
import functools

import jax
import jax.numpy as jnp
from jax import lax
from jax.experimental import pallas as pl
from jax.experimental.pallas import tpu as pltpu

F32 = jnp.float32
BF16 = jnp.bfloat16
I32 = jnp.int32
HIGHEST = lax.Precision.HIGHEST

GRID_W = 64
ATTN_HEAD_DIM = 64
ATTN_KV_HEADS = 4
WINDOW = 128
ATTN_BLOCK = 128
ROPE_THETA = 10000.0
ML_HEADS = 8
ML_CONV_W = 5
GATE_CAP = 15.0
TOP_K = 4
SWIGLU_ALPHA = 1.702
SWIGLU_LIMIT = 7.0
LN_EPS = 1e-5

LANES = 128
SUBLANES = 8
VMEM_LIMIT = 56 * 1024 * 1024
EXPERT_TILE = 512
ROUTE_TILE = 512
POST_PART = 256
ML_CHUNK = 128
GROUP_ALIGN = 16
BIG_UNIT = 2 * GROUP_ALIGN
STAGE_ROWS = ROUTE_TILE * TOP_K + 4 * LANES
TABLE_W = 4 * LANES
ZERO_ROWS = 512
ML_IN_PART = 256


def _params(n_axes, vmem=VMEM_LIMIT):
    return pltpu.CompilerParams(dimension_semantics=("arbitrary",) * n_axes, vmem_limit_bytes=vmem)


def _layer_norm(r, g, b):
    mu = jnp.mean(r, axis=-1, keepdims=True)
    rc = r - mu
    var = jnp.mean(rc * rc, axis=-1, keepdims=True)
    return rc * lax.rsqrt(var + LN_EPS) * g + b


def _sigmoid(x):
    return 1.0 / (1.0 + jnp.exp(-x))


def _mod_row(i, n_lat_tiles, tiles_per_batch, ctx_row):
    return jnp.where(i < n_lat_tiles, i // tiles_per_batch, ctx_row)


def _adaln_kernel(c_ref, w_ref, b_ref, o_ref):
    c = c_ref[...]
    s = c * _sigmoid(c)
    o_ref[0] = jnp.dot(s, w_ref[0], precision=HIGHEST, preferred_element_type=F32) + b_ref[0]


def _adaln(cvec, ada_w, ada_b):
    depth, d, n = ada_w.shape
    tn = 1536
    return pl.pallas_call(
        _adaln_kernel,
        grid=(depth, n // tn),
        in_specs=[pl.BlockSpec((SUBLANES, d), lambda l, j: (0, 0)),
                  pl.BlockSpec((1, d, tn), lambda l, j: (l, 0, j)),
                  pl.BlockSpec((1, 1, tn), lambda l, j: (l, 0, j))],
        out_specs=pl.BlockSpec((1, SUBLANES, tn), lambda l, j: (l, 0, j)),
        out_shape=jax.ShapeDtypeStruct((depth, SUBLANES, n), F32),
        compiler_params=_params(2),
        name="adaln",
    )(cvec, ada_w, ada_b.reshape(depth, 1, n))


def _token_specs(tm, d, n_lat_tiles):
    return [pl.BlockSpec((tm, d), lambda i, *_: (jnp.minimum(i, n_lat_tiles - 1), 0)),
            pl.BlockSpec((tm, d), lambda i, *_: (jnp.maximum(i - n_lat_tiles, 0), 0))]


def _qkv_kernel(x_ref, c_ref, mod_ref, w_ref, b_ref, cos_ref, sin_ref, q_ref, k_ref, vt_ref, *,
                n_lat_tiles, tiles_per_batch, ctx_row, d, qd, kvd):
    i = pl.program_id(0)
    row = _mod_row(i, n_lat_tiles, tiles_per_batch, ctx_row)
    shift = mod_ref[pl.ds(row, 1), 0:d]
    scale = mod_ref[pl.ds(row, 1), d:2 * d]
    x = jnp.where(i < n_lat_tiles, x_ref[...], c_ref[...])
    h = x * (1.0 + scale) + shift
    z = jnp.dot(h.astype(BF16), w_ref[...], preferred_element_type=F32) + b_ref[...]
    nrot = qd + kvd
    qk = z[:, :nrot]
    reps = nrot // LANES
    cos = jnp.concatenate([cos_ref[...]] * reps, axis=1)
    sin = jnp.concatenate([sin_ref[...]] * reps, axis=1)
    lane = lax.broadcasted_iota(I32, qk.shape, 1)
    low_half = (lane & 16) == 0
    partner = jnp.where(low_half, pltpu.roll(qk, nrot - 16, 1), pltpu.roll(qk, 16, 1))
    qk = qk * cos + partner * sin
    q_ref[...] = (qk[:, :qd] * (ATTN_HEAD_DIM ** -0.5)).astype(BF16)
    k_ref[...] = qk[:, qd:].astype(BF16)
    vt_ref[...] = z[:, nrot:].T.astype(BF16)


def _rope_tables(s_len, tm):
    half = ATTN_HEAD_DIM // 4
    freqs = ROPE_THETA ** (-jnp.arange(half, dtype=F32) / half)
    t = jnp.arange(s_len)
    rows = (t // GRID_W).astype(F32)[:, None] * freqs[None, :]
    cols = (t % GRID_W).astype(F32)[:, None] * freqs[None, :]
    ang = jnp.concatenate([rows, rows, cols, cols], axis=1)
    sign = jnp.tile(jnp.concatenate([-jnp.ones((half,), F32), jnp.ones((half,), F32)]), 2)
    cos = jnp.cos(ang)
    sin = jnp.sin(ang) * sign[None, :]
    reps = LANES // ATTN_HEAD_DIM
    cos = jnp.concatenate([jnp.tile(cos, (1, reps)), jnp.ones((tm, LANES), F32)], axis=0)
    sin = jnp.concatenate([jnp.tile(sin, (1, reps)), jnp.zeros((tm, LANES), F32)], axis=0)
    return cos, sin


def _attn_qkv(x_lat, x_ctx, mod, w_qkv, b_qkv, dims):
    B, S, C, D = dims
    t_all = x_lat.shape[0] + x_ctx.shape[0]
    tm = 512
    n_lat_tiles = B * S // tm
    tiles_per_batch = S // tm
    ncols = w_qkv.shape[1]
    kvd = ATTN_KV_HEADS * ATTN_HEAD_DIM
    qd = ncols - 2 * kvd
    cos, sin = _rope_tables(S, tm)

    def tab_idx(i):
        return (jnp.where(i < n_lat_tiles, i % tiles_per_batch, tiles_per_batch), 0)

    kern = functools.partial(_qkv_kernel, n_lat_tiles=n_lat_tiles, tiles_per_batch=tiles_per_batch,
                             ctx_row=B, d=D, qd=qd, kvd=kvd)
    return pl.pallas_call(
        kern,
        grid=(t_all // tm,),
        in_specs=_token_specs(tm, D, n_lat_tiles) + [
                  pl.BlockSpec(mod.shape, lambda i: (0, 0)),
                  pl.BlockSpec((D, ncols), lambda i: (0, 0)),
                  pl.BlockSpec((1, ncols), lambda i: (0, 0)),
                  pl.BlockSpec((tm, LANES), tab_idx),
                  pl.BlockSpec((tm, LANES), tab_idx)],
        out_specs=[pl.BlockSpec((tm, qd), lambda i: (i, 0)),
                   pl.BlockSpec((tm, kvd), lambda i: (i, 0)),
                   pl.BlockSpec((kvd, tm), lambda i: (0, i))],
        out_shape=[jax.ShapeDtypeStruct((t_all, qd), BF16),
                   jax.ShapeDtypeStruct((t_all, kvd), BF16),
                   jax.ShapeDtypeStruct((kvd, t_all), BF16)],
        compiler_params=_params(1),
        name="attn_qkv",
    )(x_lat, x_ctx, mod, w_qkv.astype(BF16), b_qkv.reshape(1, ncols), cos, sin)


def _attn_kernel(sink_ref, q_ref, kp_ref, ko_ref, kn_ref, kc_ref, vp_ref, vo_ref, vn_ref, vc_ref, o_ref,
                 bias_sc, s_sc, p_sc, ot_sc, *,
                 n_lat_steps, nb, s_len, c_len):
    j = pl.program_id(0)
    is_lat = j < n_lat_steps
    n = j % nb
    blk = ATTN_BLOCK
    nloc = 3 * blk
    nk = nloc + c_len
    def kind(step):
        m = step % nb
        return jnp.where(step < n_lat_steps, jnp.where(m == 0, 1, jnp.where(m == nb - 1, 2, 0)), 3)

    @pl.when((j == 0) | (kind(j) != kind(jnp.maximum(j - 1, 0))))
    def _():
        ki = lax.broadcasted_iota(I32, (nloc, blk), 0)
        qj = lax.broadcasted_iota(I32, (nloc, blk), 1)
        kpos = n * blk - WINDOW + ki
        qpos = n * blk + qj
        local_ok = (jnp.abs(kpos - qpos) <= WINDOW) & (kpos >= 0) & (kpos < s_len) & is_lat
        bias_sc[...] = jnp.where(local_ok, 0.0, -jnp.inf)
    hd = ATTN_HEAD_DIM
    group = q_ref.shape[1] // (ATTN_KV_HEADS * hd)
    kcat = jnp.concatenate([kp_ref[...], ko_ref[...], kn_ref[...], kc_ref[...]], axis=0)
    vcat_t = jnp.concatenate([vp_ref[...], vo_ref[...], vn_ref[...], vc_ref[...]], axis=1)
    def score_matmul(kh):
        q_grp = jnp.concatenate([q_ref[:, (kh * group + g) * hd:(kh * group + g + 1) * hd]
                                 for g in range(group)], axis=0)
        s_sc[kh] = lax.dot_general(kcat[:, kh * hd:(kh + 1) * hd], q_grp, (((1,), (1,)), ((), ())),
                                   preferred_element_type=F32)

    def weighted_values(kh, sink_terms):
        v_ones = jnp.concatenate([vcat_t[kh * hd:(kh + 1) * hd, :], jnp.ones((SUBLANES, nk), BF16)], axis=0)
        o_aug = jnp.dot(v_ones, p_sc[kh], preferred_element_type=F32)
        l = o_aug[hd:hd + 1, :] + jnp.concatenate(sink_terms, axis=1)
        o_t = o_aug[0:hd, :] * (1.0 / l)
        for g in range(group):
            h = kh * group + g
            ot_sc[h * hd:(h + 1) * hd, :] = o_t[:, g * blk:(g + 1) * blk]

    score_matmul(0)
    for kh in range(ATTN_KV_HEADS):
        if kh + 1 < ATTN_KV_HEADS:
            score_matmul(kh + 1)
        sink_terms = []
        for g in range(group):
            cols = slice(g * blk, (g + 1) * blk)
            def scores(a):
                s = s_sc[kh, a:a + blk, cols]
                return s + bias_sc[a:a + blk, :] if a < nloc else s

            sk = sink_ref[kh * group + g]
            top = scores(0)
            for a in range(blk, nk, blk):
                top = jnp.maximum(top, scores(a))
            m = jnp.maximum(jnp.max(top, axis=0, keepdims=True), sk)
            for a in range(0, nk, blk):
                p_sc[kh, a:a + blk, cols] = jnp.exp((scores(a) - m).astype(BF16))
            sink_terms.append(jnp.exp(sk - m))
        weighted_values(kh, sink_terms)
    o_ref[...] = ot_sc[...].T.astype(BF16)


def _attention(q_all, k_all, vt_all, sink, dims):
    B, S, C, D = dims
    blk = ATTN_BLOCK
    nb = S // blk
    n_lat_steps = B * nb
    ctx_steps_per_batch = C // blk
    n_steps = n_lat_steps + B * ctx_steps_per_batch
    qd = q_all.shape[1]
    kvd = k_all.shape[1]

    def local_idx(off):
        def idx(j):
            b = j // nb
            nn = jnp.clip(j % nb + off, 0, nb - 1)
            return (jnp.where(j < n_lat_steps, b * nb + nn, j), 0)
        return idx

    def ctx_idx(j):
        b = jnp.where(j < n_lat_steps, j // nb, (j - n_lat_steps) // ctx_steps_per_batch)
        return (B * S // C + b, 0)

    swap = lambda f: (lambda j: f(j)[::-1])
    loc = lambda off: pl.BlockSpec((blk, kvd), local_idx(off))
    ctxs = pl.BlockSpec((C, kvd), ctx_idx)
    loc_t = lambda off: pl.BlockSpec((kvd, blk), swap(local_idx(off)))
    ctxs_t = pl.BlockSpec((kvd, C), swap(ctx_idx))
    group = qd // kvd
    nk = 3 * blk + C
    kern = functools.partial(_attn_kernel, n_lat_steps=n_lat_steps, nb=nb, s_len=S, c_len=C)
    return pl.pallas_call(
        kern,
        grid=(n_steps,),
        in_specs=[pl.BlockSpec(memory_space=pltpu.SMEM),
                  pl.BlockSpec((blk, qd), lambda j: (j, 0)),
                  loc(-1), loc(0), loc(1), ctxs,
                  loc_t(-1), loc_t(0), loc_t(1), ctxs_t],
        out_specs=pl.BlockSpec((blk, qd), lambda j: (j, 0)),
        out_shape=jax.ShapeDtypeStruct((q_all.shape[0], qd), BF16),
        scratch_shapes=[pltpu.VMEM((3 * blk, blk), F32),
                        pltpu.VMEM((ATTN_KV_HEADS, nk, group * blk), F32),
                        pltpu.VMEM((ATTN_KV_HEADS, nk, group * blk), BF16),
                        pltpu.VMEM((qd, blk), F32)],
        compiler_params=_params(1),
        name="attn_core",
    )(sink, q_all, k_all, k_all, k_all, k_all, vt_all, vt_all, vt_all, vt_all)


def _split_bf16(a):
    hi = a.astype(BF16)
    return hi, (a - hi.astype(F32)).astype(BF16)


def _split_weight(w_f32):
    hi = w_f32.astype(BF16)
    return jnp.concatenate([hi, (w_f32 - hi.astype(F32)).astype(BF16)], axis=1)


def _dot_split(a, w_ref):
    a_hi, a_lo = _split_bf16(a)
    n = w_ref.shape[1] // 2
    both = jnp.dot(a_hi, w_ref[...], preferred_element_type=F32)
    return both[:, :n] + both[:, n:] + jnp.dot(a_lo, w_ref[:, :n], preferred_element_type=F32)


def _top_k_route(logits, rows, topi_ref, gate_ref, sizes_ref, part):
    lane = lax.broadcasted_iota(I32, logits.shape, 1)
    lanef = lane.astype(F32)
    vals, idxs = [], []
    l = logits
    for _ in range(TOP_K):
        m = jnp.max(l, axis=1, keepdims=True)
        idx = jnp.min(jnp.where(l == m, lanef, float(LANES)), axis=1, keepdims=True)
        vals.append(m)
        idxs.append(idx)
        l = jnp.where(lanef == idx, -jnp.inf, l)
    es = [jnp.exp(v - vals[0]) for v in vals]
    denom = es[0]
    for e in es[1:]:
        denom = denom + e
    topi = jnp.zeros(logits.shape, F32)
    gates = jnp.zeros(logits.shape, F32)
    chosen = jnp.zeros(logits.shape, F32)
    for k in range(TOP_K):
        topi = jnp.where(lane == k, idxs[k], topi)
        gates = jnp.where(lane == k, es[k] / denom, gates)
        chosen = chosen + jnp.where(lanef == idxs[k], 1.0, 0.0)
    topi_ref[rows, :] = topi.astype(I32)
    gate_ref[rows, :] = gates
    sizes_ref[0, part:part + 1, :] = jnp.sum(chosen, axis=0, keepdims=True)


def _post_common(a_fn, x_fn, mod_ref, w_ref, b_ref, lng_ref, lnb_ref, rw_ref, rb_ref,
                 x1_ref, hx_ref, topi_ref, gate_ref, sizes_ref, row, d, alpha):
    gate_mix = mod_ref[pl.ds(row, 1), 2 * d:3 * d]
    shift = mod_ref[pl.ds(row, 1), 3 * d:4 * d]
    scale = mod_ref[pl.ds(row, 1), 4 * d:5 * d]
    sizes_ref[...] = jnp.zeros_like(sizes_ref)

    def part_stages(part):
        rows = slice(part * POST_PART, (part + 1) * POST_PART)
        y = jnp.dot(a_fn(rows), w_ref[...], preferred_element_type=F32) + b_ref[...]
        yield
        x1 = _layer_norm(alpha * x_fn(rows) + gate_mix * y, lng_ref[...], lnb_ref[...])
        hx = x1 * (1.0 + scale) + shift
        x1_ref[rows, :] = x1
        hx_ref[rows, :] = hx.astype(BF16)
        logits = _dot_split(hx, rw_ref) + rb_ref[...]
        yield
        _top_k_route(logits, rows, topi_ref, gate_ref, sizes_ref, part)

    running = [part_stages(p) for p in range(x1_ref.shape[0] // POST_PART)]
    while running:
        running = [g for g in running if next(g, True) is None]


def _post_attn_kernel(o_ref, x_ref, c_ref, mod_ref, w_ref, b_ref, lng_ref, lnb_ref, rw_ref, rb_ref,
                      x1_ref, hx_ref, topi_ref, gate_ref, sizes_ref, *, n_lat_tiles, tiles_per_batch, ctx_row, d, alpha):
    i = pl.program_id(0)
    row = _mod_row(i, n_lat_tiles, tiles_per_batch, ctx_row)
    is_lat = i < n_lat_tiles
    _post_common(lambda rows: o_ref[rows, :],
                 lambda rows: jnp.where(is_lat, x_ref[rows, :], c_ref[rows, :]),
                 mod_ref, w_ref, b_ref, lng_ref, lnb_ref, rw_ref, rb_ref,
                 x1_ref, hx_ref, topi_ref, gate_ref, sizes_ref, row, d, alpha)


def _post_mlstm_kernel(hf_ref, hb_ref, og_ref, ng_ref, x_ref, mod_ref, w_ref, b_ref, lng_ref, lnb_ref,
                       rw_ref, rb_ref, x1_ref, hx_ref, topi_ref, gate_ref, sizes_ref, *,
                       n_lat_tiles, tiles_per_batch, ctx_row, d, alpha):
    row = _mod_row(pl.program_id(0), n_lat_tiles, tiles_per_batch, ctx_row)

    def gated_head_norm(rows):
        hsum = hf_ref[rows, :] + hb_ref[rows, :]
        vdim = hsum.shape[1] // ML_HEADS
        parts = []
        for h in range(ML_HEADS):
            seg = hsum[:, h * vdim:(h + 1) * vdim]
            mu = jnp.mean(seg, axis=1, keepdims=True)
            sc = seg - mu
            var = jnp.mean(sc * sc, axis=1, keepdims=True)
            parts.append(sc * lax.rsqrt(var + LN_EPS))
        y = jnp.concatenate(parts, axis=1) * ng_ref[...]
        return (og_ref[rows, :].astype(F32) * y).astype(BF16)

    _post_common(gated_head_norm, lambda rows: x_ref[rows, :],
                 mod_ref, w_ref, b_ref, lng_ref, lnb_ref, rw_ref, rb_ref,
                 x1_ref, hx_ref, topi_ref, gate_ref, sizes_ref, row, d, alpha)


def _post_mixer(mixer_inputs, norm_g, x_all, mod, w_o, b_o, ln_g, ln_b, router_w, router_b, dims, n_rows, alpha):
    B, S, C, D = dims
    tm = ROUTE_TILE
    n_lat_tiles = B * S // tm
    tiles_per_batch = S // tm
    n_exp = router_w.shape[1]
    rw = _split_weight(jnp.zeros((D, LANES), F32).at[:, :n_exp].set(router_w))
    rb = jnp.full((1, LANES), -1e30, F32).at[0, :n_exp].set(router_b)
    row_spec = lambda w: pl.BlockSpec((tm, w), lambda i: (i, 0))
    full = lambda a: pl.BlockSpec(a.shape, lambda i: (0,) * a.ndim)
    common = dict(n_lat_tiles=n_lat_tiles, tiles_per_batch=tiles_per_batch, ctx_row=B, d=D, alpha=alpha)
    w_bf = w_o.astype(BF16)
    b2 = b_o.reshape(1, D)
    if isinstance(x_all, tuple):
        x_args, x_specs = list(x_all), _token_specs(tm, D, n_lat_tiles)
    else:
        x_args, x_specs = [x_all], [row_spec(D)]
    tail = x_args + [mod, w_bf, b2, ln_g.reshape(1, D), ln_b.reshape(1, D), rw, rb]
    tail_specs = x_specs + [full(mod), full(w_bf), full(b2), pl.BlockSpec((1, D), lambda i: (0, 0)),
                            pl.BlockSpec((1, D), lambda i: (0, 0)), full(rw), full(rb)]
    if norm_g is None:
        kern = functools.partial(_post_attn_kernel, **common)
        args = list(mixer_inputs) + tail
        specs = [row_spec(mixer_inputs[0].shape[1])] + tail_specs
        name = "post_attn"
    else:
        kern = functools.partial(_post_mlstm_kernel, **common)
        ng = norm_g.reshape(1, -1)
        args = list(mixer_inputs) + [ng] + tail
        specs = [row_spec(a.shape[1]) for a in mixer_inputs] + [full(ng)] + tail_specs
        name = "post_mlstm"
    return pl.pallas_call(
        kern,
        grid=(n_rows // tm,),
        in_specs=specs,
        out_specs=[row_spec(D), row_spec(D), row_spec(LANES), row_spec(LANES),
                   pl.BlockSpec((1, SUBLANES, LANES), lambda i: (i, 0, 0))],
        out_shape=[jax.ShapeDtypeStruct((n_rows, D), F32), jax.ShapeDtypeStruct((n_rows, D), BF16),
                   jax.ShapeDtypeStruct((n_rows, LANES), I32), jax.ShapeDtypeStruct((n_rows, LANES), F32),
                   jax.ShapeDtypeStruct((n_rows // tm, SUBLANES, LANES), F32)],
        compiler_params=_params(1),
        name=name,
    )(*args)


def _exclusive_lane_cumsum(row):
    r = lax.broadcasted_iota(I32, (LANES, LANES), 0)
    c = lax.broadcasted_iota(I32, (LANES, LANES), 1)
    before = jnp.where(r < c, 1.0, 0.0)
    return jnp.dot(jnp.broadcast_to(row, (SUBLANES, LANES)), before,
                   precision=HIGHEST, preferred_element_type=F32)[0:1]


def _for_each_unit(tab_ref, tile, fn):
    base = tile * TABLE_W
    for blk, rows in ((0, BIG_UNIT), (2, GROUP_ALIGN)):
        def body(j, carry, blk=blk, rows=rows):
            slot_row = pl.multiple_of(tab_ref[base + blk * LANES + j], GROUP_ALIGN)
            stage_row = pl.multiple_of(tab_ref[base + (blk + 1) * LANES + j], GROUP_ALIGN)
            fn(slot_row, stage_row, rows)
            return carry
        lax.fori_loop(0, tab_ref[base + blk * LANES + LANES - 1], body, 0)


def _slots_kernel(topi_ref, sizes_ref, col_ref, colt_ref, tab_ref, meta_ref, base_sc, carry_sc, *, tile):
    i = pl.program_id(0)
    tm = topi_ref.shape[0]
    topi = topi_ref[...]
    lane = lax.broadcasted_iota(I32, topi.shape, 1)
    sel = [lane == topi[:, k:k + 1] for k in range(TOP_K)]
    maskf = jnp.where(sel[0], 1.0, 0.0)
    for s in sel[1:]:
        maskf = maskf + jnp.where(s, 1.0, 0.0)
    n8 = jnp.ceil(jnp.sum(maskf, axis=0, keepdims=True) / GROUP_ALIGN) * GROUP_ALIGN

    @pl.when(i == 0)
    def _():
        groups = jnp.ceil(jnp.sum(sizes_ref[...], axis=1) / GROUP_ALIGN) * GROUP_ALIGN
        tot = jnp.sum(groups, axis=0, keepdims=True)
        padded = jnp.ceil(tot / tile) * tile
        base = _exclusive_lane_cumsum(padded)
        base_sc[...] = base
        carry_sc[...] = jnp.zeros_like(carry_sc)
        rowi = lax.broadcasted_iota(I32, meta_ref.shape, 0)
        meta_ref[...] = jnp.where(rowi == 0, tot, jnp.where(rowi == 1, base, padded)).astype(I32)

    start = base_sc[...] + carry_sc[...]
    off = _exclusive_lane_cumsum(n8)
    r = lax.broadcasted_iota(I32, (tm, tm), 0)
    c = lax.broadcasted_iota(I32, (tm, tm), 1)
    earlier = jnp.where(c < r, 1.0, 0.0).astype(BF16)
    rank = jnp.dot(earlier, maskf.astype(BF16), preferred_element_type=F32)
    stage_row = rank + off
    out = jnp.zeros(topi.shape, F32)
    for k in range(TOP_K):
        pk = jnp.sum(jnp.where(sel[k], stage_row, 0.0), axis=1, keepdims=True)
        out = jnp.where(lane == k, pk, out)
    col_ref[...] = out.astype(I32)
    colt_ref[0] = out.T[0:SUBLANES].astype(I32)

    n_big = jnp.floor(n8 / BIG_UNIT)
    n_small = (n8 - BIG_UNIT * n_big) / GROUP_ALIGN
    first_big = _exclusive_lane_cumsum(n_big)
    first_small = _exclusive_lane_cumsum(n_small)
    rows = jnp.concatenate([off, start, first_big, n_big, first_small, n_small,
                            jnp.zeros((LANES - 6, LANES), F32)], axis=0)
    per_expert = rows.T
    off_c, start_c, fb_c, nb_c, fs_c, ns_c = (per_expert[:, k:k + 1] for k in range(6))
    u = lax.broadcasted_iota(I32, (LANES, LANES), 1).astype(F32)
    lane_t = lax.broadcasted_iota(I32, (1, LANES), 1)

    def unit_list(first_c, count_c, rel, count_row):
        inside = (u >= first_c) & (u < first_c + count_c)
        src = jnp.sum(jnp.where(inside, start_c + rel, 0.0), axis=0, keepdims=True)
        dst = jnp.sum(jnp.where(inside, off_c + rel, 0.0), axis=0, keepdims=True)
        total = jnp.sum(count_row, axis=1, keepdims=True)
        return [jnp.where(lane_t == LANES - 1, total, src), dst]

    lists = (unit_list(fb_c, nb_c, BIG_UNIT * (u - fb_c), n_big)
             + unit_list(fs_c, ns_c, BIG_UNIT * nb_c + GROUP_ALIGN * (u - fs_c), n_small))
    tab_ref[0] = jnp.concatenate(lists, axis=1).astype(I32)
    carry_sc[...] = carry_sc[...] + n8


def _slots(topi, sizes, tile):
    t = topi.shape[0]
    tm = ROUTE_TILE
    nt = t // tm
    return pl.pallas_call(
        functools.partial(_slots_kernel, tile=float(tile)),
        grid=(nt,),
        in_specs=[pl.BlockSpec((tm, LANES), lambda i: (i, 0)),
                  pl.BlockSpec(sizes.shape, lambda i: (0, 0, 0))],
        out_specs=[pl.BlockSpec((tm, LANES), lambda i: (i, 0)),
                   pl.BlockSpec((1, SUBLANES, tm), lambda i: (i, 0, 0)),
                   pl.BlockSpec((1, 1, TABLE_W), lambda i: (i, 0, 0)),
                   pl.BlockSpec((SUBLANES, LANES), lambda i: (0, 0))],
        out_shape=[jax.ShapeDtypeStruct((t, LANES), I32),
                   jax.ShapeDtypeStruct((nt, SUBLANES, tm), I32),
                   jax.ShapeDtypeStruct((nt, 1, TABLE_W), I32),
                   jax.ShapeDtypeStruct((SUBLANES, LANES), I32)],
        scratch_shapes=[pltpu.VMEM((1, LANES), F32), pltpu.VMEM((1, LANES), F32)],
        compiler_params=_params(1),
        name="moe_slots",
    )(topi, sizes)


def _dispatch_kernel(tab_ref, pad_start_ref, pad_units_ref, tail_ref, colt_ref, hx_ref, xs_ref, stage, zeros, sem, zsem):
    i = pl.program_id(0)
    nt = pl.num_programs(0)
    slot = i % 2
    kb = stage.shape[1]
    tm = hx_ref.shape[0]

    def scatter(buf_slot):
        def copy(slot_row, stage_row, rows):
            return pltpu.make_async_copy(stage.at[buf_slot, pl.ds(stage_row, rows)],
                                         xs_ref.at[pl.ds(slot_row, rows)], sem.at[buf_slot])
        return copy

    def drain(tile, buf_slot):
        _for_each_unit(tab_ref, tile, lambda *u: scatter(buf_slot)(*u).wait())

    @pl.when(i >= 2)
    def _():
        drain(i - 2, slot)

    colt = colt_ref[0]
    c = lax.broadcasted_iota(I32, (kb, tm), 0)
    onehot = jnp.zeros((kb, tm), F32)
    for k in range(TOP_K):
        onehot = jnp.where(c == colt[k:k + 1, :], 1.0, onehot)
    stage[slot] = jnp.dot(onehot.astype(BF16), hx_ref[...], preferred_element_type=F32).astype(BF16)

    _for_each_unit(tab_ref, i, lambda *u: scatter(slot)(*u).start())

    @pl.when(i == nt - 1)
    def _():
        zeros[...] = jnp.zeros_like(zeros)
        n_exp = pad_start_ref.shape[0]

        def zero_copy(e, u):
            dst = pl.multiple_of(pad_start_ref[e] + u * GROUP_ALIGN, GROUP_ALIGN)
            return pltpu.make_async_copy(zeros.at[pl.ds(0, GROUP_ALIGN)], xs_ref.at[pl.ds(dst, GROUP_ALIGN)], zsem)

        def per_expert(fn):
            def outer(e, carry):
                def inner(u, carry2):
                    fn(e, u)
                    return carry2
                lax.fori_loop(0, pad_units_ref[e], inner, 0)
                return carry
            lax.fori_loop(0, n_exp, outer, 0)

        per_expert(lambda e, u: zero_copy(e, u).start())
        per_expert(lambda e, u: zero_copy(e, u).wait())

        zrows = zeros.shape[0]

        def tail_copy(u):
            dst = pl.multiple_of(tail_ref[0] + u * zrows, zrows)
            return pltpu.make_async_copy(zeros, xs_ref.at[pl.ds(dst, zrows)], zsem)

        def tail_loop(fn):
            def body(u, carry):
                fn(u)
                return carry
            lax.fori_loop(0, tail_ref[1], body, 0)

        tail_loop(lambda u: tail_copy(u).start())
        tail_loop(lambda u: tail_copy(u).wait())

        @pl.when(nt >= 2)
        def _():
            drain(i - 1, 1 - slot)
        drain(i, slot)


def _dispatch(tab, pad_start, pad_units, tail, colt, hx, n_slots):
    t, d = hx.shape
    tm = ROUTE_TILE
    return pl.pallas_call(
        _dispatch_kernel,
        grid_spec=pltpu.PrefetchScalarGridSpec(
            num_scalar_prefetch=4,
            grid=(t // tm,),
            in_specs=[pl.BlockSpec((1, SUBLANES, tm), lambda i, *_: (i, 0, 0)),
                      pl.BlockSpec((tm, d), lambda i, *_: (i, 0))],
            out_specs=pl.BlockSpec(memory_space=pl.ANY),
            scratch_shapes=[pltpu.VMEM((2, STAGE_ROWS, d), BF16), pltpu.VMEM((ZERO_ROWS, d), BF16),
                            pltpu.SemaphoreType.DMA((2,)), pltpu.SemaphoreType.DMA(())]),
        out_shape=jax.ShapeDtypeStruct((n_slots, d), BF16),
        compiler_params=_params(1),
        name="moe_dispatch",
    )(tab, pad_start, pad_units, tail, colt, hx)


def _expert_kernel(te_ref, rows_ref, next_ref, nu_ref, xs_ref, wgu_hbm, bgu_ref, wd_hbm, bd_ref, ys_ref,
                   wgu_f32, wd_f32, wgu_sc, wd_sc, wsem, *, layer):
    j = pl.program_id(0)
    active = j < nu_ref[0]
    changed = (j == 0) | (te_ref[j] != te_ref[jnp.maximum(j - 1, 0)])
    ff = wd_sc.shape[0]
    te = xs_ref.shape[0]
    half = te // 2

    def weight_copies(e):
        return (pltpu.make_async_copy(wgu_hbm.at[layer, e], wgu_f32, wsem.at[0]),
                pltpu.make_async_copy(wd_hbm.at[layer, e], wd_f32, wsem.at[1]))

    def run(rows, fresh):
        x = xs_ref[0:rows, :]
        nblk = 2
        fb = ff // nblk

        def up(c):
            gl_cols = slice(c * fb, (c + 1) * fb)
            lin_cols = slice(ff + c * fb, ff + (c + 1) * fb)
            if fresh:
                wgu_sc[:, gl_cols] = wgu_f32[:, gl_cols].astype(BF16)
                wgu_sc[:, lin_cols] = wgu_f32[:, lin_cols].astype(BF16)
            gl = jnp.dot(x, wgu_sc[:, gl_cols], preferred_element_type=F32) + bgu_ref[:, gl_cols]
            lin = jnp.dot(x, wgu_sc[:, lin_cols], preferred_element_type=F32) + bgu_ref[:, lin_cols]
            return gl, lin

        y = bd_ref[...]
        nxt = up(0)
        for c in range(nblk):
            gl, lin = nxt
            if c + 1 < nblk:
                nxt = up(c + 1)
            gl = jnp.minimum(gl, SWIGLU_LIMIT)
            lin = jnp.clip(lin, -SWIGLU_LIMIT, SWIGLU_LIMIT)
            act = gl * _sigmoid(SWIGLU_ALPHA * gl) * (lin + 1.0)
            if fresh:
                wd_sc[c * fb:(c + 1) * fb, :] = wd_f32[c * fb:(c + 1) * fb, :].astype(BF16)
            y = y + jnp.dot(act.astype(BF16), wd_sc[c * fb:(c + 1) * fb, :], preferred_element_type=F32)
        ys_ref[0:rows, :] = y.astype(BF16)
        if rows < te:
            ys_ref[rows:te, :] = jnp.zeros((te - rows, ys_ref.shape[1]), BF16)

    valid_rows = rows_ref[j]
    for fresh in (True, False):
        for rows in (te, half):
            size_ok = (valid_rows > half) if rows == te else (valid_rows <= half)
            first_ok = changed if fresh else jnp.logical_not(changed)

            @pl.when(active & size_ok & first_ok)
            def _(fresh=fresh, rows=rows):
                if fresh:
                    @pl.when(j == 0)
                    def _():
                        for cp in weight_copies(te_ref[0]):
                            cp.start()

                    for cp in weight_copies(te_ref[j]):
                        cp.wait()
                run(rows, fresh)
                if fresh:
                    @pl.when(next_ref[j] >= 0)
                    def _():
                        for cp in weight_copies(next_ref[j]):
                            cp.start()

    @pl.when(jnp.logical_not(active))
    def _():
        ys_ref[...] = jnp.zeros_like(ys_ref)


def _experts(tile_expert, tile_rows, next_expert, n_used, xs, layer, w_gu, b_gu, w_down, b_down):
    n_slots, d = xs.shape
    depth, n_exp, _, ff2 = w_gu.shape
    ff = w_down.shape[2]
    te = EXPERT_TILE
    row_idx = lambda j, te_r, rows_r, nx_r, nu: (jnp.maximum(jnp.minimum(j, nu[0] - 1), 0), 0)
    b_idx = lambda j, te_r, rows_r, nx_r, nu: (layer, te_r[j], 0, 0)
    return pl.pallas_call(
        functools.partial(_expert_kernel, layer=layer),
        grid_spec=pltpu.PrefetchScalarGridSpec(
            num_scalar_prefetch=4,
            grid=(n_slots // te,),
            in_specs=[pl.BlockSpec((te, d), row_idx),
                      pl.BlockSpec(memory_space=pl.ANY),
                      pl.BlockSpec((None, None, 1, ff2), b_idx),
                      pl.BlockSpec(memory_space=pl.ANY),
                      pl.BlockSpec((None, None, 1, d), b_idx)],
            out_specs=pl.BlockSpec((te, d), lambda j, te_r, rows_r, nx_r, nu: (j, 0)),
            scratch_shapes=[pltpu.VMEM((d, ff2), F32), pltpu.VMEM((ff, d), F32),
                            pltpu.VMEM((d, ff2), BF16), pltpu.VMEM((ff, d), BF16),
                            pltpu.SemaphoreType.DMA((2,))]),
        out_shape=jax.ShapeDtypeStruct((n_slots, d), BF16),
        compiler_params=_params(1),
        name="moe_experts",
    )(tile_expert, tile_rows, next_expert, n_used, xs, w_gu, b_gu.reshape(depth, n_exp, 1, ff2),
      w_down, b_down.reshape(depth, n_exp, 1, d))


def _combine_kernel(tab_ref, col_ref, gate_ref, ys_ref, x_ref, mod_ref, lng_ref, lnb_ref, out_ref, stage, sem, *,
                    n_lat_tiles, tiles_per_batch, ctx_row, d, alpha):
    i = pl.program_id(0)
    nt = pl.num_programs(0)
    slot = i % 2
    kb = stage.shape[1]
    tm = x_ref.shape[0]

    def gather(buf_slot):
        def copy(slot_row, stage_row, rows):
            return pltpu.make_async_copy(ys_ref.at[pl.ds(slot_row, rows)],
                                         stage.at[buf_slot, pl.ds(stage_row, rows)], sem.at[buf_slot])
        return copy

    @pl.when(i == 0)
    def _():
        stage[...] = jnp.zeros_like(stage)
        _for_each_unit(tab_ref, 0, lambda *u: gather(0)(*u).start())

    @pl.when(i + 1 < nt)
    def _():
        _for_each_unit(tab_ref, i + 1, lambda *u: gather(1 - slot)(*u).start())

    _for_each_unit(tab_ref, i, lambda *u: gather(slot)(*u).wait())

    col = col_ref[...]
    gates = gate_ref[...]
    kblk = 2 * LANES

    def gate_weights(r0):
        c = r0 + lax.broadcasted_iota(I32, (tm, kblk), 1)
        w = jnp.zeros((tm, kblk), F32)
        for k in range(TOP_K):
            w = jnp.where(c == col[:, k:k + 1], gates[:, k:k + 1], w)
        return w.astype(BF16)

    y = jnp.zeros((tm, d), F32)
    nxt = gate_weights(0)
    for r0 in range(0, kb, kblk):
        w = nxt
        if r0 + kblk < kb:
            nxt = gate_weights(r0 + kblk)
        y = y + jnp.dot(w, stage[slot, r0:r0 + kblk, :], preferred_element_type=F32)
    row = _mod_row(i, n_lat_tiles, tiles_per_batch, ctx_row)
    gate_mlp = mod_ref[pl.ds(row, 1), 5 * d:6 * d]
    out_ref[...] = _layer_norm(alpha * x_ref[...] + gate_mlp * y, lng_ref[...], lnb_ref[...])


def _combine(tab, col, ys, gates, x1, mod, ln_g, ln_b, dims, alpha):
    B, S, C, D = dims
    t = x1.shape[0]
    tm = ROUTE_TILE
    kern = functools.partial(_combine_kernel, n_lat_tiles=B * S // tm, tiles_per_batch=S // tm, ctx_row=B,
                             d=D, alpha=alpha)
    return pl.pallas_call(
        kern,
        grid_spec=pltpu.PrefetchScalarGridSpec(
            num_scalar_prefetch=1,
            grid=(t // tm,),
            in_specs=[pl.BlockSpec((tm, LANES), lambda i, tab_r: (i, 0)),
                      pl.BlockSpec((tm, LANES), lambda i, tab_r: (i, 0)),
                      pl.BlockSpec(memory_space=pl.ANY),
                      pl.BlockSpec((tm, D), lambda i, tab_r: (i, 0)),
                      pl.BlockSpec(mod.shape, lambda i, tab_r: (0, 0)),
                      pl.BlockSpec((1, D), lambda i, tab_r: (0, 0)),
                      pl.BlockSpec((1, D), lambda i, tab_r: (0, 0))],
            out_specs=pl.BlockSpec((tm, D), lambda i, tab_r: (i, 0)),
            scratch_shapes=[pltpu.VMEM((2, STAGE_ROWS, D), BF16), pltpu.SemaphoreType.DMA((2,))]),
        out_shape=jax.ShapeDtypeStruct((t, D), F32),
        compiler_params=_params(1),
        name="moe_combine",
    )(tab, col, gates, ys, x1, mod, ln_g.reshape(1, D), ln_b.reshape(1, D))


def _moe(hx, topi, gates, sizes, x1, mod, ln_g, ln_b, layer, w_gu, b_gu, w_down, b_down, dims, alpha):
    t = hx.shape[0]
    n_exp = w_gu.shape[1]
    te = EXPERT_TILE
    n_route_tiles = t // ROUTE_TILE
    max_rows = t * TOP_K + (GROUP_ALIGN - 1) * n_exp * n_route_tiles
    n_tiles = -(-max_rows // te) + n_exp
    col, colt, tab, meta = _slots(topi, sizes, te)
    tab = tab.reshape(-1)
    tot, base, padded = meta[0, :n_exp], meta[1, :n_exp], meta[2, :n_exp]
    ends = jnp.cumsum(padded // te)
    n_used = ends[-1:].astype(I32)
    tile_ids = jnp.minimum(jnp.arange(n_tiles, dtype=I32), n_used[0] - 1)
    tile_expert = jnp.sum((tile_ids[:, None] >= ends[None, :]).astype(I32), axis=1)
    tile_expert = jnp.minimum(tile_expert, n_exp - 1).astype(I32)
    pad_start = (base + tot).astype(I32)
    pad_units = ((padded - tot) // GROUP_ALIGN).astype(I32)
    used_rows = n_used[0] * te
    tail = jnp.stack([used_rows, (n_tiles * te - used_rows) // ZERO_ROWS]).astype(I32)
    xs = _dispatch(tab, pad_start, pad_units, tail, colt, hx, n_tiles * te)
    experts = jnp.arange(n_exp, dtype=I32)
    of_tile = tile_expert[:, None] == experts[None, :]
    pick = lambda per_expert: jnp.sum(jnp.where(of_tile, per_expert[None, :], 0), axis=1).astype(I32)
    tile_rows = jnp.clip(pick(base + tot) - tile_ids * te, 0, te).astype(I32)
    later = (padded[None, :] > 0) & (experts[None, :] > experts[:, None])
    next_of = jnp.min(jnp.where(later, experts[None, :], n_exp), axis=1)
    next_expert = pick(jnp.where(next_of == n_exp, -1, next_of))
    ys = _experts(tile_expert, tile_rows, next_expert, n_used, xs, layer, w_gu, b_gu, w_down, b_down)
    return _combine(tab, col, ys, gates, x1, mod, ln_g, ln_b, dims, alpha)


def _ml_in_kernel(xp_ref, x_ref, xn_ref, mod_ref, wqk_ref, bqk_ref, wvo_ref, bvo_ref, wg_ref, bg_ref,
                  cw_ref, cb_ref, q_ref, k_ref, v_ref, og_ref, g_ref, *,
                  n_lat_tiles, tiles_per_batch, ctx_parts_per_seq, ctx_row, d):
    i = pl.program_id(0)
    tm = x_ref.shape[0]
    part = ML_IN_PART
    n_parts = tm // part
    halo = SUBLANES
    row = _mod_row(i, n_lat_tiles, tiles_per_batch, ctx_row)
    shift = mod_ref[pl.ds(row, 1), 0:d]
    scale = mod_ref[pl.ds(row, 1), d:2 * d]
    is_lat = i < n_lat_tiles
    nv = wvo_ref.shape[1] // 2
    n_ext = part + 2 * halo
    cw = cw_ref[...]

    def modulated(rows_ref, rows=slice(None)):
        return rows_ref[rows, :] * (1.0 + scale) + shift

    def part_stages(p):
        rows = slice(p * part, (p + 1) * part)
        seq_part = jnp.where(is_lat, (i % tiles_per_batch) * n_parts + p, p % ctx_parts_per_seq)
        seq_parts = jnp.where(is_lat, tiles_per_batch * n_parts, ctx_parts_per_seq)
        first = seq_part == 0
        last = seq_part == seq_parts - 1
        before = modulated(xp_ref) if p == 0 else modulated(x_ref, slice(p * part - halo, p * part))
        after = (modulated(xn_ref) if p == n_parts - 1
                 else modulated(x_ref, slice((p + 1) * part, (p + 1) * part + halo)))
        h = modulated(x_ref, rows)
        h_ext = jnp.concatenate([before, h, after], axis=0)
        z = jnp.dot(h_ext.astype(BF16), wqk_ref[...], preferred_element_type=F32) + bqk_ref[...]
        h_bf = h.astype(BF16)
        v_pre = jnp.dot(h_bf, wvo_ref[:, :nv], preferred_element_type=F32) + bvo_ref[:, :nv]
        o_pre = jnp.dot(h_bf, wvo_ref[:, nv:], preferred_element_type=F32) + bvo_ref[:, nv:]
        zg = _dot_split(h, wg_ref) + bg_ref[...]
        yield
        r = lax.broadcasted_iota(I32, z.shape, 0)
        z = jnp.where(((r < halo) & first) | ((r >= halo + part) & last), 0.0, z)
        acc = None
        for j in range(ML_CONV_W):
            sh = (ML_CONV_W // 2 - j) % n_ext
            zj = z if sh == 0 else pltpu.roll(z, sh, 0)
            term = zj[halo:halo + part] * cw[j:j + 1]
            acc = term if acc is None else acc + term
        qk = acc + cb_ref[...]
        qk = qk * _sigmoid(qk)
        nqk = qk.shape[1] // 2
        qk_dim = nqk // ML_HEADS
        q_ref[rows, :] = (qk[:, :nqk] * (qk_dim ** -0.5)).astype(BF16)
        k_ref[rows, :] = qk[:, nqk:].astype(BF16)
        yield
        v_ref[rows, :] = v_pre.astype(BF16)
        og_ref[rows, :] = _sigmoid(o_pre).astype(BF16)
        g = GATE_CAP * jnp.tanh(zg / GATE_CAP)
        log_sig = jnp.minimum(g, 0.0) - jnp.log(1.0 + jnp.exp(-jnp.abs(g)))
        lane = lax.broadcasted_iota(I32, g.shape, 1)
        is_forget = ((lane // ML_HEADS) % 2) == 1
        g_ref[rows, :] = jnp.where(is_forget, log_sig, g)

    running = [part_stages(p) for p in range(n_parts)]
    while running:
        running = [s for s in running if next(s, True) is None]


def _ml_in(x_all, mod, w_in, b_in, conv_w, conv_b, dims):
    B, S, C, D = dims
    t_all = x_all.shape[0]
    tm = 512
    part = ML_IN_PART
    assert C % part == 0 and S % tm == 0 and (B * C) % tm == 0
    n_lat_tiles = B * S // tm
    nqk2 = conv_w.shape[1]
    nv = (w_in.shape[1] - nqk2 - 4 * ML_HEADS) // 2
    ng = 4 * ML_HEADS
    w_qk = w_in[:, :nqk2].astype(BF16)
    w_vo = w_in[:, nqk2:nqk2 + 2 * nv].astype(BF16)
    w_g = _split_weight(jnp.zeros((D, LANES), F32).at[:, :ng].set(w_in[:, nqk2 + 2 * nv:]))
    b_qk = b_in[:nqk2].reshape(1, -1)
    b_vo = b_in[nqk2:nqk2 + 2 * nv].reshape(1, -1)
    b_g = jnp.zeros((1, LANES), F32).at[0, :ng].set(b_in[nqk2 + 2 * nv:])
    cw = jnp.zeros((SUBLANES, nqk2), F32).at[:ML_CONV_W].set(conv_w)
    cb = conv_b.reshape(1, -1)
    hb = tm // SUBLANES
    n_hblk = t_all // SUBLANES
    full = lambda a: pl.BlockSpec(a.shape, lambda i: (0,) * a.ndim)
    row_spec = lambda w: pl.BlockSpec((tm, w), lambda i: (i, 0))
    kern = functools.partial(_ml_in_kernel, n_lat_tiles=n_lat_tiles, tiles_per_batch=S // tm,
                             ctx_parts_per_seq=C // part, ctx_row=B, d=D)
    return pl.pallas_call(
        kern,
        grid=(t_all // tm,),
        in_specs=[pl.BlockSpec((SUBLANES, D), lambda i: (jnp.maximum(i * hb - 1, 0), 0)),
                  row_spec(D),
                  pl.BlockSpec((SUBLANES, D), lambda i: (jnp.minimum((i + 1) * hb, n_hblk - 1), 0)),
                  full(mod), full(w_qk), full(b_qk), full(w_vo), full(b_vo), full(w_g), full(b_g),
                  full(cw), full(cb)],
        out_specs=[row_spec(nqk2 // 2), row_spec(nqk2 // 2), row_spec(nv), row_spec(nv), row_spec(LANES)],
        out_shape=[jax.ShapeDtypeStruct((t_all, nqk2 // 2), BF16), jax.ShapeDtypeStruct((t_all, nqk2 // 2), BF16),
                   jax.ShapeDtypeStruct((t_all, nv), BF16), jax.ShapeDtypeStruct((t_all, nv), BF16),
                   jax.ShapeDtypeStruct((t_all, LANES), F32)],
        compiler_params=_params(1),
        name="mlstm_in",
    )(x_all, x_all, x_all, mod, w_qk, b_qk, w_vo, b_vo, w_g, b_g, cw, cb)


def _ml_chunk(q_ref, k_ref, v_ref, g_ref, h_ref, s_sc, m_sc, reverse):
    L = q_ref.shape[0]
    qk_dim = q_ref.shape[1] // ML_HEADS
    v_dim = v_ref.shape[1] // ML_HEADS
    nh = ML_HEADS
    gates_t = g_ref[...].T
    sr = lax.broadcasted_iota(I32, (L, L), 0)
    lc = lax.broadcasted_iota(I32, (L, L), 1)
    upto = (sr >= lc) if reverse else (sr <= lc)
    cum_t = jnp.dot(gates_t, jnp.where(upto, 1.0, 0.0), precision=HIGHEST, preferred_element_type=F32)
    off = 2 * nh if reverse else 0
    li = gates_t[off:off + nh]
    b = cum_t[off + nh:off + 2 * nh]
    c = li - b
    lane = lax.broadcasted_iota(I32, c.shape, 1)
    mu = c
    d = 1
    while d < L:
        if reverse:
            shifted = jnp.where(lane < L - d, pltpu.roll(mu, L - d, 1), -jnp.inf)
        else:
            shifted = jnp.where(lane >= d, pltpu.roll(mu, d, 1), -jnp.inf)
        mu = jnp.maximum(mu, shifted)
        d *= 2
    m_prev = m_sc[...]
    mu = jnp.maximum(mu, m_prev)
    m_t = b + mu
    end = 0 if reverse else L - 1
    mu_end = mu[:, end:end + 1]
    decay = jnp.exp(m_prev[:, 0:1] - mu_end)
    wk = jnp.exp(c - mu_end)
    m_sc[...] = jnp.broadcast_to(b[:, end:end + 1] + mu_end, m_prev.shape)
    cols = jnp.concatenate([mu, m_t, jnp.zeros((L - 2 * nh, L), F32)], axis=0).T
    k_t = k_ref[...].astype(F32).T
    rl = lax.broadcasted_iota(I32, (L, L), 0)
    cs = lax.broadcasted_iota(I32, (L, L), 1)
    allowed = (cs >= rl) if reverse else (cs <= rl)
    ones = jnp.ones((L, v_dim), BF16)

    def head(h):
        qh = q_ref[:, h * qk_dim:(h + 1) * qk_dim]
        kt_h = k_t[h * qk_dim:(h + 1) * qk_dim, :]
        qk = jnp.dot(qh, kt_h.astype(BF16), preferred_element_type=F32)
        state = s_sc[h]
        q_state = jnp.dot(qh, state.astype(BF16), preferred_element_type=F32)
        yield
        mu_col = jnp.broadcast_to(cols[:, h:h + 1], (L, L))
        p = jnp.exp(jnp.where(allowed, c[h:h + 1, :] - mu_col, -jnp.inf))
        w_inter = jnp.exp(m_prev[h:h + 1, :] - mu_col)
        s = (qk * p).astype(BF16)
        yield
        v_aug = jnp.concatenate([v_ref[:, h * v_dim:(h + 1) * v_dim], ones], axis=1)
        nd = jnp.dot(s, v_aug, preferred_element_type=F32)
        kw_t = (kt_h * wk[h:h + 1, :]).astype(BF16)
        new_state = decay[h:h + 1, :] * state + jnp.dot(kw_t, v_aug, preferred_element_type=F32)
        yield
        mt_col = jnp.broadcast_to(cols[:, nh + h:nh + h + 1], (L, v_dim))
        nd = nd + jnp.concatenate([w_inter, w_inter], axis=1) * q_state
        num = nd[:, :v_dim]
        den = nd[:, v_dim:]
        h_ref[:, h * v_dim:(h + 1) * v_dim] = num / jnp.maximum(jnp.abs(den), jnp.exp(-mt_col))
        s_sc[h] = new_state

    return head


def _ml_scan_kernel(*refs, n_batch):
    n_in = 8 * n_batch
    hf_ref, hb_ref, s_sc, m_sc = refs[n_in:]

    @pl.when(pl.program_id(0) == 0)
    def _():
        s_sc[...] = jnp.zeros_like(s_sc)
        m_sc[...] = jnp.zeros_like(m_sc)

    heads = []
    for b in range(n_batch):
        qf, kf, vf, gf, qb, kb, vb, gb = refs[8 * b:8 * b + 8]
        heads.append(_ml_chunk(qf, kf, vf, gf, hf_ref.at[b], s_sc.at[b, 0], m_sc.at[b, 0], False))
        heads.append(_ml_chunk(qb, kb, vb, gb, hb_ref.at[b], s_sc.at[b, 1], m_sc.at[b, 1], True))
    for h in range(ML_HEADS):
        running = [head(h) for head in heads]
        while running:
            running = [g for g in running if next(g, True) is None]


def _ml_scan(q, k, v, g, dims):
    B, S, C, D = dims
    L = ML_CHUNK
    assert L == LANES
    nc_ctx = C // L
    nc_lat = S // L
    qk_dim = q.shape[1] // ML_HEADS
    v_dim = v.shape[1] // ML_HEADS

    def in_index(b, reverse):
        def idx(c):
            in_ctx = c < nc_ctx
            cl = c - nc_ctx
            if reverse:
                ctx_blk = (B * S + b * C) // L + (nc_ctx - 1 - c)
                lat_blk = (b * S) // L + (nc_lat - 1 - cl)
            else:
                ctx_blk = (B * S + b * C) // L + c
                lat_blk = (b * S) // L + cl
            return (jnp.where(in_ctx, ctx_blk, lat_blk), 0)
        return idx

    def out_index(reverse):
        def idx(c):
            cl = jnp.maximum(c - nc_ctx, 0)
            return (0, nc_lat - 1 - cl if reverse else cl, 0)
        return idx

    widths = (q.shape[1], k.shape[1], v.shape[1], LANES)
    in_specs, args = [], []
    for b in range(B):
        for rev in (False, True):
            in_specs += [pl.BlockSpec((L, w), in_index(b, rev)) for w in widths]
            args += [q, k, v, g]
    out = jax.ShapeDtypeStruct((B, S, v.shape[1]), F32)
    return pl.pallas_call(
        functools.partial(_ml_scan_kernel, n_batch=B),
        grid=(nc_ctx + nc_lat,),
        in_specs=in_specs,
        out_specs=[pl.BlockSpec((B, L, v.shape[1]), out_index(False)),
                   pl.BlockSpec((B, L, v.shape[1]), out_index(True))],
        out_shape=[out, out],
        scratch_shapes=[pltpu.VMEM((B, 2, ML_HEADS, qk_dim, 2 * v_dim), F32),
                        pltpu.VMEM((B, 2, ML_HEADS, LANES), F32)],
        compiler_params=_params(1),
        name="mlstm_scan",
    )(*args)


def kernel(x, c, ctx, c_ctx, ada_w, ada_b, ln_g, ln_b, attn_w_qkv, attn_b_qkv, attn_sink, attn_w_o, attn_b_o,
           ml_w_in, ml_b_in, ml_conv_w, ml_conv_b, ml_norm_g, ml_w_out, router_w, router_b,
           exp_w_gu, exp_b_gu, exp_w_down, exp_b_down):
    B, S, D = x.shape
    C = ctx.shape[1]
    depth = ada_w.shape[0]
    dims = (B, S, C, D)
    alpha = (2.0 * depth) ** 0.25
    n_lat = B * S

    cvec = jnp.zeros((SUBLANES, D), F32).at[:B].set(c).at[B].set(c_ctx)
    mods = _adaln(cvec, ada_w, ada_b)
    x_lat, x_ctx = x.reshape(n_lat, D), ctx.reshape(B * C, D)

    q, k, v = _attn_qkv(x_lat, x_ctx, mods[0], attn_w_qkv[0], attn_b_qkv[0], dims)
    o = _attention(q, k, v, attn_sink[0], dims)
    x1, hx, topi, gates, sizes = _post_mixer((o,), None, (x_lat, x_ctx), mods[0], attn_w_o[0], attn_b_o[0],
                                      ln_g[0, 0], ln_b[0, 0], router_w[0], router_b[0], dims, n_lat + B * C, alpha)
    x_all = _moe(hx, topi, gates, sizes, x1, mods[0], ln_g[0, 1], ln_b[0, 1],
                 0, exp_w_gu, exp_b_gu, exp_w_down, exp_b_down, dims, alpha)

    q, k, v, og, g = _ml_in(x_all, mods[1], ml_w_in[0], ml_b_in[0], ml_conv_w[0], ml_conv_b[0], dims)
    hf, hb = (h.reshape(n_lat, -1) for h in _ml_scan(q, k, v, g, dims))
    zero_b = jnp.zeros((D,), F32)
    x1, hx, topi, gates, sizes = _post_mixer((hf, hb, og), ml_norm_g[0], x_all, mods[1], ml_w_out[0], zero_b,
                                      ln_g[1, 0], ln_b[1, 0], router_w[1], router_b[1], dims, n_lat, alpha)
    out = _moe(hx, topi, gates, sizes, x1, mods[1], ln_g[1, 1], ln_b[1, 1],
               1, exp_w_gu, exp_b_gu, exp_w_down, exp_b_down, dims, alpha)
    return out.reshape(B, S, D)
```
